```python
import math
import jax, jax.numpy as jnp
from jax import lax
import numpy as np

D_MODEL = 1024
BATCH = 8
SEQ = 2048
DEPTH = 1
DEC_BATCH = 128
DEC_SEQ = 8
PAST_LEN = 16384
PAGE_SIZE = 128

CONV_WIDTH = D_MODEL
CONV_K = 3
N_HEADS_M = 4
HEAD_V = D_MODEL // N_HEADS_M
HEAD_QK = HEAD_V // 2
M_QK = N_HEADS_M * HEAD_QK
M_V = N_HEADS_M * HEAD_V
CHUNK = 64
N_EXPERTS = 256
TOP_K = 8
D_EXPERT = D_MODEL // 4
D_SHARED = D_EXPERT
ROUTED_SCALE = 2.5
ROW_BLOCK = 128
ALPHA = (2.0 * DEPTH) ** 0.25
BETA = (8.0 * DEPTH) ** -0.25
LN_EPS = 1e-5
SPLIT_SIZES = (CONV_WIDTH, CONV_WIDTH, CONV_WIDTH, M_QK, M_QK, M_V, M_V,
               N_HEADS_M, N_HEADS_M, D_MODEL, D_MODEL)
IN_WIDTH = sum(SPLIT_SIZES)

kernel_name = 'hybrid_conv_mlstm_moe_step'


def layer_norm(x, g, b):
    xf = x.astype(jnp.float32)
    mu = jnp.mean(xf, -1, keepdims=True)
    var = jnp.mean(jnp.square(xf - mu), -1, keepdims=True)
    return ((xf - mu) * lax.rsqrt(var + LN_EPS) * g + b).astype(x.dtype)


def split_cols(z):
    out = []
    off = 0
    for s in SPLIT_SIZES:
        out.append(z[..., off:off + s])
        off += s
    return out


def short_conv(u, buf, w):
    T = u.shape[1]
    full = jnp.concatenate([buf.astype(u.dtype), u], axis=1)
    out = full[:, 0:T] * w[0]
    for j in range(1, CONV_K):
        out = out + full[:, j:j + T] * w[j]
    return out, full[:, T:]


def mlstm_chunk(carry, inp):
    C, n, m = carry
    q, k, v, ig, lf = inp
    L = q.shape[2]
    b = jnp.cumsum(lf, axis=-1)
    inter = b + m[..., None]
    causal = jnp.tril(jnp.ones((L, L), dtype=bool))
    dmat = jnp.where(causal, b[..., :, None] - b[..., None, :] + ig[..., None, :], -jnp.inf)
    m_t = jnp.maximum(inter, jnp.max(dmat, axis=-1))
    w_intra = jnp.exp(dmat - m_t[..., None])
    w_inter = jnp.exp(inter - m_t)
    s = jnp.einsum('bhtd,bhsd->bhts', q, k) * w_intra
    num = (w_inter[..., None] * jnp.einsum('bhtk,bhkv->bhtv', q, C)
           + jnp.einsum('bhts,bhsv->bhtv', s, v))
    den = w_inter * jnp.einsum('bhtk,bhk->bht', q, n) + jnp.sum(s, axis=-1)
    h = num / jnp.maximum(jnp.abs(den), jnp.exp(-m_t))[..., None]
    m_new = m_t[..., -1]
    ws = jnp.exp(b[..., -1:] - b + ig - m_new[..., None])
    cdecay = jnp.exp(inter[..., -1] - m_new)
    C_new = cdecay[..., None, None] * C + jnp.einsum('bhs,bhsk,bhsv->bhkv', ws, k, v)
    n_new = cdecay[..., None] * n + jnp.einsum('bhs,bhsk->bhk', ws, k)
    return (C_new, n_new, m_new), h


def mlstm(q, k, v, ig, lf, state):
    Bsz, H, T, _ = q.shape
    L = CHUNK if T % CHUNK == 0 else T
    nc = T // L

    def chunks(t):
        return jnp.moveaxis(t.reshape(t.shape[:2] + (nc, L) + t.shape[3:]), 2, 0)

    carry, h = lax.scan(mlstm_chunk, state, (chunks(q), chunks(k), chunks(v), chunks(ig), chunks(lf)))
    h = jnp.moveaxis(h, 0, 2).reshape(Bsz, H, T, HEAD_V)
    return h, carry


def token_mixers(x, conv_buf, C0, n0, m0, w_in, b_gates, w_conv, mh_gain, w_out):
    Bsz, T, _ = x.shape
    f32 = jnp.float32
    z = jnp.einsum('btd,de->bte', x, w_in)
    cb, cc, ch, q, k, v, og, ig, fg, ga, gb = split_cols(z)
    conv_out, conv_new = short_conv(cc * ch, conv_buf, w_conv)
    a = cb * conv_out
    def heads(t, d):
        return t.reshape(Bsz, T, N_HEADS_M, d).transpose(0, 2, 1, 3).astype(f32)
    qh = heads(q, HEAD_QK)
    kh = heads(k, HEAD_QK) * (HEAD_QK ** -0.5)
    vh = heads(v, HEAD_V)
    ilog = (ig + b_gates[:N_HEADS_M]).astype(f32).transpose(0, 2, 1)
    lf = jax.nn.log_sigmoid((fg + b_gates[N_HEADS_M:]).astype(f32)).transpose(0, 2, 1)
    state0 = (C0.astype(f32), n0.astype(f32), m0.astype(f32))
    h, (C1, n1, m1) = mlstm(qh, kh, vh, ilog, lf, state0)
    mu = jnp.mean(h, -1, keepdims=True)
    var = jnp.mean(jnp.square(h - mu), -1, keepdims=True)
    h = (h - mu) * lax.rsqrt(var + LN_EPS)
    h = h.transpose(0, 2, 1, 3).reshape(Bsz, T, M_V) * mh_gain
    mo = h.astype(x.dtype) * jax.nn.sigmoid(og)
    mix = jax.nn.sigmoid(ga) * a + jax.nn.sigmoid(gb) * mo
    y = jnp.einsum('bte,ed->btd', mix, w_out)
    return y, conv_new, C1, n1, m1


def moe(x, w_router, b_router, w_eg, w_eu, w_ed, w_sg, w_su, w_sd):
    T, D = x.shape
    scores = jax.nn.sigmoid(jnp.einsum('td,de->te', x, w_router).astype(jnp.float32))
    _, idx = lax.top_k(scores + b_router, TOP_K)
    sel = jnp.take_along_axis(scores, idx, axis=-1)
    gates = sel / jnp.sum(sel, -1, keepdims=True) * ROUTED_SCALE
    N = T * TOP_K
    R = ROW_BLOCK
    e_flat = idx.reshape(-1)
    tok_flat = jnp.repeat(jnp.arange(T, dtype=jnp.int32), TOP_K)
    g_flat = gates.reshape(-1)
    order = jnp.argsort(e_flat)
    e_sorted = e_flat[order]
    counts = jnp.bincount(e_flat, length=N_EXPERTS)
    padded = (counts + R - 1) // R * R
    pad_end = jnp.cumsum(padded)
    pad_start = pad_end - padded
    start = jnp.cumsum(counts) - counts
    dest = pad_start[e_sorted] + jnp.arange(N, dtype=jnp.int32) - start[e_sorted]
    n_blocks = (N + N_EXPERTS * (R - 1) + R - 1) // R
    P = n_blocks * R
    row_tok = jnp.full((P,), T, dtype=jnp.int32).at[dest].set(tok_flat[order])
    row_gate = jnp.zeros((P,), jnp.float32).at[dest].set(g_flat[order])
    block_e = jnp.minimum(jnp.searchsorted(pad_end, jnp.arange(n_blocks) * R, side='right'),
                          N_EXPERTS - 1)
    x_pad = jnp.concatenate([x, jnp.zeros((1, D), x.dtype)], axis=0)
    xb = x_pad[row_tok].reshape(n_blocks, R, D)

    def expert_block(args):
        xblk, e = args
        hid = jax.nn.silu(xblk @ w_eg[e]) * (xblk @ w_eu[e])
        return hid @ w_ed[e]

    yb = lax.map(expert_block, (xb, block_e))
    y_rows = yb.reshape(P, D) * row_gate[:, None].astype(x.dtype)
    routed = jax.ops.segment_sum(y_rows, row_tok, num_segments=T + 1)[:T]
    shared = (jax.nn.silu(x @ w_sg) * (x @ w_su)) @ w_sd
    return shared + routed


def decoder_layer(x, conv_buf, C0, n0, m0, w_in, b_gates, w_conv, mh_gain, w_out, ln1_g, ln1_b,
                  w_router, b_router, w_eg, w_eu, w_ed, w_sg, w_su, w_sd, ln2_g, ln2_b):
    y_mix, conv_new, C1, n1, m1 = token_mixers(x, conv_buf, C0, n0, m0, w_in, b_gates, w_conv,
                                               mh_gain, w_out)
    x1 = layer_norm(ALPHA * x + y_mix, ln1_g, ln1_b)
    f = moe(x1.reshape(-1, D_MODEL), w_router, b_router, w_eg, w_eu, w_ed, w_sg, w_su, w_sd)
    x2 = layer_norm(ALPHA * x1 + f.reshape(x1.shape), ln2_g, ln2_b)
    dt = x.dtype
    return x2, (conv_new.astype(dt), C1.astype(dt), n1.astype(dt), m1.astype(dt))


def setup_inputs(seed: int = 0) -> dict:
    key = jax.random.key(seed)
    ks = jax.random.split(key, 32)
    f32 = jnp.float32

    def nrm(k, shape, scale):
        return jax.random.normal(k, shape, f32) * scale

    col_scale = jnp.concatenate([jnp.full((s,), BETA if i in (2, 5) else 1.0, f32)
                                 for i, s in enumerate(SPLIT_SIZES)])
    b_gates = jnp.concatenate([nrm(ks[7], (DEPTH, N_HEADS_M), 0.1),
                               3.0 + nrm(ks[8], (DEPTH, N_HEADS_M), 0.5)], axis=-1)
    return {
        'x_prompt': nrm(ks[0], (BATCH, SEQ, D_MODEL), 1.0),
        'x_sample': nrm(ks[1], (DEC_BATCH, DEC_SEQ, D_MODEL), 1.0),
        'cache_conv': nrm(ks[2], (DEPTH, DEC_BATCH, CONV_K - 1, CONV_WIDTH), 0.5),
        'state_mlstm_C': nrm(ks[3], (DEPTH, DEC_BATCH, N_HEADS_M, HEAD_QK, HEAD_V), 0.3),
        'state_mlstm_n': nrm(ks[4], (DEPTH, DEC_BATCH, N_HEADS_M, HEAD_QK), 0.3),
        'state_mlstm_m': nrm(ks[5], (DEPTH, DEC_BATCH, N_HEADS_M), 0.5),
        'w_in': nrm(ks[6], (DEPTH, D_MODEL, IN_WIDTH), D_MODEL ** -0.5) * col_scale,
        'b_gates': b_gates,
        'w_conv': nrm(ks[9], (DEPTH, CONV_K, CONV_WIDTH), CONV_K ** -0.5),
        'mh_gain': 1.0 + nrm(ks[10], (DEPTH, M_V), 0.02),
        'w_out': nrm(ks[11], (DEPTH, D_MODEL, D_MODEL), D_MODEL ** -0.5 * BETA),
        'ln1_g': 1.0 + nrm(ks[12], (DEPTH, D_MODEL), 0.02),
        'ln1_b': nrm(ks[13], (DEPTH, D_MODEL), 0.02),
        'w_router': nrm(ks[14], (DEPTH, D_MODEL, N_EXPERTS), D_MODEL ** -0.5),
        'b_router': nrm(ks[15], (DEPTH, N_EXPERTS), 0.01),
        'w_exp_gate': nrm(ks[16], (DEPTH, N_EXPERTS, D_MODEL, D_EXPERT), D_MODEL ** -0.5 * BETA),
        'w_exp_up': nrm(ks[17], (DEPTH, N_EXPERTS, D_MODEL, D_EXPERT), D_MODEL ** -0.5 * BETA),
        'w_exp_down': nrm(ks[18], (DEPTH, N_EXPERTS, D_EXPERT, D_MODEL), D_EXPERT ** -0.5 * BETA),
        'w_sh_gate': nrm(ks[19], (DEPTH, D_MODEL, D_SHARED), D_MODEL ** -0.5 * BETA),
        'w_sh_up': nrm(ks[20], (DEPTH, D_MODEL, D_SHARED), D_MODEL ** -0.5 * BETA),
        'w_sh_down': nrm(ks[21], (DEPTH, D_SHARED, D_MODEL), D_SHARED ** -0.5 * BETA),
        'ln2_g': 1.0 + nrm(ks[22], (DEPTH, D_MODEL), 0.02),
        'ln2_b': nrm(ks[23], (DEPTH, D_MODEL), 0.02),
    }


def reference(x_prompt, x_sample, cache_conv, state_mlstm_C, state_mlstm_n, state_mlstm_m,
              w_in, b_gates, w_conv, mh_gain, w_out, ln1_g, ln1_b,
              w_router, b_router, w_exp_gate, w_exp_up, w_exp_down,
              w_sh_gate, w_sh_up, w_sh_down, ln2_g, ln2_b):
    hp, hs = x_prompt, x_sample
    Bp = x_prompt.shape[0]
    dt = x_prompt.dtype
    pc, pC, pn, pm = [], [], [], []
    sc, sC, sn, sm = [], [], [], []
    for l in range(DEPTH):
        lw = [w[l] for w in (w_in, b_gates, w_conv, mh_gain, w_out, ln1_g, ln1_b,
                             w_router, b_router, w_exp_gate, w_exp_up, w_exp_down,
                             w_sh_gate, w_sh_up, w_sh_down, ln2_g, ln2_b)]
        hp, st_p = decoder_layer(hp,
                                 jnp.zeros((Bp, CONV_K - 1, CONV_WIDTH), dt),
                                 jnp.zeros((Bp, N_HEADS_M, HEAD_QK, HEAD_V), jnp.float32),
                                 jnp.zeros((Bp, N_HEADS_M, HEAD_QK), jnp.float32),
                                 jnp.zeros((Bp, N_HEADS_M), jnp.float32),
                                 *lw)
        hs, st_s = decoder_layer(hs, cache_conv[l], state_mlstm_C[l], state_mlstm_n[l],
                                 state_mlstm_m[l], *lw)
        pc.append(st_p[0]); pC.append(st_p[1]); pn.append(st_p[2]); pm.append(st_p[3])
        sc.append(st_s[0]); sC.append(st_s[1]); sn.append(st_s[2]); sm.append(st_s[3])
    return (hp, hs,
            jnp.stack(pc), jnp.stack(pC), jnp.stack(pn), jnp.stack(pm),
            jnp.stack(sc), jnp.stack(sC), jnp.stack(sn), jnp.stack(sm))
```

```python
import functools

import jax
import jax.numpy as jnp
from jax import lax
from jax.experimental import pallas as pl
from jax.experimental.pallas import tpu as pltpu

F32 = jnp.float32
BF16 = jnp.bfloat16

D_MODEL = 1024
N_HEADS = 4
HEAD_V = 256
HEAD_QK = 128
N_EXPERTS = 256
TOP_K = 8
D_EXPERT = 256
ROUTED_SCALE = 2.5
LN_EPS = 1e-5
CONV_K = 3

LANES = 128
SUBLANES = 8
ROW_TILES = D_MODEL // LANES
Z_WIDTH = 8 * D_MODEL
GATE_COL0 = 6 * D_MODEL

VMEM_LIMIT = 56 * 1024 * 1024


def _cparams(sem):
    return pltpu.CompilerParams(dimension_semantics=sem, vmem_limit_bytes=VMEM_LIMIT)


def _sigmoid(x):
    return 1.0 / (1.0 + jnp.exp(-x))


def _log_sigmoid(x):
    return jnp.minimum(x, 0.0) - jnp.log(1.0 + jnp.exp(-jnp.abs(x)))


def _layer_norm(x, g, b):
    mu = jnp.mean(x, axis=-1, keepdims=True)
    xc = x - mu
    var = jnp.mean(xc * xc, axis=-1, keepdims=True)
    return xc * lax.rsqrt(var + LN_EPS) * g + b


def _split_hi_lo(x):
    hi = x.astype(BF16)
    lo = (x - hi.astype(F32)).astype(BF16)
    return hi, lo


def _dot(a, b):
    return jnp.dot(a, b, preferred_element_type=F32)


def _dot3(x, w_hi, w_lo):
    x_hi, x_lo = _split_hi_lo(x)
    return _dot(x_hi, w_hi) + _dot(x_lo, w_hi) + _dot(x_hi, w_lo)


def _proj_kernel(x_ref, w_ref, z_ref):
    z_ref[...] = _dot(x_ref[...].astype(BF16), w_ref[...])


def _proj(x, w, tm, tn):
    m, k = x.shape
    n = w.shape[1]
    return pl.pallas_call(
        _proj_kernel,
        grid=(m // tm, n // tn),
        in_specs=[pl.BlockSpec((tm, k), lambda i, j: (i, 0)),
                  pl.BlockSpec((k, tn), lambda i, j: (0, j))],
        out_specs=pl.BlockSpec((tm, tn), lambda i, j: (i, j)),
        out_shape=jax.ShapeDtypeStruct((m, n), F32),
        compiler_params=_cparams(("parallel", "arbitrary")),
        name="proj",
    )(x, w)


def _gate_proj_kernel(x_ref, wh_ref, wl_ref, z_ref):
    z_ref[...] = _dot3(x_ref[...], wh_ref[...], wl_ref[...])


def _gate_proj(x, w_hi, w_lo, tm):
    m, k = x.shape
    n = w_hi.shape[1]
    return pl.pallas_call(
        _gate_proj_kernel,
        grid=(m // tm,),
        in_specs=[pl.BlockSpec((tm, k), lambda i: (i, 0)),
                  pl.BlockSpec((k, n), lambda i: (0, 0)),
                  pl.BlockSpec((k, n), lambda i: (0, 0))],
        out_specs=pl.BlockSpec((tm, n), lambda i: (i, 0)),
        out_shape=jax.ShapeDtypeStruct((m, n), F32),
        compiler_params=_cparams(("parallel",)),
        name="gate_proj",
    )(x, w_hi, w_lo)


def _mlstm_chunk(q, k, v, ig_col, lf_col, ig_row, lf_row, c_state, n_row, m_prev):
    chunk = q.shape[0]
    ti = lax.broadcasted_iota(jnp.int32, (chunk, chunk), 0)
    si = lax.broadcasted_iota(jnp.int32, (chunk, chunk), 1)
    causal = si <= ti
    b_col = jnp.sum(jnp.where(causal, lf_row, 0.0), axis=1, keepdims=True)
    b_row = jnp.sum(jnp.where(ti <= si, lf_col, 0.0), axis=0, keepdims=True)
    inter = b_col + m_prev
    dmat = jnp.where(causal, b_col - b_row + ig_row, -jnp.inf)
    m_t = jnp.maximum(inter, jnp.max(dmat, axis=1, keepdims=True))
    w_intra = jnp.exp(dmat - m_t)
    w_inter = jnp.exp(inter - m_t)
    qb = q.astype(BF16)
    kb = k.astype(BF16)
    vb = v.astype(BF16)
    s = lax.dot_general(qb, kb, (((1,), (1,)), ((), ())), preferred_element_type=F32) * w_intra
    num = w_inter * _dot(qb, c_state.astype(BF16)) + _dot(s.astype(BF16), vb)
    den = (w_inter * jnp.sum(q * n_row, axis=1, keepdims=True)
           + jnp.sum(s, axis=1, keepdims=True))
    h = num / jnp.maximum(jnp.abs(den), jnp.exp(-m_t))
    m_new = m_t[chunk - 1:chunk, :]
    b_last = b_col[chunk - 1:chunk, :]
    ws_col = jnp.exp(b_last - b_col + ig_col - m_new)
    cdecay = jnp.exp(inter[chunk - 1:chunk, :] - m_new)
    kw = k * ws_col
    c_new = cdecay * c_state + lax.dot_general(
        kw.astype(BF16), vb, (((0,), (0,)), ((), ())), preferred_element_type=F32)
    n_new = cdecay * n_row + jnp.sum(kw, axis=0, keepdims=True)
    return h, c_new, n_new, m_new


def _mixer_kernel(chunk, bg_ref,
                  zcb_ref, zcc_ref, zch_ref, zqk_ref, zv_ref, zog_ref, zga_ref, zgb_ref,
                  zg_ref, gt_ref, wconv_ref, gain_ref,
                  conv0_ref, c0_ref, n0_ref, m0_ref,
                  mix_ref, conv_ref, c_ref, n_ref, m_ref, mo_ref):
    tt = zcb_ref.shape[0]

    @pl.when(pl.program_id(1) == 0)
    def _():
        conv_ref[...] = conv0_ref[...]
        c_ref[...] = c0_ref[...]
        n_ref[...] = n0_ref[...]
        m_ref[...] = m0_ref[...]

    u = zcc_ref[...] * zch_ref[...]
    carry = conv_ref[0]
    rows = lax.broadcasted_iota(jnp.int32, (tt, D_MODEL), 0)
    u1 = jnp.where(rows == 0, carry[1:2, :], pltpu.roll(u, 1, 0))
    u2 = jnp.where(rows == 0, carry[0:1, :],
                   jnp.where(rows == 1, carry[1:2, :], pltpu.roll(u, 2, 0)))
    wc = wconv_ref[...]
    conv_out = u2 * wc[0:1, :] + u1 * wc[1:2, :] + u * wc[2:3, :]
    conv_ref[0] = u[tt - 2:tt, :]
    a = zcb_ref[...] * conv_out

    zg = zg_ref[...]
    gt = gt_ref[0]
    for h in range(N_HEADS):
        ig_col_all = zg[:, h:h + 1] + bg_ref[h]
        lf_col_all = _log_sigmoid(zg[:, N_HEADS + h:N_HEADS + h + 1] + bg_ref[N_HEADS + h])
        ig_row_all = gt[h:h + 1, :] + bg_ref[h]
        lf_row_all = _log_sigmoid(gt[N_HEADS + h:N_HEADS + h + 1, :] + bg_ref[N_HEADS + h])
        for c in range(tt // chunk):
            r0, r1 = c * chunk, (c + 1) * chunk
            q = zqk_ref[r0:r1, h * HEAD_QK:(h + 1) * HEAD_QK]
            k = zqk_ref[r0:r1, N_HEADS * HEAD_QK + h * HEAD_QK:
                        N_HEADS * HEAD_QK + (h + 1) * HEAD_QK] * (HEAD_QK ** -0.5)
            v = zv_ref[r0:r1, h * HEAD_V:(h + 1) * HEAD_V]
            hh, c_new, n_new, m_new = _mlstm_chunk(
                q, k, v, ig_col_all[r0:r1, :], lf_col_all[r0:r1, :],
                ig_row_all[:, r0:r1], lf_row_all[:, r0:r1],
                c_ref[0, h], n_ref[0, h:h + 1, :], m_ref[0, :, h:h + 1])
            c_ref[0, h] = c_new
            n_ref[0, h:h + 1, :] = n_new
            m_ref[0, :, h:h + 1] = m_new
            mu = jnp.mean(hh, axis=-1, keepdims=True)
            hc = hh - mu
            var = jnp.mean(hc * hc, axis=-1, keepdims=True)
            hn = hc * lax.rsqrt(var + LN_EPS) * gain_ref[:, h * HEAD_V:(h + 1) * HEAD_V]
            mo_ref[r0:r1, h * HEAD_V:(h + 1) * HEAD_V] = (
                hn * _sigmoid(zog_ref[r0:r1, h * HEAD_V:(h + 1) * HEAD_V]))

    mix = _sigmoid(zga_ref[...]) * a + _sigmoid(zgb_ref[...]) * mo_ref[...]
    mix_ref[...] = mix.astype(mix_ref.dtype)


def _mixer(z, zg, gt3, b_gates, w_conv, mh_gain, conv0, c0, n0, m0, *, nb, nt, tt, chunk,
           row_blk0):
    def zspec(j):
        return pl.BlockSpec((tt, D_MODEL), lambda b, t, j=j: (row_blk0 + b * nt + t, j))

    def rowspec(width):
        return pl.BlockSpec((tt, width), lambda b, t: (row_blk0 + b * nt + t, 0))

    in_specs = [pl.BlockSpec(memory_space=pltpu.SMEM)]
    in_specs += [zspec(j) for j in range(8)]
    in_specs += [
        rowspec(LANES),
        pl.BlockSpec((1, SUBLANES, tt), lambda b, t: (row_blk0 + b * nt + t, 0, 0)),
        pl.BlockSpec((CONV_K, D_MODEL), lambda b, t: (0, 0)),
        pl.BlockSpec((1, D_MODEL), lambda b, t: (0, 0)),
        pl.BlockSpec((1, CONV_K - 1, D_MODEL), lambda b, t: (b, 0, 0)),
        pl.BlockSpec((1, N_HEADS, HEAD_QK, HEAD_V), lambda b, t: (b, 0, 0, 0)),
        pl.BlockSpec((1, N_HEADS, HEAD_QK), lambda b, t: (b, 0, 0)),
        pl.BlockSpec((1, 1, N_HEADS), lambda b, t: (b, 0, 0)),
    ]
    out_specs = [
        pl.BlockSpec((tt, D_MODEL), lambda b, t: (b * nt + t, 0)),
        pl.BlockSpec((1, CONV_K - 1, D_MODEL), lambda b, t: (b, 0, 0)),
        pl.BlockSpec((1, N_HEADS, HEAD_QK, HEAD_V), lambda b, t: (b, 0, 0, 0)),
        pl.BlockSpec((1, N_HEADS, HEAD_QK), lambda b, t: (b, 0, 0)),
        pl.BlockSpec((1, 1, N_HEADS), lambda b, t: (b, 0, 0)),
    ]
    out_shape = [
        jax.ShapeDtypeStruct((nb * nt * tt, D_MODEL), F32),
        jax.ShapeDtypeStruct((nb, CONV_K - 1, D_MODEL), F32),
        jax.ShapeDtypeStruct((nb, N_HEADS, HEAD_QK, HEAD_V), F32),
        jax.ShapeDtypeStruct((nb, N_HEADS, HEAD_QK), F32),
        jax.ShapeDtypeStruct((nb, 1, N_HEADS), F32),
    ]
    return pl.pallas_call(
        functools.partial(_mixer_kernel, chunk),
        grid=(nb, nt),
        in_specs=in_specs,
        out_specs=out_specs,
        out_shape=out_shape,
        scratch_shapes=[pltpu.VMEM((tt, D_MODEL), F32)],
        compiler_params=_cparams(("arbitrary", "arbitrary")),
        name="mixer_t%d" % tt,
    )(b_gates, z, z, z, z, z, z, z, z, zg, gt3, w_conv, mh_gain, conv0, c0, n0, m0)


def _outproj_kernel(alpha, x_ref, mix_ref, w_ref, g_ref, b_ref, x1_ref, x1r_ref):
    tt = x_ref.shape[0]
    y = _dot(mix_ref[...].astype(BF16), w_ref[...])
    x1 = _layer_norm(alpha * x_ref[...] + y, g_ref[...], b_ref[...])
    x1_ref[...] = x1
    for s in range(ROW_TILES):
        x1r_ref[pl.ds(s, tt, stride=ROW_TILES), :] = x1[:, s * LANES:(s + 1) * LANES]


def _outproj(x, mix, w_out, g, b, alpha, tt):
    m = x.shape[0]
    return pl.pallas_call(
        functools.partial(_outproj_kernel, alpha),
        grid=(m // tt,),
        in_specs=[pl.BlockSpec((tt, D_MODEL), lambda i: (i, 0)),
                  pl.BlockSpec((tt, D_MODEL), lambda i: (i, 0)),
                  pl.BlockSpec((D_MODEL, D_MODEL), lambda i: (0, 0)),
                  pl.BlockSpec((1, D_MODEL), lambda i: (0, 0)),
                  pl.BlockSpec((1, D_MODEL), lambda i: (0, 0))],
        out_specs=[pl.BlockSpec((tt, D_MODEL), lambda i: (i, 0)),
                   pl.BlockSpec((tt * ROW_TILES, LANES), lambda i: (i, 0))],
        out_shape=[jax.ShapeDtypeStruct((m, D_MODEL), F32),
                   jax.ShapeDtypeStruct((m * ROW_TILES, LANES), F32)],
        compiler_params=_cparams(("parallel",)),
        name="outproj",
    )(x, mix, w_out, g, b)


def _router_kernel(x_ref, wh_ref, wl_ref, br_ref, idx_ref, gate_ref, rank_ref, cnt_ref):
    tt = x_ref.shape[0]

    @pl.when(pl.program_id(0) == 0)
    def _():
        cnt_ref[...] = jnp.zeros_like(cnt_ref)

    scores = _sigmoid(_dot3(x_ref[...], wh_ref[...], wl_ref[...]))
    lane = lax.broadcasted_iota(jnp.int32, (tt, N_EXPERTS), 1)
    work = scores + br_ref[...]
    picked = []
    sel = []
    member = jnp.zeros((tt, N_EXPERTS), F32)
    for _ in range(TOP_K):
        mx = jnp.max(work, axis=1, keepdims=True)
        ik = jnp.min(jnp.where(work == mx, lane, N_EXPERTS), axis=1, keepdims=True)
        onehot = lane == ik
        picked.append((ik, onehot))
        sel.append(jnp.sum(jnp.where(onehot, scores, 0.0), axis=1, keepdims=True))
        work = jnp.where(onehot, -jnp.inf, work)
        member = member + onehot.astype(F32)
    total = sel[0]
    for sk in sel[1:]:
        total = total + sk

    ti = lax.broadcasted_iota(jnp.int32, (tt, tt), 0)
    si = lax.broadcasted_iota(jnp.int32, (tt, tt), 1)
    earlier = (si < ti).astype(BF16)
    before = _dot(earlier, member.astype(BF16)) + cnt_ref[...]

    lane_o = lax.broadcasted_iota(jnp.int32, (tt, LANES), 1)
    idx_o = jnp.zeros((tt, LANES), jnp.int32)
    gate_o = jnp.zeros((tt, LANES), F32)
    rank_o = jnp.zeros((tt, LANES), jnp.int32)
    for kk in range(TOP_K):
        ik, onehot = picked[kk]
        rk = jnp.sum(jnp.where(onehot, before, 0.0), axis=1, keepdims=True)
        idx_o = jnp.where(lane_o == kk, ik, idx_o)
        gate_o = jnp.where(lane_o == kk, sel[kk] / total * ROUTED_SCALE, gate_o)
        rank_o = jnp.where(lane_o == kk, rk.astype(jnp.int32), rank_o)
    idx_ref[...] = idx_o
    gate_ref[...] = gate_o
    rank_ref[...] = rank_o
    cnt_ref[...] = cnt_ref[...] + jnp.sum(member, axis=0, keepdims=True)


def _router(x1, wr_hi, wr_lo, b_router, tt):
    m = x1.shape[0]
    return pl.pallas_call(
        _router_kernel,
        grid=(m // tt,),
        in_specs=[pl.BlockSpec((tt, D_MODEL), lambda i: (i, 0)),
                  pl.BlockSpec((D_MODEL, N_EXPERTS), lambda i: (0, 0)),
                  pl.BlockSpec((D_MODEL, N_EXPERTS), lambda i: (0, 0)),
                  pl.BlockSpec((1, N_EXPERTS), lambda i: (0, 0))],
        out_specs=[pl.BlockSpec((tt, LANES), lambda i: (i, 0)),
                   pl.BlockSpec((tt, LANES), lambda i: (i, 0)),
                   pl.BlockSpec((tt, LANES), lambda i: (i, 0)),
                   pl.BlockSpec((1, N_EXPERTS), lambda i: (0, 0))],
        out_shape=[jax.ShapeDtypeStruct((m, LANES), jnp.int32),
                   jax.ShapeDtypeStruct((m, LANES), F32),
                   jax.ShapeDtypeStruct((m, LANES), jnp.int32),
                   jax.ShapeDtypeStruct((1, N_EXPERTS), F32)],
        compiler_params=_cparams(("arbitrary",)),
        name="router",
    )(x1, wr_hi, wr_lo, b_router)


def _row_copy(src, src_row8, dst, dst_row8, sem):
    return pltpu.make_async_copy(src.at[pl.ds(pl.multiple_of(src_row8, SUBLANES), SUBLANES)],
                                 dst.at[pl.ds(pl.multiple_of(dst_row8, SUBLANES), SUBLANES)],
                                 sem)


def _dispatch_kernel(tt, dest_ref, x1r_ref, xg_in_ref, xg_ref, sem):
    del xg_in_ref
    base = pl.program_id(0) * tt

    def issue(t, carry):
        for kk in range(TOP_K):
            _row_copy(x1r_ref, (base + t) * ROW_TILES, xg_ref,
                      dest_ref[t * TOP_K + kk], sem).start()
        return carry

    lax.fori_loop(0, tt, issue, 0)
    n_rows = tt * TOP_K * ROW_TILES
    pltpu.make_async_copy(xg_ref.at[pl.ds(0, n_rows)], xg_ref.at[pl.ds(0, n_rows)], sem).wait()


def _dispatch(dest8, x1r, xg_zero, tt):
    m = x1r.shape[0] // ROW_TILES
    return pl.pallas_call(
        functools.partial(_dispatch_kernel, tt),
        grid=(m // tt,),
        in_specs=[pl.BlockSpec((tt * TOP_K,), lambda i: (i,), memory_space=pltpu.SMEM),
                  pl.BlockSpec(memory_space=pl.ANY),
                  pl.BlockSpec(memory_space=pl.ANY)],
        out_specs=pl.BlockSpec(memory_space=pl.ANY),
        out_shape=jax.ShapeDtypeStruct(xg_zero.shape, F32),
        scratch_shapes=[pltpu.SemaphoreType.DMA(())],
        input_output_aliases={2: 0},
        compiler_params=_cparams(("arbitrary",)),
        name="dispatch",
    )(dest8, x1r, xg_zero)


def _experts_kernel(rb, blk_ref, be_ref, nu_ref, xg_ref, wg_ref, wu_ref, wd_ref, y_ref):
    del blk_ref, be_ref

    @pl.when(pl.program_id(0) < nu_ref[0])
    def _():
        xb = jnp.concatenate(
            [xg_ref[pl.ds(s, rb, stride=ROW_TILES), :] for s in range(ROW_TILES)],
            axis=1).astype(BF16)
        g = _dot(xb, wg_ref[0].astype(BF16))
        u = _dot(xb, wu_ref[0].astype(BF16))
        hid = (g * _sigmoid(g) * u).astype(BF16)
        y = _dot(hid, wd_ref[0].astype(BF16))
        for s in range(ROW_TILES):
            y_ref[pl.ds(s, rb, stride=ROW_TILES), :] = y[:, s * LANES:(s + 1) * LANES]


def _experts(blk_map, block_e, n_used, xg, w_eg, w_eu, w_ed, rb):
    n_blocks = blk_map.shape[0]
    grid_spec = pltpu.PrefetchScalarGridSpec(
        num_scalar_prefetch=3,
        grid=(n_blocks,),
        in_specs=[
            pl.BlockSpec((rb * ROW_TILES, LANES), lambda i, blk, be, nu: (blk[i], 0)),
            pl.BlockSpec((1, D_MODEL, D_EXPERT), lambda i, blk, be, nu: (be[i], 0, 0)),
            pl.BlockSpec((1, D_MODEL, D_EXPERT), lambda i, blk, be, nu: (be[i], 0, 0)),
            pl.BlockSpec((1, D_EXPERT, D_MODEL), lambda i, blk, be, nu: (be[i], 0, 0)),
        ],
        out_specs=pl.BlockSpec((rb * ROW_TILES, LANES), lambda i, blk, be, nu: (blk[i], 0)),
    )
    return pl.pallas_call(
        functools.partial(_experts_kernel, rb),
        grid_spec=grid_spec,
        out_shape=jax.ShapeDtypeStruct(xg.shape, F32),
        input_output_aliases={3: 0},
        compiler_params=_cparams(("arbitrary",)),
        name="experts",
    )(blk_map, block_e, n_used, xg, w_eg, w_eu, w_ed)


def _combine_kernel(alpha, dest_ref, gate_ref, x1_ref, y_ref, wsg_ref, wsu_ref, wsd_ref,
                    g_ref, b_ref, x2_ref, ybuf, sem):
    tt = x1_ref.shape[0]

    def issue(t, carry):
        for kk in range(TOP_K):
            _row_copy(y_ref, dest_ref[t * TOP_K + kk], ybuf,
                      (kk * tt + t) * ROW_TILES, sem).start()
        return carry

    lax.fori_loop(0, tt, issue, 0)

    x1 = x1_ref[...]
    xb = x1.astype(BF16)
    gs = _dot(xb, wsg_ref[...])
    us = _dot(xb, wsu_ref[...])
    shared = _dot((gs * _sigmoid(gs) * us).astype(BF16), wsd_ref[...])

    n_rows = TOP_K * tt * ROW_TILES
    pltpu.make_async_copy(y_ref.at[pl.ds(0, n_rows)], ybuf, sem).wait()

    gates = gate_ref[...]
    chunks = []
    for s in range(ROW_TILES):
        acc = jnp.zeros((tt, LANES), F32)
        for kk in range(TOP_K):
            acc = acc + gates[:, kk:kk + 1] * ybuf[pl.ds(kk * tt * ROW_TILES + s, tt,
                                                         stride=ROW_TILES), :]
        chunks.append(acc)
    routed = jnp.concatenate(chunks, axis=1)
    x2_ref[...] = _layer_norm(alpha * x1 + (shared + routed), g_ref[...], b_ref[...])


def _combine(dest8, gates, x1, y, wsg, wsu, wsd, g, b, alpha, tt):
    m = x1.shape[0]
    return pl.pallas_call(
        functools.partial(_combine_kernel, alpha),
        grid=(m // tt,),
        in_specs=[pl.BlockSpec((tt * TOP_K,), lambda i: (i,), memory_space=pltpu.SMEM),
                  pl.BlockSpec((tt, LANES), lambda i: (i, 0)),
                  pl.BlockSpec((tt, D_MODEL), lambda i: (i, 0)),
                  pl.BlockSpec(memory_space=pl.ANY),
                  pl.BlockSpec((D_MODEL, D_EXPERT), lambda i: (0, 0)),
                  pl.BlockSpec((D_MODEL, D_EXPERT), lambda i: (0, 0)),
                  pl.BlockSpec((D_EXPERT, D_MODEL), lambda i: (0, 0)),
                  pl.BlockSpec((1, D_MODEL), lambda i: (0, 0)),
                  pl.BlockSpec((1, D_MODEL), lambda i: (0, 0))],
        out_specs=pl.BlockSpec((tt, D_MODEL), lambda i: (i, 0)),
        out_shape=jax.ShapeDtypeStruct((m, D_MODEL), F32),
        scratch_shapes=[pltpu.VMEM((TOP_K * tt * ROW_TILES, LANES), F32),
                        pltpu.SemaphoreType.DMA(())],
        compiler_params=_cparams(("arbitrary",)),
        name="combine",
    )(dest8, gates, x1, y, wsg, wsu, wsd, g, b)


PROJ_TM = 1024
PROJ_TN = 512
MIX_TT = 256
MIX_CHUNK = 128
OUT_TT = 512
ROUTER_TT = 256
DISPATCH_TT = 512
EXPERT_RB = 256
COMBINE_TT = 128


def _moe(x1, x1r, w_router, b_router, w_eg, w_eu, w_ed, w_sg, w_su, w_sd, ln_g, ln_b, alpha):
    m = x1.shape[0]
    wr_hi, wr_lo = _split_hi_lo(w_router)
    idx, gates, rank, cnt = _router(x1, wr_hi, wr_lo, b_router[None, :], ROUTER_TT)

    rb = EXPERT_RB
    n_rows = m * TOP_K
    n_blocks = (n_rows + N_EXPERTS * (rb - 1) + rb - 1) // rb
    counts = cnt[0].astype(jnp.int32)
    padded = (counts + rb - 1) // rb * rb
    pad_end = jnp.cumsum(padded)
    pad_start = pad_end - padded
    dest = pad_start[idx[:, :TOP_K]] + rank[:, :TOP_K]
    dest8 = (dest * ROW_TILES).reshape(-1)
    n_used = pad_end[-1] // rb
    blk_map = jnp.minimum(jnp.arange(n_blocks, dtype=jnp.int32), n_used - 1)
    block_e = jnp.minimum(
        jnp.sum((pad_end[None, :] <= (blk_map * rb)[:, None]).astype(jnp.int32), axis=1),
        N_EXPERTS - 1)

    xg = _dispatch(dest8, x1r, jnp.zeros((n_blocks * rb * ROW_TILES, LANES), F32), DISPATCH_TT)
    y = _experts(blk_map, block_e, n_used.reshape(1).astype(jnp.int32), xg,
                 w_eg, w_eu, w_ed, rb)
    return _combine(dest8, gates, x1, y, w_sg.astype(BF16), w_su.astype(BF16),
                    w_sd.astype(BF16), ln_g[None, :], ln_b[None, :], alpha, COMBINE_TT)


def _layer(xp, xs, conv_s, c_s, n_s, m_s, w_in, b_gates, w_conv, mh_gain, w_out, ln1_g, ln1_b,
           w_router, b_router, w_eg, w_eu, w_ed, w_sg, w_su, w_sd, ln2_g, ln2_b, alpha):
    bp, tp, _ = xp.shape
    bs, ts, _ = xs.shape
    mp, ms = bp * tp, bs * ts
    x = jnp.concatenate([xp.reshape(mp, D_MODEL), xs.reshape(ms, D_MODEL)], axis=0)

    w_main = jnp.concatenate([w_in[:, :GATE_COL0], w_in[:, GATE_COL0 + 2 * N_HEADS:]],
                             axis=1).astype(BF16)
    w_gate = jnp.pad(w_in[:, GATE_COL0:GATE_COL0 + 2 * N_HEADS],
                     ((0, 0), (0, LANES - 2 * N_HEADS)))
    wg_hi, wg_lo = _split_hi_lo(w_gate)

    z = _proj(x, w_main, PROJ_TM, PROJ_TN)
    zg = _gate_proj(x, wg_hi, wg_lo, PROJ_TM)

    def gates_t(zg_rows, tt):
        return zg_rows[:, :SUBLANES].reshape(-1, tt, SUBLANES).transpose(0, 2, 1)

    mix_p, conv_p, c_p, n_p, m_p = _mixer(
        z, zg, gates_t(zg, MIX_TT), b_gates, w_conv, mh_gain[None, :],
        jnp.zeros((bp, CONV_K - 1, D_MODEL), F32),
        jnp.zeros((bp, N_HEADS, HEAD_QK, HEAD_V), F32),
        jnp.zeros((bp, N_HEADS, HEAD_QK), F32),
        jnp.zeros((bp, 1, N_HEADS), F32),
        nb=bp, nt=tp // MIX_TT, tt=MIX_TT, chunk=MIX_CHUNK, row_blk0=0)
    mix_s, conv_n, c_n, n_n, m_n = _mixer(
        z, zg, gates_t(zg, ts), b_gates, w_conv, mh_gain[None, :],
        conv_s, c_s, n_s, m_s[:, None, :],
        nb=bs, nt=1, tt=ts, chunk=ts, row_blk0=mp // ts)
    mix = jnp.concatenate([mix_p, mix_s], axis=0)

    x1, x1r = _outproj(x, mix, w_out.astype(BF16), ln1_g[None, :], ln1_b[None, :], alpha, OUT_TT)
    x2 = _moe(x1, x1r, w_router, b_router, w_eg, w_eu, w_ed, w_sg, w_su, w_sd,
              ln2_g, ln2_b, alpha)
    states_p = (conv_p, c_p, n_p, m_p[:, 0, :])
    states_s = (conv_n, c_n, n_n, m_n[:, 0, :])
    return x2[:mp].reshape(bp, tp, D_MODEL), x2[mp:].reshape(bs, ts, D_MODEL), states_p, states_s


def kernel(x_prompt, x_sample, cache_conv, state_mlstm_C, state_mlstm_n, state_mlstm_m, w_in, b_gates, w_conv, mh_gain, w_out, ln1_g, ln1_b, w_router, b_router, w_exp_gate, w_exp_up, w_exp_down, w_sh_gate, w_sh_up, w_sh_down, ln2_g, ln2_b):
    depth = w_in.shape[0]
    alpha = (2.0 * depth) ** 0.25
    hp, hs = x_prompt, x_sample
    outs_p = [[], [], [], []]
    outs_s = [[], [], [], []]
    for l in range(depth):
        hp, hs, st_p, st_s = _layer(
            hp, hs, cache_conv[l], state_mlstm_C[l], state_mlstm_n[l], state_mlstm_m[l],
            w_in[l], b_gates[l], w_conv[l], mh_gain[l], w_out[l], ln1_g[l], ln1_b[l],
            w_router[l], b_router[l], w_exp_gate[l], w_exp_up[l], w_exp_down[l],
            w_sh_gate[l], w_sh_up[l], w_sh_down[l], ln2_g[l], ln2_b[l], alpha)
        for acc, val in zip(outs_p, st_p):
            acc.append(val)
        for acc, val in zip(outs_s, st_s):
            acc.append(val)
    return (hp, hs) + tuple(jnp.stack(a) for a in outs_p) + tuple(jnp.stack(a) for a in outs_s)
```

```python
import functools

import jax
import jax.numpy as jnp
from jax import lax
from jax.experimental import pallas as pl
from jax.experimental.pallas import tpu as pltpu

F32 = jnp.float32
BF16 = jnp.bfloat16

D_MODEL = 1024
N_HEADS = 4
HEAD_V = 256
HEAD_QK = 128
N_EXPERTS = 256
TOP_K = 8
D_EXPERT = 256
ROUTED_SCALE = 2.5
LN_EPS = 1e-5
CONV_K = 3

LANES = 128
SUBLANES = 8
ROW_TILES = D_MODEL // LANES
Z_WIDTH = 8 * D_MODEL
GATE_COL0 = 6 * D_MODEL

VMEM_LIMIT = 56 * 1024 * 1024


def _cparams(sem):
    return pltpu.CompilerParams(dimension_semantics=sem, vmem_limit_bytes=VMEM_LIMIT)


def _sigmoid(x):
    return 1.0 / (1.0 + jnp.exp(-x))


def _log_sigmoid(x):
    return jnp.minimum(x, 0.0) - jnp.log(1.0 + jnp.exp(-jnp.abs(x)))


def _layer_norm(x, g, b):
    mu = jnp.mean(x, axis=-1, keepdims=True)
    xc = x - mu
    var = jnp.mean(xc * xc, axis=-1, keepdims=True)
    return xc * lax.rsqrt(var + LN_EPS) * g + b


def _split_hi_lo(x):
    hi = x.astype(BF16)
    lo = (x - hi.astype(F32)).astype(BF16)
    return hi, lo


def _dot(a, b):
    return jnp.dot(a, b, preferred_element_type=F32)


def _dot3(x, w_hi, w_lo):
    x_hi, x_lo = _split_hi_lo(x)
    return _dot(x_hi, w_hi) + _dot(x_lo, w_hi) + _dot(x_hi, w_lo)


def _proj_kernel(x_ref, w_ref, z_ref):
    z_ref[...] = _dot(x_ref[...].astype(BF16), w_ref[...])


def _proj(x, w, tm, tn):
    m, k = x.shape
    n = w.shape[1]
    return pl.pallas_call(
        _proj_kernel,
        grid=(m // tm, n // tn),
        in_specs=[pl.BlockSpec((tm, k), lambda i, j: (i, 0)),
                  pl.BlockSpec((k, tn), lambda i, j: (0, j))],
        out_specs=pl.BlockSpec((tm, tn), lambda i, j: (i, j)),
        out_shape=jax.ShapeDtypeStruct((m, n), F32),
        compiler_params=_cparams(("parallel", "arbitrary")),
        name="proj",
    )(x, w)


def _gate_proj_kernel(x_ref, wh_ref, wl_ref, z_ref):
    z_ref[...] = _dot3(x_ref[...], wh_ref[...], wl_ref[...])


def _gate_proj(x, w_hi, w_lo, tm):
    m, k = x.shape
    n = w_hi.shape[1]
    return pl.pallas_call(
        _gate_proj_kernel,
        grid=(m // tm,),
        in_specs=[pl.BlockSpec((tm, k), lambda i: (i, 0)),
                  pl.BlockSpec((k, n), lambda i: (0, 0)),
                  pl.BlockSpec((k, n), lambda i: (0, 0))],
        out_specs=pl.BlockSpec((tm, n), lambda i: (i, 0)),
        out_shape=jax.ShapeDtypeStruct((m, n), F32),
        compiler_params=_cparams(("parallel",)),
        name="gate_proj",
    )(x, w_hi, w_lo)


def _mlstm_chunk(q, k, v, ig_col, lf_col, ig_row, lf_row, c_state, n_row, m_prev):
    chunk = q.shape[0]
    ti = lax.broadcasted_iota(jnp.int32, (chunk, chunk), 0)
    si = lax.broadcasted_iota(jnp.int32, (chunk, chunk), 1)
    causal = si <= ti
    b_col = jnp.sum(jnp.where(causal, lf_row, 0.0), axis=1, keepdims=True)
    b_row = jnp.sum(jnp.where(ti <= si, lf_col, 0.0), axis=0, keepdims=True)
    inter = b_col + m_prev
    dmat = jnp.where(causal, b_col - b_row + ig_row, -jnp.inf)
    m_t = jnp.maximum(inter, jnp.max(dmat, axis=1, keepdims=True))
    w_intra = jnp.exp(dmat - m_t)
    w_inter = jnp.exp(inter - m_t)
    qb = q.astype(BF16)
    kb = k.astype(BF16)
    vb = v.astype(BF16)
    s = lax.dot_general(qb, kb, (((1,), (1,)), ((), ())), preferred_element_type=F32) * w_intra
    num = w_inter * _dot(qb, c_state.astype(BF16)) + _dot(s.astype(BF16), vb)
    den = (w_inter * jnp.sum(q * n_row, axis=1, keepdims=True)
           + jnp.sum(s, axis=1, keepdims=True))
    h = num / jnp.maximum(jnp.abs(den), jnp.exp(-m_t))
    m_new = m_t[chunk - 1:chunk, :]
    b_last = b_col[chunk - 1:chunk, :]
    ws_col = jnp.exp(b_last - b_col + ig_col - m_new)
    cdecay = jnp.exp(inter[chunk - 1:chunk, :] - m_new)
    kw = k * ws_col
    c_new = cdecay * c_state + lax.dot_general(
        kw.astype(BF16), vb, (((0,), (0,)), ((), ())), preferred_element_type=F32)
    n_new = cdecay * n_row + jnp.sum(kw, axis=0, keepdims=True)
    return h, c_new, n_new, m_new


def _mixer_kernel(chunk, bg_ref,
                  zcb_ref, zcc_ref, zch_ref, zqk_ref, zv_ref, zog_ref, zga_ref, zgb_ref,
                  zg_ref, gt_ref, wconv_ref, gain_ref,
                  conv0_ref, c0_ref, n0_ref, m0_ref,
                  mix_ref, conv_ref, c_ref, n_ref, m_ref, mo_ref):
    tt = zcb_ref.shape[0]

    @pl.when(pl.program_id(1) == 0)
    def _():
        conv_ref[...] = conv0_ref[...]
        c_ref[...] = c0_ref[...]
        n_ref[...] = n0_ref[...]
        m_ref[...] = m0_ref[...]

    u = zcc_ref[...] * zch_ref[...]
    carry = conv_ref[0]
    rows = lax.broadcasted_iota(jnp.int32, (tt, D_MODEL), 0)
    u1 = jnp.where(rows == 0, carry[1:2, :], pltpu.roll(u, 1, 0))
    u2 = jnp.where(rows == 0, carry[0:1, :],
                   jnp.where(rows == 1, carry[1:2, :], pltpu.roll(u, 2, 0)))
    wc = wconv_ref[...]
    conv_out = u2 * wc[0:1, :] + u1 * wc[1:2, :] + u * wc[2:3, :]
    conv_ref[0] = u[tt - 2:tt, :]
    a = zcb_ref[...] * conv_out

    zg = zg_ref[...]
    gt = gt_ref[0]
    for h in range(N_HEADS):
        ig_col_all = zg[:, h:h + 1] + bg_ref[h]
        lf_col_all = _log_sigmoid(zg[:, N_HEADS + h:N_HEADS + h + 1] + bg_ref[N_HEADS + h])
        ig_row_all = gt[h:h + 1, :] + bg_ref[h]
        lf_row_all = _log_sigmoid(gt[N_HEADS + h:N_HEADS + h + 1, :] + bg_ref[N_HEADS + h])
        for c in range(tt // chunk):
            r0, r1 = c * chunk, (c + 1) * chunk
            q = zqk_ref[r0:r1, h * HEAD_QK:(h + 1) * HEAD_QK]
            k = zqk_ref[r0:r1, N_HEADS * HEAD_QK + h * HEAD_QK:
                        N_HEADS * HEAD_QK + (h + 1) * HEAD_QK] * (HEAD_QK ** -0.5)
            v = zv_ref[r0:r1, h * HEAD_V:(h + 1) * HEAD_V]
            hh, c_new, n_new, m_new = _mlstm_chunk(
                q, k, v, ig_col_all[r0:r1, :], lf_col_all[r0:r1, :],
                ig_row_all[:, r0:r1], lf_row_all[:, r0:r1],
                c_ref[0, h], n_ref[0, h:h + 1, :], m_ref[0, :, h:h + 1])
            c_ref[0, h] = c_new
            n_ref[0, h:h + 1, :] = n_new
            m_ref[0, :, h:h + 1] = m_new
            mu = jnp.mean(hh, axis=-1, keepdims=True)
            hc = hh - mu
            var = jnp.mean(hc * hc, axis=-1, keepdims=True)
            hn = hc * lax.rsqrt(var + LN_EPS) * gain_ref[:, h * HEAD_V:(h + 1) * HEAD_V]
            mo_ref[r0:r1, h * HEAD_V:(h + 1) * HEAD_V] = (
                hn * _sigmoid(zog_ref[r0:r1, h * HEAD_V:(h + 1) * HEAD_V]))

    mix = _sigmoid(zga_ref[...]) * a + _sigmoid(zgb_ref[...]) * mo_ref[...]
    mix_ref[...] = mix.astype(mix_ref.dtype)


def _mixer(z, zg, gt3, b_gates, w_conv, mh_gain, conv0, c0, n0, m0, *, nb, nt, tt, chunk,
           row_blk0):
    def zspec(j):
        return pl.BlockSpec((tt, D_MODEL), lambda b, t, j=j: (row_blk0 + b * nt + t, j))

    def rowspec(width):
        return pl.BlockSpec((tt, width), lambda b, t: (row_blk0 + b * nt + t, 0))

    in_specs = [pl.BlockSpec(memory_space=pltpu.SMEM)]
    in_specs += [zspec(j) for j in range(8)]
    in_specs += [
        rowspec(LANES),
        pl.BlockSpec((1, SUBLANES, tt), lambda b, t: (row_blk0 + b * nt + t, 0, 0)),
        pl.BlockSpec((CONV_K, D_MODEL), lambda b, t: (0, 0)),
        pl.BlockSpec((1, D_MODEL), lambda b, t: (0, 0)),
        pl.BlockSpec((1, CONV_K - 1, D_MODEL), lambda b, t: (b, 0, 0)),
        pl.BlockSpec((1, N_HEADS, HEAD_QK, HEAD_V), lambda b, t: (b, 0, 0, 0)),
        pl.BlockSpec((1, N_HEADS, HEAD_QK), lambda b, t: (b, 0, 0)),
        pl.BlockSpec((1, 1, N_HEADS), lambda b, t: (b, 0, 0)),
    ]
    out_specs = [
        pl.BlockSpec((tt, D_MODEL), lambda b, t: (b * nt + t, 0)),
        pl.BlockSpec((1, CONV_K - 1, D_MODEL), lambda b, t: (b, 0, 0)),
        pl.BlockSpec((1, N_HEADS, HEAD_QK, HEAD_V), lambda b, t: (b, 0, 0, 0)),
        pl.BlockSpec((1, N_HEADS, HEAD_QK), lambda b, t: (b, 0, 0)),
        pl.BlockSpec((1, 1, N_HEADS), lambda b, t: (b, 0, 0)),
    ]
    out_shape = [
        jax.ShapeDtypeStruct((nb * nt * tt, D_MODEL), F32),
        jax.ShapeDtypeStruct((nb, CONV_K - 1, D_MODEL), F32),
        jax.ShapeDtypeStruct((nb, N_HEADS, HEAD_QK, HEAD_V), F32),
        jax.ShapeDtypeStruct((nb, N_HEADS, HEAD_QK), F32),
        jax.ShapeDtypeStruct((nb, 1, N_HEADS), F32),
    ]
    return pl.pallas_call(
        functools.partial(_mixer_kernel, chunk),
        grid=(nb, nt),
        in_specs=in_specs,
        out_specs=out_specs,
        out_shape=out_shape,
        scratch_shapes=[pltpu.VMEM((tt, D_MODEL), F32)],
        compiler_params=_cparams(("arbitrary", "arbitrary")),
        name="mixer_t%d" % tt,
    )(b_gates, z, z, z, z, z, z, z, z, zg, gt3, w_conv, mh_gain, conv0, c0, n0, m0)


def _outproj_kernel(alpha, x_ref, mix_ref, w_ref, g_ref, b_ref, x1_ref, x1r_ref):
    tt = x_ref.shape[0]
    y = _dot(mix_ref[...].astype(BF16), w_ref[...])
    x1 = _layer_norm(alpha * x_ref[...] + y, g_ref[...], b_ref[...])
    x1_ref[...] = x1
    for s in range(ROW_TILES):
        x1r_ref[pl.ds(s, tt, stride=ROW_TILES), :] = x1[:, s * LANES:(s + 1) * LANES]


def _outproj(x, mix, w_out, g, b, alpha, tt):
    m = x.shape[0]
    return pl.pallas_call(
        functools.partial(_outproj_kernel, alpha),
        grid=(m // tt,),
        in_specs=[pl.BlockSpec((tt, D_MODEL), lambda i: (i, 0)),
                  pl.BlockSpec((tt, D_MODEL), lambda i: (i, 0)),
                  pl.BlockSpec((D_MODEL, D_MODEL), lambda i: (0, 0)),
                  pl.BlockSpec((1, D_MODEL), lambda i: (0, 0)),
                  pl.BlockSpec((1, D_MODEL), lambda i: (0, 0))],
        out_specs=[pl.BlockSpec((tt, D_MODEL), lambda i: (i, 0)),
                   pl.BlockSpec((tt * ROW_TILES, LANES), lambda i: (i, 0))],
        out_shape=[jax.ShapeDtypeStruct((m, D_MODEL), F32),
                   jax.ShapeDtypeStruct((m * ROW_TILES, LANES), F32)],
        compiler_params=_cparams(("parallel",)),
        name="outproj",
    )(x, mix, w_out, g, b)


def _router_kernel(x_ref, wh_ref, wl_ref, br_ref, idx_ref, gate_ref, rank_ref, cnt_ref):
    tt = x_ref.shape[0]

    @pl.when(pl.program_id(0) == 0)
    def _():
        cnt_ref[...] = jnp.zeros_like(cnt_ref)

    scores = _sigmoid(_dot3(x_ref[...], wh_ref[...], wl_ref[...]))
    lane = lax.broadcasted_iota(jnp.int32, (tt, N_EXPERTS), 1)
    work = scores + br_ref[...]
    picked = []
    sel = []
    member = jnp.zeros((tt, N_EXPERTS), F32)
    for _ in range(TOP_K):
        mx = jnp.max(work, axis=1, keepdims=True)
        ik = jnp.min(jnp.where(work == mx, lane, N_EXPERTS), axis=1, keepdims=True)
        onehot = lane == ik
        picked.append((ik, onehot))
        sel.append(jnp.sum(jnp.where(onehot, scores, 0.0), axis=1, keepdims=True))
        work = jnp.where(onehot, -jnp.inf, work)
        member = member + onehot.astype(F32)
    total = sel[0]
    for sk in sel[1:]:
        total = total + sk

    ti = lax.broadcasted_iota(jnp.int32, (tt, tt), 0)
    si = lax.broadcasted_iota(jnp.int32, (tt, tt), 1)
    earlier = (si < ti).astype(BF16)
    before = _dot(earlier, member.astype(BF16)) + cnt_ref[...]

    lane_o = lax.broadcasted_iota(jnp.int32, (tt, LANES), 1)
    idx_o = jnp.zeros((tt, LANES), jnp.int32)
    gate_o = jnp.zeros((tt, LANES), F32)
    rank_o = jnp.zeros((tt, LANES), jnp.int32)
    for kk in range(TOP_K):
        ik, onehot = picked[kk]
        rk = jnp.sum(jnp.where(onehot, before, 0.0), axis=1, keepdims=True)
        idx_o = jnp.where(lane_o == kk, ik, idx_o)
        gate_o = jnp.where(lane_o == kk, sel[kk] / total * ROUTED_SCALE, gate_o)
        rank_o = jnp.where(lane_o == kk, rk.astype(jnp.int32), rank_o)
    idx_ref[...] = idx_o
    gate_ref[...] = gate_o
    rank_ref[...] = rank_o
    cnt_ref[...] = cnt_ref[...] + jnp.sum(member, axis=0, keepdims=True)


def _router(x1, wr_hi, wr_lo, b_router, tt):
    m = x1.shape[0]
    return pl.pallas_call(
        _router_kernel,
        grid=(m // tt,),
        in_specs=[pl.BlockSpec((tt, D_MODEL), lambda i: (i, 0)),
                  pl.BlockSpec((D_MODEL, N_EXPERTS), lambda i: (0, 0)),
                  pl.BlockSpec((D_MODEL, N_EXPERTS), lambda i: (0, 0)),
                  pl.BlockSpec((1, N_EXPERTS), lambda i: (0, 0))],
        out_specs=[pl.BlockSpec((tt, LANES), lambda i: (i, 0)),
                   pl.BlockSpec((tt, LANES), lambda i: (i, 0)),
                   pl.BlockSpec((tt, LANES), lambda i: (i, 0)),
                   pl.BlockSpec((1, N_EXPERTS), lambda i: (0, 0))],
        out_shape=[jax.ShapeDtypeStruct((m, LANES), jnp.int32),
                   jax.ShapeDtypeStruct((m, LANES), F32),
                   jax.ShapeDtypeStruct((m, LANES), jnp.int32),
                   jax.ShapeDtypeStruct((1, N_EXPERTS), F32)],
        compiler_params=_cparams(("arbitrary",)),
        name="router",
    )(x1, wr_hi, wr_lo, b_router)


def _row_copy(src, src_row8, dst, dst_row8, sem):
    return pltpu.make_async_copy(src.at[pl.ds(pl.multiple_of(src_row8, SUBLANES), SUBLANES)],
                                 dst.at[pl.ds(pl.multiple_of(dst_row8, SUBLANES), SUBLANES)],
                                 sem)


def _dispatch_kernel(tt, dest_ref, x1r_ref, xg_in_ref, xg_ref, sem):
    del xg_in_ref

    def issue(t, carry):
        for kk in range(TOP_K):
            _row_copy(x1r_ref, t * ROW_TILES, xg_ref, dest_ref[t * TOP_K + kk], sem).start()
        return carry

    lax.fori_loop(0, tt, issue, 0)
    n_rows = tt * TOP_K * ROW_TILES
    pltpu.make_async_copy(xg_ref.at[pl.ds(0, n_rows)], xg_ref.at[pl.ds(0, n_rows)], sem).wait()


def _dispatch(dest8, x1r, xg_zero, tt):
    m = x1r.shape[0] // ROW_TILES
    return pl.pallas_call(
        functools.partial(_dispatch_kernel, tt),
        grid=(m // tt,),
        in_specs=[pl.BlockSpec((tt * TOP_K,), lambda i: (i,), memory_space=pltpu.SMEM),
                  pl.BlockSpec((tt * ROW_TILES, LANES), lambda i: (i, 0)),
                  pl.BlockSpec(memory_space=pl.ANY)],
        out_specs=pl.BlockSpec(memory_space=pl.ANY),
        out_shape=jax.ShapeDtypeStruct(xg_zero.shape, F32),
        scratch_shapes=[pltpu.SemaphoreType.DMA(())],
        input_output_aliases={2: 0},
        compiler_params=_cparams(("arbitrary",)),
        name="dispatch",
    )(dest8, x1r, xg_zero)


def _experts_kernel(rb, blk_ref, be_ref, nu_ref, xg_ref, wg_ref, wu_ref, wd_ref, y_ref):
    del blk_ref, be_ref

    @pl.when(pl.program_id(0) < nu_ref[0])
    def _():
        xb = jnp.concatenate(
            [xg_ref[pl.ds(s, rb, stride=ROW_TILES), :] for s in range(ROW_TILES)],
            axis=1).astype(BF16)
        g = _dot(xb, wg_ref[0].astype(BF16))
        u = _dot(xb, wu_ref[0].astype(BF16))
        hid = (g * _sigmoid(g) * u).astype(BF16)
        y = _dot(hid, wd_ref[0].astype(BF16))
        for s in range(ROW_TILES):
            y_ref[pl.ds(s, rb, stride=ROW_TILES), :] = y[:, s * LANES:(s + 1) * LANES]


def _experts(blk_map, block_e, n_used, xg, w_eg, w_eu, w_ed, rb):
    n_blocks = blk_map.shape[0]
    grid_spec = pltpu.PrefetchScalarGridSpec(
        num_scalar_prefetch=3,
        grid=(n_blocks,),
        in_specs=[
            pl.BlockSpec((rb * ROW_TILES, LANES), lambda i, blk, be, nu: (blk[i], 0)),
            pl.BlockSpec((1, D_MODEL, D_EXPERT), lambda i, blk, be, nu: (be[i], 0, 0)),
            pl.BlockSpec((1, D_MODEL, D_EXPERT), lambda i, blk, be, nu: (be[i], 0, 0)),
            pl.BlockSpec((1, D_EXPERT, D_MODEL), lambda i, blk, be, nu: (be[i], 0, 0)),
        ],
        out_specs=pl.BlockSpec((rb * ROW_TILES, LANES), lambda i, blk, be, nu: (blk[i], 0)),
    )
    return pl.pallas_call(
        functools.partial(_experts_kernel, rb),
        grid_spec=grid_spec,
        out_shape=jax.ShapeDtypeStruct(xg.shape, F32),
        input_output_aliases={3: 0},
        compiler_params=_cparams(("arbitrary",)),
        name="experts",
    )(blk_map, block_e, n_used, xg, w_eg, w_eu, w_ed)


def _combine_kernel(alpha, dest_ref, gate_ref, x1_ref, y_ref, wsg_ref, wsu_ref, wsd_ref,
                    g_ref, b_ref, x2_ref, ybuf, sem):
    tt = x1_ref.shape[0]

    def issue(t, carry):
        for kk in range(TOP_K):
            _row_copy(y_ref, dest_ref[t * TOP_K + kk], ybuf,
                      (kk * tt + t) * ROW_TILES, sem).start()
        return carry

    lax.fori_loop(0, tt, issue, 0)

    x1 = x1_ref[...]
    xb = x1.astype(BF16)
    gs = _dot(xb, wsg_ref[...])
    us = _dot(xb, wsu_ref[...])
    shared = _dot((gs * _sigmoid(gs) * us).astype(BF16), wsd_ref[...])

    n_rows = TOP_K * tt * ROW_TILES
    pltpu.make_async_copy(y_ref.at[pl.ds(0, n_rows)], ybuf, sem).wait()

    gates = gate_ref[...]
    chunks = []
    for s in range(ROW_TILES):
        acc = jnp.zeros((tt, LANES), F32)
        for kk in range(TOP_K):
            acc = acc + gates[:, kk:kk + 1] * ybuf[pl.ds(kk * tt * ROW_TILES + s, tt,
                                                         stride=ROW_TILES), :]
        chunks.append(acc)
    routed = jnp.concatenate(chunks, axis=1)
    x2_ref[...] = _layer_norm(alpha * x1 + (shared + routed), g_ref[...], b_ref[...])


def _combine(dest8, gates, x1, y, wsg, wsu, wsd, g, b, alpha, tt):
    m = x1.shape[0]
    return pl.pallas_call(
        functools.partial(_combine_kernel, alpha),
        grid=(m // tt,),
        in_specs=[pl.BlockSpec((tt * TOP_K,), lambda i: (i,), memory_space=pltpu.SMEM),
                  pl.BlockSpec((tt, LANES), lambda i: (i, 0)),
                  pl.BlockSpec((tt, D_MODEL), lambda i: (i, 0)),
                  pl.BlockSpec(memory_space=pl.ANY),
                  pl.BlockSpec((D_MODEL, D_EXPERT), lambda i: (0, 0)),
                  pl.BlockSpec((D_MODEL, D_EXPERT), lambda i: (0, 0)),
                  pl.BlockSpec((D_EXPERT, D_MODEL), lambda i: (0, 0)),
                  pl.BlockSpec((1, D_MODEL), lambda i: (0, 0)),
                  pl.BlockSpec((1, D_MODEL), lambda i: (0, 0))],
        out_specs=pl.BlockSpec((tt, D_MODEL), lambda i: (i, 0)),
        out_shape=jax.ShapeDtypeStruct((m, D_MODEL), F32),
        scratch_shapes=[pltpu.VMEM((TOP_K * tt * ROW_TILES, LANES), F32),
                        pltpu.SemaphoreType.DMA(())],
        compiler_params=_cparams(("arbitrary",)),
        name="combine",
    )(dest8, gates, x1, y, wsg, wsu, wsd, g, b)


PROJ_TM = 1024
PROJ_TN = 512
MIX_TT = 256
MIX_CHUNK = 128
OUT_TT = 512
ROUTER_TT = 256
DISPATCH_TT = 512
EXPERT_RB = 256
COMBINE_TT = 128


def _moe(x1, x1r, w_router, b_router, w_eg, w_eu, w_ed, w_sg, w_su, w_sd, ln_g, ln_b, alpha):
    m = x1.shape[0]
    wr_hi, wr_lo = _split_hi_lo(w_router)
    idx, gates, rank, cnt = _router(x1, wr_hi, wr_lo, b_router[None, :], ROUTER_TT)

    rb = EXPERT_RB
    n_rows = m * TOP_K
    n_blocks = (n_rows + N_EXPERTS * (rb - 1) + rb - 1) // rb
    counts = cnt[0].astype(jnp.int32)
    padded = (counts + rb - 1) // rb * rb
    pad_end = jnp.cumsum(padded)
    pad_start = pad_end - padded
    expert_ids = jnp.arange(N_EXPERTS, dtype=jnp.int32)
    dest = rank[:, :TOP_K] + jnp.sum(
        jnp.where(idx[:, :TOP_K, None] == expert_ids, pad_start, 0), axis=-1)
    dest8 = (dest * ROW_TILES).reshape(-1)
    n_used = pad_end[-1] // rb
    blk_map = jnp.minimum(jnp.arange(n_blocks, dtype=jnp.int32), n_used - 1)
    block_e = jnp.minimum(
        jnp.sum((pad_end[None, :] <= (blk_map * rb)[:, None]).astype(jnp.int32), axis=1),
        N_EXPERTS - 1)

    xg = _dispatch(dest8, x1r, jnp.zeros((n_blocks * rb * ROW_TILES, LANES), F32), DISPATCH_TT)
    y = _experts(blk_map, block_e, n_used.reshape(1).astype(jnp.int32), xg,
                 w_eg, w_eu, w_ed, rb)
    return _combine(dest8, gates, x1, y, w_sg.astype(BF16), w_su.astype(BF16),
                    w_sd.astype(BF16), ln_g[None, :], ln_b[None, :], alpha, COMBINE_TT)


def _layer(xp, xs, conv_s, c_s, n_s, m_s, w_in, b_gates, w_conv, mh_gain, w_out, ln1_g, ln1_b,
           w_router, b_router, w_eg, w_eu, w_ed, w_sg, w_su, w_sd, ln2_g, ln2_b, alpha):
    bp, tp, _ = xp.shape
    bs, ts, _ = xs.shape
    mp, ms = bp * tp, bs * ts
    x = jnp.concatenate([xp.reshape(mp, D_MODEL), xs.reshape(ms, D_MODEL)], axis=0)

    w_main = jnp.concatenate([w_in[:, :GATE_COL0], w_in[:, GATE_COL0 + 2 * N_HEADS:]],
                             axis=1).astype(BF16)
    w_gate = jnp.pad(w_in[:, GATE_COL0:GATE_COL0 + 2 * N_HEADS],
                     ((0, 0), (0, LANES - 2 * N_HEADS)))
    wg_hi, wg_lo = _split_hi_lo(w_gate)

    z = _proj(x, w_main, PROJ_TM, PROJ_TN)
    zg = _gate_proj(x, wg_hi, wg_lo, PROJ_TM)

    def gates_t(zg_rows, tt):
        return zg_rows[:, :SUBLANES].reshape(-1, tt, SUBLANES).transpose(0, 2, 1)

    mix_p, conv_p, c_p, n_p, m_p = _mixer(
        z, zg, gates_t(zg, MIX_TT), b_gates, w_conv, mh_gain[None, :],
        jnp.zeros((bp, CONV_K - 1, D_MODEL), F32),
        jnp.zeros((bp, N_HEADS, HEAD_QK, HEAD_V), F32),
        jnp.zeros((bp, N_HEADS, HEAD_QK), F32),
        jnp.zeros((bp, 1, N_HEADS), F32),
        nb=bp, nt=tp // MIX_TT, tt=MIX_TT, chunk=MIX_CHUNK, row_blk0=0)
    mix_s, conv_n, c_n, n_n, m_n = _mixer(
        z, zg, gates_t(zg, ts), b_gates, w_conv, mh_gain[None, :],
        conv_s, c_s, n_s, m_s[:, None, :],
        nb=bs, nt=1, tt=ts, chunk=ts, row_blk0=mp // ts)
    mix = jnp.concatenate([mix_p, mix_s], axis=0)

    x1, x1r = _outproj(x, mix, w_out.astype(BF16), ln1_g[None, :], ln1_b[None, :], alpha, OUT_TT)
    x2 = _moe(x1, x1r, w_router, b_router, w_eg, w_eu, w_ed, w_sg, w_su, w_sd,
              ln2_g, ln2_b, alpha)
    states_p = (conv_p, c_p, n_p, m_p[:, 0, :])
    states_s = (conv_n, c_n, n_n, m_n[:, 0, :])
    return x2[:mp].reshape(bp, tp, D_MODEL), x2[mp:].reshape(bs, ts, D_MODEL), states_p, states_s


def kernel(x_prompt, x_sample, cache_conv, state_mlstm_C, state_mlstm_n, state_mlstm_m, w_in, b_gates, w_conv, mh_gain, w_out, ln1_g, ln1_b, w_router, b_router, w_exp_gate, w_exp_up, w_exp_down, w_sh_gate, w_sh_up, w_sh_down, ln2_g, ln2_b):
    depth = w_in.shape[0]
    alpha = (2.0 * depth) ** 0.25
    hp, hs = x_prompt, x_sample
    outs_p = [[], [], [], []]
    outs_s = [[], [], [], []]
    for l in range(depth):
        hp, hs, st_p, st_s = _layer(
            hp, hs, cache_conv[l], state_mlstm_C[l], state_mlstm_n[l], state_mlstm_m[l],
            w_in[l], b_gates[l], w_conv[l], mh_gain[l], w_out[l], ln1_g[l], ln1_b[l],
            w_router[l], b_router[l], w_exp_gate[l], w_exp_up[l], w_exp_down[l],
            w_sh_gate[l], w_sh_up[l], w_sh_down[l], ln2_g[l], ln2_b[l], alpha)
        for acc, val in zip(outs_p, st_p):
            acc.append(val)
        for acc, val in zip(outs_s, st_s):
            acc.append(val)
    return (hp, hs) + tuple(jnp.stack(a) for a in outs_p) + tuple(jnp.stack(a) for a in outs_s)
```

```python
import functools

import jax
import jax.numpy as jnp
from jax import lax
from jax.experimental import pallas as pl
from jax.experimental.pallas import tpu as pltpu

F32 = jnp.float32
BF16 = jnp.bfloat16

D_MODEL = 1024
N_HEADS = 4
HEAD_V = 256
HEAD_QK = 128
N_EXPERTS = 256
TOP_K = 8
D_EXPERT = 256
ROUTED_SCALE = 2.5
LN_EPS = 1e-5
CONV_K = 3

LANES = 128
SUBLANES = 8
ROW_TILES = D_MODEL // LANES
GATE_COL0 = 6 * D_MODEL

VMEM_LIMIT = 56 * 1024 * 1024


def _cparams(sem):
    return pltpu.CompilerParams(dimension_semantics=sem, vmem_limit_bytes=VMEM_LIMIT)


def _sigmoid(x):
    return 1.0 / (1.0 + jnp.exp(-x))


def _log_sigmoid(x):
    return jnp.minimum(x, 0.0) - jnp.log(1.0 + jnp.exp(-jnp.abs(x)))


def _layer_norm(x, g, b):
    mu = jnp.mean(x, axis=-1, keepdims=True)
    xc = x - mu
    var = jnp.mean(xc * xc, axis=-1, keepdims=True)
    return xc * lax.rsqrt(var + LN_EPS) * g + b


def _split_hi_lo(x):
    hi = x.astype(BF16)
    lo = (x - hi.astype(F32)).astype(BF16)
    return hi, lo


def _dot(a, b):
    return jnp.dot(a, b, preferred_element_type=F32)


def _dot3(x, w_hi, w_lo):
    x_hi, x_lo = _split_hi_lo(x)
    return _dot(x_hi, w_hi) + _dot(x_lo, w_hi) + _dot(x_hi, w_lo)


def _proj_kernel(x_ref, w_ref, z_ref):
    z_ref[...] = _dot(x_ref[...].astype(BF16), w_ref[...])


def _proj(x, w, tm, tn):
    m, k = x.shape
    n = w.shape[1]
    return pl.pallas_call(
        _proj_kernel,
        grid=(m // tm, n // tn),
        in_specs=[pl.BlockSpec((tm, k), lambda i, j: (i, 0)),
                  pl.BlockSpec((k, tn), lambda i, j: (0, j))],
        out_specs=pl.BlockSpec((tm, tn), lambda i, j: (i, j)),
        out_shape=jax.ShapeDtypeStruct((m, n), F32),
        compiler_params=_cparams(("parallel", "arbitrary")),
        name="proj",
    )(x, w)


def _gate_proj_kernel(x_ref, wh_ref, wl_ref, z_ref):
    z_ref[...] = _dot3(x_ref[...], wh_ref[...], wl_ref[...])


def _gate_proj(x, w_hi, w_lo, tm):
    m, k = x.shape
    n = w_hi.shape[1]
    return pl.pallas_call(
        _gate_proj_kernel,
        grid=(m // tm,),
        in_specs=[pl.BlockSpec((tm, k), lambda i: (i, 0)),
                  pl.BlockSpec((k, n), lambda i: (0, 0)),
                  pl.BlockSpec((k, n), lambda i: (0, 0))],
        out_specs=pl.BlockSpec((tm, n), lambda i: (i, 0)),
        out_shape=jax.ShapeDtypeStruct((m, n), F32),
        compiler_params=_cparams(("parallel",)),
        name="gate_proj",
    )(x, w_hi, w_lo)


def _conv_branch(u, zcb, wc, conv_ref):
    tt = u.shape[0]
    carry = conv_ref[0]
    rows = lax.broadcasted_iota(jnp.int32, (tt, D_MODEL), 0)
    u1 = jnp.where(rows == 0, carry[1:2, :], pltpu.roll(u, 1, 0))
    u2 = jnp.where(rows == 0, carry[0:1, :],
                   jnp.where(rows == 1, carry[1:2, :], pltpu.roll(u, 2, 0)))
    conv_out = u2 * wc[0:1, :] + u1 * wc[1:2, :] + u * wc[2:3, :]
    conv_ref[0] = u[tt - 2:tt, :]
    return zcb * conv_out


def _mlstm_chunk(q, k, v, ig_col, lf_col, ig_row, lf_row, c_state, n_row, m_prev):
    chunk = q.shape[0]
    ti = lax.broadcasted_iota(jnp.int32, (chunk, chunk), 0)
    si = lax.broadcasted_iota(jnp.int32, (chunk, chunk), 1)
    causal = si <= ti
    b_col = jnp.sum(jnp.where(causal, lf_row, 0.0), axis=1, keepdims=True)
    b_row = jnp.sum(jnp.where(ti <= si, lf_col, 0.0), axis=0, keepdims=True)
    inter = b_col + m_prev
    dmat = jnp.where(causal, b_col - b_row + ig_row, -jnp.inf)
    m_t = jnp.maximum(inter, jnp.max(dmat, axis=1, keepdims=True))
    w_intra = jnp.exp(dmat - m_t)
    w_inter = jnp.exp(inter - m_t)
    qb = q.astype(BF16)
    kb = k.astype(BF16)
    vb = v.astype(BF16)
    s = lax.dot_general(qb, kb, (((1,), (1,)), ((), ())), preferred_element_type=F32) * w_intra
    num = w_inter * _dot(qb, c_state.astype(BF16)) + _dot(s.astype(BF16), vb)
    den = (w_inter * jnp.sum(q * n_row, axis=1, keepdims=True)
           + jnp.sum(s, axis=1, keepdims=True))
    h = num / jnp.maximum(jnp.abs(den), jnp.exp(-m_t))
    m_new = m_t[chunk - 1:chunk, :]
    b_last = b_col[chunk - 1:chunk, :]
    ws_col = jnp.exp(b_last - b_col + ig_col - m_new)
    cdecay = jnp.exp(inter[chunk - 1:chunk, :] - m_new)
    kw = k * ws_col
    c_new = cdecay * c_state + lax.dot_general(
        kw.astype(BF16), vb, (((0,), (0,)), ((), ())), preferred_element_type=F32)
    n_new = cdecay * n_row + jnp.sum(kw, axis=0, keepdims=True)
    return h, c_new, n_new, m_new


def _mlstm_branch(chunk, bg_ref, zg, gt, zqk, zv, zog, gain_ref, c_ref, n_ref, m_ref, mo_ref):
    tt = zqk.shape[0]
    for h in range(N_HEADS):
        ig_col_all = zg[:, h:h + 1] + bg_ref[h]
        lf_col_all = _log_sigmoid(zg[:, N_HEADS + h:N_HEADS + h + 1] + bg_ref[N_HEADS + h])
        ig_row_all = gt[h:h + 1, :] + bg_ref[h]
        lf_row_all = _log_sigmoid(gt[N_HEADS + h:N_HEADS + h + 1, :] + bg_ref[N_HEADS + h])
        vcols = slice(h * HEAD_V, (h + 1) * HEAD_V)
        for c in range(tt // chunk):
            r0, r1 = c * chunk, (c + 1) * chunk
            q = zqk[r0:r1, h * HEAD_QK:(h + 1) * HEAD_QK]
            k = zqk[r0:r1, (N_HEADS + h) * HEAD_QK:(N_HEADS + h + 1) * HEAD_QK] * (HEAD_QK ** -0.5)
            hh, c_new, n_new, m_new = _mlstm_chunk(
                q, k, zv[r0:r1, vcols], ig_col_all[r0:r1, :], lf_col_all[r0:r1, :],
                ig_row_all[:, r0:r1], lf_row_all[:, r0:r1],
                c_ref[0, h], n_ref[0, h:h + 1, :], m_ref[0, :, h:h + 1])
            c_ref[0, h] = c_new
            n_ref[0, h:h + 1, :] = n_new
            m_ref[0, :, h:h + 1] = m_new
            mu = jnp.mean(hh, axis=-1, keepdims=True)
            hc = hh - mu
            var = jnp.mean(hc * hc, axis=-1, keepdims=True)
            hn = hc * lax.rsqrt(var + LN_EPS) * gain_ref[:, vcols]
            mo_ref[r0:r1, vcols] = hn * _sigmoid(zog[r0:r1, vcols])


def _store_x1(x, mix, wout_ref, g_ref, b_ref, alpha, x1_ref, x1r_ref):
    tt = x.shape[0]
    x1 = _layer_norm(alpha * x + _dot(mix.astype(BF16), wout_ref[...]), g_ref[...], b_ref[...])
    x1_ref[...] = x1
    for s in range(ROW_TILES):
        x1r_ref[pl.ds(s, tt, stride=ROW_TILES), :] = x1[:, s * LANES:(s + 1) * LANES]


def _mixer_kernel(chunk, bg_ref,
                  zcb_ref, zcc_ref, zch_ref, zqk_ref, zv_ref, zog_ref, zga_ref, zgb_ref,
                  zg_ref, gt_ref, wconv_ref, gain_ref,
                  conv0_ref, c0_ref, n0_ref, m0_ref,
                  mix_ref, conv_ref, c_ref, n_ref, m_ref, mo_ref):
    @pl.when(pl.program_id(1) == 0)
    def _():
        conv_ref[...] = conv0_ref[...]
        c_ref[...] = c0_ref[...]
        n_ref[...] = n0_ref[...]
        m_ref[...] = m0_ref[...]

    a = _conv_branch(zcc_ref[...] * zch_ref[...], zcb_ref[...], wconv_ref[...], conv_ref)
    _mlstm_branch(chunk, bg_ref, zg_ref[...], gt_ref[0], zqk_ref[...], zv_ref[...], zog_ref[...],
                  gain_ref, c_ref, n_ref, m_ref, mo_ref)
    mix_ref[...] = _sigmoid(zga_ref[...]) * a + _sigmoid(zgb_ref[...]) * mo_ref[...]


def _mixer(z, zg, gt3, b_gates, w_conv, mh_gain, conv0, c0, n0, m0, *, nb, nt, tt, chunk):
    def zspec(j):
        return pl.BlockSpec((tt, D_MODEL), lambda b, t, j=j: (b * nt + t, j))

    in_specs = [pl.BlockSpec(memory_space=pltpu.SMEM)]
    in_specs += [zspec(j) for j in range(8)]
    in_specs += [
        pl.BlockSpec((tt, LANES), lambda b, t: (b * nt + t, 0)),
        pl.BlockSpec((1, SUBLANES, tt), lambda b, t: (b * nt + t, 0, 0)),
        pl.BlockSpec((CONV_K, D_MODEL), lambda b, t: (0, 0)),
        pl.BlockSpec((1, D_MODEL), lambda b, t: (0, 0)),
        pl.BlockSpec((1, CONV_K - 1, D_MODEL), lambda b, t: (b, 0, 0)),
        pl.BlockSpec((1, N_HEADS, HEAD_QK, HEAD_V), lambda b, t: (b, 0, 0, 0)),
        pl.BlockSpec((1, N_HEADS, HEAD_QK), lambda b, t: (b, 0, 0)),
        pl.BlockSpec((1, 1, N_HEADS), lambda b, t: (b, 0, 0)),
    ]
    out_specs = [
        pl.BlockSpec((tt, D_MODEL), lambda b, t: (b * nt + t, 0)),
        pl.BlockSpec((1, CONV_K - 1, D_MODEL), lambda b, t: (b, 0, 0)),
        pl.BlockSpec((1, N_HEADS, HEAD_QK, HEAD_V), lambda b, t: (b, 0, 0, 0)),
        pl.BlockSpec((1, N_HEADS, HEAD_QK), lambda b, t: (b, 0, 0)),
        pl.BlockSpec((1, 1, N_HEADS), lambda b, t: (b, 0, 0)),
    ]
    out_shape = [
        jax.ShapeDtypeStruct((nb * nt * tt, D_MODEL), F32),
        jax.ShapeDtypeStruct((nb, CONV_K - 1, D_MODEL), F32),
        jax.ShapeDtypeStruct((nb, N_HEADS, HEAD_QK, HEAD_V), F32),
        jax.ShapeDtypeStruct((nb, N_HEADS, HEAD_QK), F32),
        jax.ShapeDtypeStruct((nb, 1, N_HEADS), F32),
    ]
    return pl.pallas_call(
        functools.partial(_mixer_kernel, chunk),
        grid=(nb, nt),
        in_specs=in_specs,
        out_specs=out_specs,
        out_shape=out_shape,
        scratch_shapes=[pltpu.VMEM((tt, D_MODEL), F32)],
        compiler_params=_cparams(("arbitrary", "arbitrary")),
        name="mixer_t%d" % tt,
    )(b_gates, z, z, z, z, z, z, z, z, zg, gt3, w_conv, mh_gain, conv0, c0, n0, m0)


def _prompt_kernel(chunk, nb, ns, alpha, bg_ref, x_ref, w_ref, wgh_ref, wgl_ref, wconv_ref,
                   gain_ref, wout_ref, g_ref, b_ref, xs_ref, mixs_ref,
                   x1_ref, x1r_ref, conv_ref, c_ref, n_ref, m_ref, mo_ref):
    b = pl.program_id(0)
    t = pl.program_id(1)

    @pl.when(b < nb)
    def _():
        @pl.when(t == 0)
        def _():
            conv_ref[...] = jnp.zeros_like(conv_ref)
            c_ref[...] = jnp.zeros_like(c_ref)
            n_ref[...] = jnp.zeros_like(n_ref)
            m_ref[...] = jnp.zeros_like(m_ref)

        x = x_ref[...]
        xb = x.astype(BF16)

        def z(j):
            return _dot(xb, w_ref[:, j * D_MODEL:(j + 1) * D_MODEL])

        a = _conv_branch(z(1) * z(2), z(0), wconv_ref[...], conv_ref)
        zg = _dot3(x, wgh_ref[...], wgl_ref[...])
        _mlstm_branch(chunk, bg_ref, zg, zg.T, z(3), z(4), z(5), gain_ref,
                      c_ref, n_ref, m_ref, mo_ref)
        mix = _sigmoid(z(6)) * a + _sigmoid(z(7)) * mo_ref[...]
        _store_x1(x, mix, wout_ref, g_ref, b_ref, alpha, x1_ref, x1r_ref)

    @pl.when(jnp.logical_and(b == nb, t < ns))
    def _():
        _store_x1(xs_ref[...], mixs_ref[...], wout_ref, g_ref, b_ref, alpha, x1_ref, x1r_ref)


def _prompt_layer_half(xp, xs, mix_s, b_gates, w_main, wg_hi, wg_lo, w_conv, mh_gain, w_out,
                       ln_g, ln_b, alpha, *, nb, nt, tt, chunk):
    mp, ms = xp.shape[0], xs.shape[0]
    ns = ms // tt
    last_p = nb * nt - 1

    def prow(b, t):
        return jnp.minimum(b * nt + t, last_p)

    def srow(b, t):
        return jnp.where(b == nb, jnp.minimum(t, ns - 1), 0)

    def orow(b, t):
        return jnp.where(b == nb, nb * nt + jnp.minimum(t, ns - 1), b * nt + t)

    def state(b):
        return jnp.minimum(b, nb - 1)

    const2 = lambda b, t: (0, 0)
    in_specs = [
        pl.BlockSpec(memory_space=pltpu.SMEM),
        pl.BlockSpec((tt, D_MODEL), lambda b, t: (prow(b, t), 0)),
        pl.BlockSpec(w_main.shape, const2, pipeline_mode=pl.Buffered(1)),
        pl.BlockSpec(wg_hi.shape, const2),
        pl.BlockSpec(wg_lo.shape, const2),
        pl.BlockSpec((CONV_K, D_MODEL), const2),
        pl.BlockSpec((1, D_MODEL), const2),
        pl.BlockSpec((D_MODEL, D_MODEL), const2),
        pl.BlockSpec((1, D_MODEL), const2),
        pl.BlockSpec((1, D_MODEL), const2),
        pl.BlockSpec((tt, D_MODEL), lambda b, t: (srow(b, t), 0)),
        pl.BlockSpec((tt, D_MODEL), lambda b, t: (srow(b, t), 0)),
    ]
    out_specs = [
        pl.BlockSpec((tt, D_MODEL), lambda b, t: (orow(b, t), 0)),
        pl.BlockSpec((tt * ROW_TILES, LANES), lambda b, t: (orow(b, t), 0)),
        pl.BlockSpec((1, CONV_K - 1, D_MODEL), lambda b, t: (state(b), 0, 0)),
        pl.BlockSpec((1, N_HEADS, HEAD_QK, HEAD_V), lambda b, t: (state(b), 0, 0, 0)),
        pl.BlockSpec((1, N_HEADS, HEAD_QK), lambda b, t: (state(b), 0, 0)),
        pl.BlockSpec((1, 1, N_HEADS), lambda b, t: (state(b), 0, 0)),
    ]
    out_shape = [
        jax.ShapeDtypeStruct((mp + ms, D_MODEL), F32),
        jax.ShapeDtypeStruct(((mp + ms) * ROW_TILES, LANES), F32),
        jax.ShapeDtypeStruct((nb, CONV_K - 1, D_MODEL), F32),
        jax.ShapeDtypeStruct((nb, N_HEADS, HEAD_QK, HEAD_V), F32),
        jax.ShapeDtypeStruct((nb, N_HEADS, HEAD_QK), F32),
        jax.ShapeDtypeStruct((nb, 1, N_HEADS), F32),
    ]
    return pl.pallas_call(
        functools.partial(_prompt_kernel, chunk, nb, ns, alpha),
        grid=(nb + 1, nt),
        in_specs=in_specs,
        out_specs=out_specs,
        out_shape=out_shape,
        scratch_shapes=[pltpu.VMEM((tt, D_MODEL), F32)],
        compiler_params=_cparams(("arbitrary", "arbitrary")),
        name="prompt_half",
    )(b_gates, xp, w_main, wg_hi, wg_lo, w_conv, mh_gain, w_out, ln_g, ln_b, xs, mix_s)


def _router_kernel(x_ref, wh_ref, wl_ref, br_ref, idx_ref, gate_ref, rank_ref, cnt_ref):
    tt = x_ref.shape[0]

    @pl.when(pl.program_id(0) == 0)
    def _():
        cnt_ref[...] = jnp.zeros_like(cnt_ref)

    scores = _sigmoid(_dot3(x_ref[...], wh_ref[...], wl_ref[...]))
    lane = lax.broadcasted_iota(jnp.int32, (tt, N_EXPERTS), 1)
    work = scores + br_ref[...]
    picked = []
    sel = []
    member = jnp.zeros((tt, N_EXPERTS), F32)
    for _ in range(TOP_K):
        mx = jnp.max(work, axis=1, keepdims=True)
        ik = jnp.min(jnp.where(work == mx, lane, N_EXPERTS), axis=1, keepdims=True)
        onehot = lane == ik
        picked.append((ik, onehot))
        sel.append(jnp.sum(jnp.where(onehot, scores, 0.0), axis=1, keepdims=True))
        work = jnp.where(onehot, -jnp.inf, work)
        member = member + onehot.astype(F32)
    total = sel[0]
    for sk in sel[1:]:
        total = total + sk

    ti = lax.broadcasted_iota(jnp.int32, (tt, tt), 0)
    si = lax.broadcasted_iota(jnp.int32, (tt, tt), 1)
    earlier = (si < ti).astype(BF16)
    before = _dot(earlier, member.astype(BF16)) + cnt_ref[...]

    lane_o = lax.broadcasted_iota(jnp.int32, (tt, TOP_K), 1)
    idx_o = jnp.zeros((tt, TOP_K), jnp.int32)
    gate_o = jnp.zeros((tt, TOP_K), F32)
    rank_o = jnp.zeros((tt, TOP_K), jnp.int32)
    for kk in range(TOP_K):
        ik, onehot = picked[kk]
        rk = jnp.sum(jnp.where(onehot, before, 0.0), axis=1, keepdims=True)
        idx_o = jnp.where(lane_o == kk, ik, idx_o)
        gate_o = jnp.where(lane_o == kk, sel[kk] / total * ROUTED_SCALE, gate_o)
        rank_o = jnp.where(lane_o == kk, rk.astype(jnp.int32), rank_o)
    idx_ref[...] = idx_o
    gate_ref[...] = gate_o
    rank_ref[...] = rank_o
    cnt_ref[...] = cnt_ref[...] + jnp.sum(member, axis=0, keepdims=True)


def _router(x1, wr_hi, wr_lo, b_router, tt):
    m = x1.shape[0]
    return pl.pallas_call(
        _router_kernel,
        grid=(m // tt,),
        in_specs=[pl.BlockSpec((tt, D_MODEL), lambda i: (i, 0)),
                  pl.BlockSpec((D_MODEL, N_EXPERTS), lambda i: (0, 0)),
                  pl.BlockSpec((D_MODEL, N_EXPERTS), lambda i: (0, 0)),
                  pl.BlockSpec((1, N_EXPERTS), lambda i: (0, 0))],
        out_specs=[pl.BlockSpec((tt, TOP_K), lambda i: (i, 0)),
                   pl.BlockSpec((tt, TOP_K), lambda i: (i, 0)),
                   pl.BlockSpec((tt, TOP_K), lambda i: (i, 0)),
                   pl.BlockSpec((1, N_EXPERTS), lambda i: (0, 0))],
        out_shape=[jax.ShapeDtypeStruct((m, TOP_K), jnp.int32),
                   jax.ShapeDtypeStruct((m, TOP_K), F32),
                   jax.ShapeDtypeStruct((m, TOP_K), jnp.int32),
                   jax.ShapeDtypeStruct((1, N_EXPERTS), F32)],
        compiler_params=_cparams(("arbitrary",)),
        name="router",
    )(x1, wr_hi, wr_lo, b_router)


def _row_copy(src, src_row8, dst, dst_row8, sem):
    return pltpu.make_async_copy(src.at[pl.ds(pl.multiple_of(src_row8, SUBLANES), SUBLANES)],
                                 dst.at[pl.ds(pl.multiple_of(dst_row8, SUBLANES), SUBLANES)],
                                 sem)


def _dispatch_kernel(tt, start_ref, idx_ref, rank_ref, x1r_ref, xg_in_ref, xg_ref, dest_ref, sem):
    del xg_in_ref

    def issue(t, carry):
        for kk in range(TOP_K):
            j = t * TOP_K + kk
            dest8 = (start_ref[idx_ref[j]] + rank_ref[j]) * ROW_TILES
            dest_ref[j] = dest8
            _row_copy(x1r_ref, t * ROW_TILES, xg_ref, dest8, sem).start()
        return carry

    lax.fori_loop(0, tt, issue, 0)
    n_rows = tt * TOP_K * ROW_TILES
    pltpu.make_async_copy(xg_ref.at[pl.ds(0, n_rows)], xg_ref.at[pl.ds(0, n_rows)], sem).wait()


def _dispatch(seg_start, idx, rank, x1r, xg_zero, tt):
    m = x1r.shape[0] // ROW_TILES
    smem_blk = pl.BlockSpec((tt * TOP_K,), lambda i: (i,), memory_space=pltpu.SMEM)
    return pl.pallas_call(
        functools.partial(_dispatch_kernel, tt),
        grid=(m // tt,),
        in_specs=[pl.BlockSpec(memory_space=pltpu.SMEM), smem_blk, smem_blk,
                  pl.BlockSpec((tt * ROW_TILES, LANES), lambda i: (i, 0)),
                  pl.BlockSpec(memory_space=pl.ANY)],
        out_specs=[pl.BlockSpec(memory_space=pl.ANY), smem_blk],
        out_shape=[jax.ShapeDtypeStruct(xg_zero.shape, F32),
                   jax.ShapeDtypeStruct((m * TOP_K,), jnp.int32)],
        scratch_shapes=[pltpu.SemaphoreType.DMA(())],
        input_output_aliases={4: 0},
        compiler_params=_cparams(("arbitrary",)),
        name="dispatch",
    )(seg_start, idx, rank, x1r, xg_zero)


def _experts_kernel(rb, blk_ref, be_ref, nu_ref, xg_ref, wg_ref, wu_ref, wd_ref, y_ref):
    del blk_ref, be_ref

    @pl.when(pl.program_id(0) < nu_ref[0])
    def _():
        xb = jnp.concatenate(
            [xg_ref[pl.ds(s, rb, stride=ROW_TILES), :] for s in range(ROW_TILES)],
            axis=1).astype(BF16)
        g = _dot(xb, wg_ref[0].astype(BF16))
        u = _dot(xb, wu_ref[0].astype(BF16))
        hid = (g * _sigmoid(g) * u).astype(BF16)
        y = _dot(hid, wd_ref[0].astype(BF16))
        for s in range(ROW_TILES):
            y_ref[pl.ds(s, rb, stride=ROW_TILES), :] = y[:, s * LANES:(s + 1) * LANES]


def _experts(blk_map, block_e, n_used, xg, w_eg, w_eu, w_ed, rb):
    n_blocks = blk_map.shape[0]
    grid_spec = pltpu.PrefetchScalarGridSpec(
        num_scalar_prefetch=3,
        grid=(n_blocks,),
        in_specs=[
            pl.BlockSpec((rb * ROW_TILES, LANES), lambda i, blk, be, nu: (blk[i], 0)),
            pl.BlockSpec((1, D_MODEL, D_EXPERT), lambda i, blk, be, nu: (be[i], 0, 0)),
            pl.BlockSpec((1, D_MODEL, D_EXPERT), lambda i, blk, be, nu: (be[i], 0, 0)),
            pl.BlockSpec((1, D_EXPERT, D_MODEL), lambda i, blk, be, nu: (be[i], 0, 0)),
        ],
        out_specs=pl.BlockSpec((rb * ROW_TILES, LANES), lambda i, blk, be, nu: (blk[i], 0)),
    )
    return pl.pallas_call(
        functools.partial(_experts_kernel, rb),
        grid_spec=grid_spec,
        out_shape=jax.ShapeDtypeStruct(xg.shape, F32),
        input_output_aliases={3: 0},
        compiler_params=_cparams(("arbitrary",)),
        name="experts",
    )(blk_map, block_e, n_used, xg, w_eg, w_eu, w_ed)


def _combine_kernel(alpha, n_p, dest_ref, gate_ref, x1_ref, y_ref, wsg_ref, wsu_ref, wsd_ref,
                    g_ref, b_ref, x2p_ref, x2s_ref, ybuf, sem):
    tt = x1_ref.shape[0]

    def issue(t, carry):
        for kk in range(TOP_K):
            _row_copy(y_ref, dest_ref[t * TOP_K + kk], ybuf,
                      (kk * tt + t) * ROW_TILES, sem).start()
        return carry

    lax.fori_loop(0, tt, issue, 0)

    x1 = x1_ref[...]
    xb = x1.astype(BF16)
    gs = _dot(xb, wsg_ref[...])
    us = _dot(xb, wsu_ref[...])
    shared = _dot((gs * _sigmoid(gs) * us).astype(BF16), wsd_ref[...])

    n_rows = TOP_K * tt * ROW_TILES
    pltpu.make_async_copy(y_ref.at[pl.ds(0, n_rows)], ybuf, sem).wait()

    gates = gate_ref[...]
    chunks = []
    for s in range(ROW_TILES):
        acc = jnp.zeros((tt, LANES), F32)
        for kk in range(TOP_K):
            acc = acc + gates[:, kk:kk + 1] * ybuf[pl.ds(kk * tt * ROW_TILES + s, tt,
                                                         stride=ROW_TILES), :]
        chunks.append(acc)
    routed = jnp.concatenate(chunks, axis=1)
    x2 = _layer_norm(alpha * x1 + (shared + routed), g_ref[...], b_ref[...])

    @pl.when(pl.program_id(0) < n_p)
    def _():
        x2p_ref[...] = x2

    @pl.when(pl.program_id(0) >= n_p)
    def _():
        x2s_ref[...] = x2


def _combine(dest8, gates, x1, y, wsg, wsu, wsd, g, b, alpha, tt, mp):
    m = x1.shape[0]
    n_p = mp // tt
    return pl.pallas_call(
        functools.partial(_combine_kernel, alpha, n_p),
        grid=(m // tt,),
        in_specs=[pl.BlockSpec((tt * TOP_K,), lambda i: (i,), memory_space=pltpu.SMEM),
                  pl.BlockSpec((tt, TOP_K), lambda i: (i, 0)),
                  pl.BlockSpec((tt, D_MODEL), lambda i: (i, 0)),
                  pl.BlockSpec(memory_space=pl.ANY),
                  pl.BlockSpec((D_MODEL, D_EXPERT), lambda i: (0, 0)),
                  pl.BlockSpec((D_MODEL, D_EXPERT), lambda i: (0, 0)),
                  pl.BlockSpec((D_EXPERT, D_MODEL), lambda i: (0, 0)),
                  pl.BlockSpec((1, D_MODEL), lambda i: (0, 0)),
                  pl.BlockSpec((1, D_MODEL), lambda i: (0, 0))],
        out_specs=[pl.BlockSpec((tt, D_MODEL), lambda i: (jnp.minimum(i, n_p - 1), 0)),
                   pl.BlockSpec((tt, D_MODEL), lambda i: (jnp.maximum(i - n_p, 0), 0))],
        out_shape=[jax.ShapeDtypeStruct((mp, D_MODEL), F32),
                   jax.ShapeDtypeStruct((m - mp, D_MODEL), F32)],
        scratch_shapes=[pltpu.VMEM((TOP_K * tt * ROW_TILES, LANES), F32),
                        pltpu.SemaphoreType.DMA(())],
        compiler_params=_cparams(("arbitrary",)),
        name="combine",
    )(dest8, gates, x1, y, wsg, wsu, wsd, g, b)


PROJ_TM = 512
PROJ_TN = 1024
MIX_TT = 256
MIX_CHUNK = 128
ROUTER_TT = 256
DISPATCH_TT = 512
EXPERT_RB = 256
COMBINE_TT = 128


def _moe(x1, x1r, mp, w_router, b_router, w_eg, w_eu, w_ed, w_sg, w_su, w_sd, ln_g, ln_b, alpha):
    m = x1.shape[0]
    wr_hi, wr_lo = _split_hi_lo(w_router)
    idx, gates, rank, cnt = _router(x1, wr_hi, wr_lo, b_router[None, :], ROUTER_TT)

    rb = EXPERT_RB
    n_rows = m * TOP_K
    n_blocks = (n_rows + N_EXPERTS * (rb - 1) + rb - 1) // rb
    counts = cnt[0].astype(jnp.int32)
    padded = (counts + rb - 1) // rb * rb
    pad_end = jnp.cumsum(padded)
    pad_start = pad_end - padded
    n_used = pad_end[-1] // rb
    blk_map = jnp.minimum(jnp.arange(n_blocks, dtype=jnp.int32), n_used - 1)
    block_e = jnp.minimum(
        jnp.sum((pad_end[None, :] <= (blk_map * rb)[:, None]).astype(jnp.int32), axis=1),
        N_EXPERTS - 1)

    xg, dest8 = _dispatch(pad_start, idx.reshape(-1), rank.reshape(-1), x1r,
                          jnp.zeros((n_blocks * rb * ROW_TILES, LANES), F32), DISPATCH_TT)
    y = _experts(blk_map, block_e, n_used.reshape(1).astype(jnp.int32), xg,
                 w_eg, w_eu, w_ed, rb)
    return _combine(dest8, gates, x1, y, w_sg.astype(BF16), w_su.astype(BF16),
                    w_sd.astype(BF16), ln_g[None, :], ln_b[None, :], alpha, COMBINE_TT, mp)


def _layer(xp, xs, conv_s, c_s, n_s, m_s, w_in, b_gates, w_conv, mh_gain, w_out, ln1_g, ln1_b,
           w_router, b_router, w_eg, w_eu, w_ed, w_sg, w_su, w_sd, ln2_g, ln2_b, alpha):
    bp, tp, _ = xp.shape
    bs, ts, _ = xs.shape
    mp, ms = bp * tp, bs * ts
    xp2 = xp.reshape(mp, D_MODEL)
    xs2 = xs.reshape(ms, D_MODEL)

    w_main = jnp.concatenate([w_in[:, :GATE_COL0], w_in[:, GATE_COL0 + 2 * N_HEADS:]],
                             axis=1).astype(BF16)
    w_gate = jnp.pad(w_in[:, GATE_COL0:GATE_COL0 + 2 * N_HEADS],
                     ((0, 0), (0, LANES - 2 * N_HEADS)))
    wg_hi, wg_lo = _split_hi_lo(w_gate)

    z_s = _proj(xs2, w_main, PROJ_TM, PROJ_TN)
    zg_s = _gate_proj(xs2, wg_hi, wg_lo, PROJ_TM)
    gt_s = zg_s[:, :SUBLANES].reshape(bs, ts, SUBLANES).transpose(0, 2, 1)
    mix_s, conv_n, c_n, n_n, m_n = _mixer(
        z_s, zg_s, gt_s, b_gates, w_conv, mh_gain[None, :],
        conv_s, c_s, n_s, m_s[:, None, :], nb=bs, nt=1, tt=ts, chunk=ts)

    x1, x1r, conv_p, c_p, n_p, m_p = _prompt_layer_half(
        xp2, xs2, mix_s, b_gates, w_main, wg_hi, wg_lo, w_conv, mh_gain[None, :],
        w_out.astype(BF16), ln1_g[None, :], ln1_b[None, :], alpha,
        nb=bp, nt=tp // MIX_TT, tt=MIX_TT, chunk=MIX_CHUNK)

    x2p, x2s = _moe(x1, x1r, mp, w_router, b_router, w_eg, w_eu, w_ed, w_sg, w_su, w_sd,
                    ln2_g, ln2_b, alpha)
    states_p = (conv_p, c_p, n_p, m_p[:, 0, :])
    states_s = (conv_n, c_n, n_n, m_n[:, 0, :])
    return x2p.reshape(bp, tp, D_MODEL), x2s.reshape(bs, ts, D_MODEL), states_p, states_s


def kernel(x_prompt, x_sample, cache_conv, state_mlstm_C, state_mlstm_n, state_mlstm_m, w_in, b_gates, w_conv, mh_gain, w_out, ln1_g, ln1_b, w_router, b_router, w_exp_gate, w_exp_up, w_exp_down, w_sh_gate, w_sh_up, w_sh_down, ln2_g, ln2_b):
    depth = w_in.shape[0]
    alpha = (2.0 * depth) ** 0.25
    hp, hs = x_prompt, x_sample
    outs_p = [[], [], [], []]
    outs_s = [[], [], [], []]
    for l in range(depth):
        hp, hs, st_p, st_s = _layer(
            hp, hs, cache_conv[l], state_mlstm_C[l], state_mlstm_n[l], state_mlstm_m[l],
            w_in[l], b_gates[l], w_conv[l], mh_gain[l], w_out[l], ln1_g[l], ln1_b[l],
            w_router[l], b_router[l], w_exp_gate[l], w_exp_up[l], w_exp_down[l],
            w_sh_gate[l], w_sh_up[l], w_sh_down[l], ln2_g[l], ln2_b[l], alpha)
        for acc, val in zip(outs_p, st_p):
            acc.append(val)
        for acc, val in zip(outs_s, st_s):
            acc.append(val)
    return (hp, hs) + tuple(jnp.stack(a) for a in outs_p) + tuple(jnp.stack(a) for a in outs_s)
```

```python
import functools

import jax
import jax.numpy as jnp
from jax import lax
from jax.experimental import pallas as pl
from jax.experimental.pallas import tpu as pltpu

F32 = jnp.float32
BF16 = jnp.bfloat16

D_MODEL = 1024
N_HEADS = 4
HEAD_V = 256
HEAD_QK = 128
N_EXPERTS = 256
TOP_K = 8
D_EXPERT = 256
ROUTED_SCALE = 2.5
LN_EPS = 1e-5
CONV_K = 3

LANES = 128
SUBLANES = 8
ROW_TILES = D_MODEL // LANES
GATE_COL0 = 6 * D_MODEL

VMEM_LIMIT = 56 * 1024 * 1024


def _cparams(sem):
    return pltpu.CompilerParams(dimension_semantics=sem, vmem_limit_bytes=VMEM_LIMIT)


def _sigmoid(x):
    return 1.0 / (1.0 + jnp.exp(-x))


def _log_sigmoid(x):
    return jnp.minimum(x, 0.0) - jnp.log(1.0 + jnp.exp(-jnp.abs(x)))


def _layer_norm(x, g, b):
    mu = jnp.mean(x, axis=-1, keepdims=True)
    xc = x - mu
    var = jnp.mean(xc * xc, axis=-1, keepdims=True)
    return xc * lax.rsqrt(var + LN_EPS) * g + b


def _split_hi_lo(x):
    hi = x.astype(BF16)
    lo = (x - hi.astype(F32)).astype(BF16)
    return hi, lo


def _dot(a, b):
    return jnp.dot(a, b, preferred_element_type=F32)


def _dot3(x, w_hi, w_lo):
    x_hi, x_lo = _split_hi_lo(x)
    return _dot(x_hi, w_hi) + _dot(x_lo, w_hi) + _dot(x_hi, w_lo)


def _proj_kernel(x_ref, w_ref, z_ref):
    z_ref[...] = _dot(x_ref[...].astype(BF16), w_ref[...])


def _proj(x, w, tm, tn):
    m, k = x.shape
    n = w.shape[1]
    return pl.pallas_call(
        _proj_kernel,
        grid=(m // tm, n // tn),
        in_specs=[pl.BlockSpec((tm, k), lambda i, j: (i, 0)),
                  pl.BlockSpec((k, tn), lambda i, j: (0, j))],
        out_specs=pl.BlockSpec((tm, tn), lambda i, j: (i, j)),
        out_shape=jax.ShapeDtypeStruct((m, n), F32),
        compiler_params=_cparams(("parallel", "arbitrary")),
        name="proj",
    )(x, w)


def _gate_proj_kernel(x_ref, wh_ref, wl_ref, z_ref):
    z_ref[...] = _dot3(x_ref[...], wh_ref[...], wl_ref[...])


def _gate_proj(x, w_hi, w_lo, tm):
    m, k = x.shape
    n = w_hi.shape[1]
    return pl.pallas_call(
        _gate_proj_kernel,
        grid=(m // tm,),
        in_specs=[pl.BlockSpec((tm, k), lambda i: (i, 0)),
                  pl.BlockSpec((k, n), lambda i: (0, 0)),
                  pl.BlockSpec((k, n), lambda i: (0, 0))],
        out_specs=pl.BlockSpec((tm, n), lambda i: (i, 0)),
        out_shape=jax.ShapeDtypeStruct((m, n), F32),
        compiler_params=_cparams(("parallel",)),
        name="gate_proj",
    )(x, w_hi, w_lo)


def _conv_branch(u, zcb, wc, conv_ref):
    tt = u.shape[0]
    carry = conv_ref[0]
    rows = lax.broadcasted_iota(jnp.int32, (tt, D_MODEL), 0)
    u1 = jnp.where(rows == 0, carry[1:2, :], pltpu.roll(u, 1, 0))
    u2 = jnp.where(rows == 0, carry[0:1, :],
                   jnp.where(rows == 1, carry[1:2, :], pltpu.roll(u, 2, 0)))
    conv_out = u2 * wc[0:1, :] + u1 * wc[1:2, :] + u * wc[2:3, :]
    conv_ref[0] = u[tt - 2:tt, :]
    return zcb * conv_out


def _mlstm_chunk(q, k, v, ig_col, lf_col, ig_row, lf_row, c_state, n_row, m_prev):
    chunk = q.shape[0]
    ti = lax.broadcasted_iota(jnp.int32, (chunk, chunk), 0)
    si = lax.broadcasted_iota(jnp.int32, (chunk, chunk), 1)
    causal = si <= ti
    b_col = jnp.sum(jnp.where(causal, lf_row, 0.0), axis=1, keepdims=True)
    b_row = jnp.sum(jnp.where(ti <= si, lf_col, 0.0), axis=0, keepdims=True)
    inter = b_col + m_prev
    dmat = jnp.where(causal, b_col - b_row + ig_row, -jnp.inf)
    m_t = jnp.maximum(inter, jnp.max(dmat, axis=1, keepdims=True))
    w_intra = jnp.exp(dmat - m_t)
    w_inter = jnp.exp(inter - m_t)
    qb = q.astype(BF16)
    kb = k.astype(BF16)
    vb = v.astype(BF16)
    s = lax.dot_general(qb, kb, (((1,), (1,)), ((), ())), preferred_element_type=F32) * w_intra
    num = w_inter * _dot(qb, c_state.astype(BF16)) + _dot(s.astype(BF16), vb)
    den = (w_inter * jnp.sum(q * n_row, axis=1, keepdims=True)
           + jnp.sum(s, axis=1, keepdims=True))
    h = num / jnp.maximum(jnp.abs(den), jnp.exp(-m_t))
    m_new = m_t[chunk - 1:chunk, :]
    b_last = b_col[chunk - 1:chunk, :]
    ws_col = jnp.exp(b_last - b_col + ig_col - m_new)
    cdecay = jnp.exp(inter[chunk - 1:chunk, :] - m_new)
    kw = k * ws_col
    c_new = cdecay * c_state + lax.dot_general(
        kw.astype(BF16), vb, (((0,), (0,)), ((), ())), preferred_element_type=F32)
    n_new = cdecay * n_row + jnp.sum(kw, axis=0, keepdims=True)
    return h, c_new, n_new, m_new


def _mlstm_branch(chunk, bg_ref, zg, gt, zqk, zv, zog, gain_ref, c_ref, n_ref, m_ref, mo_ref):
    tt = zqk.shape[0]
    for h in range(N_HEADS):
        ig_col_all = zg[:, h:h + 1] + bg_ref[h]
        lf_col_all = _log_sigmoid(zg[:, N_HEADS + h:N_HEADS + h + 1] + bg_ref[N_HEADS + h])
        ig_row_all = gt[h:h + 1, :] + bg_ref[h]
        lf_row_all = _log_sigmoid(gt[N_HEADS + h:N_HEADS + h + 1, :] + bg_ref[N_HEADS + h])
        vcols = slice(h * HEAD_V, (h + 1) * HEAD_V)
        for c in range(tt // chunk):
            r0, r1 = c * chunk, (c + 1) * chunk
            q = zqk[r0:r1, h * HEAD_QK:(h + 1) * HEAD_QK]
            k = zqk[r0:r1, (N_HEADS + h) * HEAD_QK:(N_HEADS + h + 1) * HEAD_QK] * (HEAD_QK ** -0.5)
            hh, c_new, n_new, m_new = _mlstm_chunk(
                q, k, zv[r0:r1, vcols], ig_col_all[r0:r1, :], lf_col_all[r0:r1, :],
                ig_row_all[:, r0:r1], lf_row_all[:, r0:r1],
                c_ref[0, h], n_ref[0, h:h + 1, :], m_ref[0, :, h:h + 1])
            c_ref[0, h] = c_new
            n_ref[0, h:h + 1, :] = n_new
            m_ref[0, :, h:h + 1] = m_new
            mu = jnp.mean(hh, axis=-1, keepdims=True)
            hc = hh - mu
            var = jnp.mean(hc * hc, axis=-1, keepdims=True)
            hn = hc * lax.rsqrt(var + LN_EPS) * gain_ref[:, vcols]
            mo_ref[r0:r1, vcols] = hn * _sigmoid(zog[r0:r1, vcols])


def _store_x1(x, mix, wout_ref, g_ref, b_ref, alpha, x1_ref, x1r_ref):
    tt = x.shape[0]
    x1 = _layer_norm(alpha * x + _dot(mix.astype(BF16), wout_ref[...]), g_ref[...], b_ref[...])
    x1_ref[...] = x1
    for s in range(ROW_TILES):
        x1r_ref[pl.ds(s, tt, stride=ROW_TILES), :] = x1[:, s * LANES:(s + 1) * LANES]


def _mixer_kernel(chunk, bg_ref,
                  zcb_ref, zcc_ref, zch_ref, zqk_ref, zv_ref, zog_ref, zga_ref, zgb_ref,
                  zg_ref, gt_ref, wconv_ref, gain_ref,
                  conv0_ref, c0_ref, n0_ref, m0_ref,
                  mix_ref, conv_ref, c_ref, n_ref, m_ref, mo_ref):
    @pl.when(pl.program_id(1) == 0)
    def _():
        conv_ref[...] = conv0_ref[...]
        c_ref[...] = c0_ref[...]
        n_ref[...] = n0_ref[...]
        m_ref[...] = m0_ref[...]

    a = _conv_branch(zcc_ref[...] * zch_ref[...], zcb_ref[...], wconv_ref[...], conv_ref)
    _mlstm_branch(chunk, bg_ref, zg_ref[...], gt_ref[0], zqk_ref[...], zv_ref[...], zog_ref[...],
                  gain_ref, c_ref, n_ref, m_ref, mo_ref)
    mix_ref[...] = _sigmoid(zga_ref[...]) * a + _sigmoid(zgb_ref[...]) * mo_ref[...]


def _mixer(z, zg, gt3, b_gates, w_conv, mh_gain, conv0, c0, n0, m0, *, nb, nt, tt, chunk):
    def zspec(j):
        return pl.BlockSpec((tt, D_MODEL), lambda b, t, j=j: (b * nt + t, j))

    in_specs = [pl.BlockSpec(memory_space=pltpu.SMEM)]
    in_specs += [zspec(j) for j in range(8)]
    in_specs += [
        pl.BlockSpec((tt, LANES), lambda b, t: (b * nt + t, 0)),
        pl.BlockSpec((1, SUBLANES, tt), lambda b, t: (b * nt + t, 0, 0)),
        pl.BlockSpec((CONV_K, D_MODEL), lambda b, t: (0, 0)),
        pl.BlockSpec((1, D_MODEL), lambda b, t: (0, 0)),
        pl.BlockSpec((1, CONV_K - 1, D_MODEL), lambda b, t: (b, 0, 0)),
        pl.BlockSpec((1, N_HEADS, HEAD_QK, HEAD_V), lambda b, t: (b, 0, 0, 0)),
        pl.BlockSpec((1, N_HEADS, HEAD_QK), lambda b, t: (b, 0, 0)),
        pl.BlockSpec((1, 1, N_HEADS), lambda b, t: (b, 0, 0)),
    ]
    out_specs = [
        pl.BlockSpec((tt, D_MODEL), lambda b, t: (b * nt + t, 0)),
        pl.BlockSpec((1, CONV_K - 1, D_MODEL), lambda b, t: (b, 0, 0)),
        pl.BlockSpec((1, N_HEADS, HEAD_QK, HEAD_V), lambda b, t: (b, 0, 0, 0)),
        pl.BlockSpec((1, N_HEADS, HEAD_QK), lambda b, t: (b, 0, 0)),
        pl.BlockSpec((1, 1, N_HEADS), lambda b, t: (b, 0, 0)),
    ]
    out_shape = [
        jax.ShapeDtypeStruct((nb * nt * tt, D_MODEL), F32),
        jax.ShapeDtypeStruct((nb, CONV_K - 1, D_MODEL), F32),
        jax.ShapeDtypeStruct((nb, N_HEADS, HEAD_QK, HEAD_V), F32),
        jax.ShapeDtypeStruct((nb, N_HEADS, HEAD_QK), F32),
        jax.ShapeDtypeStruct((nb, 1, N_HEADS), F32),
    ]
    return pl.pallas_call(
        functools.partial(_mixer_kernel, chunk),
        grid=(nb, nt),
        in_specs=in_specs,
        out_specs=out_specs,
        out_shape=out_shape,
        scratch_shapes=[pltpu.VMEM((tt, D_MODEL), F32)],
        compiler_params=_cparams(("arbitrary", "arbitrary")),
        name="mixer_t%d" % tt,
    )(b_gates, z, z, z, z, z, z, z, z, zg, gt3, w_conv, mh_gain, conv0, c0, n0, m0)


def _prompt_kernel(chunk, nb, ns, alpha, bg_ref, x_ref, w_ref, wgh_ref, wgl_ref, wconv_ref,
                   gain_ref, wout_ref, g_ref, b_ref, xs_ref, mixs_ref,
                   x1_ref, x1r_ref, conv_ref, c_ref, n_ref, m_ref, mo_ref):
    b = pl.program_id(0)
    t = pl.program_id(1)

    @pl.when(b < nb)
    def _():
        @pl.when(t == 0)
        def _():
            conv_ref[...] = jnp.zeros_like(conv_ref)
            c_ref[...] = jnp.zeros_like(c_ref)
            n_ref[...] = jnp.zeros_like(n_ref)
            m_ref[...] = jnp.zeros_like(m_ref)

        x = x_ref[...]
        xb = x.astype(BF16)

        def z(j):
            return _dot(xb, w_ref[:, j * D_MODEL:(j + 1) * D_MODEL])

        a = _conv_branch(z(1) * z(2), z(0), wconv_ref[...], conv_ref)
        zg = _dot3(x, wgh_ref[...], wgl_ref[...])
        _mlstm_branch(chunk, bg_ref, zg, zg.T, z(3), z(4), z(5), gain_ref,
                      c_ref, n_ref, m_ref, mo_ref)
        mix = _sigmoid(z(6)) * a + _sigmoid(z(7)) * mo_ref[...]
        _store_x1(x, mix, wout_ref, g_ref, b_ref, alpha, x1_ref, x1r_ref)

    @pl.when(jnp.logical_and(b == nb, t < ns))
    def _():
        _store_x1(xs_ref[...], mixs_ref[...], wout_ref, g_ref, b_ref, alpha, x1_ref, x1r_ref)


def _prompt_layer_half(xp, xs, mix_s, b_gates, w_main, wg_hi, wg_lo, w_conv, mh_gain, w_out,
                       ln_g, ln_b, alpha, *, nb, nt, tt, chunk):
    mp, ms = xp.shape[0], xs.shape[0]
    ns = ms // tt
    last_p = nb * nt - 1

    def prow(b, t):
        return jnp.minimum(b * nt + t, last_p)

    def srow(b, t):
        return jnp.where(b == nb, jnp.minimum(t, ns - 1), 0)

    def orow(b, t):
        return jnp.where(b == nb, nb * nt + jnp.minimum(t, ns - 1), b * nt + t)

    def state(b):
        return jnp.minimum(b, nb - 1)

    const2 = lambda b, t: (0, 0)
    in_specs = [
        pl.BlockSpec(memory_space=pltpu.SMEM),
        pl.BlockSpec((tt, D_MODEL), lambda b, t: (prow(b, t), 0)),
        pl.BlockSpec(w_main.shape, const2, pipeline_mode=pl.Buffered(1)),
        pl.BlockSpec(wg_hi.shape, const2),
        pl.BlockSpec(wg_lo.shape, const2),
        pl.BlockSpec((CONV_K, D_MODEL), const2),
        pl.BlockSpec((1, D_MODEL), const2),
        pl.BlockSpec((D_MODEL, D_MODEL), const2),
        pl.BlockSpec((1, D_MODEL), const2),
        pl.BlockSpec((1, D_MODEL), const2),
        pl.BlockSpec((tt, D_MODEL), lambda b, t: (srow(b, t), 0)),
        pl.BlockSpec((tt, D_MODEL), lambda b, t: (srow(b, t), 0)),
    ]
    out_specs = [
        pl.BlockSpec((tt, D_MODEL), lambda b, t: (orow(b, t), 0)),
        pl.BlockSpec((tt * ROW_TILES, LANES), lambda b, t: (orow(b, t), 0)),
        pl.BlockSpec((1, CONV_K - 1, D_MODEL), lambda b, t: (state(b), 0, 0)),
        pl.BlockSpec((1, N_HEADS, HEAD_QK, HEAD_V), lambda b, t: (state(b), 0, 0, 0)),
        pl.BlockSpec((1, N_HEADS, HEAD_QK), lambda b, t: (state(b), 0, 0)),
        pl.BlockSpec((1, 1, N_HEADS), lambda b, t: (state(b), 0, 0)),
    ]
    out_shape = [
        jax.ShapeDtypeStruct((mp + ms, D_MODEL), F32),
        jax.ShapeDtypeStruct(((mp + ms) * ROW_TILES, LANES), F32),
        jax.ShapeDtypeStruct((nb, CONV_K - 1, D_MODEL), F32),
        jax.ShapeDtypeStruct((nb, N_HEADS, HEAD_QK, HEAD_V), F32),
        jax.ShapeDtypeStruct((nb, N_HEADS, HEAD_QK), F32),
        jax.ShapeDtypeStruct((nb, 1, N_HEADS), F32),
    ]
    return pl.pallas_call(
        functools.partial(_prompt_kernel, chunk, nb, ns, alpha),
        grid=(nb + 1, nt),
        in_specs=in_specs,
        out_specs=out_specs,
        out_shape=out_shape,
        scratch_shapes=[pltpu.VMEM((tt, D_MODEL), F32)],
        compiler_params=_cparams(("arbitrary", "arbitrary")),
        name="prompt_half",
    )(b_gates, xp, w_main, wg_hi, wg_lo, w_conv, mh_gain, w_out, ln_g, ln_b, xs, mix_s)


def _router_kernel(x_ref, wh_ref, wl_ref, br_ref, idx_ref, gate_ref, rank_ref, cnt_ref):
    tt = x_ref.shape[0]

    @pl.when(pl.program_id(0) == 0)
    def _():
        cnt_ref[...] = jnp.zeros_like(cnt_ref)

    scores = _sigmoid(_dot3(x_ref[...], wh_ref[...], wl_ref[...]))
    lane = lax.broadcasted_iota(jnp.int32, (tt, N_EXPERTS), 1)
    work = scores + br_ref[...]
    picked = []
    sel = []
    member = jnp.zeros((tt, N_EXPERTS), F32)
    for _ in range(TOP_K):
        mx = jnp.max(work, axis=1, keepdims=True)
        ik = jnp.min(jnp.where(work == mx, lane, N_EXPERTS), axis=1, keepdims=True)
        onehot = lane == ik
        picked.append((ik, onehot))
        sel.append(jnp.sum(jnp.where(onehot, scores, 0.0), axis=1, keepdims=True))
        work = jnp.where(onehot, -jnp.inf, work)
        member = member + onehot.astype(F32)
    total = sel[0]
    for sk in sel[1:]:
        total = total + sk

    ti = lax.broadcasted_iota(jnp.int32, (tt, tt), 0)
    si = lax.broadcasted_iota(jnp.int32, (tt, tt), 1)
    earlier = (si < ti).astype(BF16)
    before = _dot(earlier, member.astype(BF16)) + cnt_ref[...]

    lane_o = lax.broadcasted_iota(jnp.int32, (tt, TOP_K), 1)
    idx_o = jnp.zeros((tt, TOP_K), jnp.int32)
    gate_o = jnp.zeros((tt, TOP_K), F32)
    rank_o = jnp.zeros((tt, TOP_K), jnp.int32)
    for kk in range(TOP_K):
        ik, onehot = picked[kk]
        rk = jnp.sum(jnp.where(onehot, before, 0.0), axis=1, keepdims=True)
        idx_o = jnp.where(lane_o == kk, ik, idx_o)
        gate_o = jnp.where(lane_o == kk, sel[kk] / total * ROUTED_SCALE, gate_o)
        rank_o = jnp.where(lane_o == kk, rk.astype(jnp.int32), rank_o)
    idx_ref[...] = idx_o
    gate_ref[...] = gate_o
    rank_ref[...] = rank_o
    cnt_ref[...] = cnt_ref[...] + jnp.sum(member, axis=0, keepdims=True)


def _router(x1, wr_hi, wr_lo, b_router, tt):
    m = x1.shape[0]
    return pl.pallas_call(
        _router_kernel,
        grid=(m // tt,),
        in_specs=[pl.BlockSpec((tt, D_MODEL), lambda i: (i, 0)),
                  pl.BlockSpec((D_MODEL, N_EXPERTS), lambda i: (0, 0)),
                  pl.BlockSpec((D_MODEL, N_EXPERTS), lambda i: (0, 0)),
                  pl.BlockSpec((1, N_EXPERTS), lambda i: (0, 0))],
        out_specs=[pl.BlockSpec((tt, TOP_K), lambda i: (i, 0)),
                   pl.BlockSpec((tt, TOP_K), lambda i: (i, 0)),
                   pl.BlockSpec((tt, TOP_K), lambda i: (i, 0)),
                   pl.BlockSpec((1, N_EXPERTS), lambda i: (0, 0))],
        out_shape=[jax.ShapeDtypeStruct((m, TOP_K), jnp.int32),
                   jax.ShapeDtypeStruct((m, TOP_K), F32),
                   jax.ShapeDtypeStruct((m, TOP_K), jnp.int32),
                   jax.ShapeDtypeStruct((1, N_EXPERTS), F32)],
        compiler_params=_cparams(("arbitrary",)),
        name="router",
    )(x1, wr_hi, wr_lo, b_router)


def _row_copy(src, src_row8, dst, dst_row8, sem):
    return pltpu.make_async_copy(src.at[pl.ds(pl.multiple_of(src_row8, SUBLANES), SUBLANES)],
                                 dst.at[pl.ds(pl.multiple_of(dst_row8, SUBLANES), SUBLANES)],
                                 sem)


def _dispatch_kernel(tt, rb, start_ref, zflag_ref, idx_ref, rank_ref, x1r_ref, xg_ref, dest_ref,
                     zbuf, sem, zsem):
    blk_rows = rb * ROW_TILES

    @pl.when(pl.program_id(0) == 0)
    def _():
        zbuf[...] = jnp.zeros_like(zbuf)

        def zero_copy(i):
            return pltpu.make_async_copy(
                zbuf, xg_ref.at[pl.ds(pl.multiple_of(i * blk_rows, SUBLANES), blk_rows)], zsem)

        def start(i, n):
            @pl.when(zflag_ref[i] != 0)
            def _():
                zero_copy(i).start()
            return n + zflag_ref[i]

        n_started = lax.fori_loop(0, zflag_ref.shape[0], start, 0)

        def wait(i, carry):
            zero_copy(0).wait()
            return carry

        lax.fori_loop(0, n_started, wait, 0)

    def issue(t, carry):
        for kk in range(TOP_K):
            j = t * TOP_K + kk
            dest8 = (start_ref[idx_ref[j]] + rank_ref[j]) * ROW_TILES
            dest_ref[j] = dest8
            _row_copy(x1r_ref, t * ROW_TILES, xg_ref, dest8, sem).start()
        return carry

    lax.fori_loop(0, tt, issue, 0)
    n_rows = tt * TOP_K * ROW_TILES
    pltpu.make_async_copy(xg_ref.at[pl.ds(0, n_rows)], xg_ref.at[pl.ds(0, n_rows)], sem).wait()


def _dispatch(seg_start, zero_flag, idx, rank, x1r, tt, rb):
    m = x1r.shape[0] // ROW_TILES
    n_blocks = zero_flag.shape[0]
    smem_blk = pl.BlockSpec((tt * TOP_K,), lambda i: (i,), memory_space=pltpu.SMEM)
    return pl.pallas_call(
        functools.partial(_dispatch_kernel, tt, rb),
        grid=(m // tt,),
        in_specs=[pl.BlockSpec(memory_space=pltpu.SMEM), pl.BlockSpec(memory_space=pltpu.SMEM),
                  smem_blk, smem_blk,
                  pl.BlockSpec((tt * ROW_TILES, LANES), lambda i: (i, 0))],
        out_specs=[pl.BlockSpec(memory_space=pl.ANY), smem_blk],
        out_shape=[jax.ShapeDtypeStruct((n_blocks * rb * ROW_TILES, LANES), F32),
                   jax.ShapeDtypeStruct((m * TOP_K,), jnp.int32)],
        scratch_shapes=[pltpu.VMEM((rb * ROW_TILES, LANES), F32),
                        pltpu.SemaphoreType.DMA(()), pltpu.SemaphoreType.DMA(())],
        compiler_params=_cparams(("arbitrary",)),
        name="dispatch",
    )(seg_start, zero_flag, idx, rank, x1r)


def _experts_kernel(rb, blk_ref, be_ref, nu_ref, xg_ref, wg_ref, wu_ref, wd_ref, y_ref):
    del blk_ref, be_ref

    @pl.when(pl.program_id(0) < nu_ref[0])
    def _():
        xb = jnp.concatenate(
            [xg_ref[pl.ds(s, rb, stride=ROW_TILES), :] for s in range(ROW_TILES)],
            axis=1).astype(BF16)
        g = _dot(xb, wg_ref[0].astype(BF16))
        u = _dot(xb, wu_ref[0].astype(BF16))
        hid = (g * _sigmoid(g) * u).astype(BF16)
        y = _dot(hid, wd_ref[0].astype(BF16))
        for s in range(ROW_TILES):
            y_ref[pl.ds(s, rb, stride=ROW_TILES), :] = y[:, s * LANES:(s + 1) * LANES]


def _experts(blk_map, block_e, n_used, xg, w_eg, w_eu, w_ed, rb):
    n_blocks = blk_map.shape[0]
    grid_spec = pltpu.PrefetchScalarGridSpec(
        num_scalar_prefetch=3,
        grid=(n_blocks,),
        in_specs=[
            pl.BlockSpec((rb * ROW_TILES, LANES), lambda i, blk, be, nu: (blk[i], 0)),
            pl.BlockSpec((1, D_MODEL, D_EXPERT), lambda i, blk, be, nu: (be[i], 0, 0)),
            pl.BlockSpec((1, D_MODEL, D_EXPERT), lambda i, blk, be, nu: (be[i], 0, 0)),
            pl.BlockSpec((1, D_EXPERT, D_MODEL), lambda i, blk, be, nu: (be[i], 0, 0)),
        ],
        out_specs=pl.BlockSpec((rb * ROW_TILES, LANES), lambda i, blk, be, nu: (blk[i], 0)),
    )
    return pl.pallas_call(
        functools.partial(_experts_kernel, rb),
        grid_spec=grid_spec,
        out_shape=jax.ShapeDtypeStruct(xg.shape, F32),
        input_output_aliases={3: 0},
        compiler_params=_cparams(("arbitrary",)),
        name="experts",
    )(blk_map, block_e, n_used, xg, w_eg, w_eu, w_ed)


def _combine_kernel(alpha, n_p, n_steps, dest_ref, dnext_ref, gate_ref, x1_ref, y_ref,
                    wsg_ref, wsu_ref, wsd_ref, g_ref, b_ref, x2p_ref, x2s_ref,
                    ybuf_a, ybuf_b, sem_a, sem_b):
    step = pl.program_id(0)
    tt = x1_ref.shape[0] // 2
    n_rows = TOP_K * tt * ROW_TILES

    def issue(dref, half, buf, sem):
        def body(t, carry):
            for kk in range(TOP_K):
                _row_copy(y_ref, dref[(half * tt + t) * TOP_K + kk], buf,
                          (kk * tt + t) * ROW_TILES, sem).start()
            return carry

        lax.fori_loop(0, tt, body, 0)

    def wait(buf, sem):
        pltpu.make_async_copy(y_ref.at[pl.ds(0, n_rows)], buf, sem).wait()

    def routed(half, buf):
        gates = gate_ref[half * tt:(half + 1) * tt, :]
        chunks = []
        for s in range(ROW_TILES):
            acc = jnp.zeros((tt, LANES), F32)
            for kk in range(TOP_K):
                acc = acc + gates[:, kk:kk + 1] * buf[pl.ds(kk * tt * ROW_TILES + s, tt,
                                                            stride=ROW_TILES), :]
            chunks.append(acc)
        return jnp.concatenate(chunks, axis=1)

    @pl.when(step == 0)
    def _():
        issue(dest_ref, 0, ybuf_a, sem_a)

    issue(dest_ref, 1, ybuf_b, sem_b)

    x1 = x1_ref[...]
    xb = x1.astype(BF16)
    gs = _dot(xb, wsg_ref[...])
    us = _dot(xb, wsu_ref[...])
    base = alpha * x1 + _dot((gs * _sigmoid(gs) * us).astype(BF16), wsd_ref[...])

    wait(ybuf_a, sem_a)
    x2a = _layer_norm(base[:tt, :] + routed(0, ybuf_a), g_ref[...], b_ref[...])

    @pl.when(step + 1 < n_steps)
    def _():
        issue(dnext_ref, 0, ybuf_a, sem_a)

    wait(ybuf_b, sem_b)
    x2b = _layer_norm(base[tt:, :] + routed(1, ybuf_b), g_ref[...], b_ref[...])

    @pl.when(step < n_p)
    def _():
        x2p_ref[:tt, :] = x2a
        x2p_ref[tt:, :] = x2b

    @pl.when(step >= n_p)
    def _():
        x2s_ref[:tt, :] = x2a
        x2s_ref[tt:, :] = x2b


def _combine(dest8, gates, x1, y, wsg, wsu, wsd, g, b, alpha, tt, mp):
    m = x1.shape[0]
    st = 2 * tt
    n_steps = m // st
    n_p = mp // st
    return pl.pallas_call(
        functools.partial(_combine_kernel, alpha, n_p, n_steps),
        grid=(n_steps,),
        in_specs=[pl.BlockSpec((st * TOP_K,), lambda i: (i,), memory_space=pltpu.SMEM),
                  pl.BlockSpec((st * TOP_K,), lambda i: (jnp.minimum(i + 1, n_steps - 1),),
                               memory_space=pltpu.SMEM),
                  pl.BlockSpec((st, TOP_K), lambda i: (i, 0)),
                  pl.BlockSpec((st, D_MODEL), lambda i: (i, 0)),
                  pl.BlockSpec(memory_space=pl.ANY),
                  pl.BlockSpec((D_MODEL, D_EXPERT), lambda i: (0, 0)),
                  pl.BlockSpec((D_MODEL, D_EXPERT), lambda i: (0, 0)),
                  pl.BlockSpec((D_EXPERT, D_MODEL), lambda i: (0, 0)),
                  pl.BlockSpec((1, D_MODEL), lambda i: (0, 0)),
                  pl.BlockSpec((1, D_MODEL), lambda i: (0, 0))],
        out_specs=[pl.BlockSpec((st, D_MODEL), lambda i: (jnp.minimum(i, n_p - 1), 0)),
                   pl.BlockSpec((st, D_MODEL), lambda i: (jnp.maximum(i - n_p, 0), 0))],
        out_shape=[jax.ShapeDtypeStruct((mp, D_MODEL), F32),
                   jax.ShapeDtypeStruct((m - mp, D_MODEL), F32)],
        scratch_shapes=[pltpu.VMEM((TOP_K * tt * ROW_TILES, LANES), F32),
                        pltpu.VMEM((TOP_K * tt * ROW_TILES, LANES), F32),
                        pltpu.SemaphoreType.DMA(()), pltpu.SemaphoreType.DMA(())],
        compiler_params=_cparams(("arbitrary",)),
        name="combine",
    )(dest8, dest8, gates, x1, y, wsg, wsu, wsd, g, b)


PROJ_TM = 512
PROJ_TN = 1024
MIX_TT = 256
MIX_CHUNK = 128
ROUTER_TT = 256
DISPATCH_TT = 512
EXPERT_RB = 256
COMBINE_TT = 128


def _moe(x1, x1r, mp, w_router, b_router, w_eg, w_eu, w_ed, w_sg, w_su, w_sd, ln_g, ln_b, alpha):
    m = x1.shape[0]
    wr_hi, wr_lo = _split_hi_lo(w_router)
    idx, gates, rank, cnt = _router(x1, wr_hi, wr_lo, b_router[None, :], ROUTER_TT)

    rb = EXPERT_RB
    n_rows = m * TOP_K
    n_blocks = (n_rows + N_EXPERTS * (rb - 1) + rb - 1) // rb
    counts = cnt[0].astype(jnp.int32)
    padded = (counts + rb - 1) // rb * rb
    pad_end = jnp.cumsum(padded)
    pad_start = pad_end - padded
    n_used = pad_end[-1] // rb
    blk_map = jnp.minimum(jnp.arange(n_blocks, dtype=jnp.int32), n_used - 1)
    block_e = jnp.minimum(
        jnp.sum((pad_end[None, :] <= (blk_map * rb)[:, None]).astype(jnp.int32), axis=1),
        N_EXPERTS - 1)

    blk = jnp.arange(n_blocks, dtype=jnp.int32)

    def segments_ended(row):
        return jnp.sum((pad_end[None, :] <= row[:, None]).astype(jnp.int32), axis=1)

    zero_flag = jnp.logical_or(
        blk >= n_used, segments_ended((blk + 1) * rb) != segments_ended(blk * rb)
    ).astype(jnp.int32)

    xg, dest8 = _dispatch(pad_start, zero_flag, idx.reshape(-1), rank.reshape(-1), x1r,
                          DISPATCH_TT, rb)
    y = _experts(blk_map, block_e, n_used.reshape(1).astype(jnp.int32), xg,
                 w_eg, w_eu, w_ed, rb)
    return _combine(dest8, gates, x1, y, w_sg.astype(BF16), w_su.astype(BF16),
                    w_sd.astype(BF16), ln_g[None, :], ln_b[None, :], alpha, COMBINE_TT, mp)


def _layer(xp, xs, conv_s, c_s, n_s, m_s, w_in, b_gates, w_conv, mh_gain, w_out, ln1_g, ln1_b,
           w_router, b_router, w_eg, w_eu, w_ed, w_sg, w_su, w_sd, ln2_g, ln2_b, alpha):
    bp, tp, _ = xp.shape
    bs, ts, _ = xs.shape
    mp, ms = bp * tp, bs * ts
    xp2 = xp.reshape(mp, D_MODEL)
    xs2 = xs.reshape(ms, D_MODEL)

    w_main = jnp.concatenate([w_in[:, :GATE_COL0], w_in[:, GATE_COL0 + 2 * N_HEADS:]],
                             axis=1).astype(BF16)
    w_gate = jnp.pad(w_in[:, GATE_COL0:GATE_COL0 + 2 * N_HEADS],
                     ((0, 0), (0, LANES - 2 * N_HEADS)))
    wg_hi, wg_lo = _split_hi_lo(w_gate)

    z_s = _proj(xs2, w_main, PROJ_TM, PROJ_TN)
    zg_s = _gate_proj(xs2, wg_hi, wg_lo, PROJ_TM)
    gt_s = zg_s[:, :SUBLANES].reshape(bs, ts, SUBLANES).transpose(0, 2, 1)
    mix_s, conv_n, c_n, n_n, m_n = _mixer(
        z_s, zg_s, gt_s, b_gates, w_conv, mh_gain[None, :],
        conv_s, c_s, n_s, m_s[:, None, :], nb=bs, nt=1, tt=ts, chunk=ts)

    x1, x1r, conv_p, c_p, n_p, m_p = _prompt_layer_half(
        xp2, xs2, mix_s, b_gates, w_main, wg_hi, wg_lo, w_conv, mh_gain[None, :],
        w_out.astype(BF16), ln1_g[None, :], ln1_b[None, :], alpha,
        nb=bp, nt=tp // MIX_TT, tt=MIX_TT, chunk=MIX_CHUNK)

    x2p, x2s = _moe(x1, x1r, mp, w_router, b_router, w_eg, w_eu, w_ed, w_sg, w_su, w_sd,
                    ln2_g, ln2_b, alpha)
    states_p = (conv_p, c_p, n_p, m_p[:, 0, :])
    states_s = (conv_n, c_n, n_n, m_n[:, 0, :])
    return x2p.reshape(bp, tp, D_MODEL), x2s.reshape(bs, ts, D_MODEL), states_p, states_s


def kernel(x_prompt, x_sample, cache_conv, state_mlstm_C, state_mlstm_n, state_mlstm_m, w_in, b_gates, w_conv, mh_gain, w_out, ln1_g, ln1_b, w_router, b_router, w_exp_gate, w_exp_up, w_exp_down, w_sh_gate, w_sh_up, w_sh_down, ln2_g, ln2_b):
    depth = w_in.shape[0]
    alpha = (2.0 * depth) ** 0.25
    hp, hs = x_prompt, x_sample
    outs_p = [[], [], [], []]
    outs_s = [[], [], [], []]
    for l in range(depth):
        hp, hs, st_p, st_s = _layer(
            hp, hs, cache_conv[l], state_mlstm_C[l], state_mlstm_n[l], state_mlstm_m[l],
            w_in[l], b_gates[l], w_conv[l], mh_gain[l], w_out[l], ln1_g[l], ln1_b[l],
            w_router[l], b_router[l], w_exp_gate[l], w_exp_up[l], w_exp_down[l],
            w_sh_gate[l], w_sh_up[l], w_sh_down[l], ln2_g[l], ln2_b[l], alpha)
        for acc, val in zip(outs_p, st_p):
            acc.append(val)
        for acc, val in zip(outs_s, st_s):
            acc.append(val)
    return (hp, hs) + tuple(jnp.stack(a) for a in outs_p) + tuple(jnp.stack(a) for a in outs_s)
```

```python
import functools

import jax
import jax.numpy as jnp
from jax import lax
from jax.experimental import pallas as pl
from jax.experimental.pallas import tpu as pltpu

F32 = jnp.float32
BF16 = jnp.bfloat16

D_MODEL = 1024
N_HEADS = 4
HEAD_V = 256
HEAD_QK = 128
N_EXPERTS = 256
TOP_K = 8
D_EXPERT = 256
ROUTED_SCALE = 2.5
LN_EPS = 1e-5
CONV_K = 3

LANES = 128
SUBLANES = 8
ROW_TILES = D_MODEL // LANES
GATE_COL0 = 6 * D_MODEL

VMEM_LIMIT = 56 * 1024 * 1024


def _cparams(sem):
    return pltpu.CompilerParams(dimension_semantics=sem, vmem_limit_bytes=VMEM_LIMIT)


def _sigmoid(x):
    return 1.0 / (1.0 + jnp.exp(-x))


def _log_sigmoid(x):
    return jnp.minimum(x, 0.0) - jnp.log(1.0 + jnp.exp(-jnp.abs(x)))


def _layer_norm(x, g, b):
    mu = jnp.mean(x, axis=-1, keepdims=True)
    xc = x - mu
    var = jnp.mean(xc * xc, axis=-1, keepdims=True)
    return xc * lax.rsqrt(var + LN_EPS) * g + b


def _split_hi_lo(x):
    hi = x.astype(BF16)
    lo = (x - hi.astype(F32)).astype(BF16)
    return hi, lo


def _dot(a, b):
    return jnp.dot(a, b, preferred_element_type=F32)


def _dot3(x, w_hi, w_lo):
    x_hi, x_lo = _split_hi_lo(x)
    return _dot(x_hi, w_hi) + _dot(x_lo, w_hi) + _dot(x_hi, w_lo)


def _proj_kernel(x_ref, w_ref, z_ref):
    z_ref[...] = _dot(x_ref[...].astype(BF16), w_ref[...])


def _proj(x, w, tm, tn):
    m, k = x.shape
    n = w.shape[1]
    return pl.pallas_call(
        _proj_kernel,
        grid=(m // tm, n // tn),
        in_specs=[pl.BlockSpec((tm, k), lambda i, j: (i, 0)),
                  pl.BlockSpec((k, tn), lambda i, j: (0, j))],
        out_specs=pl.BlockSpec((tm, tn), lambda i, j: (i, j)),
        out_shape=jax.ShapeDtypeStruct((m, n), F32),
        compiler_params=_cparams(("parallel", "arbitrary")),
        name="proj",
    )(x, w)


def _gate_proj_kernel(x_ref, wh_ref, wl_ref, z_ref):
    z_ref[...] = _dot3(x_ref[...], wh_ref[...], wl_ref[...])


def _gate_proj(x, w_hi, w_lo, tm):
    m, k = x.shape
    n = w_hi.shape[1]
    return pl.pallas_call(
        _gate_proj_kernel,
        grid=(m // tm,),
        in_specs=[pl.BlockSpec((tm, k), lambda i: (i, 0)),
                  pl.BlockSpec((k, n), lambda i: (0, 0)),
                  pl.BlockSpec((k, n), lambda i: (0, 0))],
        out_specs=pl.BlockSpec((tm, n), lambda i: (i, 0)),
        out_shape=jax.ShapeDtypeStruct((m, n), F32),
        compiler_params=_cparams(("parallel",)),
        name="gate_proj",
    )(x, w_hi, w_lo)


def _conv_branch(u, zcb, wc, conv_ref):
    tt = u.shape[0]
    carry = conv_ref[0]
    rows = lax.broadcasted_iota(jnp.int32, (tt, D_MODEL), 0)
    u1 = jnp.where(rows == 0, carry[1:2, :], pltpu.roll(u, 1, 0))
    u2 = jnp.where(rows == 0, carry[0:1, :],
                   jnp.where(rows == 1, carry[1:2, :], pltpu.roll(u, 2, 0)))
    conv_out = u2 * wc[0:1, :] + u1 * wc[1:2, :] + u * wc[2:3, :]
    conv_ref[0] = u[tt - 2:tt, :]
    return zcb * conv_out


def _mlstm_chunk(q, k, v, ig_col, lf_col, ig_row, lf_row, c_state, n_row, m_prev):
    chunk = q.shape[0]
    ti = lax.broadcasted_iota(jnp.int32, (chunk, chunk), 0)
    si = lax.broadcasted_iota(jnp.int32, (chunk, chunk), 1)
    causal = si <= ti
    b_col = jnp.sum(jnp.where(causal, lf_row, 0.0), axis=1, keepdims=True)
    b_row = jnp.sum(jnp.where(ti <= si, lf_col, 0.0), axis=0, keepdims=True)
    inter = b_col + m_prev
    dmat = jnp.where(causal, b_col - b_row + ig_row, -jnp.inf)
    m_t = jnp.maximum(inter, jnp.max(dmat, axis=1, keepdims=True))
    w_intra = jnp.exp(dmat - m_t)
    w_inter = jnp.exp(inter - m_t)
    qb = q.astype(BF16)
    kb = k.astype(BF16)
    vb = v.astype(BF16)
    s = lax.dot_general(qb, kb, (((1,), (1,)), ((), ())), preferred_element_type=F32) * w_intra
    num = w_inter * _dot(qb, c_state.astype(BF16)) + _dot(s.astype(BF16), vb)
    den = (w_inter * jnp.sum(q * n_row, axis=1, keepdims=True)
           + jnp.sum(s, axis=1, keepdims=True))
    h = num / jnp.maximum(jnp.abs(den), jnp.exp(-m_t))
    m_new = m_t[chunk - 1:chunk, :]
    b_last = b_col[chunk - 1:chunk, :]
    ws_col = jnp.exp(b_last - b_col + ig_col - m_new)
    cdecay = jnp.exp(inter[chunk - 1:chunk, :] - m_new)
    kw = k * ws_col
    c_new = cdecay * c_state + lax.dot_general(
        kw.astype(BF16), vb, (((0,), (0,)), ((), ())), preferred_element_type=F32)
    n_new = cdecay * n_row + jnp.sum(kw, axis=0, keepdims=True)
    return h, c_new, n_new, m_new


def _mlstm_branch(chunk, bg_ref, zg, gt, zqk, zv, zog, gain_ref, c_ref, n_ref, m_ref, mo_ref):
    tt = zqk.shape[0]
    for h in range(N_HEADS):
        ig_col_all = zg[:, h:h + 1] + bg_ref[h]
        lf_col_all = _log_sigmoid(zg[:, N_HEADS + h:N_HEADS + h + 1] + bg_ref[N_HEADS + h])
        ig_row_all = gt[h:h + 1, :] + bg_ref[h]
        lf_row_all = _log_sigmoid(gt[N_HEADS + h:N_HEADS + h + 1, :] + bg_ref[N_HEADS + h])
        vcols = slice(h * HEAD_V, (h + 1) * HEAD_V)
        for c in range(tt // chunk):
            r0, r1 = c * chunk, (c + 1) * chunk
            q = zqk[r0:r1, h * HEAD_QK:(h + 1) * HEAD_QK]
            k = zqk[r0:r1, (N_HEADS + h) * HEAD_QK:(N_HEADS + h + 1) * HEAD_QK] * (HEAD_QK ** -0.5)
            hh, c_new, n_new, m_new = _mlstm_chunk(
                q, k, zv[r0:r1, vcols], ig_col_all[r0:r1, :], lf_col_all[r0:r1, :],
                ig_row_all[:, r0:r1], lf_row_all[:, r0:r1],
                c_ref[0, h], n_ref[0, h:h + 1, :], m_ref[0, :, h:h + 1])
            c_ref[0, h] = c_new
            n_ref[0, h:h + 1, :] = n_new
            m_ref[0, :, h:h + 1] = m_new
            mu = jnp.mean(hh, axis=-1, keepdims=True)
            hc = hh - mu
            var = jnp.mean(hc * hc, axis=-1, keepdims=True)
            hn = hc * lax.rsqrt(var + LN_EPS) * gain_ref[:, vcols]
            mo_ref[r0:r1, vcols] = hn * _sigmoid(zog[r0:r1, vcols])


def _store_x1(x, mix, wout_ref, g_ref, b_ref, alpha, x1_ref, x1r_ref):
    tt = x.shape[0]
    x1 = _layer_norm(alpha * x + _dot(mix.astype(BF16), wout_ref[...]), g_ref[...], b_ref[...])
    x1_ref[...] = x1
    for s in range(ROW_TILES):
        x1r_ref[pl.ds(s, tt, stride=ROW_TILES), :] = x1[:, s * LANES:(s + 1) * LANES]


def _mixer_kernel(chunk, bg_ref,
                  zcb_ref, zcc_ref, zch_ref, zqk_ref, zv_ref, zog_ref, zga_ref, zgb_ref,
                  zg_ref, gt_ref, wconv_ref, gain_ref,
                  conv0_ref, c0_ref, n0_ref, m0_ref,
                  mix_ref, conv_ref, c_ref, n_ref, m_ref, mo_ref):
    @pl.when(pl.program_id(1) == 0)
    def _():
        conv_ref[...] = conv0_ref[...]
        c_ref[...] = c0_ref[...]
        n_ref[...] = n0_ref[...]
        m_ref[...] = m0_ref[...]

    a = _conv_branch(zcc_ref[...] * zch_ref[...], zcb_ref[...], wconv_ref[...], conv_ref)
    _mlstm_branch(chunk, bg_ref, zg_ref[...], gt_ref[0], zqk_ref[...], zv_ref[...], zog_ref[...],
                  gain_ref, c_ref, n_ref, m_ref, mo_ref)
    mix_ref[...] = _sigmoid(zga_ref[...]) * a + _sigmoid(zgb_ref[...]) * mo_ref[...]


def _mixer(z, zg, gt3, b_gates, w_conv, mh_gain, conv0, c0, n0, m0, *, nb, nt, tt, chunk):
    def zspec(j):
        return pl.BlockSpec((tt, D_MODEL), lambda b, t, j=j: (b * nt + t, j))

    in_specs = [pl.BlockSpec(memory_space=pltpu.SMEM)]
    in_specs += [zspec(j) for j in range(8)]
    in_specs += [
        pl.BlockSpec((tt, LANES), lambda b, t: (b * nt + t, 0)),
        pl.BlockSpec((1, SUBLANES, tt), lambda b, t: (b * nt + t, 0, 0)),
        pl.BlockSpec((CONV_K, D_MODEL), lambda b, t: (0, 0)),
        pl.BlockSpec((1, D_MODEL), lambda b, t: (0, 0)),
        pl.BlockSpec((1, CONV_K - 1, D_MODEL), lambda b, t: (b, 0, 0)),
        pl.BlockSpec((1, N_HEADS, HEAD_QK, HEAD_V), lambda b, t: (b, 0, 0, 0)),
        pl.BlockSpec((1, N_HEADS, HEAD_QK), lambda b, t: (b, 0, 0)),
        pl.BlockSpec((1, 1, N_HEADS), lambda b, t: (b, 0, 0)),
    ]
    out_specs = [
        pl.BlockSpec((tt, D_MODEL), lambda b, t: (b * nt + t, 0)),
        pl.BlockSpec((1, CONV_K - 1, D_MODEL), lambda b, t: (b, 0, 0)),
        pl.BlockSpec((1, N_HEADS, HEAD_QK, HEAD_V), lambda b, t: (b, 0, 0, 0)),
        pl.BlockSpec((1, N_HEADS, HEAD_QK), lambda b, t: (b, 0, 0)),
        pl.BlockSpec((1, 1, N_HEADS), lambda b, t: (b, 0, 0)),
    ]
    out_shape = [
        jax.ShapeDtypeStruct((nb * nt * tt, D_MODEL), F32),
        jax.ShapeDtypeStruct((nb, CONV_K - 1, D_MODEL), F32),
        jax.ShapeDtypeStruct((nb, N_HEADS, HEAD_QK, HEAD_V), F32),
        jax.ShapeDtypeStruct((nb, N_HEADS, HEAD_QK), F32),
        jax.ShapeDtypeStruct((nb, 1, N_HEADS), F32),
    ]
    return pl.pallas_call(
        functools.partial(_mixer_kernel, chunk),
        grid=(nb, nt),
        in_specs=in_specs,
        out_specs=out_specs,
        out_shape=out_shape,
        scratch_shapes=[pltpu.VMEM((tt, D_MODEL), F32)],
        compiler_params=_cparams(("arbitrary", "arbitrary")),
        name="mixer_t%d" % tt,
    )(b_gates, z, z, z, z, z, z, z, z, zg, gt3, w_conv, mh_gain, conv0, c0, n0, m0)


def _prompt_kernel(chunk, nb, ns, alpha, bg_ref, x_ref, w_ref, wgh_ref, wgl_ref, wconv_ref,
                   gain_ref, wout_ref, g_ref, b_ref, xs_ref, mixs_ref,
                   x1_ref, x1r_ref, conv_ref, c_ref, n_ref, m_ref, mo_ref):
    b = pl.program_id(0)
    t = pl.program_id(1)

    @pl.when(b < nb)
    def _():
        @pl.when(t == 0)
        def _():
            conv_ref[...] = jnp.zeros_like(conv_ref)
            c_ref[...] = jnp.zeros_like(c_ref)
            n_ref[...] = jnp.zeros_like(n_ref)
            m_ref[...] = jnp.zeros_like(m_ref)

        x = x_ref[...]
        xb = x.astype(BF16)

        def z(j):
            return _dot(xb, w_ref[:, j * D_MODEL:(j + 1) * D_MODEL])

        a = _conv_branch(z(1) * z(2), z(0), wconv_ref[...], conv_ref)
        zg = _dot3(x, wgh_ref[...], wgl_ref[...])
        _mlstm_branch(chunk, bg_ref, zg, zg.T, z(3), z(4), z(5), gain_ref,
                      c_ref, n_ref, m_ref, mo_ref)
        mix = _sigmoid(z(6)) * a + _sigmoid(z(7)) * mo_ref[...]
        _store_x1(x, mix, wout_ref, g_ref, b_ref, alpha, x1_ref, x1r_ref)

    @pl.when(jnp.logical_and(b == nb, t < ns))
    def _():
        _store_x1(xs_ref[...], mixs_ref[...], wout_ref, g_ref, b_ref, alpha, x1_ref, x1r_ref)


def _prompt_layer_half(xp, xs, mix_s, b_gates, w_main, wg_hi, wg_lo, w_conv, mh_gain, w_out,
                       ln_g, ln_b, alpha, *, nb, nt, tt, chunk):
    mp, ms = xp.shape[0], xs.shape[0]
    ns = ms // tt
    last_p = nb * nt - 1

    def prow(b, t):
        return jnp.minimum(b * nt + t, last_p)

    def srow(b, t):
        return jnp.where(b == nb, jnp.minimum(t, ns - 1), 0)

    def orow(b, t):
        return jnp.where(b == nb, nb * nt + jnp.minimum(t, ns - 1), b * nt + t)

    def state(b):
        return jnp.minimum(b, nb - 1)

    const2 = lambda b, t: (0, 0)
    in_specs = [
        pl.BlockSpec(memory_space=pltpu.SMEM),
        pl.BlockSpec((tt, D_MODEL), lambda b, t: (prow(b, t), 0)),
        pl.BlockSpec(w_main.shape, const2, pipeline_mode=pl.Buffered(1)),
        pl.BlockSpec(wg_hi.shape, const2),
        pl.BlockSpec(wg_lo.shape, const2),
        pl.BlockSpec((CONV_K, D_MODEL), const2),
        pl.BlockSpec((1, D_MODEL), const2),
        pl.BlockSpec((D_MODEL, D_MODEL), const2),
        pl.BlockSpec((1, D_MODEL), const2),
        pl.BlockSpec((1, D_MODEL), const2),
        pl.BlockSpec((tt, D_MODEL), lambda b, t: (srow(b, t), 0)),
        pl.BlockSpec((tt, D_MODEL), lambda b, t: (srow(b, t), 0)),
    ]
    out_specs = [
        pl.BlockSpec((tt, D_MODEL), lambda b, t: (orow(b, t), 0)),
        pl.BlockSpec((tt * ROW_TILES, LANES), lambda b, t: (orow(b, t), 0)),
        pl.BlockSpec((1, CONV_K - 1, D_MODEL), lambda b, t: (state(b), 0, 0)),
        pl.BlockSpec((1, N_HEADS, HEAD_QK, HEAD_V), lambda b, t: (state(b), 0, 0, 0)),
        pl.BlockSpec((1, N_HEADS, HEAD_QK), lambda b, t: (state(b), 0, 0)),
        pl.BlockSpec((1, 1, N_HEADS), lambda b, t: (state(b), 0, 0)),
    ]
    out_shape = [
        jax.ShapeDtypeStruct((mp + ms, D_MODEL), F32),
        jax.ShapeDtypeStruct(((mp + ms) * ROW_TILES, LANES), F32),
        jax.ShapeDtypeStruct((nb, CONV_K - 1, D_MODEL), F32),
        jax.ShapeDtypeStruct((nb, N_HEADS, HEAD_QK, HEAD_V), F32),
        jax.ShapeDtypeStruct((nb, N_HEADS, HEAD_QK), F32),
        jax.ShapeDtypeStruct((nb, 1, N_HEADS), F32),
    ]
    return pl.pallas_call(
        functools.partial(_prompt_kernel, chunk, nb, ns, alpha),
        grid=(nb + 1, nt),
        in_specs=in_specs,
        out_specs=out_specs,
        out_shape=out_shape,
        scratch_shapes=[pltpu.VMEM((tt, D_MODEL), F32)],
        compiler_params=_cparams(("arbitrary", "arbitrary")),
        name="prompt_half",
    )(b_gates, xp, w_main, wg_hi, wg_lo, w_conv, mh_gain, w_out, ln_g, ln_b, xs, mix_s)


def _router_kernel(x_ref, wh_ref, wl_ref, br_ref, idx_ref, gate_ref, rank_ref, cnt_ref):
    tt = x_ref.shape[0]

    @pl.when(pl.program_id(0) == 0)
    def _():
        cnt_ref[...] = jnp.zeros_like(cnt_ref)

    scores = _sigmoid(_dot3(x_ref[...], wh_ref[...], wl_ref[...]))
    lane = lax.broadcasted_iota(jnp.int32, (tt, N_EXPERTS), 1)
    work = scores + br_ref[...]
    picked = []
    sel = []
    member = jnp.zeros((tt, N_EXPERTS), F32)
    for _ in range(TOP_K):
        mx = jnp.max(work, axis=1, keepdims=True)
        ik = jnp.min(jnp.where(work == mx, lane, N_EXPERTS), axis=1, keepdims=True)
        onehot = lane == ik
        picked.append((ik, onehot))
        sel.append(jnp.sum(jnp.where(onehot, scores, 0.0), axis=1, keepdims=True))
        work = jnp.where(onehot, -jnp.inf, work)
        member = member + onehot.astype(F32)
    total = sel[0]
    for sk in sel[1:]:
        total = total + sk

    ti = lax.broadcasted_iota(jnp.int32, (tt, tt), 0)
    si = lax.broadcasted_iota(jnp.int32, (tt, tt), 1)
    earlier = (si < ti).astype(BF16)
    before = _dot(earlier, member.astype(BF16)) + cnt_ref[...]

    lane_o = lax.broadcasted_iota(jnp.int32, (tt, TOP_K), 1)
    idx_o = jnp.zeros((tt, TOP_K), jnp.int32)
    gate_o = jnp.zeros((tt, TOP_K), F32)
    rank_o = jnp.zeros((tt, TOP_K), jnp.int32)
    for kk in range(TOP_K):
        ik, onehot = picked[kk]
        rk = jnp.sum(jnp.where(onehot, before, 0.0), axis=1, keepdims=True)
        idx_o = jnp.where(lane_o == kk, ik, idx_o)
        gate_o = jnp.where(lane_o == kk, sel[kk] / total * ROUTED_SCALE, gate_o)
        rank_o = jnp.where(lane_o == kk, rk.astype(jnp.int32), rank_o)
    idx_ref[...] = idx_o
    gate_ref[...] = gate_o
    rank_ref[...] = rank_o
    cnt_ref[...] = cnt_ref[...] + jnp.sum(member, axis=0, keepdims=True)


def _router(x1, wr_hi, wr_lo, b_router, tt):
    m = x1.shape[0]
    return pl.pallas_call(
        _router_kernel,
        grid=(m // tt,),
        in_specs=[pl.BlockSpec((tt, D_MODEL), lambda i: (i, 0)),
                  pl.BlockSpec((D_MODEL, N_EXPERTS), lambda i: (0, 0)),
                  pl.BlockSpec((D_MODEL, N_EXPERTS), lambda i: (0, 0)),
                  pl.BlockSpec((1, N_EXPERTS), lambda i: (0, 0))],
        out_specs=[pl.BlockSpec((tt, TOP_K), lambda i: (i, 0)),
                   pl.BlockSpec((tt, TOP_K), lambda i: (i, 0)),
                   pl.BlockSpec((tt, TOP_K), lambda i: (i, 0)),
                   pl.BlockSpec((1, N_EXPERTS), lambda i: (0, 0))],
        out_shape=[jax.ShapeDtypeStruct((m, TOP_K), jnp.int32),
                   jax.ShapeDtypeStruct((m, TOP_K), F32),
                   jax.ShapeDtypeStruct((m, TOP_K), jnp.int32),
                   jax.ShapeDtypeStruct((1, N_EXPERTS), F32)],
        compiler_params=_cparams(("arbitrary",)),
        name="router",
    )(x1, wr_hi, wr_lo, b_router)


def _row_copy(src, src_row8, dst, dst_row8, sem):
    return pltpu.make_async_copy(src.at[pl.ds(pl.multiple_of(src_row8, SUBLANES), SUBLANES)],
                                 dst.at[pl.ds(pl.multiple_of(dst_row8, SUBLANES), SUBLANES)],
                                 sem)


def _dispatch_kernel(tt, start_ref, idx_ref, rank_ref, x1r_ref, xg_ref, dest_ref, sem):
    def issue(t, carry):
        for kk in range(TOP_K):
            j = t * TOP_K + kk
            dest8 = (start_ref[idx_ref[j]] + rank_ref[j]) * ROW_TILES
            dest_ref[j] = dest8
            _row_copy(x1r_ref, t * ROW_TILES, xg_ref, dest8, sem).start()
        return carry

    lax.fori_loop(0, tt, issue, 0)
    n_rows = tt * TOP_K * ROW_TILES
    pltpu.make_async_copy(xg_ref.at[pl.ds(0, n_rows)], xg_ref.at[pl.ds(0, n_rows)], sem).wait()


def _dispatch(seg_start, idx, rank, x1r, tt):
    m = x1r.shape[0] // ROW_TILES
    smem_blk = pl.BlockSpec((tt * TOP_K,), lambda i: (i,), memory_space=pltpu.SMEM)
    return pl.pallas_call(
        functools.partial(_dispatch_kernel, tt),
        grid=(m // tt,),
        in_specs=[pl.BlockSpec(memory_space=pltpu.SMEM), smem_blk, smem_blk,
                  pl.BlockSpec((tt * ROW_TILES, LANES), lambda i: (i, 0))],
        out_specs=[pl.BlockSpec(memory_space=pl.ANY), smem_blk],
        out_shape=[jax.ShapeDtypeStruct((m * TOP_K * ROW_TILES, LANES), F32),
                   jax.ShapeDtypeStruct((m * TOP_K,), jnp.int32)],
        scratch_shapes=[pltpu.SemaphoreType.DMA(())],
        compiler_params=_cparams(("arbitrary",)),
        name="dispatch",
    )(seg_start, idx, rank, x1r)


def _experts_kernel(rb, blk_ref, e_ref, lo_ref, hi_ref, xg_ref, wg_ref, wu_ref, wd_ref, y_ref,
                    wgb, wub, wdb):
    j = pl.program_id(0)
    prev = jnp.maximum(j - 1, 0)
    lo = lo_ref[j]
    hi = hi_ref[j]

    @pl.when(jnp.logical_or(j == 0, e_ref[j] != e_ref[prev]))
    def _():
        wgb[...] = wg_ref[0].astype(BF16)
        wub[...] = wu_ref[0].astype(BF16)
        wdb[...] = wd_ref[0].astype(BF16)

    @pl.when(hi > lo)
    def _():
        xb = jnp.concatenate(
            [xg_ref[pl.ds(s, rb, stride=ROW_TILES), :] for s in range(ROW_TILES)],
            axis=1).astype(BF16)
        g = _dot(xb, wgb[...])
        u = _dot(xb, wub[...])
        hid = (g * _sigmoid(g) * u).astype(BF16)
        y = _dot(hid, wdb[...])
        rows = lax.broadcasted_iota(jnp.int32, (rb, LANES), 0)
        mine = jnp.logical_and(rows >= lo, rows < hi)
        first = jnp.logical_or(j == 0, blk_ref[j] != blk_ref[prev])

        @pl.when(first)
        def _():
            for s in range(ROW_TILES):
                y_ref[pl.ds(s, rb, stride=ROW_TILES), :] = jnp.where(
                    mine, y[:, s * LANES:(s + 1) * LANES], 0.0)

        @pl.when(jnp.logical_not(first))
        def _():
            for s in range(ROW_TILES):
                rows_s = pl.ds(s, rb, stride=ROW_TILES)
                y_ref[rows_s, :] = jnp.where(mine, y[:, s * LANES:(s + 1) * LANES],
                                             y_ref[rows_s, :])


def _experts(item_blk, item_e, item_lo, item_hi, xg, w_eg, w_eu, w_ed, rb):
    n_items = item_blk.shape[0]
    grid_spec = pltpu.PrefetchScalarGridSpec(
        num_scalar_prefetch=4,
        grid=(n_items,),
        in_specs=[
            pl.BlockSpec((rb * ROW_TILES, LANES), lambda i, blk, e, lo, hi: (blk[i], 0)),
            pl.BlockSpec((1, D_MODEL, D_EXPERT), lambda i, blk, e, lo, hi: (e[i], 0, 0)),
            pl.BlockSpec((1, D_MODEL, D_EXPERT), lambda i, blk, e, lo, hi: (e[i], 0, 0)),
            pl.BlockSpec((1, D_EXPERT, D_MODEL), lambda i, blk, e, lo, hi: (e[i], 0, 0)),
        ],
        out_specs=pl.BlockSpec((rb * ROW_TILES, LANES), lambda i, blk, e, lo, hi: (blk[i], 0)),
        scratch_shapes=[pltpu.VMEM((D_MODEL, D_EXPERT), BF16),
                        pltpu.VMEM((D_MODEL, D_EXPERT), BF16),
                        pltpu.VMEM((D_EXPERT, D_MODEL), BF16)],
    )
    return pl.pallas_call(
        functools.partial(_experts_kernel, rb),
        grid_spec=grid_spec,
        out_shape=jax.ShapeDtypeStruct(xg.shape, F32),
        compiler_params=_cparams(("arbitrary",)),
        name="experts",
    )(item_blk, item_e, item_lo, item_hi, xg, w_eg, w_eu, w_ed)


def _combine_kernel(alpha, n_p, n_steps, dest_ref, dnext_ref, gate_ref, x1_ref, y_ref,
                    wsg_ref, wsu_ref, wsd_ref, g_ref, b_ref, x2p_ref, x2s_ref,
                    ybuf_a, ybuf_b, sem_a, sem_b):
    step = pl.program_id(0)
    tt = x1_ref.shape[0] // 2
    n_rows = TOP_K * tt * ROW_TILES

    def issue(dref, half, buf, sem):
        def body(t, carry):
            for kk in range(TOP_K):
                _row_copy(y_ref, dref[(half * tt + t) * TOP_K + kk], buf,
                          (kk * tt + t) * ROW_TILES, sem).start()
            return carry

        lax.fori_loop(0, tt, body, 0)

    def wait(buf, sem):
        pltpu.make_async_copy(y_ref.at[pl.ds(0, n_rows)], buf, sem).wait()

    def routed(half, buf):
        gates = gate_ref[half * tt:(half + 1) * tt, :]
        chunks = []
        for s in range(ROW_TILES):
            acc = jnp.zeros((tt, LANES), F32)
            for kk in range(TOP_K):
                acc = acc + gates[:, kk:kk + 1] * buf[pl.ds(kk * tt * ROW_TILES + s, tt,
                                                            stride=ROW_TILES), :]
            chunks.append(acc)
        return jnp.concatenate(chunks, axis=1)

    @pl.when(step == 0)
    def _():
        issue(dest_ref, 0, ybuf_a, sem_a)

    issue(dest_ref, 1, ybuf_b, sem_b)

    x1 = x1_ref[...]
    xb = x1.astype(BF16)
    gs = _dot(xb, wsg_ref[...])
    us = _dot(xb, wsu_ref[...])
    base = alpha * x1 + _dot((gs * _sigmoid(gs) * us).astype(BF16), wsd_ref[...])

    wait(ybuf_a, sem_a)
    x2a = _layer_norm(base[:tt, :] + routed(0, ybuf_a), g_ref[...], b_ref[...])

    @pl.when(step + 1 < n_steps)
    def _():
        issue(dnext_ref, 0, ybuf_a, sem_a)

    wait(ybuf_b, sem_b)
    x2b = _layer_norm(base[tt:, :] + routed(1, ybuf_b), g_ref[...], b_ref[...])

    @pl.when(step < n_p)
    def _():
        x2p_ref[:tt, :] = x2a
        x2p_ref[tt:, :] = x2b

    @pl.when(step >= n_p)
    def _():
        x2s_ref[:tt, :] = x2a
        x2s_ref[tt:, :] = x2b


def _combine(dest8, gates, x1, y, wsg, wsu, wsd, g, b, alpha, tt, mp):
    m = x1.shape[0]
    st = 2 * tt
    n_steps = m // st
    n_p = mp // st
    return pl.pallas_call(
        functools.partial(_combine_kernel, alpha, n_p, n_steps),
        grid=(n_steps,),
        in_specs=[pl.BlockSpec((st * TOP_K,), lambda i: (i,), memory_space=pltpu.SMEM),
                  pl.BlockSpec((st * TOP_K,), lambda i: (jnp.minimum(i + 1, n_steps - 1),),
                               memory_space=pltpu.SMEM),
                  pl.BlockSpec((st, TOP_K), lambda i: (i, 0)),
                  pl.BlockSpec((st, D_MODEL), lambda i: (i, 0)),
                  pl.BlockSpec(memory_space=pl.ANY),
                  pl.BlockSpec((D_MODEL, D_EXPERT), lambda i: (0, 0)),
                  pl.BlockSpec((D_MODEL, D_EXPERT), lambda i: (0, 0)),
                  pl.BlockSpec((D_EXPERT, D_MODEL), lambda i: (0, 0)),
                  pl.BlockSpec((1, D_MODEL), lambda i: (0, 0)),
                  pl.BlockSpec((1, D_MODEL), lambda i: (0, 0))],
        out_specs=[pl.BlockSpec((st, D_MODEL), lambda i: (jnp.minimum(i, n_p - 1), 0)),
                   pl.BlockSpec((st, D_MODEL), lambda i: (jnp.maximum(i - n_p, 0), 0))],
        out_shape=[jax.ShapeDtypeStruct((mp, D_MODEL), F32),
                   jax.ShapeDtypeStruct((m - mp, D_MODEL), F32)],
        scratch_shapes=[pltpu.VMEM((TOP_K * tt * ROW_TILES, LANES), F32),
                        pltpu.VMEM((TOP_K * tt * ROW_TILES, LANES), F32),
                        pltpu.SemaphoreType.DMA(()), pltpu.SemaphoreType.DMA(())],
        compiler_params=_cparams(("arbitrary",)),
        name="combine",
    )(dest8, dest8, gates, x1, y, wsg, wsu, wsd, g, b)


PROJ_TM = 512
PROJ_TN = 1024
MIX_TT = 256
MIX_CHUNK = 128
ROUTER_TT = 256
DISPATCH_TT = 512
EXPERT_RB = 256
COMBINE_TT = 128


def _moe(x1, x1r, mp, w_router, b_router, w_eg, w_eu, w_ed, w_sg, w_su, w_sd, ln_g, ln_b, alpha):
    m = x1.shape[0]
    wr_hi, wr_lo = _split_hi_lo(w_router)
    idx, gates, rank, cnt = _router(x1, wr_hi, wr_lo, b_router[None, :], ROUTER_TT)

    rb = EXPERT_RB
    n_rows = m * TOP_K
    assert n_rows % rb == 0
    n_items = n_rows // rb + N_EXPERTS - 1
    counts = cnt[0].astype(jnp.int32)
    seg_end = jnp.cumsum(counts)
    seg_start = seg_end - counts
    first_blk = seg_start // rb
    n_blk = jnp.where(counts > 0, (seg_end - 1) // rb - first_blk + 1, 0)
    item_end = jnp.cumsum(n_blk)
    item_start = item_end - n_blk
    item = jnp.arange(n_items, dtype=jnp.int32)
    item_c = jnp.minimum(item, item_end[-1] - 1)
    item_e = jnp.sum((item_end[None, :] <= item_c[:, None]).astype(jnp.int32), axis=1)
    onehot_e = item_e[:, None] == jnp.arange(N_EXPERTS, dtype=jnp.int32)[None, :]

    def pick(table):
        return jnp.sum(jnp.where(onehot_e, table[None, :], 0), axis=1)

    item_blk = pick(first_blk) + item_c - pick(item_start)
    valid = item < item_end[-1]
    item_lo = jnp.where(valid, jnp.clip(pick(seg_start) - item_blk * rb, 0, rb), 0)
    item_hi = jnp.where(valid, jnp.clip(pick(seg_end) - item_blk * rb, 0, rb), 0)

    xg, dest8 = _dispatch(seg_start, idx.reshape(-1), rank.reshape(-1), x1r, DISPATCH_TT)
    y = _experts(item_blk, item_e, item_lo, item_hi, xg, w_eg, w_eu, w_ed, rb)
    return _combine(dest8, gates, x1, y, w_sg.astype(BF16), w_su.astype(BF16),
                    w_sd.astype(BF16), ln_g[None, :], ln_b[None, :], alpha, COMBINE_TT, mp)


def _layer(xp, xs, conv_s, c_s, n_s, m_s, w_in, b_gates, w_conv, mh_gain, w_out, ln1_g, ln1_b,
           w_router, b_router, w_eg, w_eu, w_ed, w_sg, w_su, w_sd, ln2_g, ln2_b, alpha):
    bp, tp, _ = xp.shape
    bs, ts, _ = xs.shape
    mp, ms = bp * tp, bs * ts
    xp2 = xp.reshape(mp, D_MODEL)
    xs2 = xs.reshape(ms, D_MODEL)

    w_main = jnp.concatenate([w_in[:, :GATE_COL0], w_in[:, GATE_COL0 + 2 * N_HEADS:]],
                             axis=1).astype(BF16)
    w_gate = jnp.pad(w_in[:, GATE_COL0:GATE_COL0 + 2 * N_HEADS],
                     ((0, 0), (0, LANES - 2 * N_HEADS)))
    wg_hi, wg_lo = _split_hi_lo(w_gate)

    z_s = _proj(xs2, w_main, PROJ_TM, PROJ_TN)
    zg_s = _gate_proj(xs2, wg_hi, wg_lo, PROJ_TM)
    gt_s = zg_s[:, :SUBLANES].reshape(bs, ts, SUBLANES).transpose(0, 2, 1)
    mix_s, conv_n, c_n, n_n, m_n = _mixer(
        z_s, zg_s, gt_s, b_gates, w_conv, mh_gain[None, :],
        conv_s, c_s, n_s, m_s[:, None, :], nb=bs, nt=1, tt=ts, chunk=ts)

    x1, x1r, conv_p, c_p, n_p, m_p = _prompt_layer_half(
        xp2, xs2, mix_s, b_gates, w_main, wg_hi, wg_lo, w_conv, mh_gain[None, :],
        w_out.astype(BF16), ln1_g[None, :], ln1_b[None, :], alpha,
        nb=bp, nt=tp // MIX_TT, tt=MIX_TT, chunk=MIX_CHUNK)

    x2p, x2s = _moe(x1, x1r, mp, w_router, b_router, w_eg, w_eu, w_ed, w_sg, w_su, w_sd,
                    ln2_g, ln2_b, alpha)
    states_p = (conv_p, c_p, n_p, m_p[:, 0, :])
    states_s = (conv_n, c_n, n_n, m_n[:, 0, :])
    return x2p.reshape(bp, tp, D_MODEL), x2s.reshape(bs, ts, D_MODEL), states_p, states_s


def kernel(x_prompt, x_sample, cache_conv, state_mlstm_C, state_mlstm_n, state_mlstm_m, w_in, b_gates, w_conv, mh_gain, w_out, ln1_g, ln1_b, w_router, b_router, w_exp_gate, w_exp_up, w_exp_down, w_sh_gate, w_sh_up, w_sh_down, ln2_g, ln2_b):
    depth = w_in.shape[0]
    alpha = (2.0 * depth) ** 0.25
    hp, hs = x_prompt, x_sample
    outs_p = [[], [], [], []]
    outs_s = [[], [], [], []]
    for l in range(depth):
        hp, hs, st_p, st_s = _layer(
            hp, hs, cache_conv[l], state_mlstm_C[l], state_mlstm_n[l], state_mlstm_m[l],
            w_in[l], b_gates[l], w_conv[l], mh_gain[l], w_out[l], ln1_g[l], ln1_b[l],
            w_router[l], b_router[l], w_exp_gate[l], w_exp_up[l], w_exp_down[l],
            w_sh_gate[l], w_sh_up[l], w_sh_down[l], ln2_g[l], ln2_b[l], alpha)
        for acc, val in zip(outs_p, st_p):
            acc.append(val)
        for acc, val in zip(outs_s, st_s):
            acc.append(val)
    return (hp, hs) + tuple(jnp.stack(a) for a in outs_p) + tuple(jnp.stack(a) for a in outs_s)
```

```python
import functools

import jax
import jax.numpy as jnp
from jax import lax
from jax.experimental import pallas as pl
from jax.experimental.pallas import tpu as pltpu

F32 = jnp.float32
BF16 = jnp.bfloat16

D_MODEL = 1024
N_HEADS = 4
HEAD_V = 256
HEAD_QK = 128
N_EXPERTS = 256
TOP_K = 8
D_EXPERT = 256
ROUTED_SCALE = 2.5
LN_EPS = 1e-5
CONV_K = 3

LANES = 128
SUBLANES = 8
ROW_TILES = D_MODEL // LANES
GATE_COL0 = 6 * D_MODEL

VMEM_LIMIT = 56 * 1024 * 1024


def _cparams(sem):
    return pltpu.CompilerParams(dimension_semantics=sem, vmem_limit_bytes=VMEM_LIMIT)


def _sigmoid(x):
    return 1.0 / (1.0 + jnp.exp(-x))


def _log_sigmoid(x):
    return jnp.minimum(x, 0.0) - jnp.log(1.0 + jnp.exp(-jnp.abs(x)))


def _layer_norm(x, g, b):
    mu = jnp.mean(x, axis=-1, keepdims=True)
    xc = x - mu
    var = jnp.mean(xc * xc, axis=-1, keepdims=True)
    return xc * lax.rsqrt(var + LN_EPS) * g + b


def _split_hi_lo(x):
    hi = x.astype(BF16)
    lo = (x - hi.astype(F32)).astype(BF16)
    return hi, lo


def _dot(a, b):
    return jnp.dot(a, b, preferred_element_type=F32)


def _dot3(x, w_hi, w_lo):
    x_hi, x_lo = _split_hi_lo(x)
    return _dot(x_hi, w_hi) + _dot(x_lo, w_hi) + _dot(x_hi, w_lo)


def _proj_kernel(x_ref, w_ref, z_ref):
    z_ref[...] = _dot(x_ref[...].astype(BF16), w_ref[...])


def _proj(x, w, tm, tn):
    m, k = x.shape
    n = w.shape[1]
    return pl.pallas_call(
        _proj_kernel,
        grid=(m // tm, n // tn),
        in_specs=[pl.BlockSpec((tm, k), lambda i, j: (i, 0)),
                  pl.BlockSpec((k, tn), lambda i, j: (0, j))],
        out_specs=pl.BlockSpec((tm, tn), lambda i, j: (i, j)),
        out_shape=jax.ShapeDtypeStruct((m, n), F32),
        compiler_params=_cparams(("parallel", "arbitrary")),
        name="proj",
    )(x, w)


def _gate_proj_kernel(x_ref, wh_ref, wl_ref, z_ref):
    z_ref[...] = _dot3(x_ref[...], wh_ref[...], wl_ref[...])


def _gate_proj(x, w_hi, w_lo, tm):
    m, k = x.shape
    n = w_hi.shape[1]
    return pl.pallas_call(
        _gate_proj_kernel,
        grid=(m // tm,),
        in_specs=[pl.BlockSpec((tm, k), lambda i: (i, 0)),
                  pl.BlockSpec((k, n), lambda i: (0, 0)),
                  pl.BlockSpec((k, n), lambda i: (0, 0))],
        out_specs=pl.BlockSpec((tm, n), lambda i: (i, 0)),
        out_shape=jax.ShapeDtypeStruct((m, n), F32),
        compiler_params=_cparams(("parallel",)),
        name="gate_proj",
    )(x, w_hi, w_lo)


def _conv_branch(u, zcb, wc, conv_ref):
    tt = u.shape[0]
    carry = conv_ref[0]
    rows = lax.broadcasted_iota(jnp.int32, (tt, D_MODEL), 0)
    u1 = jnp.where(rows == 0, carry[1:2, :], pltpu.roll(u, 1, 0))
    u2 = jnp.where(rows == 0, carry[0:1, :],
                   jnp.where(rows == 1, carry[1:2, :], pltpu.roll(u, 2, 0)))
    conv_out = u2 * wc[0:1, :] + u1 * wc[1:2, :] + u * wc[2:3, :]
    conv_ref[0] = u[tt - 2:tt, :]
    return zcb * conv_out


def _mlstm_chunk(q, k, v, ig_col, lf_col, ig_row, lf_row, c_state, n_row, m_prev):
    chunk = q.shape[0]
    ti = lax.broadcasted_iota(jnp.int32, (chunk, chunk), 0)
    si = lax.broadcasted_iota(jnp.int32, (chunk, chunk), 1)
    causal = si <= ti
    b_col = jnp.sum(jnp.where(causal, lf_row, 0.0), axis=1, keepdims=True)
    b_row = jnp.sum(jnp.where(ti <= si, lf_col, 0.0), axis=0, keepdims=True)
    inter = b_col + m_prev
    dmat = jnp.where(causal, b_col - b_row + ig_row, -jnp.inf)
    m_t = jnp.maximum(inter, jnp.max(dmat, axis=1, keepdims=True))
    w_intra = jnp.exp(dmat - m_t)
    w_inter = jnp.exp(inter - m_t)
    qb = q.astype(BF16)
    kb = k.astype(BF16)
    vb = v.astype(BF16)
    s = lax.dot_general(qb, kb, (((1,), (1,)), ((), ())), preferred_element_type=F32) * w_intra
    num = w_inter * _dot(qb, c_state.astype(BF16)) + _dot(s.astype(BF16), vb)
    den = (w_inter * jnp.sum(q * n_row, axis=1, keepdims=True)
           + jnp.sum(s, axis=1, keepdims=True))
    h = num / jnp.maximum(jnp.abs(den), jnp.exp(-m_t))
    m_new = m_t[chunk - 1:chunk, :]
    b_last = b_col[chunk - 1:chunk, :]
    ws_col = jnp.exp(b_last - b_col + ig_col - m_new)
    cdecay = jnp.exp(inter[chunk - 1:chunk, :] - m_new)
    kw = k * ws_col
    c_new = cdecay * c_state + lax.dot_general(
        kw.astype(BF16), vb, (((0,), (0,)), ((), ())), preferred_element_type=F32)
    n_new = cdecay * n_row + jnp.sum(kw, axis=0, keepdims=True)
    return h, c_new, n_new, m_new


def _mlstm_branch(chunk, bg_ref, zg, gt, zqk, zv, zog, gain_ref, c_ref, n_ref, m_ref, mo_ref):
    tt = zqk.shape[0]
    for h in range(N_HEADS):
        ig_col_all = zg[:, h:h + 1] + bg_ref[h]
        lf_col_all = _log_sigmoid(zg[:, N_HEADS + h:N_HEADS + h + 1] + bg_ref[N_HEADS + h])
        ig_row_all = gt[h:h + 1, :] + bg_ref[h]
        lf_row_all = _log_sigmoid(gt[N_HEADS + h:N_HEADS + h + 1, :] + bg_ref[N_HEADS + h])
        vcols = slice(h * HEAD_V, (h + 1) * HEAD_V)
        for c in range(tt // chunk):
            r0, r1 = c * chunk, (c + 1) * chunk
            q = zqk[r0:r1, h * HEAD_QK:(h + 1) * HEAD_QK]
            k = zqk[r0:r1, (N_HEADS + h) * HEAD_QK:(N_HEADS + h + 1) * HEAD_QK] * (HEAD_QK ** -0.5)
            hh, c_new, n_new, m_new = _mlstm_chunk(
                q, k, zv[r0:r1, vcols], ig_col_all[r0:r1, :], lf_col_all[r0:r1, :],
                ig_row_all[:, r0:r1], lf_row_all[:, r0:r1],
                c_ref[0, h], n_ref[0, h:h + 1, :], m_ref[0, :, h:h + 1])
            c_ref[0, h] = c_new
            n_ref[0, h:h + 1, :] = n_new
            m_ref[0, :, h:h + 1] = m_new
            mu = jnp.mean(hh, axis=-1, keepdims=True)
            hc = hh - mu
            var = jnp.mean(hc * hc, axis=-1, keepdims=True)
            hn = hc * lax.rsqrt(var + LN_EPS) * gain_ref[:, vcols]
            mo_ref[r0:r1, vcols] = hn * _sigmoid(zog[r0:r1, vcols])


def _store_x1(x, mix, wout_ref, g_ref, b_ref, alpha, x1_ref, x1r_ref):
    tt = x.shape[0]
    x1 = _layer_norm(alpha * x + _dot(mix.astype(BF16), wout_ref[...]), g_ref[...], b_ref[...])
    x1_ref[...] = x1
    for s in range(ROW_TILES):
        x1r_ref[pl.ds(s, tt, stride=ROW_TILES), :] = x1[:, s * LANES:(s + 1) * LANES]


def _mixer_kernel(chunk, bg_ref,
                  zcb_ref, zcc_ref, zch_ref, zqk_ref, zv_ref, zog_ref, zga_ref, zgb_ref,
                  zg_ref, gt_ref, wconv_ref, gain_ref,
                  conv0_ref, c0_ref, n0_ref, m0_ref,
                  mix_ref, conv_ref, c_ref, n_ref, m_ref, mo_ref):
    @pl.when(pl.program_id(1) == 0)
    def _():
        conv_ref[...] = conv0_ref[...]
        c_ref[...] = c0_ref[...]
        n_ref[...] = n0_ref[...]
        m_ref[...] = m0_ref[...]

    a = _conv_branch(zcc_ref[...] * zch_ref[...], zcb_ref[...], wconv_ref[...], conv_ref)
    _mlstm_branch(chunk, bg_ref, zg_ref[...], gt_ref[0], zqk_ref[...], zv_ref[...], zog_ref[...],
                  gain_ref, c_ref, n_ref, m_ref, mo_ref)
    mix_ref[...] = _sigmoid(zga_ref[...]) * a + _sigmoid(zgb_ref[...]) * mo_ref[...]


def _mixer(z, zg, gt3, b_gates, w_conv, mh_gain, conv0, c0, n0, m0, *, nb, nt, tt, chunk):
    def zspec(j):
        return pl.BlockSpec((tt, D_MODEL), lambda b, t, j=j: (b * nt + t, j))

    in_specs = [pl.BlockSpec(memory_space=pltpu.SMEM)]
    in_specs += [zspec(j) for j in range(8)]
    in_specs += [
        pl.BlockSpec((tt, LANES), lambda b, t: (b * nt + t, 0)),
        pl.BlockSpec((1, SUBLANES, tt), lambda b, t: (b * nt + t, 0, 0)),
        pl.BlockSpec((CONV_K, D_MODEL), lambda b, t: (0, 0)),
        pl.BlockSpec((1, D_MODEL), lambda b, t: (0, 0)),
        pl.BlockSpec((1, CONV_K - 1, D_MODEL), lambda b, t: (b, 0, 0)),
        pl.BlockSpec((1, N_HEADS, HEAD_QK, HEAD_V), lambda b, t: (b, 0, 0, 0)),
        pl.BlockSpec((1, N_HEADS, HEAD_QK), lambda b, t: (b, 0, 0)),
        pl.BlockSpec((1, 1, N_HEADS), lambda b, t: (b, 0, 0)),
    ]
    out_specs = [
        pl.BlockSpec((tt, D_MODEL), lambda b, t: (b * nt + t, 0)),
        pl.BlockSpec((1, CONV_K - 1, D_MODEL), lambda b, t: (b, 0, 0)),
        pl.BlockSpec((1, N_HEADS, HEAD_QK, HEAD_V), lambda b, t: (b, 0, 0, 0)),
        pl.BlockSpec((1, N_HEADS, HEAD_QK), lambda b, t: (b, 0, 0)),
        pl.BlockSpec((1, 1, N_HEADS), lambda b, t: (b, 0, 0)),
    ]
    out_shape = [
        jax.ShapeDtypeStruct((nb * nt * tt, D_MODEL), F32),
        jax.ShapeDtypeStruct((nb, CONV_K - 1, D_MODEL), F32),
        jax.ShapeDtypeStruct((nb, N_HEADS, HEAD_QK, HEAD_V), F32),
        jax.ShapeDtypeStruct((nb, N_HEADS, HEAD_QK), F32),
        jax.ShapeDtypeStruct((nb, 1, N_HEADS), F32),
    ]
    return pl.pallas_call(
        functools.partial(_mixer_kernel, chunk),
        grid=(nb, nt),
        in_specs=in_specs,
        out_specs=out_specs,
        out_shape=out_shape,
        scratch_shapes=[pltpu.VMEM((tt, D_MODEL), F32)],
        compiler_params=_cparams(("arbitrary", "arbitrary")),
        name="mixer_t%d" % tt,
    )(b_gates, z, z, z, z, z, z, z, z, zg, gt3, w_conv, mh_gain, conv0, c0, n0, m0)


def _prompt_kernel(chunk, nb, ns, alpha, bg_ref, x_ref, w_ref, wgh_ref, wgl_ref, wconv_ref,
                   gain_ref, wout_ref, g_ref, b_ref, xs_ref, mixs_ref,
                   x1_ref, x1r_ref, conv_ref, c_ref, n_ref, m_ref, mo_ref):
    b = pl.program_id(0)
    t = pl.program_id(1)

    @pl.when(b < nb)
    def _():
        @pl.when(t == 0)
        def _():
            conv_ref[...] = jnp.zeros_like(conv_ref)
            c_ref[...] = jnp.zeros_like(c_ref)
            n_ref[...] = jnp.zeros_like(n_ref)
            m_ref[...] = jnp.zeros_like(m_ref)

        x = x_ref[...]
        xb = x.astype(BF16)

        def z(j):
            return _dot(xb, w_ref[:, j * D_MODEL:(j + 1) * D_MODEL])

        a = _conv_branch(z(1) * z(2), z(0), wconv_ref[...], conv_ref)
        zg = _dot3(x, wgh_ref[...], wgl_ref[...])
        _mlstm_branch(chunk, bg_ref, zg, zg.T, z(3), z(4), z(5), gain_ref,
                      c_ref, n_ref, m_ref, mo_ref)
        mix = _sigmoid(z(6)) * a + _sigmoid(z(7)) * mo_ref[...]
        _store_x1(x, mix, wout_ref, g_ref, b_ref, alpha, x1_ref, x1r_ref)

    @pl.when(jnp.logical_and(b == nb, t < ns))
    def _():
        _store_x1(xs_ref[...], mixs_ref[...], wout_ref, g_ref, b_ref, alpha, x1_ref, x1r_ref)


def _prompt_layer_half(xp, xs, mix_s, b_gates, w_main, wg_hi, wg_lo, w_conv, mh_gain, w_out,
                       ln_g, ln_b, alpha, *, nb, nt, tt, chunk):
    mp, ms = xp.shape[0], xs.shape[0]
    ns = ms // tt
    last_p = nb * nt - 1

    def prow(b, t):
        return jnp.minimum(b * nt + t, last_p)

    def srow(b, t):
        return jnp.where(b == nb, jnp.minimum(t, ns - 1), 0)

    def orow(b, t):
        return jnp.where(b == nb, nb * nt + jnp.minimum(t, ns - 1), b * nt + t)

    def state(b):
        return jnp.minimum(b, nb - 1)

    const2 = lambda b, t: (0, 0)
    in_specs = [
        pl.BlockSpec(memory_space=pltpu.SMEM),
        pl.BlockSpec((tt, D_MODEL), lambda b, t: (prow(b, t), 0)),
        pl.BlockSpec(w_main.shape, const2, pipeline_mode=pl.Buffered(1)),
        pl.BlockSpec(wg_hi.shape, const2),
        pl.BlockSpec(wg_lo.shape, const2),
        pl.BlockSpec((CONV_K, D_MODEL), const2),
        pl.BlockSpec((1, D_MODEL), const2),
        pl.BlockSpec((D_MODEL, D_MODEL), const2),
        pl.BlockSpec((1, D_MODEL), const2),
        pl.BlockSpec((1, D_MODEL), const2),
        pl.BlockSpec((tt, D_MODEL), lambda b, t: (srow(b, t), 0)),
        pl.BlockSpec((tt, D_MODEL), lambda b, t: (srow(b, t), 0)),
    ]
    out_specs = [
        pl.BlockSpec((tt, D_MODEL), lambda b, t: (orow(b, t), 0)),
        pl.BlockSpec((tt * ROW_TILES, LANES), lambda b, t: (orow(b, t), 0)),
        pl.BlockSpec((1, CONV_K - 1, D_MODEL), lambda b, t: (state(b), 0, 0)),
        pl.BlockSpec((1, N_HEADS, HEAD_QK, HEAD_V), lambda b, t: (state(b), 0, 0, 0)),
        pl.BlockSpec((1, N_HEADS, HEAD_QK), lambda b, t: (state(b), 0, 0)),
        pl.BlockSpec((1, 1, N_HEADS), lambda b, t: (state(b), 0, 0)),
    ]
    out_shape = [
        jax.ShapeDtypeStruct((mp + ms, D_MODEL), F32),
        jax.ShapeDtypeStruct(((mp + ms) * ROW_TILES, LANES), F32),
        jax.ShapeDtypeStruct((nb, CONV_K - 1, D_MODEL), F32),
        jax.ShapeDtypeStruct((nb, N_HEADS, HEAD_QK, HEAD_V), F32),
        jax.ShapeDtypeStruct((nb, N_HEADS, HEAD_QK), F32),
        jax.ShapeDtypeStruct((nb, 1, N_HEADS), F32),
    ]
    return pl.pallas_call(
        functools.partial(_prompt_kernel, chunk, nb, ns, alpha),
        grid=(nb + 1, nt),
        in_specs=in_specs,
        out_specs=out_specs,
        out_shape=out_shape,
        scratch_shapes=[pltpu.VMEM((tt, D_MODEL), F32)],
        compiler_params=_cparams(("arbitrary", "arbitrary")),
        name="prompt_half",
    )(b_gates, xp, w_main, wg_hi, wg_lo, w_conv, mh_gain, w_out, ln_g, ln_b, xs, mix_s)


def _router_kernel(x_ref, wh_ref, wl_ref, br_ref, idx_ref, gate_ref, rank_ref, cnt_ref):
    tt = x_ref.shape[0]

    @pl.when(pl.program_id(0) == 0)
    def _():
        cnt_ref[...] = jnp.zeros_like(cnt_ref)

    scores = _sigmoid(_dot3(x_ref[...], wh_ref[...], wl_ref[...]))
    lane = lax.broadcasted_iota(jnp.int32, (tt, N_EXPERTS), 1)
    work = scores + br_ref[...]
    picked = []
    sel = []
    member = jnp.zeros((tt, N_EXPERTS), F32)
    for _ in range(TOP_K):
        mx = jnp.max(work, axis=1, keepdims=True)
        ik = jnp.min(jnp.where(work == mx, lane, N_EXPERTS), axis=1, keepdims=True)
        onehot = lane == ik
        picked.append((ik, onehot))
        sel.append(jnp.sum(jnp.where(onehot, scores, 0.0), axis=1, keepdims=True))
        work = jnp.where(onehot, -jnp.inf, work)
        member = member + onehot.astype(F32)
    total = sel[0]
    for sk in sel[1:]:
        total = total + sk

    ti = lax.broadcasted_iota(jnp.int32, (tt, tt), 0)
    si = lax.broadcasted_iota(jnp.int32, (tt, tt), 1)
    earlier = (si < ti).astype(BF16)
    before = _dot(earlier, member.astype(BF16)) + cnt_ref[...]

    lane_o = lax.broadcasted_iota(jnp.int32, (tt, TOP_K), 1)
    idx_o = jnp.zeros((tt, TOP_K), jnp.int32)
    gate_o = jnp.zeros((tt, TOP_K), F32)
    rank_o = jnp.zeros((tt, TOP_K), jnp.int32)
    for kk in range(TOP_K):
        ik, onehot = picked[kk]
        rk = jnp.sum(jnp.where(onehot, before, 0.0), axis=1, keepdims=True)
        idx_o = jnp.where(lane_o == kk, ik, idx_o)
        gate_o = jnp.where(lane_o == kk, sel[kk] / total * ROUTED_SCALE, gate_o)
        rank_o = jnp.where(lane_o == kk, rk.astype(jnp.int32), rank_o)
    idx_ref[...] = idx_o
    gate_ref[...] = gate_o
    rank_ref[...] = rank_o
    cnt_ref[...] = cnt_ref[...] + jnp.sum(member, axis=0, keepdims=True)


def _router(x1, wr_hi, wr_lo, b_router, tt):
    m = x1.shape[0]
    return pl.pallas_call(
        _router_kernel,
        grid=(m // tt,),
        in_specs=[pl.BlockSpec((tt, D_MODEL), lambda i: (i, 0)),
                  pl.BlockSpec((D_MODEL, N_EXPERTS), lambda i: (0, 0)),
                  pl.BlockSpec((D_MODEL, N_EXPERTS), lambda i: (0, 0)),
                  pl.BlockSpec((1, N_EXPERTS), lambda i: (0, 0))],
        out_specs=[pl.BlockSpec((tt, TOP_K), lambda i: (i, 0)),
                   pl.BlockSpec((tt, TOP_K), lambda i: (i, 0)),
                   pl.BlockSpec((tt, TOP_K), lambda i: (i, 0)),
                   pl.BlockSpec((1, N_EXPERTS), lambda i: (0, 0))],
        out_shape=[jax.ShapeDtypeStruct((m, TOP_K), jnp.int32),
                   jax.ShapeDtypeStruct((m, TOP_K), F32),
                   jax.ShapeDtypeStruct((m, TOP_K), jnp.int32),
                   jax.ShapeDtypeStruct((1, N_EXPERTS), F32)],
        compiler_params=_cparams(("arbitrary",)),
        name="router",
    )(x1, wr_hi, wr_lo, b_router)


def _row_copy(src, src_row8, dst, dst_row8, sem):
    return pltpu.make_async_copy(src.at[pl.ds(pl.multiple_of(src_row8, SUBLANES), SUBLANES)],
                                 dst.at[pl.ds(pl.multiple_of(dst_row8, SUBLANES), SUBLANES)],
                                 sem)


def _dispatch_kernel(tt, start_ref, idx_ref, rank_ref, x1r_ref, xg_ref, dest_ref, sem):
    def issue(t, carry):
        for kk in range(TOP_K):
            j = t * TOP_K + kk
            dest8 = (start_ref[idx_ref[j]] + rank_ref[j]) * ROW_TILES
            dest_ref[j] = dest8
            _row_copy(x1r_ref, t * ROW_TILES, xg_ref, dest8, sem).start()
        return carry

    lax.fori_loop(0, tt, issue, 0)
    n_rows = tt * TOP_K * ROW_TILES
    pltpu.make_async_copy(xg_ref.at[pl.ds(0, n_rows)], xg_ref.at[pl.ds(0, n_rows)], sem).wait()


def _dispatch(seg_start, idx, rank, x1r, tt):
    m = x1r.shape[0] // ROW_TILES
    smem_blk = pl.BlockSpec((tt * TOP_K,), lambda i: (i,), memory_space=pltpu.SMEM)
    return pl.pallas_call(
        functools.partial(_dispatch_kernel, tt),
        grid=(m // tt,),
        in_specs=[pl.BlockSpec(memory_space=pltpu.SMEM), smem_blk, smem_blk,
                  pl.BlockSpec((tt * ROW_TILES, LANES), lambda i: (i, 0))],
        out_specs=[pl.BlockSpec(memory_space=pl.ANY), smem_blk],
        out_shape=[jax.ShapeDtypeStruct((m * TOP_K * ROW_TILES, LANES), F32),
                   jax.ShapeDtypeStruct((m * TOP_K,), jnp.int32)],
        scratch_shapes=[pltpu.SemaphoreType.DMA(())],
        compiler_params=_cparams(("arbitrary",)),
        name="dispatch",
    )(seg_start, idx, rank, x1r)


def _experts_kernel(rb, blk_ref, e_ref, lo_ref, hi_ref, new_ref, slot_ref, nxt_ref,
                    xg_ref, wg_hbm, wu_hbm, wd_hbm, y_ref,
                    wg_buf, wu_buf, wd_buf, wgb, wub, wdb, sem):
    j = pl.program_id(0)
    prev = jnp.maximum(j - 1, 0)
    lo = lo_ref[j]
    hi = hi_ref[j]

    def fetch(e, slot):
        return (pltpu.make_async_copy(wg_hbm.at[e], wg_buf.at[slot], sem.at[slot, 0]),
                pltpu.make_async_copy(wu_hbm.at[e], wu_buf.at[slot], sem.at[slot, 1]),
                pltpu.make_async_copy(wd_hbm.at[e], wd_buf.at[slot], sem.at[slot, 2]))

    @pl.when(new_ref[j] != 0)
    def _():
        slot = slot_ref[j]

        @pl.when(j == 0)
        def _():
            for copy in fetch(e_ref[j], slot):
                copy.start()

        for copy in fetch(e_ref[j], slot):
            copy.wait()
        wgb[...] = wg_buf[slot].astype(BF16)
        wub[...] = wu_buf[slot].astype(BF16)
        wdb[...] = wd_buf[slot].astype(BF16)

        @pl.when(nxt_ref[j] >= 0)
        def _():
            for copy in fetch(nxt_ref[j], 1 - slot):
                copy.start()

    def expert_rows():
        xb = jnp.concatenate(
            [xg_ref[pl.ds(s, rb, stride=ROW_TILES), :] for s in range(ROW_TILES)],
            axis=1).astype(BF16)
        g = _dot(xb, wgb[...])
        u = _dot(xb, wub[...])
        hid = (g * _sigmoid(g) * u).astype(BF16)
        rows = lax.broadcasted_iota(jnp.int32, (rb, LANES), 0)
        return _dot(hid, wdb[...]), jnp.logical_and(rows >= lo, rows < hi)

    first = jnp.logical_or(j == 0, blk_ref[j] != blk_ref[prev])

    @pl.when(jnp.logical_and(hi > lo, first))
    def _():
        y, mine = expert_rows()
        for s in range(ROW_TILES):
            y_ref[pl.ds(s, rb, stride=ROW_TILES), :] = jnp.where(
                mine, y[:, s * LANES:(s + 1) * LANES], 0.0)

    @pl.when(jnp.logical_and(hi > lo, jnp.logical_not(first)))
    def _():
        y, mine = expert_rows()
        for s in range(ROW_TILES):
            rows_s = pl.ds(s, rb, stride=ROW_TILES)
            y_ref[rows_s, :] = jnp.where(mine, y[:, s * LANES:(s + 1) * LANES], y_ref[rows_s, :])


def _experts(item_blk, item_e, item_lo, item_hi, item_new, item_slot, item_nxt,
             xg, w_eg, w_eu, w_ed, rb):
    n_items = item_blk.shape[0]
    n_prefetch = 7
    blk_map = lambda i, blk, *_: (blk[i], 0)
    grid_spec = pltpu.PrefetchScalarGridSpec(
        num_scalar_prefetch=n_prefetch,
        grid=(n_items,),
        in_specs=[
            pl.BlockSpec((rb * ROW_TILES, LANES), blk_map),
            pl.BlockSpec(memory_space=pl.ANY),
            pl.BlockSpec(memory_space=pl.ANY),
            pl.BlockSpec(memory_space=pl.ANY),
        ],
        out_specs=pl.BlockSpec((rb * ROW_TILES, LANES), blk_map),
        scratch_shapes=[pltpu.VMEM((2, D_MODEL, D_EXPERT), F32),
                        pltpu.VMEM((2, D_MODEL, D_EXPERT), F32),
                        pltpu.VMEM((2, D_EXPERT, D_MODEL), F32),
                        pltpu.VMEM((D_MODEL, D_EXPERT), BF16),
                        pltpu.VMEM((D_MODEL, D_EXPERT), BF16),
                        pltpu.VMEM((D_EXPERT, D_MODEL), BF16),
                        pltpu.SemaphoreType.DMA((2, 3))],
    )
    return pl.pallas_call(
        functools.partial(_experts_kernel, rb),
        grid_spec=grid_spec,
        out_shape=jax.ShapeDtypeStruct(xg.shape, F32),
        compiler_params=_cparams(("arbitrary",)),
        name="experts",
    )(item_blk, item_e, item_lo, item_hi, item_new, item_slot, item_nxt, xg, w_eg, w_eu, w_ed)


def _combine_kernel(alpha, n_p, n_steps, dest_ref, dnext_ref, gate_ref, x1_ref, y_ref,
                    wsg_ref, wsu_ref, wsd_ref, g_ref, b_ref, x2p_ref, x2s_ref,
                    ybuf_a, ybuf_b, sem_a, sem_b):
    step = pl.program_id(0)
    tt = x1_ref.shape[0] // 2
    n_rows = TOP_K * tt * ROW_TILES

    def issue(dref, half, buf, sem):
        def body(t, carry):
            for kk in range(TOP_K):
                _row_copy(y_ref, dref[(half * tt + t) * TOP_K + kk], buf,
                          (kk * tt + t) * ROW_TILES, sem).start()
            return carry

        lax.fori_loop(0, tt, body, 0)

    def wait(buf, sem):
        pltpu.make_async_copy(y_ref.at[pl.ds(0, n_rows)], buf, sem).wait()

    def routed(half, buf):
        gates = gate_ref[half * tt:(half + 1) * tt, :]
        chunks = []
        for s in range(ROW_TILES):
            acc = jnp.zeros((tt, LANES), F32)
            for kk in range(TOP_K):
                acc = acc + gates[:, kk:kk + 1] * buf[pl.ds(kk * tt * ROW_TILES + s, tt,
                                                            stride=ROW_TILES), :]
            chunks.append(acc)
        return jnp.concatenate(chunks, axis=1)

    @pl.when(step == 0)
    def _():
        issue(dest_ref, 0, ybuf_a, sem_a)

    issue(dest_ref, 1, ybuf_b, sem_b)

    x1 = x1_ref[...]
    xb = x1.astype(BF16)
    gs = _dot(xb, wsg_ref[...])
    us = _dot(xb, wsu_ref[...])
    base = alpha * x1 + _dot((gs * _sigmoid(gs) * us).astype(BF16), wsd_ref[...])

    wait(ybuf_a, sem_a)
    x2a = _layer_norm(base[:tt, :] + routed(0, ybuf_a), g_ref[...], b_ref[...])

    @pl.when(step + 1 < n_steps)
    def _():
        issue(dnext_ref, 0, ybuf_a, sem_a)

    wait(ybuf_b, sem_b)
    x2b = _layer_norm(base[tt:, :] + routed(1, ybuf_b), g_ref[...], b_ref[...])

    @pl.when(step < n_p)
    def _():
        x2p_ref[:tt, :] = x2a
        x2p_ref[tt:, :] = x2b

    @pl.when(step >= n_p)
    def _():
        x2s_ref[:tt, :] = x2a
        x2s_ref[tt:, :] = x2b


def _combine(dest8, gates, x1, y, wsg, wsu, wsd, g, b, alpha, tt, mp):
    m = x1.shape[0]
    st = 2 * tt
    n_steps = m // st
    n_p = mp // st
    return pl.pallas_call(
        functools.partial(_combine_kernel, alpha, n_p, n_steps),
        grid=(n_steps,),
        in_specs=[pl.BlockSpec((st * TOP_K,), lambda i: (i,), memory_space=pltpu.SMEM),
                  pl.BlockSpec((st * TOP_K,), lambda i: (jnp.minimum(i + 1, n_steps - 1),),
                               memory_space=pltpu.SMEM),
                  pl.BlockSpec((st, TOP_K), lambda i: (i, 0)),
                  pl.BlockSpec((st, D_MODEL), lambda i: (i, 0)),
                  pl.BlockSpec(memory_space=pl.ANY),
                  pl.BlockSpec((D_MODEL, D_EXPERT), lambda i: (0, 0)),
                  pl.BlockSpec((D_MODEL, D_EXPERT), lambda i: (0, 0)),
                  pl.BlockSpec((D_EXPERT, D_MODEL), lambda i: (0, 0)),
                  pl.BlockSpec((1, D_MODEL), lambda i: (0, 0)),
                  pl.BlockSpec((1, D_MODEL), lambda i: (0, 0))],
        out_specs=[pl.BlockSpec((st, D_MODEL), lambda i: (jnp.minimum(i, n_p - 1), 0)),
                   pl.BlockSpec((st, D_MODEL), lambda i: (jnp.maximum(i - n_p, 0), 0))],
        out_shape=[jax.ShapeDtypeStruct((mp, D_MODEL), F32),
                   jax.ShapeDtypeStruct((m - mp, D_MODEL), F32)],
        scratch_shapes=[pltpu.VMEM((TOP_K * tt * ROW_TILES, LANES), F32),
                        pltpu.VMEM((TOP_K * tt * ROW_TILES, LANES), F32),
                        pltpu.SemaphoreType.DMA(()), pltpu.SemaphoreType.DMA(())],
        compiler_params=_cparams(("arbitrary",)),
        name="combine",
    )(dest8, dest8, gates, x1, y, wsg, wsu, wsd, g, b)


PROJ_TM = 512
PROJ_TN = 1024
MIX_TT = 256
MIX_CHUNK = 128
ROUTER_TT = 256
DISPATCH_TT = 512
EXPERT_RB = 256
COMBINE_TT = 128


def _moe(x1, x1r, mp, w_router, b_router, w_eg, w_eu, w_ed, w_sg, w_su, w_sd, ln_g, ln_b, alpha):
    m = x1.shape[0]
    wr_hi, wr_lo = _split_hi_lo(w_router)
    idx, gates, rank, cnt = _router(x1, wr_hi, wr_lo, b_router[None, :], ROUTER_TT)

    rb = EXPERT_RB
    n_rows = m * TOP_K
    assert n_rows % rb == 0
    n_items = n_rows // rb + N_EXPERTS - 1
    counts = cnt[0].astype(jnp.int32)
    seg_end = jnp.cumsum(counts)
    seg_start = seg_end - counts
    first_blk = seg_start // rb
    n_blk = jnp.where(counts > 0, (seg_end - 1) // rb - first_blk + 1, 0)
    item_end = jnp.cumsum(n_blk)
    item_start = item_end - n_blk
    item = jnp.arange(n_items, dtype=jnp.int32)
    item_c = jnp.minimum(item, item_end[-1] - 1)
    item_e = jnp.sum((item_end[None, :] <= item_c[:, None]).astype(jnp.int32), axis=1)
    onehot_e = item_e[:, None] == jnp.arange(N_EXPERTS, dtype=jnp.int32)[None, :]

    def pick(table):
        return jnp.sum(jnp.where(onehot_e, table[None, :], 0), axis=1)

    item_blk = pick(first_blk) + item_c - pick(item_start)
    valid = item < item_end[-1]
    item_lo = jnp.where(valid, jnp.clip(pick(seg_start) - item_blk * rb, 0, rb), 0)
    item_hi = jnp.where(valid, jnp.clip(pick(seg_end) - item_blk * rb, 0, rb), 0)

    experts = jnp.arange(N_EXPERTS, dtype=jnp.int32)
    nonempty = counts > 0
    slot_e = (jnp.cumsum(nonempty.astype(jnp.int32)) - 1) % 2
    later = jnp.logical_and(experts[None, :] > experts[:, None], nonempty[None, :])
    nxt_e = jnp.min(jnp.where(later, experts[None, :], N_EXPERTS), axis=1)
    nxt_e = jnp.where(nxt_e == N_EXPERTS, -1, nxt_e)
    item_new = jnp.logical_and(valid, item_c == pick(item_start)).astype(jnp.int32)

    xg, dest8 = _dispatch(seg_start, idx.reshape(-1), rank.reshape(-1), x1r, DISPATCH_TT)
    y = _experts(item_blk, item_e, item_lo, item_hi, item_new, pick(slot_e), pick(nxt_e),
                 xg, w_eg, w_eu, w_ed, rb)
    return _combine(dest8, gates, x1, y, w_sg.astype(BF16), w_su.astype(BF16),
                    w_sd.astype(BF16), ln_g[None, :], ln_b[None, :], alpha, COMBINE_TT, mp)


def _layer(xp, xs, conv_s, c_s, n_s, m_s, w_in, b_gates, w_conv, mh_gain, w_out, ln1_g, ln1_b,
           w_router, b_router, w_eg, w_eu, w_ed, w_sg, w_su, w_sd, ln2_g, ln2_b, alpha):
    bp, tp, _ = xp.shape
    bs, ts, _ = xs.shape
    mp, ms = bp * tp, bs * ts
    xp2 = xp.reshape(mp, D_MODEL)
    xs2 = xs.reshape(ms, D_MODEL)

    w_main = jnp.concatenate([w_in[:, :GATE_COL0], w_in[:, GATE_COL0 + 2 * N_HEADS:]],
                             axis=1).astype(BF16)
    w_gate = jnp.pad(w_in[:, GATE_COL0:GATE_COL0 + 2 * N_HEADS],
                     ((0, 0), (0, LANES - 2 * N_HEADS)))
    wg_hi, wg_lo = _split_hi_lo(w_gate)

    z_s = _proj(xs2, w_main, PROJ_TM, PROJ_TN)
    zg_s = _gate_proj(xs2, wg_hi, wg_lo, PROJ_TM)
    gt_s = zg_s[:, :SUBLANES].reshape(bs, ts, SUBLANES).transpose(0, 2, 1)
    mix_s, conv_n, c_n, n_n, m_n = _mixer(
        z_s, zg_s, gt_s, b_gates, w_conv, mh_gain[None, :],
        conv_s, c_s, n_s, m_s[:, None, :], nb=bs, nt=1, tt=ts, chunk=ts)

    x1, x1r, conv_p, c_p, n_p, m_p = _prompt_layer_half(
        xp2, xs2, mix_s, b_gates, w_main, wg_hi, wg_lo, w_conv, mh_gain[None, :],
        w_out.astype(BF16), ln1_g[None, :], ln1_b[None, :], alpha,
        nb=bp, nt=tp // MIX_TT, tt=MIX_TT, chunk=MIX_CHUNK)

    x2p, x2s = _moe(x1, x1r, mp, w_router, b_router, w_eg, w_eu, w_ed, w_sg, w_su, w_sd,
                    ln2_g, ln2_b, alpha)
    states_p = (conv_p, c_p, n_p, m_p[:, 0, :])
    states_s = (conv_n, c_n, n_n, m_n[:, 0, :])
    return x2p.reshape(bp, tp, D_MODEL), x2s.reshape(bs, ts, D_MODEL), states_p, states_s


def kernel(x_prompt, x_sample, cache_conv, state_mlstm_C, state_mlstm_n, state_mlstm_m, w_in, b_gates, w_conv, mh_gain, w_out, ln1_g, ln1_b, w_router, b_router, w_exp_gate, w_exp_up, w_exp_down, w_sh_gate, w_sh_up, w_sh_down, ln2_g, ln2_b):
    depth = w_in.shape[0]
    alpha = (2.0 * depth) ** 0.25
    hp, hs = x_prompt, x_sample
    outs_p = [[], [], [], []]
    outs_s = [[], [], [], []]
    for l in range(depth):
        hp, hs, st_p, st_s = _layer(
            hp, hs, cache_conv[l], state_mlstm_C[l], state_mlstm_n[l], state_mlstm_m[l],
            w_in[l], b_gates[l], w_conv[l], mh_gain[l], w_out[l], ln1_g[l], ln1_b[l],
            w_router[l], b_router[l], w_exp_gate[l], w_exp_up[l], w_exp_down[l],
            w_sh_gate[l], w_sh_up[l], w_sh_down[l], ln2_g[l], ln2_b[l], alpha)
        for acc, val in zip(outs_p, st_p):
            acc.append(val)
        for acc, val in zip(outs_s, st_s):
            acc.append(val)
    return (hp, hs) + tuple(jnp.stack(a) for a in outs_p) + tuple(jnp.stack(a) for a in outs_s)
```

```python
import functools

import jax
import jax.numpy as jnp
from jax import lax
from jax.experimental import pallas as pl
from jax.experimental.pallas import tpu as pltpu

F32 = jnp.float32
BF16 = jnp.bfloat16

D_MODEL = 1024
N_HEADS = 4
HEAD_V = 256
HEAD_QK = 128
N_EXPERTS = 256
TOP_K = 8
D_EXPERT = 256
ROUTED_SCALE = 2.5
LN_EPS = 1e-5
CONV_K = 3

LANES = 128
SUBLANES = 8
ROW_TILES = D_MODEL // LANES
GATE_COL0 = 6 * D_MODEL

VMEM_LIMIT = 56 * 1024 * 1024


def _cparams(sem):
    return pltpu.CompilerParams(dimension_semantics=sem, vmem_limit_bytes=VMEM_LIMIT)


def _sigmoid(x):
    return 1.0 / (1.0 + jnp.exp(-x))


def _log_sigmoid(x):
    return jnp.minimum(x, 0.0) - jnp.log(1.0 + jnp.exp(-jnp.abs(x)))


def _layer_norm(x, g, b):
    mu = jnp.mean(x, axis=-1, keepdims=True)
    xc = x - mu
    var = jnp.mean(xc * xc, axis=-1, keepdims=True)
    return xc * lax.rsqrt(var + LN_EPS) * g + b


def _split_hi_lo(x):
    hi = x.astype(BF16)
    lo = (x - hi.astype(F32)).astype(BF16)
    return hi, lo


def _dot(a, b):
    return jnp.dot(a, b, preferred_element_type=F32)


def _dot3(x, w_hi, w_lo):
    x_hi, x_lo = _split_hi_lo(x)
    return _dot(x_hi, w_hi) + _dot(x_lo, w_hi) + _dot(x_hi, w_lo)


def _proj_kernel(x_ref, w_ref, z_ref):
    z_ref[...] = _dot(x_ref[...].astype(BF16), w_ref[...])


def _proj(x, w, tm, tn):
    m, k = x.shape
    n = w.shape[1]
    return pl.pallas_call(
        _proj_kernel,
        grid=(m // tm, n // tn),
        in_specs=[pl.BlockSpec((tm, k), lambda i, j: (i, 0)),
                  pl.BlockSpec((k, tn), lambda i, j: (0, j))],
        out_specs=pl.BlockSpec((tm, tn), lambda i, j: (i, j)),
        out_shape=jax.ShapeDtypeStruct((m, n), F32),
        compiler_params=_cparams(("parallel", "arbitrary")),
        name="proj",
    )(x, w)


def _gate_proj_kernel(x_ref, wh_ref, wl_ref, z_ref):
    z_ref[...] = _dot3(x_ref[...], wh_ref[...], wl_ref[...])


def _gate_proj(x, w_hi, w_lo, tm):
    m, k = x.shape
    n = w_hi.shape[1]
    return pl.pallas_call(
        _gate_proj_kernel,
        grid=(m // tm,),
        in_specs=[pl.BlockSpec((tm, k), lambda i: (i, 0)),
                  pl.BlockSpec((k, n), lambda i: (0, 0)),
                  pl.BlockSpec((k, n), lambda i: (0, 0))],
        out_specs=pl.BlockSpec((tm, n), lambda i: (i, 0)),
        out_shape=jax.ShapeDtypeStruct((m, n), F32),
        compiler_params=_cparams(("parallel",)),
        name="gate_proj",
    )(x, w_hi, w_lo)


def _conv_branch(u, zcb, wc, conv_ref):
    tt = u.shape[0]
    carry = conv_ref[0]
    rows = lax.broadcasted_iota(jnp.int32, (tt, D_MODEL), 0)
    u1 = jnp.where(rows == 0, carry[1:2, :], pltpu.roll(u, 1, 0))
    u2 = jnp.where(rows == 0, carry[0:1, :],
                   jnp.where(rows == 1, carry[1:2, :], pltpu.roll(u, 2, 0)))
    conv_out = u2 * wc[0:1, :] + u1 * wc[1:2, :] + u * wc[2:3, :]
    conv_ref[0] = u[tt - 2:tt, :]
    return zcb * conv_out


def _mlstm_chunk(q, k, v, ig_col, lf_col, ig_row, lf_row, c_state, n_row, m_prev):
    chunk = q.shape[0]
    ti = lax.broadcasted_iota(jnp.int32, (chunk, chunk), 0)
    si = lax.broadcasted_iota(jnp.int32, (chunk, chunk), 1)
    causal = si <= ti
    b_col = jnp.sum(jnp.where(causal, lf_row, 0.0), axis=1, keepdims=True)
    b_row = jnp.sum(jnp.where(ti <= si, lf_col, 0.0), axis=0, keepdims=True)
    inter = b_col + m_prev
    dmat = jnp.where(causal, b_col - b_row + ig_row, -jnp.inf)
    m_t = jnp.maximum(inter, jnp.max(dmat, axis=1, keepdims=True))
    w_intra = jnp.exp(dmat - m_t)
    w_inter = jnp.exp(inter - m_t)
    qb = q.astype(BF16)
    kb = k.astype(BF16)
    vb = v.astype(BF16)
    s = lax.dot_general(qb, kb, (((1,), (1,)), ((), ())), preferred_element_type=F32) * w_intra
    num = w_inter * _dot(qb, c_state.astype(BF16)) + _dot(s.astype(BF16), vb)
    den = (w_inter * jnp.sum(q * n_row, axis=1, keepdims=True)
           + jnp.sum(s, axis=1, keepdims=True))
    h = num / jnp.maximum(jnp.abs(den), jnp.exp(-m_t))
    m_new = m_t[chunk - 1:chunk, :]
    b_last = b_col[chunk - 1:chunk, :]
    ws_col = jnp.exp(b_last - b_col + ig_col - m_new)
    cdecay = jnp.exp(inter[chunk - 1:chunk, :] - m_new)
    kw = k * ws_col
    c_new = cdecay * c_state + lax.dot_general(
        kw.astype(BF16), vb, (((0,), (0,)), ((), ())), preferred_element_type=F32)
    n_new = cdecay * n_row + jnp.sum(kw, axis=0, keepdims=True)
    return h, c_new, n_new, m_new


def _mlstm_branch(chunk, bg_ref, zg, gt, zqk, zv, zog, gain_ref, c_ref, n_ref, m_ref, mo_ref):
    tt = zqk.shape[0]
    for h in range(N_HEADS):
        ig_col_all = zg[:, h:h + 1] + bg_ref[h]
        lf_col_all = _log_sigmoid(zg[:, N_HEADS + h:N_HEADS + h + 1] + bg_ref[N_HEADS + h])
        ig_row_all = gt[h:h + 1, :] + bg_ref[h]
        lf_row_all = _log_sigmoid(gt[N_HEADS + h:N_HEADS + h + 1, :] + bg_ref[N_HEADS + h])
        vcols = slice(h * HEAD_V, (h + 1) * HEAD_V)
        for c in range(tt // chunk):
            r0, r1 = c * chunk, (c + 1) * chunk
            q = zqk[r0:r1, h * HEAD_QK:(h + 1) * HEAD_QK]
            k = zqk[r0:r1, (N_HEADS + h) * HEAD_QK:(N_HEADS + h + 1) * HEAD_QK] * (HEAD_QK ** -0.5)
            hh, c_new, n_new, m_new = _mlstm_chunk(
                q, k, zv[r0:r1, vcols], ig_col_all[r0:r1, :], lf_col_all[r0:r1, :],
                ig_row_all[:, r0:r1], lf_row_all[:, r0:r1],
                c_ref[0, h], n_ref[0, h:h + 1, :], m_ref[0, :, h:h + 1])
            c_ref[0, h] = c_new
            n_ref[0, h:h + 1, :] = n_new
            m_ref[0, :, h:h + 1] = m_new
            mu = jnp.mean(hh, axis=-1, keepdims=True)
            hc = hh - mu
            var = jnp.mean(hc * hc, axis=-1, keepdims=True)
            hn = hc * lax.rsqrt(var + LN_EPS) * gain_ref[:, vcols]
            mo_ref[r0:r1, vcols] = hn * _sigmoid(zog[r0:r1, vcols])


def _store_x1(x, mix, wout_ref, g_ref, b_ref, alpha, x1_ref, x1r_ref):
    tt = x.shape[0]
    x1 = _layer_norm(alpha * x + _dot(mix.astype(BF16), wout_ref[...]), g_ref[...], b_ref[...])
    x1_ref[...] = x1
    for s in range(ROW_TILES):
        x1r_ref[pl.ds(s, tt, stride=ROW_TILES), :] = x1[:, s * LANES:(s + 1) * LANES]


def _mixer_kernel(chunk, bg_ref,
                  zcb_ref, zcc_ref, zch_ref, zqk_ref, zv_ref, zog_ref, zga_ref, zgb_ref,
                  zg_ref, gt_ref, wconv_ref, gain_ref,
                  conv0_ref, c0_ref, n0_ref, m0_ref,
                  mix_ref, conv_ref, c_ref, n_ref, m_ref, mo_ref):
    @pl.when(pl.program_id(1) == 0)
    def _():
        conv_ref[...] = conv0_ref[...]
        c_ref[...] = c0_ref[...]
        n_ref[...] = n0_ref[...]
        m_ref[...] = m0_ref[...]

    a = _conv_branch(zcc_ref[...] * zch_ref[...], zcb_ref[...], wconv_ref[...], conv_ref)
    _mlstm_branch(chunk, bg_ref, zg_ref[...], gt_ref[0], zqk_ref[...], zv_ref[...], zog_ref[...],
                  gain_ref, c_ref, n_ref, m_ref, mo_ref)
    mix_ref[...] = _sigmoid(zga_ref[...]) * a + _sigmoid(zgb_ref[...]) * mo_ref[...]


def _mixer(z, zg, gt3, b_gates, w_conv, mh_gain, conv0, c0, n0, m0, *, nb, nt, tt, chunk):
    def zspec(j):
        return pl.BlockSpec((tt, D_MODEL), lambda b, t, j=j: (b * nt + t, j))

    in_specs = [pl.BlockSpec(memory_space=pltpu.SMEM)]
    in_specs += [zspec(j) for j in range(8)]
    in_specs += [
        pl.BlockSpec((tt, LANES), lambda b, t: (b * nt + t, 0)),
        pl.BlockSpec((1, SUBLANES, tt), lambda b, t: (b * nt + t, 0, 0)),
        pl.BlockSpec((CONV_K, D_MODEL), lambda b, t: (0, 0)),
        pl.BlockSpec((1, D_MODEL), lambda b, t: (0, 0)),
        pl.BlockSpec((1, CONV_K - 1, D_MODEL), lambda b, t: (b, 0, 0)),
        pl.BlockSpec((1, N_HEADS, HEAD_QK, HEAD_V), lambda b, t: (b, 0, 0, 0)),
        pl.BlockSpec((1, N_HEADS, HEAD_QK), lambda b, t: (b, 0, 0)),
        pl.BlockSpec((1, 1, N_HEADS), lambda b, t: (b, 0, 0)),
    ]
    out_specs = [
        pl.BlockSpec((tt, D_MODEL), lambda b, t: (b * nt + t, 0)),
        pl.BlockSpec((1, CONV_K - 1, D_MODEL), lambda b, t: (b, 0, 0)),
        pl.BlockSpec((1, N_HEADS, HEAD_QK, HEAD_V), lambda b, t: (b, 0, 0, 0)),
        pl.BlockSpec((1, N_HEADS, HEAD_QK), lambda b, t: (b, 0, 0)),
        pl.BlockSpec((1, 1, N_HEADS), lambda b, t: (b, 0, 0)),
    ]
    out_shape = [
        jax.ShapeDtypeStruct((nb * nt * tt, D_MODEL), F32),
        jax.ShapeDtypeStruct((nb, CONV_K - 1, D_MODEL), F32),
        jax.ShapeDtypeStruct((nb, N_HEADS, HEAD_QK, HEAD_V), F32),
        jax.ShapeDtypeStruct((nb, N_HEADS, HEAD_QK), F32),
        jax.ShapeDtypeStruct((nb, 1, N_HEADS), F32),
    ]
    return pl.pallas_call(
        functools.partial(_mixer_kernel, chunk),
        grid=(nb, nt),
        in_specs=in_specs,
        out_specs=out_specs,
        out_shape=out_shape,
        scratch_shapes=[pltpu.VMEM((tt, D_MODEL), F32)],
        compiler_params=_cparams(("arbitrary", "arbitrary")),
        name="mixer_t%d" % tt,
    )(b_gates, z, z, z, z, z, z, z, z, zg, gt3, w_conv, mh_gain, conv0, c0, n0, m0)


def _prompt_kernel(chunk, nb, ns, alpha, bg_ref, x_ref, w_ref, wgh_ref, wgl_ref, wconv_ref,
                   gain_ref, wout_ref, g_ref, b_ref, xs_ref, mixs_ref,
                   x1_ref, x1r_ref, conv_ref, c_ref, n_ref, m_ref, mo_ref):
    b = pl.program_id(0)
    t = pl.program_id(1)

    @pl.when(b < nb)
    def _():
        @pl.when(t == 0)
        def _():
            conv_ref[...] = jnp.zeros_like(conv_ref)
            c_ref[...] = jnp.zeros_like(c_ref)
            n_ref[...] = jnp.zeros_like(n_ref)
            m_ref[...] = jnp.zeros_like(m_ref)

        x = x_ref[...]
        xb = x.astype(BF16)

        def z(j):
            return _dot(xb, w_ref[:, j * D_MODEL:(j + 1) * D_MODEL])

        a = _conv_branch(z(1) * z(2), z(0), wconv_ref[...], conv_ref)
        zg = _dot3(x, wgh_ref[...], wgl_ref[...])
        _mlstm_branch(chunk, bg_ref, zg, zg.T, z(3), z(4), z(5), gain_ref,
                      c_ref, n_ref, m_ref, mo_ref)
        mix = _sigmoid(z(6)) * a + _sigmoid(z(7)) * mo_ref[...]
        _store_x1(x, mix, wout_ref, g_ref, b_ref, alpha, x1_ref, x1r_ref)

    @pl.when(jnp.logical_and(b == nb, t < ns))
    def _():
        _store_x1(xs_ref[...], mixs_ref[...], wout_ref, g_ref, b_ref, alpha, x1_ref, x1r_ref)


def _prompt_layer_half(xp, xs, mix_s, b_gates, w_main, wg_hi, wg_lo, w_conv, mh_gain, w_out,
                       ln_g, ln_b, alpha, *, nb, nt, tt, chunk):
    mp, ms = xp.shape[0], xs.shape[0]
    ns = ms // tt
    last_p = nb * nt - 1

    def prow(b, t):
        return jnp.minimum(b * nt + t, last_p)

    def srow(b, t):
        return jnp.where(b == nb, jnp.minimum(t, ns - 1), 0)

    def orow(b, t):
        return jnp.where(b == nb, nb * nt + jnp.minimum(t, ns - 1), b * nt + t)

    def state(b):
        return jnp.minimum(b, nb - 1)

    const2 = lambda b, t: (0, 0)
    in_specs = [
        pl.BlockSpec(memory_space=pltpu.SMEM),
        pl.BlockSpec((tt, D_MODEL), lambda b, t: (prow(b, t), 0)),
        pl.BlockSpec(w_main.shape, const2, pipeline_mode=pl.Buffered(1)),
        pl.BlockSpec(wg_hi.shape, const2),
        pl.BlockSpec(wg_lo.shape, const2),
        pl.BlockSpec((CONV_K, D_MODEL), const2),
        pl.BlockSpec((1, D_MODEL), const2),
        pl.BlockSpec((D_MODEL, D_MODEL), const2),
        pl.BlockSpec((1, D_MODEL), const2),
        pl.BlockSpec((1, D_MODEL), const2),
        pl.BlockSpec((tt, D_MODEL), lambda b, t: (srow(b, t), 0)),
        pl.BlockSpec((tt, D_MODEL), lambda b, t: (srow(b, t), 0)),
    ]
    out_specs = [
        pl.BlockSpec((tt, D_MODEL), lambda b, t: (orow(b, t), 0)),
        pl.BlockSpec((tt * ROW_TILES, LANES), lambda b, t: (orow(b, t), 0)),
        pl.BlockSpec((1, CONV_K - 1, D_MODEL), lambda b, t: (state(b), 0, 0)),
        pl.BlockSpec((1, N_HEADS, HEAD_QK, HEAD_V), lambda b, t: (state(b), 0, 0, 0)),
        pl.BlockSpec((1, N_HEADS, HEAD_QK), lambda b, t: (state(b), 0, 0)),
        pl.BlockSpec((1, 1, N_HEADS), lambda b, t: (state(b), 0, 0)),
    ]
    out_shape = [
        jax.ShapeDtypeStruct((mp + ms, D_MODEL), F32),
        jax.ShapeDtypeStruct(((mp + ms) * ROW_TILES, LANES), F32),
        jax.ShapeDtypeStruct((nb, CONV_K - 1, D_MODEL), F32),
        jax.ShapeDtypeStruct((nb, N_HEADS, HEAD_QK, HEAD_V), F32),
        jax.ShapeDtypeStruct((nb, N_HEADS, HEAD_QK), F32),
        jax.ShapeDtypeStruct((nb, 1, N_HEADS), F32),
    ]
    return pl.pallas_call(
        functools.partial(_prompt_kernel, chunk, nb, ns, alpha),
        grid=(nb + 1, nt),
        in_specs=in_specs,
        out_specs=out_specs,
        out_shape=out_shape,
        scratch_shapes=[pltpu.VMEM((tt, D_MODEL), F32)],
        compiler_params=_cparams(("arbitrary", "arbitrary")),
        name="prompt_half",
    )(b_gates, xp, w_main, wg_hi, wg_lo, w_conv, mh_gain, w_out, ln_g, ln_b, xs, mix_s)


def _router_kernel(x_ref, wh_ref, wl_ref, br_ref, idx_ref, gate_ref, rank_ref, cnt_ref):
    tt = x_ref.shape[0]

    @pl.when(pl.program_id(0) == 0)
    def _():
        cnt_ref[...] = jnp.zeros_like(cnt_ref)

    scores = _sigmoid(_dot3(x_ref[...], wh_ref[...], wl_ref[...]))
    lane = lax.broadcasted_iota(jnp.int32, (tt, N_EXPERTS), 1)
    work = scores + br_ref[...]
    picked = []
    sel = []
    member = jnp.zeros((tt, N_EXPERTS), F32)
    for _ in range(TOP_K):
        mx = jnp.max(work, axis=1, keepdims=True)
        ik = jnp.min(jnp.where(work == mx, lane, N_EXPERTS), axis=1, keepdims=True)
        onehot = lane == ik
        picked.append((ik, onehot))
        sel.append(jnp.sum(jnp.where(onehot, scores, 0.0), axis=1, keepdims=True))
        work = jnp.where(onehot, -jnp.inf, work)
        member = member + onehot.astype(F32)
    total = sel[0]
    for sk in sel[1:]:
        total = total + sk

    ti = lax.broadcasted_iota(jnp.int32, (tt, tt), 0)
    si = lax.broadcasted_iota(jnp.int32, (tt, tt), 1)
    earlier = (si < ti).astype(BF16)
    before = _dot(earlier, member.astype(BF16)) + cnt_ref[...]

    lane_o = lax.broadcasted_iota(jnp.int32, (tt, TOP_K), 1)
    idx_o = jnp.zeros((tt, TOP_K), jnp.int32)
    gate_o = jnp.zeros((tt, TOP_K), F32)
    rank_o = jnp.zeros((tt, TOP_K), jnp.int32)
    for kk in range(TOP_K):
        ik, onehot = picked[kk]
        rk = jnp.sum(jnp.where(onehot, before, 0.0), axis=1, keepdims=True)
        idx_o = jnp.where(lane_o == kk, ik, idx_o)
        gate_o = jnp.where(lane_o == kk, sel[kk] / total * ROUTED_SCALE, gate_o)
        rank_o = jnp.where(lane_o == kk, rk.astype(jnp.int32), rank_o)
    idx_ref[...] = idx_o
    gate_ref[...] = gate_o
    rank_ref[...] = rank_o
    cnt_ref[...] = cnt_ref[...] + jnp.sum(member, axis=0, keepdims=True)


def _router(x1, wr_hi, wr_lo, b_router, tt):
    m = x1.shape[0]
    return pl.pallas_call(
        _router_kernel,
        grid=(m // tt,),
        in_specs=[pl.BlockSpec((tt, D_MODEL), lambda i: (i, 0)),
                  pl.BlockSpec((D_MODEL, N_EXPERTS), lambda i: (0, 0)),
                  pl.BlockSpec((D_MODEL, N_EXPERTS), lambda i: (0, 0)),
                  pl.BlockSpec((1, N_EXPERTS), lambda i: (0, 0))],
        out_specs=[pl.BlockSpec((tt, TOP_K), lambda i: (i, 0)),
                   pl.BlockSpec((tt, TOP_K), lambda i: (i, 0)),
                   pl.BlockSpec((tt, TOP_K), lambda i: (i, 0)),
                   pl.BlockSpec((1, N_EXPERTS), lambda i: (0, 0))],
        out_shape=[jax.ShapeDtypeStruct((m, TOP_K), jnp.int32),
                   jax.ShapeDtypeStruct((m, TOP_K), F32),
                   jax.ShapeDtypeStruct((m, TOP_K), jnp.int32),
                   jax.ShapeDtypeStruct((1, N_EXPERTS), F32)],
        compiler_params=_cparams(("arbitrary",)),
        name="router",
    )(x1, wr_hi, wr_lo, b_router)


def _row_copy(src, src_row8, dst, dst_row8, sem):
    return pltpu.make_async_copy(src.at[pl.ds(pl.multiple_of(src_row8, SUBLANES), SUBLANES)],
                                 dst.at[pl.ds(pl.multiple_of(dst_row8, SUBLANES), SUBLANES)],
                                 sem)


def _dispatch_kernel(tt, start_ref, idx_ref, rank_ref, x1r_ref, xg_ref, dest_ref, zbuf, sem, zsem):
    @pl.when(pl.program_id(0) == 0)
    def _():
        zbuf[...] = jnp.zeros_like(zbuf)
        n_sorted = xg_ref.shape[0] - zbuf.shape[0]
        fill = pltpu.make_async_copy(zbuf, xg_ref.at[pl.ds(n_sorted, zbuf.shape[0])], zsem)
        fill.start()
        fill.wait()

    def issue(t, carry):
        for kk in range(TOP_K):
            j = t * TOP_K + kk
            dest8 = (start_ref[idx_ref[j]] + rank_ref[j]) * ROW_TILES
            dest_ref[j] = dest8
            _row_copy(x1r_ref, t * ROW_TILES, xg_ref, dest8, sem).start()
        return carry

    lax.fori_loop(0, tt, issue, 0)
    n_rows = tt * TOP_K * ROW_TILES
    pltpu.make_async_copy(xg_ref.at[pl.ds(0, n_rows)], xg_ref.at[pl.ds(0, n_rows)], sem).wait()


def _dispatch(seg_start, idx, rank, x1r, tt, slack):
    m = x1r.shape[0] // ROW_TILES
    smem_blk = pl.BlockSpec((tt * TOP_K,), lambda i: (i,), memory_space=pltpu.SMEM)
    return pl.pallas_call(
        functools.partial(_dispatch_kernel, tt),
        grid=(m // tt,),
        in_specs=[pl.BlockSpec(memory_space=pltpu.SMEM), smem_blk, smem_blk,
                  pl.BlockSpec((tt * ROW_TILES, LANES), lambda i: (i, 0))],
        out_specs=[pl.BlockSpec(memory_space=pl.ANY), smem_blk],
        out_shape=[jax.ShapeDtypeStruct(((m * TOP_K + slack) * ROW_TILES, LANES), F32),
                   jax.ShapeDtypeStruct((m * TOP_K,), jnp.int32)],
        scratch_shapes=[pltpu.VMEM((slack * ROW_TILES, LANES), F32),
                        pltpu.SemaphoreType.DMA(()), pltpu.SemaphoreType.DMA(())],
        compiler_params=_cparams(("arbitrary",)),
        name="dispatch",
    )(seg_start, idx, rank, x1r)


def _experts_kernel(rb, e_ref, row_ref, valid_ref, new_ref, slot_ref, nxt_ref,
                    xg_hbm, wg_hbm, wu_hbm, wd_hbm, y_hbm,
                    xbuf0, xbuf1, ybuf0, ybuf1, wg_buf, wu_buf, wd_buf, wgb, wub, wdb,
                    wsem, xsem, ysem):
    j = pl.program_id(0)
    n_steps = pl.num_programs(0)
    parity = lax.rem(j, 2)
    valid = valid_ref[j]
    xbufs = (xbuf0, xbuf1)
    ybufs = (ybuf0, ybuf1)

    def tile_rows(row, n):
        return pl.ds(pl.multiple_of(row * ROW_TILES, SUBLANES), n * ROW_TILES)

    def x_copy(item, slot):
        return pltpu.make_async_copy(xg_hbm.at[tile_rows(row_ref[item], rb)], xbufs[slot],
                                     xsem.at[slot])

    def y_copies(item, slot, start):
        row0 = row_ref[item]
        n = valid_ref[item]

        def run(copy):
            if start:
                copy.start()
            else:
                copy.wait()

        @pl.when(n == rb)
        def _():
            run(pltpu.make_async_copy(ybufs[slot], y_hbm.at[tile_rows(row0, rb)], ysem.at[slot]))

        piece = rb // 2
        while piece >= 1:
            off = n - jnp.bitwise_and(n, 2 * piece - 1)

            @pl.when(jnp.logical_and(n < rb, jnp.bitwise_and(n, piece) != 0))
            def _(off=off, piece=piece):
                run(pltpu.make_async_copy(ybufs[slot].at[tile_rows(off, piece)],
                                          y_hbm.at[tile_rows(row0 + off, piece)], ysem.at[slot]))

            piece //= 2

    def fetch(e, slot):
        return (pltpu.make_async_copy(wg_hbm.at[e], wg_buf.at[slot], wsem.at[slot, 0]),
                pltpu.make_async_copy(wu_hbm.at[e], wu_buf.at[slot], wsem.at[slot, 1]),
                pltpu.make_async_copy(wd_hbm.at[e], wd_buf.at[slot], wsem.at[slot, 2]))

    @pl.when(j == 0)
    def _():
        x_copy(0, 0).start()

    @pl.when(new_ref[j] != 0)
    def _():
        slot = slot_ref[j]

        @pl.when(j == 0)
        def _():
            for copy in fetch(e_ref[j], slot):
                copy.start()

        for copy in fetch(e_ref[j], slot):
            copy.wait()
        wgb[...] = wg_buf[slot].astype(BF16)
        wub[...] = wu_buf[slot].astype(BF16)
        wdb[...] = wd_buf[slot].astype(BF16)

        @pl.when(nxt_ref[j] >= 0)
        def _():
            for copy in fetch(nxt_ref[j], 1 - slot):
                copy.start()

    nxt_item = jnp.minimum(j + 1, n_steps - 1)
    for slot in range(2):
        here = parity == slot

        @pl.when(jnp.logical_and(here, j >= 2))
        def _(slot=slot):
            y_copies(j - 2, slot, start=False)

        @pl.when(jnp.logical_and(here, valid > 0))
        def _(slot=slot):
            x_copy(j, slot).wait()

            @pl.when(jnp.logical_and(j + 1 < n_steps, valid_ref[nxt_item] > 0))
            def _():
                x_copy(nxt_item, 1 - slot).start()

            xb = jnp.concatenate(
                [xbufs[slot][pl.ds(s, rb, stride=ROW_TILES), :] for s in range(ROW_TILES)],
                axis=1).astype(BF16)
            g = _dot(xb, wgb[...])
            u = _dot(xb, wub[...])
            y = _dot((g * _sigmoid(g) * u).astype(BF16), wdb[...])
            for s in range(ROW_TILES):
                ybufs[slot][pl.ds(s, rb, stride=ROW_TILES), :] = y[:, s * LANES:(s + 1) * LANES]
            y_copies(j, slot, start=True)

        @pl.when(jnp.logical_and(here, j == n_steps - 1))
        def _(slot=slot):
            y_copies(j - 1, 1 - slot, start=False)
            y_copies(j, slot, start=False)


def _experts(item_e, item_row, item_valid, item_new, item_slot, item_nxt,
             xg, w_eg, w_eu, w_ed, rb):
    n_items = item_e.shape[0]
    n_rows = xg.shape[0] // ROW_TILES - rb
    hbm = pl.BlockSpec(memory_space=pl.ANY)
    row_buf = pltpu.VMEM((rb * ROW_TILES, LANES), F32)
    grid_spec = pltpu.PrefetchScalarGridSpec(
        num_scalar_prefetch=6,
        grid=(n_items,),
        in_specs=[hbm, hbm, hbm, hbm],
        out_specs=hbm,
        scratch_shapes=[row_buf, row_buf, row_buf, row_buf,
                        pltpu.VMEM((2, D_MODEL, D_EXPERT), F32),
                        pltpu.VMEM((2, D_MODEL, D_EXPERT), F32),
                        pltpu.VMEM((2, D_EXPERT, D_MODEL), F32),
                        pltpu.VMEM((D_MODEL, D_EXPERT), BF16),
                        pltpu.VMEM((D_MODEL, D_EXPERT), BF16),
                        pltpu.VMEM((D_EXPERT, D_MODEL), BF16),
                        pltpu.SemaphoreType.DMA((2, 3)),
                        pltpu.SemaphoreType.DMA((2,)),
                        pltpu.SemaphoreType.DMA((2,))],
    )
    return pl.pallas_call(
        functools.partial(_experts_kernel, rb),
        grid_spec=grid_spec,
        out_shape=jax.ShapeDtypeStruct((n_rows * ROW_TILES, LANES), F32),
        compiler_params=_cparams(("arbitrary",)),
        name="experts",
    )(item_e, item_row, item_valid, item_new, item_slot, item_nxt, xg, w_eg, w_eu, w_ed)


def _combine_kernel(alpha, n_p, n_steps, dest_ref, dnext_ref, gate_ref, x1_ref, y_ref,
                    wsg_ref, wsu_ref, wsd_ref, g_ref, b_ref, x2p_ref, x2s_ref,
                    ybuf_a, ybuf_b, sem_a, sem_b):
    step = pl.program_id(0)
    tt = x1_ref.shape[0] // 2
    n_rows = TOP_K * tt * ROW_TILES

    def issue(dref, half, buf, sem):
        def body(t, carry):
            for kk in range(TOP_K):
                _row_copy(y_ref, dref[(half * tt + t) * TOP_K + kk], buf,
                          (kk * tt + t) * ROW_TILES, sem).start()
            return carry

        lax.fori_loop(0, tt, body, 0)

    def wait(buf, sem):
        pltpu.make_async_copy(y_ref.at[pl.ds(0, n_rows)], buf, sem).wait()

    def routed(half, buf):
        gates = gate_ref[half * tt:(half + 1) * tt, :]
        chunks = []
        for s in range(ROW_TILES):
            acc = jnp.zeros((tt, LANES), F32)
            for kk in range(TOP_K):
                acc = acc + gates[:, kk:kk + 1] * buf[pl.ds(kk * tt * ROW_TILES + s, tt,
                                                            stride=ROW_TILES), :]
            chunks.append(acc)
        return jnp.concatenate(chunks, axis=1)

    @pl.when(step == 0)
    def _():
        issue(dest_ref, 0, ybuf_a, sem_a)

    issue(dest_ref, 1, ybuf_b, sem_b)

    x1 = x1_ref[...]
    xb = x1.astype(BF16)
    gs = _dot(xb, wsg_ref[...])
    us = _dot(xb, wsu_ref[...])
    base = alpha * x1 + _dot((gs * _sigmoid(gs) * us).astype(BF16), wsd_ref[...])

    wait(ybuf_a, sem_a)
    x2a = _layer_norm(base[:tt, :] + routed(0, ybuf_a), g_ref[...], b_ref[...])

    @pl.when(step + 1 < n_steps)
    def _():
        issue(dnext_ref, 0, ybuf_a, sem_a)

    wait(ybuf_b, sem_b)
    x2b = _layer_norm(base[tt:, :] + routed(1, ybuf_b), g_ref[...], b_ref[...])

    @pl.when(step < n_p)
    def _():
        x2p_ref[:tt, :] = x2a
        x2p_ref[tt:, :] = x2b

    @pl.when(step >= n_p)
    def _():
        x2s_ref[:tt, :] = x2a
        x2s_ref[tt:, :] = x2b


def _combine(dest8, gates, x1, y, wsg, wsu, wsd, g, b, alpha, tt, mp):
    m = x1.shape[0]
    st = 2 * tt
    n_steps = m // st
    n_p = mp // st
    return pl.pallas_call(
        functools.partial(_combine_kernel, alpha, n_p, n_steps),
        grid=(n_steps,),
        in_specs=[pl.BlockSpec((st * TOP_K,), lambda i: (i,), memory_space=pltpu.SMEM),
                  pl.BlockSpec((st * TOP_K,), lambda i: (jnp.minimum(i + 1, n_steps - 1),),
                               memory_space=pltpu.SMEM),
                  pl.BlockSpec((st, TOP_K), lambda i: (i, 0)),
                  pl.BlockSpec((st, D_MODEL), lambda i: (i, 0)),
                  pl.BlockSpec(memory_space=pl.ANY),
                  pl.BlockSpec((D_MODEL, D_EXPERT), lambda i: (0, 0)),
                  pl.BlockSpec((D_MODEL, D_EXPERT), lambda i: (0, 0)),
                  pl.BlockSpec((D_EXPERT, D_MODEL), lambda i: (0, 0)),
                  pl.BlockSpec((1, D_MODEL), lambda i: (0, 0)),
                  pl.BlockSpec((1, D_MODEL), lambda i: (0, 0))],
        out_specs=[pl.BlockSpec((st, D_MODEL), lambda i: (jnp.minimum(i, n_p - 1), 0)),
                   pl.BlockSpec((st, D_MODEL), lambda i: (jnp.maximum(i - n_p, 0), 0))],
        out_shape=[jax.ShapeDtypeStruct((mp, D_MODEL), F32),
                   jax.ShapeDtypeStruct((m - mp, D_MODEL), F32)],
        scratch_shapes=[pltpu.VMEM((TOP_K * tt * ROW_TILES, LANES), F32),
                        pltpu.VMEM((TOP_K * tt * ROW_TILES, LANES), F32),
                        pltpu.SemaphoreType.DMA(()), pltpu.SemaphoreType.DMA(())],
        compiler_params=_cparams(("arbitrary",)),
        name="combine",
    )(dest8, dest8, gates, x1, y, wsg, wsu, wsd, g, b)


PROJ_TM = 512
PROJ_TN = 1024
MIX_TT = 256
MIX_CHUNK = 128
ROUTER_TT = 256
DISPATCH_TT = 512
EXPERT_RB = 256
COMBINE_TT = 128


def _moe(x1, x1r, mp, w_router, b_router, w_eg, w_eu, w_ed, w_sg, w_su, w_sd, ln_g, ln_b, alpha):
    m = x1.shape[0]
    wr_hi, wr_lo = _split_hi_lo(w_router)
    idx, gates, rank, cnt = _router(x1, wr_hi, wr_lo, b_router[None, :], ROUTER_TT)

    rb = EXPERT_RB
    n_rows = m * TOP_K
    n_items = n_rows // rb + N_EXPERTS
    counts = cnt[0].astype(jnp.int32)
    seg_end = jnp.cumsum(counts)
    seg_start = seg_end - counts
    n_chunk = (counts + rb - 1) // rb
    item_end = jnp.cumsum(n_chunk)
    item_start = item_end - n_chunk
    item = jnp.arange(n_items, dtype=jnp.int32)
    item_c = jnp.minimum(item, item_end[-1] - 1)
    item_e = jnp.sum((item_end[None, :] <= item_c[:, None]).astype(jnp.int32), axis=1)
    onehot_e = item_e[:, None] == jnp.arange(N_EXPERTS, dtype=jnp.int32)[None, :]

    def pick(table):
        return jnp.sum(jnp.where(onehot_e, table[None, :], 0), axis=1)

    chunk = item_c - pick(item_start)
    valid = item < item_end[-1]
    item_row = pick(seg_start) + chunk * rb
    item_valid = jnp.where(valid, jnp.clip(pick(counts) - chunk * rb, 0, rb), 0)

    experts = jnp.arange(N_EXPERTS, dtype=jnp.int32)
    nonempty = counts > 0
    slot_e = (jnp.cumsum(nonempty.astype(jnp.int32)) - 1) % 2
    later = jnp.logical_and(experts[None, :] > experts[:, None], nonempty[None, :])
    nxt_e = jnp.min(jnp.where(later, experts[None, :], N_EXPERTS), axis=1)
    nxt_e = jnp.where(nxt_e == N_EXPERTS, -1, nxt_e)
    item_new = jnp.logical_and(valid, chunk == 0).astype(jnp.int32)

    xg, dest8 = _dispatch(seg_start, idx.reshape(-1), rank.reshape(-1), x1r, DISPATCH_TT, rb)
    y = _experts(item_e, item_row, item_valid, item_new, pick(slot_e), pick(nxt_e),
                 xg, w_eg, w_eu, w_ed, rb)
    return _combine(dest8, gates, x1, y, w_sg.astype(BF16), w_su.astype(BF16),
                    w_sd.astype(BF16), ln_g[None, :], ln_b[None, :], alpha, COMBINE_TT, mp)


def _layer(xp, xs, conv_s, c_s, n_s, m_s, w_in, b_gates, w_conv, mh_gain, w_out, ln1_g, ln1_b,
           w_router, b_router, w_eg, w_eu, w_ed, w_sg, w_su, w_sd, ln2_g, ln2_b, alpha):
    bp, tp, _ = xp.shape
    bs, ts, _ = xs.shape
    mp, ms = bp * tp, bs * ts
    xp2 = xp.reshape(mp, D_MODEL)
    xs2 = xs.reshape(ms, D_MODEL)

    w_main = jnp.concatenate([w_in[:, :GATE_COL0], w_in[:, GATE_COL0 + 2 * N_HEADS:]],
                             axis=1).astype(BF16)
    w_gate = jnp.pad(w_in[:, GATE_COL0:GATE_COL0 + 2 * N_HEADS],
                     ((0, 0), (0, LANES - 2 * N_HEADS)))
    wg_hi, wg_lo = _split_hi_lo(w_gate)

    z_s = _proj(xs2, w_main, PROJ_TM, PROJ_TN)
    zg_s = _gate_proj(xs2, wg_hi, wg_lo, PROJ_TM)
    gt_s = zg_s[:, :SUBLANES].reshape(bs, ts, SUBLANES).transpose(0, 2, 1)
    mix_s, conv_n, c_n, n_n, m_n = _mixer(
        z_s, zg_s, gt_s, b_gates, w_conv, mh_gain[None, :],
        conv_s, c_s, n_s, m_s[:, None, :], nb=bs, nt=1, tt=ts, chunk=ts)

    x1, x1r, conv_p, c_p, n_p, m_p = _prompt_layer_half(
        xp2, xs2, mix_s, b_gates, w_main, wg_hi, wg_lo, w_conv, mh_gain[None, :],
        w_out.astype(BF16), ln1_g[None, :], ln1_b[None, :], alpha,
        nb=bp, nt=tp // MIX_TT, tt=MIX_TT, chunk=MIX_CHUNK)

    x2p, x2s = _moe(x1, x1r, mp, w_router, b_router, w_eg, w_eu, w_ed, w_sg, w_su, w_sd,
                    ln2_g, ln2_b, alpha)
    states_p = (conv_p, c_p, n_p, m_p[:, 0, :])
    states_s = (conv_n, c_n, n_n, m_n[:, 0, :])
    return x2p.reshape(bp, tp, D_MODEL), x2s.reshape(bs, ts, D_MODEL), states_p, states_s


def kernel(x_prompt, x_sample, cache_conv, state_mlstm_C, state_mlstm_n, state_mlstm_m, w_in, b_gates, w_conv, mh_gain, w_out, ln1_g, ln1_b, w_router, b_router, w_exp_gate, w_exp_up, w_exp_down, w_sh_gate, w_sh_up, w_sh_down, ln2_g, ln2_b):
    depth = w_in.shape[0]
    alpha = (2.0 * depth) ** 0.25
    hp, hs = x_prompt, x_sample
    outs_p = [[], [], [], []]
    outs_s = [[], [], [], []]
    for l in range(depth):
        hp, hs, st_p, st_s = _layer(
            hp, hs, cache_conv[l], state_mlstm_C[l], state_mlstm_n[l], state_mlstm_m[l],
            w_in[l], b_gates[l], w_conv[l], mh_gain[l], w_out[l], ln1_g[l], ln1_b[l],
            w_router[l], b_router[l], w_exp_gate[l], w_exp_up[l], w_exp_down[l],
            w_sh_gate[l], w_sh_up[l], w_sh_down[l], ln2_g[l], ln2_b[l], alpha)
        for acc, val in zip(outs_p, st_p):
            acc.append(val)
        for acc, val in zip(outs_s, st_s):
            acc.append(val)
    return (hp, hs) + tuple(jnp.stack(a) for a in outs_p) + tuple(jnp.stack(a) for a in outs_s)
```

```python
import functools

import jax
import jax.numpy as jnp
from jax import lax
from jax.experimental import pallas as pl
from jax.experimental.pallas import tpu as pltpu

F32 = jnp.float32
BF16 = jnp.bfloat16

D_MODEL = 1024
N_HEADS = 4
HEAD_V = 256
HEAD_QK = 128
N_EXPERTS = 256
TOP_K = 8
D_EXPERT = 256
ROUTED_SCALE = 2.5
LN_EPS = 1e-5
CONV_K = 3

LANES = 128
SUBLANES = 8
ROW_TILES = D_MODEL // LANES
GATE_COL0 = 6 * D_MODEL

VMEM_LIMIT = 56 * 1024 * 1024


def _cparams(sem):
    return pltpu.CompilerParams(dimension_semantics=sem, vmem_limit_bytes=VMEM_LIMIT)


def _sigmoid(x):
    return 1.0 / (1.0 + jnp.exp(-x))


def _log_sigmoid(x):
    return jnp.minimum(x, 0.0) - jnp.log(1.0 + jnp.exp(-jnp.abs(x)))


def _layer_norm(x, g, b):
    mu = jnp.mean(x, axis=-1, keepdims=True)
    xc = x - mu
    var = jnp.mean(xc * xc, axis=-1, keepdims=True)
    return xc * lax.rsqrt(var + LN_EPS) * g + b


def _split_hi_lo(x):
    hi = x.astype(BF16)
    lo = (x - hi.astype(F32)).astype(BF16)
    return hi, lo


def _dot(a, b):
    return jnp.dot(a, b, preferred_element_type=F32)


def _dot3(x, w_hi, w_lo):
    x_hi, x_lo = _split_hi_lo(x)
    return _dot(x_hi, w_hi) + _dot(x_lo, w_hi) + _dot(x_hi, w_lo)


def _proj_kernel(x_ref, w_ref, z_ref):
    z_ref[...] = _dot(x_ref[...].astype(BF16), w_ref[...])


def _proj(x, w, tm, tn):
    m, k = x.shape
    n = w.shape[1]
    return pl.pallas_call(
        _proj_kernel,
        grid=(m // tm, n // tn),
        in_specs=[pl.BlockSpec((tm, k), lambda i, j: (i, 0)),
                  pl.BlockSpec((k, tn), lambda i, j: (0, j))],
        out_specs=pl.BlockSpec((tm, tn), lambda i, j: (i, j)),
        out_shape=jax.ShapeDtypeStruct((m, n), F32),
        compiler_params=_cparams(("parallel", "arbitrary")),
        name="proj",
    )(x, w)


def _gate_proj_kernel(x_ref, wh_ref, wl_ref, z_ref):
    z_ref[...] = _dot3(x_ref[...], wh_ref[...], wl_ref[...])


def _gate_proj(x, w_hi, w_lo, tm):
    m, k = x.shape
    n = w_hi.shape[1]
    return pl.pallas_call(
        _gate_proj_kernel,
        grid=(m // tm,),
        in_specs=[pl.BlockSpec((tm, k), lambda i: (i, 0)),
                  pl.BlockSpec((k, n), lambda i: (0, 0)),
                  pl.BlockSpec((k, n), lambda i: (0, 0))],
        out_specs=pl.BlockSpec((tm, n), lambda i: (i, 0)),
        out_shape=jax.ShapeDtypeStruct((m, n), F32),
        compiler_params=_cparams(("parallel",)),
        name="gate_proj",
    )(x, w_hi, w_lo)


def _conv_branch(u, zcb, wc, conv_ref):
    tt = u.shape[0]
    carry = conv_ref[0]
    rows = lax.broadcasted_iota(jnp.int32, (tt, D_MODEL), 0)
    u1 = jnp.where(rows == 0, carry[1:2, :], pltpu.roll(u, 1, 0))
    u2 = jnp.where(rows == 0, carry[0:1, :],
                   jnp.where(rows == 1, carry[1:2, :], pltpu.roll(u, 2, 0)))
    conv_out = u2 * wc[0:1, :] + u1 * wc[1:2, :] + u * wc[2:3, :]
    conv_ref[0] = u[tt - 2:tt, :]
    return zcb * conv_out


def _mlstm_chunk(q, k, v, ig_col, lf_col, ig_row, lf_row, c_state, n_row, m_prev):
    chunk = q.shape[0]
    ti = lax.broadcasted_iota(jnp.int32, (chunk, chunk), 0)
    si = lax.broadcasted_iota(jnp.int32, (chunk, chunk), 1)
    causal = si <= ti
    b_col = jnp.sum(jnp.where(causal, lf_row, 0.0), axis=1, keepdims=True)
    b_row = jnp.sum(jnp.where(ti <= si, lf_col, 0.0), axis=0, keepdims=True)
    inter = b_col + m_prev
    dmat = jnp.where(causal, b_col - b_row + ig_row, -jnp.inf)
    m_t = jnp.maximum(inter, jnp.max(dmat, axis=1, keepdims=True))
    w_intra = jnp.exp(dmat - m_t)
    w_inter = jnp.exp(inter - m_t)
    qb = q.astype(BF16)
    kb = k.astype(BF16)
    vb = v.astype(BF16)
    s = lax.dot_general(qb, kb, (((1,), (1,)), ((), ())), preferred_element_type=F32) * w_intra
    num = w_inter * _dot(qb, c_state.astype(BF16)) + _dot(s.astype(BF16), vb)
    den = (w_inter * jnp.sum(q * n_row, axis=1, keepdims=True)
           + jnp.sum(s, axis=1, keepdims=True))
    h = num / jnp.maximum(jnp.abs(den), jnp.exp(-m_t))
    m_new = m_t[chunk - 1:chunk, :]
    b_last = b_col[chunk - 1:chunk, :]
    ws_col = jnp.exp(b_last - b_col + ig_col - m_new)
    cdecay = jnp.exp(inter[chunk - 1:chunk, :] - m_new)
    kw = k * ws_col
    c_new = cdecay * c_state + lax.dot_general(
        kw.astype(BF16), vb, (((0,), (0,)), ((), ())), preferred_element_type=F32)
    n_new = cdecay * n_row + jnp.sum(kw, axis=0, keepdims=True)
    return h, c_new, n_new, m_new


def _mlstm_branch(chunk, bg_ref, zg, gt, zqk, zv, zog, gain_ref, c_ref, n_ref, m_ref, mo_ref):
    tt = zqk.shape[0]
    for h in range(N_HEADS):
        ig_col_all = zg[:, h:h + 1] + bg_ref[h]
        lf_col_all = _log_sigmoid(zg[:, N_HEADS + h:N_HEADS + h + 1] + bg_ref[N_HEADS + h])
        ig_row_all = gt[h:h + 1, :] + bg_ref[h]
        lf_row_all = _log_sigmoid(gt[N_HEADS + h:N_HEADS + h + 1, :] + bg_ref[N_HEADS + h])
        vcols = slice(h * HEAD_V, (h + 1) * HEAD_V)
        for c in range(tt // chunk):
            r0, r1 = c * chunk, (c + 1) * chunk
            q = zqk[r0:r1, h * HEAD_QK:(h + 1) * HEAD_QK]
            k = zqk[r0:r1, (N_HEADS + h) * HEAD_QK:(N_HEADS + h + 1) * HEAD_QK] * (HEAD_QK ** -0.5)
            hh, c_new, n_new, m_new = _mlstm_chunk(
                q, k, zv[r0:r1, vcols], ig_col_all[r0:r1, :], lf_col_all[r0:r1, :],
                ig_row_all[:, r0:r1], lf_row_all[:, r0:r1],
                c_ref[0, h], n_ref[0, h:h + 1, :], m_ref[0, :, h:h + 1])
            c_ref[0, h] = c_new
            n_ref[0, h:h + 1, :] = n_new
            m_ref[0, :, h:h + 1] = m_new
            mu = jnp.mean(hh, axis=-1, keepdims=True)
            hc = hh - mu
            var = jnp.mean(hc * hc, axis=-1, keepdims=True)
            hn = hc * lax.rsqrt(var + LN_EPS) * gain_ref[:, vcols]
            mo_ref[r0:r1, vcols] = hn * _sigmoid(zog[r0:r1, vcols])


def _store_x1(x, mix, wout_ref, g_ref, b_ref, alpha, x1_ref, x1r_ref):
    tt = x.shape[0]
    x1 = _layer_norm(alpha * x + _dot(mix.astype(BF16), wout_ref[...]), g_ref[...], b_ref[...])
    x1_ref[...] = x1
    for s in range(ROW_TILES):
        x1r_ref[pl.ds(s, tt, stride=ROW_TILES), :] = x1[:, s * LANES:(s + 1) * LANES]


def _mixer_kernel(chunk, bg_ref,
                  zcb_ref, zcc_ref, zch_ref, zqk_ref, zv_ref, zog_ref, zga_ref, zgb_ref,
                  zg_ref, gt_ref, wconv_ref, gain_ref,
                  conv0_ref, c0_ref, n0_ref, m0_ref,
                  mix_ref, conv_ref, c_ref, n_ref, m_ref, mo_ref):
    @pl.when(pl.program_id(1) == 0)
    def _():
        conv_ref[...] = conv0_ref[...]
        c_ref[...] = c0_ref[...]
        n_ref[...] = n0_ref[...]
        m_ref[...] = m0_ref[...]

    a = _conv_branch(zcc_ref[...] * zch_ref[...], zcb_ref[...], wconv_ref[...], conv_ref)
    _mlstm_branch(chunk, bg_ref, zg_ref[...], gt_ref[0], zqk_ref[...], zv_ref[...], zog_ref[...],
                  gain_ref, c_ref, n_ref, m_ref, mo_ref)
    mix_ref[...] = _sigmoid(zga_ref[...]) * a + _sigmoid(zgb_ref[...]) * mo_ref[...]


def _mixer(z, zg, gt3, b_gates, w_conv, mh_gain, conv0, c0, n0, m0, *, nb, nt, tt, chunk):
    def zspec(j):
        return pl.BlockSpec((tt, D_MODEL), lambda b, t, j=j: (b * nt + t, j))

    in_specs = [pl.BlockSpec(memory_space=pltpu.SMEM)]
    in_specs += [zspec(j) for j in range(8)]
    in_specs += [
        pl.BlockSpec((tt, LANES), lambda b, t: (b * nt + t, 0)),
        pl.BlockSpec((1, SUBLANES, tt), lambda b, t: (b * nt + t, 0, 0)),
        pl.BlockSpec((CONV_K, D_MODEL), lambda b, t: (0, 0)),
        pl.BlockSpec((1, D_MODEL), lambda b, t: (0, 0)),
        pl.BlockSpec((1, CONV_K - 1, D_MODEL), lambda b, t: (b, 0, 0)),
        pl.BlockSpec((1, N_HEADS, HEAD_QK, HEAD_V), lambda b, t: (b, 0, 0, 0)),
        pl.BlockSpec((1, N_HEADS, HEAD_QK), lambda b, t: (b, 0, 0)),
        pl.BlockSpec((1, 1, N_HEADS), lambda b, t: (b, 0, 0)),
    ]
    out_specs = [
        pl.BlockSpec((tt, D_MODEL), lambda b, t: (b * nt + t, 0)),
        pl.BlockSpec((1, CONV_K - 1, D_MODEL), lambda b, t: (b, 0, 0)),
        pl.BlockSpec((1, N_HEADS, HEAD_QK, HEAD_V), lambda b, t: (b, 0, 0, 0)),
        pl.BlockSpec((1, N_HEADS, HEAD_QK), lambda b, t: (b, 0, 0)),
        pl.BlockSpec((1, 1, N_HEADS), lambda b, t: (b, 0, 0)),
    ]
    out_shape = [
        jax.ShapeDtypeStruct((nb * nt * tt, D_MODEL), F32),
        jax.ShapeDtypeStruct((nb, CONV_K - 1, D_MODEL), F32),
        jax.ShapeDtypeStruct((nb, N_HEADS, HEAD_QK, HEAD_V), F32),
        jax.ShapeDtypeStruct((nb, N_HEADS, HEAD_QK), F32),
        jax.ShapeDtypeStruct((nb, 1, N_HEADS), F32),
    ]
    return pl.pallas_call(
        functools.partial(_mixer_kernel, chunk),
        grid=(nb, nt),
        in_specs=in_specs,
        out_specs=out_specs,
        out_shape=out_shape,
        scratch_shapes=[pltpu.VMEM((tt, D_MODEL), F32)],
        compiler_params=_cparams(("arbitrary", "arbitrary")),
        name="mixer_t%d" % tt,
    )(b_gates, z, z, z, z, z, z, z, z, zg, gt3, w_conv, mh_gain, conv0, c0, n0, m0)


def _prompt_kernel(chunk, nb, ns, alpha, bg_ref, x_ref, w_ref, wgh_ref, wgl_ref, wconv_ref,
                   gain_ref, wout_ref, g_ref, b_ref, xs_ref, mixs_ref,
                   x1_ref, x1r_ref, conv_ref, c_ref, n_ref, m_ref, mo_ref):
    b = pl.program_id(0)
    t = pl.program_id(1)

    @pl.when(b < nb)
    def _():
        @pl.when(t == 0)
        def _():
            conv_ref[...] = jnp.zeros_like(conv_ref)
            c_ref[...] = jnp.zeros_like(c_ref)
            n_ref[...] = jnp.zeros_like(n_ref)
            m_ref[...] = jnp.zeros_like(m_ref)

        x = x_ref[...]
        xb = x.astype(BF16)

        def z(j):
            return _dot(xb, w_ref[:, j * D_MODEL:(j + 1) * D_MODEL])

        a = _conv_branch(z(1) * z(2), z(0), wconv_ref[...], conv_ref)
        zg = _dot3(x, wgh_ref[...], wgl_ref[...])
        _mlstm_branch(chunk, bg_ref, zg, zg.T, z(3), z(4), z(5), gain_ref,
                      c_ref, n_ref, m_ref, mo_ref)
        mix = _sigmoid(z(6)) * a + _sigmoid(z(7)) * mo_ref[...]
        _store_x1(x, mix, wout_ref, g_ref, b_ref, alpha, x1_ref, x1r_ref)

    @pl.when(jnp.logical_and(b == nb, t < ns))
    def _():
        _store_x1(xs_ref[...], mixs_ref[...], wout_ref, g_ref, b_ref, alpha, x1_ref, x1r_ref)


def _prompt_layer_half(xp, xs, mix_s, b_gates, w_main, wg_hi, wg_lo, w_conv, mh_gain, w_out,
                       ln_g, ln_b, alpha, *, nb, nt, tt, chunk):
    mp, ms = xp.shape[0], xs.shape[0]
    ns = ms // tt
    last_p = nb * nt - 1

    def prow(b, t):
        return jnp.minimum(b * nt + t, last_p)

    def srow(b, t):
        return jnp.where(b == nb, jnp.minimum(t, ns - 1), 0)

    def orow(b, t):
        return jnp.where(b == nb, nb * nt + jnp.minimum(t, ns - 1), b * nt + t)

    def state(b):
        return jnp.minimum(b, nb - 1)

    const2 = lambda b, t: (0, 0)
    in_specs = [
        pl.BlockSpec(memory_space=pltpu.SMEM),
        pl.BlockSpec((tt, D_MODEL), lambda b, t: (prow(b, t), 0)),
        pl.BlockSpec(w_main.shape, const2, pipeline_mode=pl.Buffered(1)),
        pl.BlockSpec(wg_hi.shape, const2),
        pl.BlockSpec(wg_lo.shape, const2),
        pl.BlockSpec((CONV_K, D_MODEL), const2),
        pl.BlockSpec((1, D_MODEL), const2),
        pl.BlockSpec((D_MODEL, D_MODEL), const2),
        pl.BlockSpec((1, D_MODEL), const2),
        pl.BlockSpec((1, D_MODEL), const2),
        pl.BlockSpec((tt, D_MODEL), lambda b, t: (srow(b, t), 0)),
        pl.BlockSpec((tt, D_MODEL), lambda b, t: (srow(b, t), 0)),
    ]
    out_specs = [
        pl.BlockSpec((tt, D_MODEL), lambda b, t: (orow(b, t), 0)),
        pl.BlockSpec((tt * ROW_TILES, LANES), lambda b, t: (orow(b, t), 0)),
        pl.BlockSpec((1, CONV_K - 1, D_MODEL), lambda b, t: (state(b), 0, 0)),
        pl.BlockSpec((1, N_HEADS, HEAD_QK, HEAD_V), lambda b, t: (state(b), 0, 0, 0)),
        pl.BlockSpec((1, N_HEADS, HEAD_QK), lambda b, t: (state(b), 0, 0)),
        pl.BlockSpec((1, 1, N_HEADS), lambda b, t: (state(b), 0, 0)),
    ]
    out_shape = [
        jax.ShapeDtypeStruct((mp + ms, D_MODEL), F32),
        jax.ShapeDtypeStruct(((mp + ms) * ROW_TILES, LANES), F32),
        jax.ShapeDtypeStruct((nb, CONV_K - 1, D_MODEL), F32),
        jax.ShapeDtypeStruct((nb, N_HEADS, HEAD_QK, HEAD_V), F32),
        jax.ShapeDtypeStruct((nb, N_HEADS, HEAD_QK), F32),
        jax.ShapeDtypeStruct((nb, 1, N_HEADS), F32),
    ]
    return pl.pallas_call(
        functools.partial(_prompt_kernel, chunk, nb, ns, alpha),
        grid=(nb + 1, nt),
        in_specs=in_specs,
        out_specs=out_specs,
        out_shape=out_shape,
        scratch_shapes=[pltpu.VMEM((tt, D_MODEL), F32)],
        compiler_params=_cparams(("arbitrary", "arbitrary")),
        name="prompt_half",
    )(b_gates, xp, w_main, wg_hi, wg_lo, w_conv, mh_gain, w_out, ln_g, ln_b, xs, mix_s)


def _router_kernel(x_ref, wh_ref, wl_ref, br_ref, idx_ref, gate_ref, rank_ref, cnt_ref):
    tt = x_ref.shape[0]

    @pl.when(pl.program_id(0) == 0)
    def _():
        cnt_ref[...] = jnp.zeros_like(cnt_ref)

    scores = _sigmoid(_dot3(x_ref[...], wh_ref[...], wl_ref[...]))
    lane = lax.broadcasted_iota(jnp.int32, (tt, N_EXPERTS), 1)
    work = scores + br_ref[...]
    picked = []
    sel = []
    member = jnp.zeros((tt, N_EXPERTS), F32)
    for _ in range(TOP_K):
        mx = jnp.max(work, axis=1, keepdims=True)
        ik = jnp.min(jnp.where(work == mx, lane, N_EXPERTS), axis=1, keepdims=True)
        onehot = lane == ik
        picked.append((ik, onehot))
        sel.append(jnp.sum(jnp.where(onehot, scores, 0.0), axis=1, keepdims=True))
        work = jnp.where(onehot, -jnp.inf, work)
        member = member + onehot.astype(F32)
    total = sel[0]
    for sk in sel[1:]:
        total = total + sk

    ti = lax.broadcasted_iota(jnp.int32, (tt, tt), 0)
    si = lax.broadcasted_iota(jnp.int32, (tt, tt), 1)
    earlier = (si < ti).astype(BF16)
    before = _dot(earlier, member.astype(BF16)) + cnt_ref[...]

    lane_o = lax.broadcasted_iota(jnp.int32, (tt, TOP_K), 1)
    idx_o = jnp.zeros((tt, TOP_K), jnp.int32)
    gate_o = jnp.zeros((tt, TOP_K), F32)
    rank_o = jnp.zeros((tt, TOP_K), jnp.int32)
    for kk in range(TOP_K):
        ik, onehot = picked[kk]
        rk = jnp.sum(jnp.where(onehot, before, 0.0), axis=1, keepdims=True)
        idx_o = jnp.where(lane_o == kk, ik, idx_o)
        gate_o = jnp.where(lane_o == kk, sel[kk] / total * ROUTED_SCALE, gate_o)
        rank_o = jnp.where(lane_o == kk, rk.astype(jnp.int32), rank_o)
    idx_ref[...] = idx_o
    gate_ref[...] = gate_o
    rank_ref[...] = rank_o
    cnt_ref[...] = cnt_ref[...] + jnp.sum(member, axis=0, keepdims=True)


def _router(x1, wr_hi, wr_lo, b_router, tt):
    m = x1.shape[0]
    return pl.pallas_call(
        _router_kernel,
        grid=(m // tt,),
        in_specs=[pl.BlockSpec((tt, D_MODEL), lambda i: (i, 0)),
                  pl.BlockSpec((D_MODEL, N_EXPERTS), lambda i: (0, 0)),
                  pl.BlockSpec((D_MODEL, N_EXPERTS), lambda i: (0, 0)),
                  pl.BlockSpec((1, N_EXPERTS), lambda i: (0, 0))],
        out_specs=[pl.BlockSpec((tt, TOP_K), lambda i: (i, 0)),
                   pl.BlockSpec((tt, TOP_K), lambda i: (i, 0)),
                   pl.BlockSpec((tt, TOP_K), lambda i: (i, 0)),
                   pl.BlockSpec((1, N_EXPERTS), lambda i: (0, 0))],
        out_shape=[jax.ShapeDtypeStruct((m, TOP_K), jnp.int32),
                   jax.ShapeDtypeStruct((m, TOP_K), F32),
                   jax.ShapeDtypeStruct((m, TOP_K), jnp.int32),
                   jax.ShapeDtypeStruct((1, N_EXPERTS), F32)],
        compiler_params=_cparams(("arbitrary",)),
        name="router",
    )(x1, wr_hi, wr_lo, b_router)


def _row_copy(src, src_row8, dst, dst_row8, sem):
    return pltpu.make_async_copy(src.at[pl.ds(pl.multiple_of(src_row8, SUBLANES), SUBLANES)],
                                 dst.at[pl.ds(pl.multiple_of(dst_row8, SUBLANES), SUBLANES)],
                                 sem)


def _dispatch_kernel(tt, start_ref, idx_ref, rank_ref, x1r_ref, xg_ref, dest_ref, zbuf, sem, zsem):
    @pl.when(pl.program_id(0) == 0)
    def _():
        zbuf[...] = jnp.zeros_like(zbuf)
        n_sorted = xg_ref.shape[0] - zbuf.shape[0]
        fill = pltpu.make_async_copy(zbuf, xg_ref.at[pl.ds(n_sorted, zbuf.shape[0])], zsem)
        fill.start()
        fill.wait()

    def issue(t, carry):
        for kk in range(TOP_K):
            j = t * TOP_K + kk
            dest8 = (start_ref[idx_ref[j]] + rank_ref[j]) * ROW_TILES
            dest_ref[j] = dest8
            _row_copy(x1r_ref, t * ROW_TILES, xg_ref, dest8, sem).start()
        return carry

    lax.fori_loop(0, tt, issue, 0)
    n_rows = tt * TOP_K * ROW_TILES
    pltpu.make_async_copy(xg_ref.at[pl.ds(0, n_rows)], xg_ref.at[pl.ds(0, n_rows)], sem).wait()


def _dispatch(seg_start, idx, rank, x1r, tt, slack):
    m = x1r.shape[0] // ROW_TILES
    smem_blk = pl.BlockSpec((tt * TOP_K,), lambda i: (i,), memory_space=pltpu.SMEM)
    return pl.pallas_call(
        functools.partial(_dispatch_kernel, tt),
        grid=(m // tt,),
        in_specs=[pl.BlockSpec(memory_space=pltpu.SMEM), smem_blk, smem_blk,
                  pl.BlockSpec((tt * ROW_TILES, LANES), lambda i: (i, 0))],
        out_specs=[pl.BlockSpec(memory_space=pl.ANY), smem_blk],
        out_shape=[jax.ShapeDtypeStruct(((m * TOP_K + slack) * ROW_TILES, LANES), F32),
                   jax.ShapeDtypeStruct((m * TOP_K,), jnp.int32)],
        scratch_shapes=[pltpu.VMEM((slack * ROW_TILES, LANES), F32),
                        pltpu.SemaphoreType.DMA(()), pltpu.SemaphoreType.DMA(())],
        compiler_params=_cparams(("arbitrary",)),
        name="dispatch",
    )(seg_start, idx, rank, x1r)


def _experts_kernel(rb, e_ref, row_ref, valid_ref, new_ref, slot_ref, nxt_ref,
                    xg_hbm, wg_hbm, wu_hbm, wd_hbm, y_hbm,
                    *scratch):
    j = pl.program_id(0)
    n_steps = pl.num_programs(0)
    ring_pos = lax.rem(j, EXPERT_RING)
    valid = valid_ref[j]
    xbufs = scratch[:EXPERT_RING]
    ybufs = scratch[EXPERT_RING:2 * EXPERT_RING]
    wg_buf, wu_buf, wd_buf, wgb, wub, wdb, wsem, xsem, ysem = scratch[2 * EXPERT_RING:]

    def tile_rows(row, n):
        return pl.ds(pl.multiple_of(row * ROW_TILES, SUBLANES), n * ROW_TILES)

    def x_copy(item, slot):
        return pltpu.make_async_copy(xg_hbm.at[tile_rows(row_ref[item], rb)], xbufs[slot],
                                     xsem.at[slot])

    def y_copies(item, slot, start):
        row0 = row_ref[item]
        n = valid_ref[item]

        def run(copy):
            if start:
                copy.start()
            else:
                copy.wait()

        @pl.when(n == rb)
        def _():
            run(pltpu.make_async_copy(ybufs[slot], y_hbm.at[tile_rows(row0, rb)], ysem.at[slot]))

        piece = rb // 2
        while piece >= 1:
            off = n - jnp.bitwise_and(n, 2 * piece - 1)

            @pl.when(jnp.logical_and(n < rb, jnp.bitwise_and(n, piece) != 0))
            def _(off=off, piece=piece):
                run(pltpu.make_async_copy(ybufs[slot].at[tile_rows(off, piece)],
                                          y_hbm.at[tile_rows(row0 + off, piece)], ysem.at[slot]))

            piece //= 2

    def fetch(e, slot):
        return (pltpu.make_async_copy(wg_hbm.at[e], wg_buf.at[slot], wsem.at[slot, 0]),
                pltpu.make_async_copy(wu_hbm.at[e], wu_buf.at[slot], wsem.at[slot, 1]),
                pltpu.make_async_copy(wd_hbm.at[e], wd_buf.at[slot], wsem.at[slot, 2]))

    ahead = EXPERT_RING - 1

    @pl.when(j == 0)
    def _():
        for item in range(ahead):
            @pl.when(valid_ref[item] > 0)
            def _(item=item):
                x_copy(item, item).start()

    ahead_item = jnp.minimum(j + ahead, n_steps - 1)
    for slot in range(EXPERT_RING):
        here = ring_pos == slot

        @pl.when(jnp.logical_and(here, j >= EXPERT_RING))
        def _(slot=slot):
            y_copies(j - EXPERT_RING, slot, start=False)

        @pl.when(jnp.logical_and(here, valid > 0))
        def _(slot=slot):
            x_copy(j, slot).wait()

            @pl.when(jnp.logical_and(j + ahead < n_steps, valid_ref[ahead_item] > 0))
            def _():
                x_copy(ahead_item, (slot + ahead) % EXPERT_RING).start()

    @pl.when(new_ref[j] != 0)
    def _():
        slot = slot_ref[j]

        @pl.when(j == 0)
        def _():
            for copy in fetch(e_ref[j], slot):
                copy.start()

        for copy in fetch(e_ref[j], slot):
            copy.wait()
        wgb[...] = wg_buf[slot].astype(BF16)
        wub[...] = wu_buf[slot].astype(BF16)
        wdb[...] = wd_buf[slot].astype(BF16)

        @pl.when(nxt_ref[j] >= 0)
        def _():
            for copy in fetch(nxt_ref[j], 1 - slot):
                copy.start()

    for slot in range(EXPERT_RING):
        here = ring_pos == slot

        @pl.when(jnp.logical_and(here, valid > 0))
        def _(slot=slot):
            xb = jnp.concatenate(
                [xbufs[slot][pl.ds(s, rb, stride=ROW_TILES), :] for s in range(ROW_TILES)],
                axis=1).astype(BF16)
            g = _dot(xb, wgb[...])
            u = _dot(xb, wub[...])
            y = _dot((g * _sigmoid(g) * u).astype(BF16), wdb[...])
            for s in range(ROW_TILES):
                ybufs[slot][pl.ds(s, rb, stride=ROW_TILES), :] = y[:, s * LANES:(s + 1) * LANES]
            y_copies(j, slot, start=True)

        @pl.when(jnp.logical_and(here, j == n_steps - 1))
        def _(slot=slot):
            for back in range(EXPERT_RING - 1, -1, -1):
                y_copies(j - back, (slot - back) % EXPERT_RING, start=False)


def _experts(item_e, item_row, item_valid, item_new, item_slot, item_nxt,
             xg, w_eg, w_eu, w_ed, rb):
    n_items = item_e.shape[0]
    n_rows = xg.shape[0] // ROW_TILES - rb
    hbm = pl.BlockSpec(memory_space=pl.ANY)
    row_buf = pltpu.VMEM((rb * ROW_TILES, LANES), F32)
    grid_spec = pltpu.PrefetchScalarGridSpec(
        num_scalar_prefetch=6,
        grid=(n_items,),
        in_specs=[hbm, hbm, hbm, hbm],
        out_specs=hbm,
        scratch_shapes=[row_buf] * (2 * EXPERT_RING) + [
                        pltpu.VMEM((2, D_MODEL, D_EXPERT), F32),
                        pltpu.VMEM((2, D_MODEL, D_EXPERT), F32),
                        pltpu.VMEM((2, D_EXPERT, D_MODEL), F32),
                        pltpu.VMEM((D_MODEL, D_EXPERT), BF16),
                        pltpu.VMEM((D_MODEL, D_EXPERT), BF16),
                        pltpu.VMEM((D_EXPERT, D_MODEL), BF16),
                        pltpu.SemaphoreType.DMA((2, 3)),
                        pltpu.SemaphoreType.DMA((EXPERT_RING,)),
                        pltpu.SemaphoreType.DMA((EXPERT_RING,))],
    )
    return pl.pallas_call(
        functools.partial(_experts_kernel, rb),
        grid_spec=grid_spec,
        out_shape=jax.ShapeDtypeStruct((n_rows * ROW_TILES, LANES), F32),
        compiler_params=_cparams(("arbitrary",)),
        name="experts",
    )(item_e, item_row, item_valid, item_new, item_slot, item_nxt, xg, w_eg, w_eu, w_ed)


def _combine_kernel(alpha, n_p, n_steps, dest_ref, dnext_ref, gate_ref, x1_ref, y_ref,
                    wsg_ref, wsu_ref, wsd_ref, g_ref, b_ref, x2p_ref, x2s_ref,
                    ybuf_a, ybuf_b, sem_a, sem_b):
    step = pl.program_id(0)
    tt = x1_ref.shape[0] // 2
    n_rows = TOP_K * tt * ROW_TILES

    def issue(dref, half, buf, sem):
        def body(t, carry):
            for kk in range(TOP_K):
                _row_copy(y_ref, dref[(half * tt + t) * TOP_K + kk], buf,
                          (kk * tt + t) * ROW_TILES, sem).start()
            return carry

        lax.fori_loop(0, tt, body, 0)

    def wait(buf, sem):
        pltpu.make_async_copy(y_ref.at[pl.ds(0, n_rows)], buf, sem).wait()

    def routed(half, buf):
        gates = gate_ref[half * tt:(half + 1) * tt, :]
        chunks = []
        for s in range(ROW_TILES):
            acc = jnp.zeros((tt, LANES), F32)
            for kk in range(TOP_K):
                acc = acc + gates[:, kk:kk + 1] * buf[pl.ds(kk * tt * ROW_TILES + s, tt,
                                                            stride=ROW_TILES), :]
            chunks.append(acc)
        return jnp.concatenate(chunks, axis=1)

    @pl.when(step == 0)
    def _():
        issue(dest_ref, 0, ybuf_a, sem_a)

    issue(dest_ref, 1, ybuf_b, sem_b)

    x1 = x1_ref[...]
    xb = x1.astype(BF16)
    gs = _dot(xb, wsg_ref[...])
    us = _dot(xb, wsu_ref[...])
    base = alpha * x1 + _dot((gs * _sigmoid(gs) * us).astype(BF16), wsd_ref[...])

    wait(ybuf_a, sem_a)
    x2a = _layer_norm(base[:tt, :] + routed(0, ybuf_a), g_ref[...], b_ref[...])

    @pl.when(step + 1 < n_steps)
    def _():
        issue(dnext_ref, 0, ybuf_a, sem_a)

    wait(ybuf_b, sem_b)
    x2b = _layer_norm(base[tt:, :] + routed(1, ybuf_b), g_ref[...], b_ref[...])

    @pl.when(step < n_p)
    def _():
        x2p_ref[:tt, :] = x2a
        x2p_ref[tt:, :] = x2b

    @pl.when(step >= n_p)
    def _():
        x2s_ref[:tt, :] = x2a
        x2s_ref[tt:, :] = x2b


def _combine(dest8, gates, x1, y, wsg, wsu, wsd, g, b, alpha, tt, mp):
    m = x1.shape[0]
    st = 2 * tt
    n_steps = m // st
    n_p = mp // st
    return pl.pallas_call(
        functools.partial(_combine_kernel, alpha, n_p, n_steps),
        grid=(n_steps,),
        in_specs=[pl.BlockSpec((st * TOP_K,), lambda i: (i,), memory_space=pltpu.SMEM),
                  pl.BlockSpec((st * TOP_K,), lambda i: (jnp.minimum(i + 1, n_steps - 1),),
                               memory_space=pltpu.SMEM),
                  pl.BlockSpec((st, TOP_K), lambda i: (i, 0)),
                  pl.BlockSpec((st, D_MODEL), lambda i: (i, 0)),
                  pl.BlockSpec(memory_space=pl.ANY),
                  pl.BlockSpec((D_MODEL, D_EXPERT), lambda i: (0, 0)),
                  pl.BlockSpec((D_MODEL, D_EXPERT), lambda i: (0, 0)),
                  pl.BlockSpec((D_EXPERT, D_MODEL), lambda i: (0, 0)),
                  pl.BlockSpec((1, D_MODEL), lambda i: (0, 0)),
                  pl.BlockSpec((1, D_MODEL), lambda i: (0, 0))],
        out_specs=[pl.BlockSpec((st, D_MODEL), lambda i: (jnp.minimum(i, n_p - 1), 0)),
                   pl.BlockSpec((st, D_MODEL), lambda i: (jnp.maximum(i - n_p, 0), 0))],
        out_shape=[jax.ShapeDtypeStruct((mp, D_MODEL), F32),
                   jax.ShapeDtypeStruct((m - mp, D_MODEL), F32)],
        scratch_shapes=[pltpu.VMEM((TOP_K * tt * ROW_TILES, LANES), F32),
                        pltpu.VMEM((TOP_K * tt * ROW_TILES, LANES), F32),
                        pltpu.SemaphoreType.DMA(()), pltpu.SemaphoreType.DMA(())],
        compiler_params=_cparams(("arbitrary",)),
        name="combine",
    )(dest8, dest8, gates, x1, y, wsg, wsu, wsd, g, b)


PROJ_TM = 512
PROJ_TN = 1024
MIX_TT = 256
MIX_CHUNK = 128
ROUTER_TT = 256
DISPATCH_TT = 512
EXPERT_RB = 256
EXPERT_RING = 3
COMBINE_TT = 128


def _moe(x1, x1r, mp, w_router, b_router, w_eg, w_eu, w_ed, w_sg, w_su, w_sd, ln_g, ln_b, alpha):
    m = x1.shape[0]
    wr_hi, wr_lo = _split_hi_lo(w_router)
    idx, gates, rank, cnt = _router(x1, wr_hi, wr_lo, b_router[None, :], ROUTER_TT)

    rb = EXPERT_RB
    n_rows = m * TOP_K
    n_items = n_rows // rb + N_EXPERTS
    counts = cnt[0].astype(jnp.int32)
    seg_end = jnp.cumsum(counts)
    seg_start = seg_end - counts
    n_chunk = (counts + rb - 1) // rb
    item_end = jnp.cumsum(n_chunk)
    item_start = item_end - n_chunk
    item = jnp.arange(n_items, dtype=jnp.int32)
    item_c = jnp.minimum(item, item_end[-1] - 1)
    item_e = jnp.sum((item_end[None, :] <= item_c[:, None]).astype(jnp.int32), axis=1)
    onehot_e = item_e[:, None] == jnp.arange(N_EXPERTS, dtype=jnp.int32)[None, :]

    def pick(table):
        return jnp.sum(jnp.where(onehot_e, table[None, :], 0), axis=1)

    chunk = item_c - pick(item_start)
    valid = item < item_end[-1]
    item_row = pick(seg_start) + chunk * rb
    item_valid = jnp.where(valid, jnp.clip(pick(counts) - chunk * rb, 0, rb), 0)

    experts = jnp.arange(N_EXPERTS, dtype=jnp.int32)
    nonempty = counts > 0
    slot_e = (jnp.cumsum(nonempty.astype(jnp.int32)) - 1) % 2
    later = jnp.logical_and(experts[None, :] > experts[:, None], nonempty[None, :])
    nxt_e = jnp.min(jnp.where(later, experts[None, :], N_EXPERTS), axis=1)
    nxt_e = jnp.where(nxt_e == N_EXPERTS, -1, nxt_e)
    item_new = jnp.logical_and(valid, chunk == 0).astype(jnp.int32)

    xg, dest8 = _dispatch(seg_start, idx.reshape(-1), rank.reshape(-1), x1r, DISPATCH_TT, rb)
    y = _experts(item_e, item_row, item_valid, item_new, pick(slot_e), pick(nxt_e),
                 xg, w_eg, w_eu, w_ed, rb)
    return _combine(dest8, gates, x1, y, w_sg.astype(BF16), w_su.astype(BF16),
                    w_sd.astype(BF16), ln_g[None, :], ln_b[None, :], alpha, COMBINE_TT, mp)


def _layer(xp, xs, conv_s, c_s, n_s, m_s, w_in, b_gates, w_conv, mh_gain, w_out, ln1_g, ln1_b,
           w_router, b_router, w_eg, w_eu, w_ed, w_sg, w_su, w_sd, ln2_g, ln2_b, alpha):
    bp, tp, _ = xp.shape
    bs, ts, _ = xs.shape
    mp, ms = bp * tp, bs * ts
    xp2 = xp.reshape(mp, D_MODEL)
    xs2 = xs.reshape(ms, D_MODEL)

    w_main = jnp.concatenate([w_in[:, :GATE_COL0], w_in[:, GATE_COL0 + 2 * N_HEADS:]],
                             axis=1).astype(BF16)
    w_gate = jnp.pad(w_in[:, GATE_COL0:GATE_COL0 + 2 * N_HEADS],
                     ((0, 0), (0, LANES - 2 * N_HEADS)))
    wg_hi, wg_lo = _split_hi_lo(w_gate)

    z_s = _proj(xs2, w_main, PROJ_TM, PROJ_TN)
    zg_s = _gate_proj(xs2, wg_hi, wg_lo, PROJ_TM)
    gt_s = zg_s[:, :SUBLANES].reshape(bs, ts, SUBLANES).transpose(0, 2, 1)
    mix_s, conv_n, c_n, n_n, m_n = _mixer(
        z_s, zg_s, gt_s, b_gates, w_conv, mh_gain[None, :],
        conv_s, c_s, n_s, m_s[:, None, :], nb=bs, nt=1, tt=ts, chunk=ts)

    x1, x1r, conv_p, c_p, n_p, m_p = _prompt_layer_half(
        xp2, xs2, mix_s, b_gates, w_main, wg_hi, wg_lo, w_conv, mh_gain[None, :],
        w_out.astype(BF16), ln1_g[None, :], ln1_b[None, :], alpha,
        nb=bp, nt=tp // MIX_TT, tt=MIX_TT, chunk=MIX_CHUNK)

    x2p, x2s = _moe(x1, x1r, mp, w_router, b_router, w_eg, w_eu, w_ed, w_sg, w_su, w_sd,
                    ln2_g, ln2_b, alpha)
    states_p = (conv_p, c_p, n_p, m_p[:, 0, :])
    states_s = (conv_n, c_n, n_n, m_n[:, 0, :])
    return x2p.reshape(bp, tp, D_MODEL), x2s.reshape(bs, ts, D_MODEL), states_p, states_s


def kernel(x_prompt, x_sample, cache_conv, state_mlstm_C, state_mlstm_n, state_mlstm_m, w_in, b_gates, w_conv, mh_gain, w_out, ln1_g, ln1_b, w_router, b_router, w_exp_gate, w_exp_up, w_exp_down, w_sh_gate, w_sh_up, w_sh_down, ln2_g, ln2_b):
    depth = w_in.shape[0]
    alpha = (2.0 * depth) ** 0.25
    hp, hs = x_prompt, x_sample
    outs_p = [[], [], [], []]
    outs_s = [[], [], [], []]
    for l in range(depth):
        hp, hs, st_p, st_s = _layer(
            hp, hs, cache_conv[l], state_mlstm_C[l], state_mlstm_n[l], state_mlstm_m[l],
            w_in[l], b_gates[l], w_conv[l], mh_gain[l], w_out[l], ln1_g[l], ln1_b[l],
            w_router[l], b_router[l], w_exp_gate[l], w_exp_up[l], w_exp_down[l],
            w_sh_gate[l], w_sh_up[l], w_sh_down[l], ln2_g[l], ln2_b[l], alpha)
        for acc, val in zip(outs_p, st_p):
            acc.append(val)
        for acc, val in zip(outs_s, st_s):
            acc.append(val)
    return (hp, hs) + tuple(jnp.stack(a) for a in outs_p) + tuple(jnp.stack(a) for a in outs_s)
```

```python
import functools

import jax
import jax.numpy as jnp
from jax import lax
from jax.experimental import pallas as pl
from jax.experimental.pallas import tpu as pltpu

F32 = jnp.float32
BF16 = jnp.bfloat16

D_MODEL = 1024
N_HEADS = 4
HEAD_V = 256
HEAD_QK = 128
N_EXPERTS = 256
TOP_K = 8
D_EXPERT = 256
ROUTED_SCALE = 2.5
LN_EPS = 1e-5
CONV_K = 3

LANES = 128
SUBLANES = 8
ROW_TILES = D_MODEL // LANES
GATE_COL0 = 6 * D_MODEL

VMEM_LIMIT = 56 * 1024 * 1024


def _cparams(sem):
    return pltpu.CompilerParams(dimension_semantics=sem, vmem_limit_bytes=VMEM_LIMIT)


def _sigmoid(x):
    return 1.0 / (1.0 + jnp.exp(-x))


def _log_sigmoid(x):
    return jnp.minimum(x, 0.0) - jnp.log(1.0 + jnp.exp(-jnp.abs(x)))


def _layer_norm(x, g, b):
    mu = jnp.mean(x, axis=-1, keepdims=True)
    xc = x - mu
    var = jnp.mean(xc * xc, axis=-1, keepdims=True)
    return xc * lax.rsqrt(var + LN_EPS) * g + b


def _split_hi_lo(x):
    hi = x.astype(BF16)
    lo = (x - hi.astype(F32)).astype(BF16)
    return hi, lo


def _dot(a, b):
    return jnp.dot(a, b, preferred_element_type=F32)


def _dot3(x, w_hi, w_lo):
    x_hi, x_lo = _split_hi_lo(x)
    return _dot(x_hi, w_hi) + _dot(x_lo, w_hi) + _dot(x_hi, w_lo)


def _proj_kernel(x_ref, w_ref, z_ref):
    z_ref[...] = _dot(x_ref[...].astype(BF16), w_ref[...])


def _proj(x, w, tm, tn):
    m, k = x.shape
    n = w.shape[1]
    return pl.pallas_call(
        _proj_kernel,
        grid=(m // tm, n // tn),
        in_specs=[pl.BlockSpec((tm, k), lambda i, j: (i, 0)),
                  pl.BlockSpec((k, tn), lambda i, j: (0, j))],
        out_specs=pl.BlockSpec((tm, tn), lambda i, j: (i, j)),
        out_shape=jax.ShapeDtypeStruct((m, n), F32),
        compiler_params=_cparams(("parallel", "arbitrary")),
        name="proj",
    )(x, w)


def _gate_proj_kernel(x_ref, wh_ref, wl_ref, z_ref):
    z_ref[...] = _dot3(x_ref[...], wh_ref[...], wl_ref[...])


def _gate_proj(x, w_hi, w_lo, tm):
    m, k = x.shape
    n = w_hi.shape[1]
    return pl.pallas_call(
        _gate_proj_kernel,
        grid=(m // tm,),
        in_specs=[pl.BlockSpec((tm, k), lambda i: (i, 0)),
                  pl.BlockSpec((k, n), lambda i: (0, 0)),
                  pl.BlockSpec((k, n), lambda i: (0, 0))],
        out_specs=pl.BlockSpec((tm, n), lambda i: (i, 0)),
        out_shape=jax.ShapeDtypeStruct((m, n), F32),
        compiler_params=_cparams(("parallel",)),
        name="gate_proj",
    )(x, w_hi, w_lo)


def _conv_branch(u, zcb, wc, conv_ref, seq=0):
    tt = u.shape[0]
    carry = conv_ref[seq]
    rows = lax.broadcasted_iota(jnp.int32, (tt, D_MODEL), 0)
    u1 = jnp.where(rows == 0, carry[1:2, :], pltpu.roll(u, 1, 0))
    u2 = jnp.where(rows == 0, carry[0:1, :],
                   jnp.where(rows == 1, carry[1:2, :], pltpu.roll(u, 2, 0)))
    conv_out = u2 * wc[0:1, :] + u1 * wc[1:2, :] + u * wc[2:3, :]
    conv_ref[seq] = u[tt - 2:tt, :]
    return zcb * conv_out


def _mlstm_chunk(q, k, v, ig_col, lf_col, ig_row, lf_row, c_state, n_row, m_prev):
    chunk = q.shape[0]
    ti = lax.broadcasted_iota(jnp.int32, (chunk, chunk), 0)
    si = lax.broadcasted_iota(jnp.int32, (chunk, chunk), 1)
    causal = si <= ti
    b_col = jnp.sum(jnp.where(causal, lf_row, 0.0), axis=1, keepdims=True)
    b_row = jnp.sum(jnp.where(ti <= si, lf_col, 0.0), axis=0, keepdims=True)
    inter = b_col + m_prev
    dmat = jnp.where(causal, b_col - b_row + ig_row, -jnp.inf)
    m_t = jnp.maximum(inter, jnp.max(dmat, axis=1, keepdims=True))
    w_intra = jnp.exp(dmat - m_t)
    w_inter = jnp.exp(inter - m_t)
    qb = q.astype(BF16)
    kb = k.astype(BF16)
    vb = v.astype(BF16)
    s = lax.dot_general(qb, kb, (((1,), (1,)), ((), ())), preferred_element_type=F32) * w_intra
    num = w_inter * _dot(qb, c_state.astype(BF16)) + _dot(s.astype(BF16), vb)
    den = (w_inter * jnp.sum(q * n_row, axis=1, keepdims=True)
           + jnp.sum(s, axis=1, keepdims=True))
    h = num / jnp.maximum(jnp.abs(den), jnp.exp(-m_t))
    m_new = m_t[chunk - 1:chunk, :]
    b_last = b_col[chunk - 1:chunk, :]
    ws_col = jnp.exp(b_last - b_col + ig_col - m_new)
    cdecay = jnp.exp(inter[chunk - 1:chunk, :] - m_new)
    kw = k * ws_col
    c_new = cdecay * c_state + lax.dot_general(
        kw.astype(BF16), vb, (((0,), (0,)), ((), ())), preferred_element_type=F32)
    n_new = cdecay * n_row + jnp.sum(kw, axis=0, keepdims=True)
    return h, c_new, n_new, m_new


def _mlstm_branch(chunk, bg_ref, zg, gt, zqk, zv, zog, gain_ref, c_ref, n_ref, m_ref, mo_ref,
                  seq=0, row0=0):
    tt = zqk.shape[0]
    for h in range(N_HEADS):
        ig_col_all = zg[:, h:h + 1] + bg_ref[h]
        lf_col_all = _log_sigmoid(zg[:, N_HEADS + h:N_HEADS + h + 1] + bg_ref[N_HEADS + h])
        ig_row_all = gt[h:h + 1, :] + bg_ref[h]
        lf_row_all = _log_sigmoid(gt[N_HEADS + h:N_HEADS + h + 1, :] + bg_ref[N_HEADS + h])
        vcols = slice(h * HEAD_V, (h + 1) * HEAD_V)
        for c in range(tt // chunk):
            r0, r1 = c * chunk, (c + 1) * chunk
            q = zqk[r0:r1, h * HEAD_QK:(h + 1) * HEAD_QK]
            k = zqk[r0:r1, (N_HEADS + h) * HEAD_QK:(N_HEADS + h + 1) * HEAD_QK] * (HEAD_QK ** -0.5)
            hh, c_new, n_new, m_new = _mlstm_chunk(
                q, k, zv[r0:r1, vcols], ig_col_all[r0:r1, :], lf_col_all[r0:r1, :],
                ig_row_all[:, r0:r1], lf_row_all[:, r0:r1],
                c_ref[seq, h], n_ref[seq, h:h + 1, :], m_ref[seq, :, h:h + 1])
            c_ref[seq, h] = c_new
            n_ref[seq, h:h + 1, :] = n_new
            m_ref[seq, :, h:h + 1] = m_new
            mu = jnp.mean(hh, axis=-1, keepdims=True)
            hc = hh - mu
            var = jnp.mean(hc * hc, axis=-1, keepdims=True)
            hn = hc * lax.rsqrt(var + LN_EPS) * gain_ref[:, vcols]
            mo_ref[row0 + r0:row0 + r1, vcols] = hn * _sigmoid(zog[r0:r1, vcols])


def _store_x1(x, mix, wout_ref, g_ref, b_ref, alpha, x1_ref, x1r_ref):
    tt = x.shape[0]
    x1 = _layer_norm(alpha * x + _dot(mix.astype(BF16), wout_ref[...]), g_ref[...], b_ref[...])
    x1_ref[...] = x1
    for s in range(ROW_TILES):
        x1r_ref[pl.ds(s, tt, stride=ROW_TILES), :] = x1[:, s * LANES:(s + 1) * LANES]


def _mixer_kernel(ts, bg_ref,
                  zcb_ref, zcc_ref, zch_ref, zqk_ref, zv_ref, zog_ref, zga_ref, zgb_ref,
                  zg_ref, gt_ref, wconv_ref, gain_ref,
                  conv0_ref, c0_ref, n0_ref, m0_ref,
                  mix_ref, conv_ref, c_ref, n_ref, m_ref, mo_ref):
    conv_ref[...] = conv0_ref[...]
    c_ref[...] = c0_ref[...]
    n_ref[...] = n0_ref[...]
    m_ref[...] = m0_ref[...]
    for seq in range(conv_ref.shape[0]):
        rows = slice(seq * ts, (seq + 1) * ts)
        a = _conv_branch(zcc_ref[rows, :] * zch_ref[rows, :], zcb_ref[rows, :], wconv_ref[...],
                         conv_ref, seq)
        _mlstm_branch(ts, bg_ref, zg_ref[rows, :], gt_ref[seq], zqk_ref[rows, :], zv_ref[rows, :],
                      zog_ref[rows, :], gain_ref, c_ref, n_ref, m_ref, mo_ref, seq, seq * ts)
        mix_ref[rows, :] = (_sigmoid(zga_ref[rows, :]) * a
                            + _sigmoid(zgb_ref[rows, :]) * mo_ref[rows, :])


def _mixer(z, zg, gt3, b_gates, w_conv, mh_gain, conv0, c0, n0, m0, *, nb, ts, nseq):
    tt = nseq * ts

    def zspec(j):
        return pl.BlockSpec((tt, D_MODEL), lambda b, j=j: (b, j))

    in_specs = [pl.BlockSpec(memory_space=pltpu.SMEM)]
    in_specs += [zspec(j) for j in range(8)]
    state_specs = [
        pl.BlockSpec((nseq, CONV_K - 1, D_MODEL), lambda b: (b, 0, 0)),
        pl.BlockSpec((nseq, N_HEADS, HEAD_QK, HEAD_V), lambda b: (b, 0, 0, 0)),
        pl.BlockSpec((nseq, N_HEADS, HEAD_QK), lambda b: (b, 0, 0)),
        pl.BlockSpec((nseq, 1, N_HEADS), lambda b: (b, 0, 0)),
    ]
    in_specs += [
        pl.BlockSpec((tt, LANES), lambda b: (b, 0)),
        pl.BlockSpec((nseq, SUBLANES, ts), lambda b: (b, 0, 0)),
        pl.BlockSpec((CONV_K, D_MODEL), lambda b: (0, 0)),
        pl.BlockSpec((1, D_MODEL), lambda b: (0, 0)),
    ] + state_specs
    out_specs = [pl.BlockSpec((tt, D_MODEL), lambda b: (b, 0))] + state_specs
    out_shape = [
        jax.ShapeDtypeStruct((nb * ts, D_MODEL), F32),
        jax.ShapeDtypeStruct((nb, CONV_K - 1, D_MODEL), F32),
        jax.ShapeDtypeStruct((nb, N_HEADS, HEAD_QK, HEAD_V), F32),
        jax.ShapeDtypeStruct((nb, N_HEADS, HEAD_QK), F32),
        jax.ShapeDtypeStruct((nb, 1, N_HEADS), F32),
    ]
    return pl.pallas_call(
        functools.partial(_mixer_kernel, ts),
        grid=(nb // nseq,),
        in_specs=in_specs,
        out_specs=out_specs,
        out_shape=out_shape,
        scratch_shapes=[pltpu.VMEM((tt, D_MODEL), F32)],
        compiler_params=_cparams(("parallel",)),
        name="mixer_sample",
    )(b_gates, z, z, z, z, z, z, z, z, zg, gt3, w_conv, mh_gain, conv0, c0, n0, m0)


def _prompt_kernel(chunk, nb, ns, alpha, bg_ref, x_ref, w_ref, wgh_ref, wgl_ref, wconv_ref,
                   gain_ref, wout_ref, g_ref, b_ref, xs_ref, mixs_ref,
                   x1_ref, x1r_ref, conv_ref, c_ref, n_ref, m_ref, mo_ref):
    b = pl.program_id(0)
    t = pl.program_id(1)

    @pl.when(b < nb)
    def _():
        @pl.when(t == 0)
        def _():
            conv_ref[...] = jnp.zeros_like(conv_ref)
            c_ref[...] = jnp.zeros_like(c_ref)
            n_ref[...] = jnp.zeros_like(n_ref)
            m_ref[...] = jnp.zeros_like(m_ref)

        x = x_ref[...]
        xb = x.astype(BF16)

        def z(j):
            return _dot(xb, w_ref[:, j * D_MODEL:(j + 1) * D_MODEL])

        a = _conv_branch(z(1) * z(2), z(0), wconv_ref[...], conv_ref)
        zg = _dot3(x, wgh_ref[...], wgl_ref[...])
        _mlstm_branch(chunk, bg_ref, zg, zg.T, z(3), z(4), z(5), gain_ref,
                      c_ref, n_ref, m_ref, mo_ref)
        mix = _sigmoid(z(6)) * a + _sigmoid(z(7)) * mo_ref[...]
        _store_x1(x, mix, wout_ref, g_ref, b_ref, alpha, x1_ref, x1r_ref)

    @pl.when(jnp.logical_and(b == nb, t < ns))
    def _():
        _store_x1(xs_ref[...], mixs_ref[...], wout_ref, g_ref, b_ref, alpha, x1_ref, x1r_ref)


def _prompt_layer_half(xp, xs, mix_s, b_gates, w_main, wg_hi, wg_lo, w_conv, mh_gain, w_out,
                       ln_g, ln_b, alpha, *, nb, nt, tt, chunk):
    mp, ms = xp.shape[0], xs.shape[0]
    ns = ms // tt
    last_p = nb * nt - 1

    def prow(b, t):
        return jnp.minimum(b * nt + t, last_p)

    def srow(b, t):
        return jnp.where(b == nb, jnp.minimum(t, ns - 1), 0)

    def orow(b, t):
        return jnp.where(b == nb, nb * nt + jnp.minimum(t, ns - 1), b * nt + t)

    def state(b):
        return jnp.minimum(b, nb - 1)

    const2 = lambda b, t: (0, 0)
    in_specs = [
        pl.BlockSpec(memory_space=pltpu.SMEM),
        pl.BlockSpec((tt, D_MODEL), lambda b, t: (prow(b, t), 0)),
        pl.BlockSpec(w_main.shape, const2, pipeline_mode=pl.Buffered(1)),
        pl.BlockSpec(wg_hi.shape, const2),
        pl.BlockSpec(wg_lo.shape, const2),
        pl.BlockSpec((CONV_K, D_MODEL), const2),
        pl.BlockSpec((1, D_MODEL), const2),
        pl.BlockSpec((D_MODEL, D_MODEL), const2),
        pl.BlockSpec((1, D_MODEL), const2),
        pl.BlockSpec((1, D_MODEL), const2),
        pl.BlockSpec((tt, D_MODEL), lambda b, t: (srow(b, t), 0)),
        pl.BlockSpec((tt, D_MODEL), lambda b, t: (srow(b, t), 0)),
    ]
    out_specs = [
        pl.BlockSpec((tt, D_MODEL), lambda b, t: (orow(b, t), 0)),
        pl.BlockSpec((tt * ROW_TILES, LANES), lambda b, t: (orow(b, t), 0)),
        pl.BlockSpec((1, CONV_K - 1, D_MODEL), lambda b, t: (state(b), 0, 0)),
        pl.BlockSpec((1, N_HEADS, HEAD_QK, HEAD_V), lambda b, t: (state(b), 0, 0, 0)),
        pl.BlockSpec((1, N_HEADS, HEAD_QK), lambda b, t: (state(b), 0, 0)),
        pl.BlockSpec((1, 1, N_HEADS), lambda b, t: (state(b), 0, 0)),
    ]
    out_shape = [
        jax.ShapeDtypeStruct((mp + ms, D_MODEL), F32),
        jax.ShapeDtypeStruct(((mp + ms) * ROW_TILES, LANES), F32),
        jax.ShapeDtypeStruct((nb, CONV_K - 1, D_MODEL), F32),
        jax.ShapeDtypeStruct((nb, N_HEADS, HEAD_QK, HEAD_V), F32),
        jax.ShapeDtypeStruct((nb, N_HEADS, HEAD_QK), F32),
        jax.ShapeDtypeStruct((nb, 1, N_HEADS), F32),
    ]
    return pl.pallas_call(
        functools.partial(_prompt_kernel, chunk, nb, ns, alpha),
        grid=(nb + 1, nt),
        in_specs=in_specs,
        out_specs=out_specs,
        out_shape=out_shape,
        scratch_shapes=[pltpu.VMEM((tt, D_MODEL), F32)],
        compiler_params=_cparams(("arbitrary", "arbitrary")),
        name="prompt_half",
    )(b_gates, xp, w_main, wg_hi, wg_lo, w_conv, mh_gain, w_out, ln_g, ln_b, xs, mix_s)


def _router_kernel(x_ref, wh_ref, wl_ref, br_ref, idx_ref, gate_ref, rank_ref, cnt_ref):
    tt = x_ref.shape[0]

    @pl.when(pl.program_id(0) == 0)
    def _():
        cnt_ref[...] = jnp.zeros_like(cnt_ref)

    scores = _sigmoid(_dot3(x_ref[...], wh_ref[...], wl_ref[...]))
    lane = lax.broadcasted_iota(jnp.int32, (tt, N_EXPERTS), 1)
    work = scores + br_ref[...]
    picked = []
    sel = []
    member = jnp.zeros((tt, N_EXPERTS), F32)
    for _ in range(TOP_K):
        mx = jnp.max(work, axis=1, keepdims=True)
        ik = jnp.min(jnp.where(work == mx, lane, N_EXPERTS), axis=1, keepdims=True)
        onehot = lane == ik
        picked.append((ik, onehot))
        sel.append(jnp.sum(jnp.where(onehot, scores, 0.0), axis=1, keepdims=True))
        work = jnp.where(onehot, -jnp.inf, work)
        member = member + onehot.astype(F32)
    total = sel[0]
    for sk in sel[1:]:
        total = total + sk

    ti = lax.broadcasted_iota(jnp.int32, (tt, tt), 0)
    si = lax.broadcasted_iota(jnp.int32, (tt, tt), 1)
    earlier = (si < ti).astype(BF16)
    before = _dot(earlier, member.astype(BF16)) + cnt_ref[...]

    lane_o = lax.broadcasted_iota(jnp.int32, (tt, TOP_K), 1)
    idx_o = jnp.zeros((tt, TOP_K), jnp.int32)
    gate_o = jnp.zeros((tt, TOP_K), F32)
    rank_o = jnp.zeros((tt, TOP_K), jnp.int32)
    for kk in range(TOP_K):
        ik, onehot = picked[kk]
        rk = jnp.sum(jnp.where(onehot, before, 0.0), axis=1, keepdims=True)
        idx_o = jnp.where(lane_o == kk, ik, idx_o)
        gate_o = jnp.where(lane_o == kk, sel[kk] / total * ROUTED_SCALE, gate_o)
        rank_o = jnp.where(lane_o == kk, rk.astype(jnp.int32), rank_o)
    idx_ref[...] = idx_o
    gate_ref[...] = gate_o
    rank_ref[...] = rank_o
    cnt_ref[...] = cnt_ref[...] + jnp.sum(member, axis=0, keepdims=True)


def _router(x1, wr_hi, wr_lo, b_router, tt):
    m = x1.shape[0]
    return pl.pallas_call(
        _router_kernel,
        grid=(m // tt,),
        in_specs=[pl.BlockSpec((tt, D_MODEL), lambda i: (i, 0)),
                  pl.BlockSpec((D_MODEL, N_EXPERTS), lambda i: (0, 0)),
                  pl.BlockSpec((D_MODEL, N_EXPERTS), lambda i: (0, 0)),
                  pl.BlockSpec((1, N_EXPERTS), lambda i: (0, 0))],
        out_specs=[pl.BlockSpec((tt, TOP_K), lambda i: (i, 0)),
                   pl.BlockSpec((tt, TOP_K), lambda i: (i, 0)),
                   pl.BlockSpec((tt, TOP_K), lambda i: (i, 0)),
                   pl.BlockSpec((1, N_EXPERTS), lambda i: (0, 0))],
        out_shape=[jax.ShapeDtypeStruct((m, TOP_K), jnp.int32),
                   jax.ShapeDtypeStruct((m, TOP_K), F32),
                   jax.ShapeDtypeStruct((m, TOP_K), jnp.int32),
                   jax.ShapeDtypeStruct((1, N_EXPERTS), F32)],
        compiler_params=_cparams(("arbitrary",)),
        name="router",
    )(x1, wr_hi, wr_lo, b_router)


def _row_copy(src, src_row8, dst, dst_row8, sem):
    return pltpu.make_async_copy(src.at[pl.ds(pl.multiple_of(src_row8, SUBLANES), SUBLANES)],
                                 dst.at[pl.ds(pl.multiple_of(dst_row8, SUBLANES), SUBLANES)],
                                 sem)


def _dispatch_kernel(tt, start_ref, idx_ref, rank_ref, x1r_ref, xg_ref, dest_ref, zbuf, sem, zsem):
    @pl.when(pl.program_id(0) == 0)
    def _():
        zbuf[...] = jnp.zeros_like(zbuf)
        n_sorted = xg_ref.shape[0] - zbuf.shape[0]
        fill = pltpu.make_async_copy(zbuf, xg_ref.at[pl.ds(n_sorted, zbuf.shape[0])], zsem)
        fill.start()
        fill.wait()

    def issue(t, carry):
        for kk in range(TOP_K):
            j = t * TOP_K + kk
            dest8 = (start_ref[idx_ref[j]] + rank_ref[j]) * ROW_TILES
            dest_ref[j] = dest8
            _row_copy(x1r_ref, t * ROW_TILES, xg_ref, dest8, sem).start()
        return carry

    lax.fori_loop(0, tt, issue, 0)
    n_rows = tt * TOP_K * ROW_TILES
    pltpu.make_async_copy(xg_ref.at[pl.ds(0, n_rows)], xg_ref.at[pl.ds(0, n_rows)], sem).wait()


def _dispatch(seg_start, idx, rank, x1r, tt, slack):
    m = x1r.shape[0] // ROW_TILES
    smem_blk = pl.BlockSpec((tt * TOP_K,), lambda i: (i,), memory_space=pltpu.SMEM)
    return pl.pallas_call(
        functools.partial(_dispatch_kernel, tt),
        grid=(m // tt,),
        in_specs=[pl.BlockSpec(memory_space=pltpu.SMEM), smem_blk, smem_blk,
                  pl.BlockSpec((tt * ROW_TILES, LANES), lambda i: (i, 0))],
        out_specs=[pl.BlockSpec(memory_space=pl.ANY), smem_blk],
        out_shape=[jax.ShapeDtypeStruct(((m * TOP_K + slack) * ROW_TILES, LANES), F32),
                   jax.ShapeDtypeStruct((m * TOP_K,), jnp.int32)],
        scratch_shapes=[pltpu.VMEM((slack * ROW_TILES, LANES), F32),
                        pltpu.SemaphoreType.DMA(()), pltpu.SemaphoreType.DMA(())],
        compiler_params=_cparams(("arbitrary",)),
        name="dispatch",
    )(seg_start, idx, rank, x1r)


def _experts_kernel(rb, e_ref, row_ref, valid_ref, new_ref, slot_ref, nxt_ref,
                    xg_hbm, wg_hbm, wu_hbm, wd_hbm, y_hbm,
                    *scratch):
    j = pl.program_id(0)
    n_steps = pl.num_programs(0)
    ring_pos = lax.rem(j, EXPERT_RING)
    valid = valid_ref[j]
    xbufs = scratch[:EXPERT_RING]
    ybufs = scratch[EXPERT_RING:2 * EXPERT_RING]
    wg_buf, wu_buf, wd_buf, wgb, wub, wdb, wsem, xsem, ysem = scratch[2 * EXPERT_RING:]

    def tile_rows(row, n):
        return pl.ds(pl.multiple_of(row * ROW_TILES, SUBLANES), n * ROW_TILES)

    def x_copy(item, slot):
        return pltpu.make_async_copy(xg_hbm.at[tile_rows(row_ref[item], rb)], xbufs[slot],
                                     xsem.at[slot])

    def y_copies(item, slot, start):
        row0 = row_ref[item]
        n = valid_ref[item]

        def run(copy):
            if start:
                copy.start()
            else:
                copy.wait()

        @pl.when(n == rb)
        def _():
            run(pltpu.make_async_copy(ybufs[slot], y_hbm.at[tile_rows(row0, rb)], ysem.at[slot]))

        @pl.when(n < rb)
        def _():
            piece = rb // 2
            while piece >= 1:
                off = n - jnp.bitwise_and(n, 2 * piece - 1)

                @pl.when(jnp.bitwise_and(n, piece) != 0)
                def _(off=off, piece=piece):
                    run(pltpu.make_async_copy(ybufs[slot].at[tile_rows(off, piece)],
                                              y_hbm.at[tile_rows(row0 + off, piece)],
                                              ysem.at[slot]))

                piece //= 2

    def fetch(e, slot):
        return (pltpu.make_async_copy(wg_hbm.at[e], wg_buf.at[slot], wsem.at[slot, 0]),
                pltpu.make_async_copy(wu_hbm.at[e], wu_buf.at[slot], wsem.at[slot, 1]),
                pltpu.make_async_copy(wd_hbm.at[e], wd_buf.at[slot], wsem.at[slot, 2]))

    ahead = EXPERT_RING - 1

    @pl.when(j == 0)
    def _():
        for item in range(ahead):
            @pl.when(valid_ref[item] > 0)
            def _(item=item):
                x_copy(item, item).start()

    ahead_item = jnp.minimum(j + ahead, n_steps - 1)
    for slot in range(EXPERT_RING):
        here = ring_pos == slot

        @pl.when(jnp.logical_and(here, j >= EXPERT_RING))
        def _(slot=slot):
            y_copies(j - EXPERT_RING, slot, start=False)

        @pl.when(jnp.logical_and(here, valid > 0))
        def _(slot=slot):
            x_copy(j, slot).wait()

            @pl.when(jnp.logical_and(j + ahead < n_steps, valid_ref[ahead_item] > 0))
            def _():
                x_copy(ahead_item, (slot + ahead) % EXPERT_RING).start()

    @pl.when(new_ref[j] != 0)
    def _():
        slot = slot_ref[j]

        @pl.when(j == 0)
        def _():
            for copy in fetch(e_ref[j], slot):
                copy.start()

        for copy in fetch(e_ref[j], slot):
            copy.wait()
        wgb[...] = wg_buf[slot].astype(BF16)
        wub[...] = wu_buf[slot].astype(BF16)
        wdb[...] = wd_buf[slot].astype(BF16)

        @pl.when(nxt_ref[j] >= 0)
        def _():
            for copy in fetch(nxt_ref[j], 1 - slot):
                copy.start()

    for slot in range(EXPERT_RING):
        here = ring_pos == slot

        @pl.when(jnp.logical_and(here, valid > 0))
        def _(slot=slot):
            xb = jnp.concatenate(
                [xbufs[slot][pl.ds(s, rb, stride=ROW_TILES), :] for s in range(ROW_TILES)],
                axis=1).astype(BF16)
            g = _dot(xb, wgb[...])
            u = _dot(xb, wub[...])
            y = _dot((g * _sigmoid(g) * u).astype(BF16), wdb[...])
            for s in range(ROW_TILES):
                ybufs[slot][pl.ds(s, rb, stride=ROW_TILES), :] = y[:, s * LANES:(s + 1) * LANES]
            y_copies(j, slot, start=True)

        @pl.when(jnp.logical_and(here, j == n_steps - 1))
        def _(slot=slot):
            for back in range(EXPERT_RING - 1, -1, -1):
                y_copies(j - back, (slot - back) % EXPERT_RING, start=False)


def _experts(item_e, item_row, item_valid, item_new, item_slot, item_nxt,
             xg, w_eg, w_eu, w_ed, rb):
    n_items = item_e.shape[0]
    n_rows = xg.shape[0] // ROW_TILES - rb
    hbm = pl.BlockSpec(memory_space=pl.ANY)
    row_buf = pltpu.VMEM((rb * ROW_TILES, LANES), F32)
    grid_spec = pltpu.PrefetchScalarGridSpec(
        num_scalar_prefetch=6,
        grid=(n_items,),
        in_specs=[hbm, hbm, hbm, hbm],
        out_specs=hbm,
        scratch_shapes=[row_buf] * (2 * EXPERT_RING) + [
                        pltpu.VMEM((2, D_MODEL, D_EXPERT), F32),
                        pltpu.VMEM((2, D_MODEL, D_EXPERT), F32),
                        pltpu.VMEM((2, D_EXPERT, D_MODEL), F32),
                        pltpu.VMEM((D_MODEL, D_EXPERT), BF16),
                        pltpu.VMEM((D_MODEL, D_EXPERT), BF16),
                        pltpu.VMEM((D_EXPERT, D_MODEL), BF16),
                        pltpu.SemaphoreType.DMA((2, 3)),
                        pltpu.SemaphoreType.DMA((EXPERT_RING,)),
                        pltpu.SemaphoreType.DMA((EXPERT_RING,))],
    )
    return pl.pallas_call(
        functools.partial(_experts_kernel, rb),
        grid_spec=grid_spec,
        out_shape=jax.ShapeDtypeStruct((n_rows * ROW_TILES, LANES), F32),
        compiler_params=_cparams(("arbitrary",)),
        name="experts",
    )(item_e, item_row, item_valid, item_new, item_slot, item_nxt, xg, w_eg, w_eu, w_ed)


def _combine_kernel(alpha, n_p, n_steps, dest_ref, dnext_ref, gate_ref, x1_ref, y_ref,
                    wsg_ref, wsu_ref, wsd_ref, g_ref, b_ref, x2p_ref, x2s_ref,
                    ybuf_a, ybuf_b, sem_a, sem_b):
    step = pl.program_id(0)
    tt = x1_ref.shape[0] // 2
    n_rows = TOP_K * tt * ROW_TILES

    def issue(dref, half, buf, sem):
        def body(t, carry):
            for kk in range(TOP_K):
                _row_copy(y_ref, dref[(half * tt + t) * TOP_K + kk], buf,
                          (kk * tt + t) * ROW_TILES, sem).start()
            return carry

        lax.fori_loop(0, tt, body, 0)

    def wait(buf, sem):
        pltpu.make_async_copy(y_ref.at[pl.ds(0, n_rows)], buf, sem).wait()

    def routed(half, buf):
        gates = gate_ref[half * tt:(half + 1) * tt, :]
        chunks = []
        for s in range(ROW_TILES):
            acc = jnp.zeros((tt, LANES), F32)
            for kk in range(TOP_K):
                acc = acc + gates[:, kk:kk + 1] * buf[pl.ds(kk * tt * ROW_TILES + s, tt,
                                                            stride=ROW_TILES), :]
            chunks.append(acc)
        return jnp.concatenate(chunks, axis=1)

    @pl.when(step == 0)
    def _():
        issue(dest_ref, 0, ybuf_a, sem_a)

    issue(dest_ref, 1, ybuf_b, sem_b)

    x1 = x1_ref[...]
    xb = x1.astype(BF16)
    gs = _dot(xb, wsg_ref[...])
    us = _dot(xb, wsu_ref[...])
    base = alpha * x1 + _dot((gs * _sigmoid(gs) * us).astype(BF16), wsd_ref[...])

    wait(ybuf_a, sem_a)
    x2a = _layer_norm(base[:tt, :] + routed(0, ybuf_a), g_ref[...], b_ref[...])

    @pl.when(step + 1 < n_steps)
    def _():
        issue(dnext_ref, 0, ybuf_a, sem_a)

    wait(ybuf_b, sem_b)
    x2b = _layer_norm(base[tt:, :] + routed(1, ybuf_b), g_ref[...], b_ref[...])

    @pl.when(step < n_p)
    def _():
        x2p_ref[:tt, :] = x2a
        x2p_ref[tt:, :] = x2b

    @pl.when(step >= n_p)
    def _():
        x2s_ref[:tt, :] = x2a
        x2s_ref[tt:, :] = x2b


def _combine(dest8, gates, x1, y, wsg, wsu, wsd, g, b, alpha, tt, mp):
    m = x1.shape[0]
    st = 2 * tt
    n_steps = m // st
    n_p = mp // st
    return pl.pallas_call(
        functools.partial(_combine_kernel, alpha, n_p, n_steps),
        grid=(n_steps,),
        in_specs=[pl.BlockSpec((st * TOP_K,), lambda i: (i,), memory_space=pltpu.SMEM),
                  pl.BlockSpec((st * TOP_K,), lambda i: (jnp.minimum(i + 1, n_steps - 1),),
                               memory_space=pltpu.SMEM),
                  pl.BlockSpec((st, TOP_K), lambda i: (i, 0)),
                  pl.BlockSpec((st, D_MODEL), lambda i: (i, 0)),
                  pl.BlockSpec(memory_space=pl.ANY),
                  pl.BlockSpec((D_MODEL, D_EXPERT), lambda i: (0, 0)),
                  pl.BlockSpec((D_MODEL, D_EXPERT), lambda i: (0, 0)),
                  pl.BlockSpec((D_EXPERT, D_MODEL), lambda i: (0, 0)),
                  pl.BlockSpec((1, D_MODEL), lambda i: (0, 0)),
                  pl.BlockSpec((1, D_MODEL), lambda i: (0, 0))],
        out_specs=[pl.BlockSpec((st, D_MODEL), lambda i: (jnp.minimum(i, n_p - 1), 0)),
                   pl.BlockSpec((st, D_MODEL), lambda i: (jnp.maximum(i - n_p, 0), 0))],
        out_shape=[jax.ShapeDtypeStruct((mp, D_MODEL), F32),
                   jax.ShapeDtypeStruct((m - mp, D_MODEL), F32)],
        scratch_shapes=[pltpu.VMEM((TOP_K * tt * ROW_TILES, LANES), F32),
                        pltpu.VMEM((TOP_K * tt * ROW_TILES, LANES), F32),
                        pltpu.SemaphoreType.DMA(()), pltpu.SemaphoreType.DMA(())],
        compiler_params=_cparams(("arbitrary",)),
        name="combine",
    )(dest8, dest8, gates, x1, y, wsg, wsu, wsd, g, b)


PROJ_TM = 512
PROJ_TN = 1024
MIX_TT = 256
MIX_CHUNK = 128
SAMPLE_NSEQ = 4
ROUTER_TT = 256
DISPATCH_TT = 512
EXPERT_RB = 256
EXPERT_RING = 3
COMBINE_TT = 128


def _moe(x1, x1r, mp, w_router, b_router, w_eg, w_eu, w_ed, w_sg, w_su, w_sd, ln_g, ln_b, alpha):
    m = x1.shape[0]
    wr_hi, wr_lo = _split_hi_lo(w_router)
    idx, gates, rank, cnt = _router(x1, wr_hi, wr_lo, b_router[None, :], ROUTER_TT)

    rb = EXPERT_RB
    n_rows = m * TOP_K
    n_items = n_rows // rb + N_EXPERTS
    counts = cnt[0].astype(jnp.int32)
    seg_end = jnp.cumsum(counts)
    seg_start = seg_end - counts
    n_chunk = (counts + rb - 1) // rb
    item_end = jnp.cumsum(n_chunk)
    item_start = item_end - n_chunk
    item = jnp.arange(n_items, dtype=jnp.int32)
    item_c = jnp.minimum(item, item_end[-1] - 1)
    item_e = jnp.sum((item_end[None, :] <= item_c[:, None]).astype(jnp.int32), axis=1)
    onehot_e = item_e[:, None] == jnp.arange(N_EXPERTS, dtype=jnp.int32)[None, :]

    def pick(table):
        return jnp.sum(jnp.where(onehot_e, table[None, :], 0), axis=1)

    chunk = item_c - pick(item_start)
    valid = item < item_end[-1]
    item_row = pick(seg_start) + chunk * rb
    item_valid = jnp.where(valid, jnp.clip(pick(counts) - chunk * rb, 0, rb), 0)

    experts = jnp.arange(N_EXPERTS, dtype=jnp.int32)
    nonempty = counts > 0
    slot_e = (jnp.cumsum(nonempty.astype(jnp.int32)) - 1) % 2
    later = jnp.logical_and(experts[None, :] > experts[:, None], nonempty[None, :])
    nxt_e = jnp.min(jnp.where(later, experts[None, :], N_EXPERTS), axis=1)
    nxt_e = jnp.where(nxt_e == N_EXPERTS, -1, nxt_e)
    item_new = jnp.logical_and(valid, chunk == 0).astype(jnp.int32)

    xg, dest8 = _dispatch(seg_start, idx.reshape(-1), rank.reshape(-1), x1r, DISPATCH_TT, rb)
    y = _experts(item_e, item_row, item_valid, item_new, pick(slot_e), pick(nxt_e),
                 xg, w_eg, w_eu, w_ed, rb)
    return _combine(dest8, gates, x1, y, w_sg.astype(BF16), w_su.astype(BF16),
                    w_sd.astype(BF16), ln_g[None, :], ln_b[None, :], alpha, COMBINE_TT, mp)


def _layer(xp, xs, conv_s, c_s, n_s, m_s, w_in, b_gates, w_conv, mh_gain, w_out, ln1_g, ln1_b,
           w_router, b_router, w_eg, w_eu, w_ed, w_sg, w_su, w_sd, ln2_g, ln2_b, alpha):
    bp, tp, _ = xp.shape
    bs, ts, _ = xs.shape
    mp, ms = bp * tp, bs * ts
    xp2 = xp.reshape(mp, D_MODEL)
    xs2 = xs.reshape(ms, D_MODEL)

    w_main = jnp.concatenate([w_in[:, :GATE_COL0], w_in[:, GATE_COL0 + 2 * N_HEADS:]],
                             axis=1).astype(BF16)
    w_gate = jnp.pad(w_in[:, GATE_COL0:GATE_COL0 + 2 * N_HEADS],
                     ((0, 0), (0, LANES - 2 * N_HEADS)))
    wg_hi, wg_lo = _split_hi_lo(w_gate)

    z_s = _proj(xs2, w_main, PROJ_TM, PROJ_TN)
    zg_s = _gate_proj(xs2, wg_hi, wg_lo, PROJ_TM)
    gt_s = zg_s[:, :SUBLANES].reshape(bs, ts, SUBLANES).transpose(0, 2, 1)
    mix_s, conv_n, c_n, n_n, m_n = _mixer(
        z_s, zg_s, gt_s, b_gates, w_conv, mh_gain[None, :],
        conv_s, c_s, n_s, m_s[:, None, :], nb=bs, ts=ts, nseq=SAMPLE_NSEQ)

    x1, x1r, conv_p, c_p, n_p, m_p = _prompt_layer_half(
        xp2, xs2, mix_s, b_gates, w_main, wg_hi, wg_lo, w_conv, mh_gain[None, :],
        w_out.astype(BF16), ln1_g[None, :], ln1_b[None, :], alpha,
        nb=bp, nt=tp // MIX_TT, tt=MIX_TT, chunk=MIX_CHUNK)

    x2p, x2s = _moe(x1, x1r, mp, w_router, b_router, w_eg, w_eu, w_ed, w_sg, w_su, w_sd,
                    ln2_g, ln2_b, alpha)
    states_p = (conv_p, c_p, n_p, m_p[:, 0, :])
    states_s = (conv_n, c_n, n_n, m_n[:, 0, :])
    return x2p.reshape(bp, tp, D_MODEL), x2s.reshape(bs, ts, D_MODEL), states_p, states_s


def kernel(x_prompt, x_sample, cache_conv, state_mlstm_C, state_mlstm_n, state_mlstm_m, w_in, b_gates, w_conv, mh_gain, w_out, ln1_g, ln1_b, w_router, b_router, w_exp_gate, w_exp_up, w_exp_down, w_sh_gate, w_sh_up, w_sh_down, ln2_g, ln2_b):
    depth = w_in.shape[0]
    alpha = (2.0 * depth) ** 0.25
    hp, hs = x_prompt, x_sample
    outs_p = [[], [], [], []]
    outs_s = [[], [], [], []]
    for l in range(depth):
        hp, hs, st_p, st_s = _layer(
            hp, hs, cache_conv[l], state_mlstm_C[l], state_mlstm_n[l], state_mlstm_m[l],
            w_in[l], b_gates[l], w_conv[l], mh_gain[l], w_out[l], ln1_g[l], ln1_b[l],
            w_router[l], b_router[l], w_exp_gate[l], w_exp_up[l], w_exp_down[l],
            w_sh_gate[l], w_sh_up[l], w_sh_down[l], ln2_g[l], ln2_b[l], alpha)
        for acc, val in zip(outs_p, st_p):
            acc.append(val)
        for acc, val in zip(outs_s, st_s):
            acc.append(val)
    return (hp, hs) + tuple(jnp.stack(a) for a in outs_p) + tuple(jnp.stack(a) for a in outs_s)
```

```python
import functools

import jax
import jax.numpy as jnp
from jax import lax
from jax.experimental import pallas as pl
from jax.experimental.pallas import tpu as pltpu

F32 = jnp.float32
BF16 = jnp.bfloat16

D_MODEL = 1024
N_HEADS = 4
HEAD_V = 256
HEAD_QK = 128
N_EXPERTS = 256
TOP_K = 8
D_EXPERT = 256
ROUTED_SCALE = 2.5
LN_EPS = 1e-5
CONV_K = 3

LANES = 128
SUBLANES = 8
ROW_TILES = D_MODEL // LANES
GATE_COL0 = 6 * D_MODEL

VMEM_LIMIT = 56 * 1024 * 1024


def _cparams(sem):
    return pltpu.CompilerParams(dimension_semantics=sem, vmem_limit_bytes=VMEM_LIMIT)


def _sigmoid(x):
    return 1.0 / (1.0 + jnp.exp(-x))


def _log_sigmoid(x):
    return jnp.minimum(x, 0.0) - jnp.log(1.0 + jnp.exp(-jnp.abs(x)))


def _layer_norm(x, g, b):
    mu = jnp.mean(x, axis=-1, keepdims=True)
    xc = x - mu
    var = jnp.mean(xc * xc, axis=-1, keepdims=True)
    return xc * lax.rsqrt(var + LN_EPS) * g + b


def _split_hi_lo(x):
    hi = x.astype(BF16)
    lo = (x - hi.astype(F32)).astype(BF16)
    return hi, lo


def _dot(a, b):
    return jnp.dot(a, b, preferred_element_type=F32)


def _dot3(x, w_hi, w_lo):
    x_hi, x_lo = _split_hi_lo(x)
    return _dot(x_hi, w_hi) + _dot(x_lo, w_hi) + _dot(x_hi, w_lo)


def _proj_kernel(x_ref, w_ref, z_ref):
    z_ref[...] = _dot(x_ref[...].astype(BF16), w_ref[...])


def _proj(x, w, tm, tn):
    m, k = x.shape
    n = w.shape[1]
    return pl.pallas_call(
        _proj_kernel,
        grid=(m // tm, n // tn),
        in_specs=[pl.BlockSpec((tm, k), lambda i, j: (i, 0)),
                  pl.BlockSpec((k, tn), lambda i, j: (0, j))],
        out_specs=pl.BlockSpec((tm, tn), lambda i, j: (i, j)),
        out_shape=jax.ShapeDtypeStruct((m, n), F32),
        compiler_params=_cparams(("parallel", "arbitrary")),
        name="proj",
    )(x, w)


def _gate_proj_kernel(x_ref, wh_ref, wl_ref, z_ref):
    z_ref[...] = _dot3(x_ref[...], wh_ref[...], wl_ref[...])


def _gate_proj(x, w_hi, w_lo, tm):
    m, k = x.shape
    n = w_hi.shape[1]
    return pl.pallas_call(
        _gate_proj_kernel,
        grid=(m // tm,),
        in_specs=[pl.BlockSpec((tm, k), lambda i: (i, 0)),
                  pl.BlockSpec((k, n), lambda i: (0, 0)),
                  pl.BlockSpec((k, n), lambda i: (0, 0))],
        out_specs=pl.BlockSpec((tm, n), lambda i: (i, 0)),
        out_shape=jax.ShapeDtypeStruct((m, n), F32),
        compiler_params=_cparams(("parallel",)),
        name="gate_proj",
    )(x, w_hi, w_lo)


def _conv_branch(u, zcb, wc, conv_ref, seq=0):
    tt = u.shape[0]
    carry = conv_ref[seq]
    rows = lax.broadcasted_iota(jnp.int32, (tt, D_MODEL), 0)
    u1 = jnp.where(rows == 0, carry[1:2, :], pltpu.roll(u, 1, 0))
    u2 = jnp.where(rows == 0, carry[0:1, :],
                   jnp.where(rows == 1, carry[1:2, :], pltpu.roll(u, 2, 0)))
    conv_out = u2 * wc[0:1, :] + u1 * wc[1:2, :] + u * wc[2:3, :]
    conv_ref[seq] = u[tt - 2:tt, :]
    return zcb * conv_out


def _mlstm_chunk(q, k, v, ig_col, lf_col, ig_row, lf_row, c_state, n_row, m_prev):
    chunk = q.shape[0]
    ti = lax.broadcasted_iota(jnp.int32, (chunk, chunk), 0)
    si = lax.broadcasted_iota(jnp.int32, (chunk, chunk), 1)
    causal = si <= ti
    b_col = jnp.sum(jnp.where(causal, lf_row, 0.0), axis=1, keepdims=True)
    b_row = jnp.sum(jnp.where(ti <= si, lf_col, 0.0), axis=0, keepdims=True)
    inter = b_col + m_prev
    dmat = jnp.where(causal, b_col - b_row + ig_row, -jnp.inf)
    m_t = jnp.maximum(inter, jnp.max(dmat, axis=1, keepdims=True))
    w_intra = jnp.exp(dmat - m_t)
    w_inter = jnp.exp(inter - m_t)
    qb = q.astype(BF16)
    kb = k.astype(BF16)
    vb = v.astype(BF16)
    s = lax.dot_general(qb, kb, (((1,), (1,)), ((), ())), preferred_element_type=F32) * w_intra
    num = w_inter * _dot(qb, c_state.astype(BF16)) + _dot(s.astype(BF16), vb)
    den = (w_inter * jnp.sum(q * n_row, axis=1, keepdims=True)
           + jnp.sum(s, axis=1, keepdims=True))
    h = num / jnp.maximum(jnp.abs(den), jnp.exp(-m_t))
    m_new = m_t[chunk - 1:chunk, :]
    b_last = b_col[chunk - 1:chunk, :]
    ws_col = jnp.exp(b_last - b_col + ig_col - m_new)
    cdecay = jnp.exp(inter[chunk - 1:chunk, :] - m_new)
    kw = k * ws_col
    c_new = cdecay * c_state + lax.dot_general(
        kw.astype(BF16), vb, (((0,), (0,)), ((), ())), preferred_element_type=F32)
    n_new = cdecay * n_row + jnp.sum(kw, axis=0, keepdims=True)
    return h, c_new, n_new, m_new


def _mlstm_branch(chunk, bg_ref, zg, gt, zqk, zv, zog, gain_ref, c_ref, n_ref, m_ref, mo_ref,
                  seq=0, row0=0):
    tt = zqk.shape[0]
    for h in range(N_HEADS):
        ig_col_all = zg[:, h:h + 1] + bg_ref[h]
        lf_col_all = _log_sigmoid(zg[:, N_HEADS + h:N_HEADS + h + 1] + bg_ref[N_HEADS + h])
        ig_row_all = gt[h:h + 1, :] + bg_ref[h]
        lf_row_all = _log_sigmoid(gt[N_HEADS + h:N_HEADS + h + 1, :] + bg_ref[N_HEADS + h])
        vcols = slice(h * HEAD_V, (h + 1) * HEAD_V)
        for c in range(tt // chunk):
            r0, r1 = c * chunk, (c + 1) * chunk
            q = zqk[r0:r1, h * HEAD_QK:(h + 1) * HEAD_QK]
            k = zqk[r0:r1, (N_HEADS + h) * HEAD_QK:(N_HEADS + h + 1) * HEAD_QK] * (HEAD_QK ** -0.5)
            hh, c_new, n_new, m_new = _mlstm_chunk(
                q, k, zv[r0:r1, vcols], ig_col_all[r0:r1, :], lf_col_all[r0:r1, :],
                ig_row_all[:, r0:r1], lf_row_all[:, r0:r1],
                c_ref[seq, h], n_ref[seq, h:h + 1, :], m_ref[seq, :, h:h + 1])
            c_ref[seq, h] = c_new
            n_ref[seq, h:h + 1, :] = n_new
            m_ref[seq, :, h:h + 1] = m_new
            mu = jnp.mean(hh, axis=-1, keepdims=True)
            hc = hh - mu
            var = jnp.mean(hc * hc, axis=-1, keepdims=True)
            hn = hc * lax.rsqrt(var + LN_EPS) * gain_ref[:, vcols]
            mo_ref[row0 + r0:row0 + r1, vcols] = hn * _sigmoid(zog[r0:r1, vcols])


def _store_x1(x, mix, wout_ref, g_ref, b_ref, alpha, x1_ref, x1r_ref):
    tt = x.shape[0]
    x1 = _layer_norm(alpha * x + _dot(mix.astype(BF16), wout_ref[...]), g_ref[...], b_ref[...])
    x1_ref[...] = x1
    for s in range(ROW_TILES):
        x1r_ref[pl.ds(s, tt, stride=ROW_TILES), :] = x1[:, s * LANES:(s + 1) * LANES]


def _mixer_kernel(ts, bg_ref,
                  zcb_ref, zcc_ref, zch_ref, zqk_ref, zv_ref, zog_ref, zga_ref, zgb_ref,
                  zg_ref, gt_ref, wconv_ref, gain_ref,
                  conv0_ref, c0_ref, n0_ref, m0_ref,
                  mix_ref, conv_ref, c_ref, n_ref, m_ref, mo_ref):
    conv_ref[...] = conv0_ref[...]
    c_ref[...] = c0_ref[...]
    n_ref[...] = n0_ref[...]
    m_ref[...] = m0_ref[...]
    for seq in range(conv_ref.shape[0]):
        rows = slice(seq * ts, (seq + 1) * ts)
        a = _conv_branch(zcc_ref[rows, :] * zch_ref[rows, :], zcb_ref[rows, :], wconv_ref[...],
                         conv_ref, seq)
        _mlstm_branch(ts, bg_ref, zg_ref[rows, :], gt_ref[seq], zqk_ref[rows, :], zv_ref[rows, :],
                      zog_ref[rows, :], gain_ref, c_ref, n_ref, m_ref, mo_ref, seq, seq * ts)
        mix_ref[rows, :] = (_sigmoid(zga_ref[rows, :]) * a
                            + _sigmoid(zgb_ref[rows, :]) * mo_ref[rows, :])


def _mixer(z, zg, gt3, b_gates, w_conv, mh_gain, conv0, c0, n0, m0, *, nb, ts, nseq):
    tt = nseq * ts

    def zspec(j):
        return pl.BlockSpec((tt, D_MODEL), lambda b, j=j: (b, j))

    in_specs = [pl.BlockSpec(memory_space=pltpu.SMEM)]
    in_specs += [zspec(j) for j in range(8)]
    state_specs = [
        pl.BlockSpec((nseq, CONV_K - 1, D_MODEL), lambda b: (b, 0, 0)),
        pl.BlockSpec((nseq, N_HEADS, HEAD_QK, HEAD_V), lambda b: (b, 0, 0, 0)),
        pl.BlockSpec((nseq, N_HEADS, HEAD_QK), lambda b: (b, 0, 0)),
        pl.BlockSpec((nseq, 1, N_HEADS), lambda b: (b, 0, 0)),
    ]
    in_specs += [
        pl.BlockSpec((tt, LANES), lambda b: (b, 0)),
        pl.BlockSpec((nseq, SUBLANES, ts), lambda b: (b, 0, 0)),
        pl.BlockSpec((CONV_K, D_MODEL), lambda b: (0, 0)),
        pl.BlockSpec((1, D_MODEL), lambda b: (0, 0)),
    ] + state_specs
    out_specs = [pl.BlockSpec((tt, D_MODEL), lambda b: (b, 0))] + state_specs
    out_shape = [
        jax.ShapeDtypeStruct((nb * ts, D_MODEL), F32),
        jax.ShapeDtypeStruct((nb, CONV_K - 1, D_MODEL), F32),
        jax.ShapeDtypeStruct((nb, N_HEADS, HEAD_QK, HEAD_V), F32),
        jax.ShapeDtypeStruct((nb, N_HEADS, HEAD_QK), F32),
        jax.ShapeDtypeStruct((nb, 1, N_HEADS), F32),
    ]
    return pl.pallas_call(
        functools.partial(_mixer_kernel, ts),
        grid=(nb // nseq,),
        in_specs=in_specs,
        out_specs=out_specs,
        out_shape=out_shape,
        scratch_shapes=[pltpu.VMEM((tt, D_MODEL), F32)],
        compiler_params=_cparams(("parallel",)),
        name="mixer_sample",
    )(b_gates, z, z, z, z, z, z, z, z, zg, gt3, w_conv, mh_gain, conv0, c0, n0, m0)


def _prompt_kernel(chunk, nb, ns, alpha, bg_ref, x_ref, w_ref, wgh_ref, wgl_ref, wconv_ref,
                   gain_ref, wout_ref, g_ref, b_ref, xs_ref, mixs_ref,
                   x1_ref, x1r_ref, conv_ref, c_ref, n_ref, m_ref, mo_ref):
    b = pl.program_id(0)
    t = pl.program_id(1)

    @pl.when(b < nb)
    def _():
        @pl.when(t == 0)
        def _():
            conv_ref[...] = jnp.zeros_like(conv_ref)
            c_ref[...] = jnp.zeros_like(c_ref)
            n_ref[...] = jnp.zeros_like(n_ref)
            m_ref[...] = jnp.zeros_like(m_ref)

        x = x_ref[...]
        xb = x.astype(BF16)

        def z(j):
            return _dot(xb, w_ref[:, j * D_MODEL:(j + 1) * D_MODEL])

        a = _conv_branch(z(1) * z(2), z(0), wconv_ref[...], conv_ref)
        zg = _dot3(x, wgh_ref[...], wgl_ref[...])
        _mlstm_branch(chunk, bg_ref, zg, zg.T, z(3), z(4), z(5), gain_ref,
                      c_ref, n_ref, m_ref, mo_ref)
        mix = _sigmoid(z(6)) * a + _sigmoid(z(7)) * mo_ref[...]
        _store_x1(x, mix, wout_ref, g_ref, b_ref, alpha, x1_ref, x1r_ref)

    @pl.when(jnp.logical_and(b == nb, t < ns))
    def _():
        _store_x1(xs_ref[...], mixs_ref[...], wout_ref, g_ref, b_ref, alpha, x1_ref, x1r_ref)


def _prompt_layer_half(xp, xs, mix_s, b_gates, w_main, wg_hi, wg_lo, w_conv, mh_gain, w_out,
                       ln_g, ln_b, alpha, *, nb, nt, tt, chunk):
    mp, ms = xp.shape[0], xs.shape[0]
    ns = ms // tt
    last_p = nb * nt - 1

    def prow(b, t):
        return jnp.minimum(b * nt + t, last_p)

    def srow(b, t):
        return jnp.where(b == nb, jnp.minimum(t, ns - 1), 0)

    def orow(b, t):
        return jnp.where(b == nb, nb * nt + jnp.minimum(t, ns - 1), b * nt + t)

    def state(b):
        return jnp.minimum(b, nb - 1)

    const2 = lambda b, t: (0, 0)
    in_specs = [
        pl.BlockSpec(memory_space=pltpu.SMEM),
        pl.BlockSpec((tt, D_MODEL), lambda b, t: (prow(b, t), 0)),
        pl.BlockSpec(w_main.shape, const2, pipeline_mode=pl.Buffered(1)),
        pl.BlockSpec(wg_hi.shape, const2),
        pl.BlockSpec(wg_lo.shape, const2),
        pl.BlockSpec((CONV_K, D_MODEL), const2),
        pl.BlockSpec((1, D_MODEL), const2),
        pl.BlockSpec((D_MODEL, D_MODEL), const2),
        pl.BlockSpec((1, D_MODEL), const2),
        pl.BlockSpec((1, D_MODEL), const2),
        pl.BlockSpec((tt, D_MODEL), lambda b, t: (srow(b, t), 0)),
        pl.BlockSpec((tt, D_MODEL), lambda b, t: (srow(b, t), 0)),
    ]
    out_specs = [
        pl.BlockSpec((tt, D_MODEL), lambda b, t: (orow(b, t), 0)),
        pl.BlockSpec((tt * ROW_TILES, LANES), lambda b, t: (orow(b, t), 0)),
        pl.BlockSpec((1, CONV_K - 1, D_MODEL), lambda b, t: (state(b), 0, 0)),
        pl.BlockSpec((1, N_HEADS, HEAD_QK, HEAD_V), lambda b, t: (state(b), 0, 0, 0)),
        pl.BlockSpec((1, N_HEADS, HEAD_QK), lambda b, t: (state(b), 0, 0)),
        pl.BlockSpec((1, 1, N_HEADS), lambda b, t: (state(b), 0, 0)),
    ]
    out_shape = [
        jax.ShapeDtypeStruct((mp + ms, D_MODEL), F32),
        jax.ShapeDtypeStruct(((mp + ms) * ROW_TILES, LANES), F32),
        jax.ShapeDtypeStruct((nb, CONV_K - 1, D_MODEL), F32),
        jax.ShapeDtypeStruct((nb, N_HEADS, HEAD_QK, HEAD_V), F32),
        jax.ShapeDtypeStruct((nb, N_HEADS, HEAD_QK), F32),
        jax.ShapeDtypeStruct((nb, 1, N_HEADS), F32),
    ]
    return pl.pallas_call(
        functools.partial(_prompt_kernel, chunk, nb, ns, alpha),
        grid=(nb + 1, nt),
        in_specs=in_specs,
        out_specs=out_specs,
        out_shape=out_shape,
        scratch_shapes=[pltpu.VMEM((tt, D_MODEL), F32)],
        compiler_params=_cparams(("arbitrary", "arbitrary")),
        name="prompt_half",
    )(b_gates, xp, w_main, wg_hi, wg_lo, w_conv, mh_gain, w_out, ln_g, ln_b, xs, mix_s)


def _router_kernel(x_ref, wh_ref, wl_ref, br_ref, idx_ref, gate_ref, rank_ref, cnt_ref):
    tt = x_ref.shape[0]

    @pl.when(pl.program_id(0) == 0)
    def _():
        cnt_ref[...] = jnp.zeros_like(cnt_ref)

    scores = _sigmoid(_dot3(x_ref[...], wh_ref[...], wl_ref[...]))
    lane = lax.broadcasted_iota(jnp.int32, (tt, N_EXPERTS), 1)
    work = scores + br_ref[...]
    picked = []
    sel = []
    member = jnp.zeros((tt, N_EXPERTS), F32)
    for _ in range(TOP_K):
        mx = jnp.max(work, axis=1, keepdims=True)
        ik = jnp.min(jnp.where(work == mx, lane, N_EXPERTS), axis=1, keepdims=True)
        onehot = lane == ik
        picked.append((ik, onehot))
        sel.append(jnp.sum(jnp.where(onehot, scores, 0.0), axis=1, keepdims=True))
        work = jnp.where(onehot, -jnp.inf, work)
        member = member + onehot.astype(F32)
    total = sel[0]
    for sk in sel[1:]:
        total = total + sk

    ti = lax.broadcasted_iota(jnp.int32, (tt, tt), 0)
    si = lax.broadcasted_iota(jnp.int32, (tt, tt), 1)
    earlier = (si < ti).astype(BF16)
    before = _dot(earlier, member.astype(BF16)) + cnt_ref[...]

    lane_o = lax.broadcasted_iota(jnp.int32, (tt, TOP_K), 1)
    idx_o = jnp.zeros((tt, TOP_K), jnp.int32)
    gate_o = jnp.zeros((tt, TOP_K), F32)
    rank_o = jnp.zeros((tt, TOP_K), jnp.int32)
    for kk in range(TOP_K):
        ik, onehot = picked[kk]
        rk = jnp.sum(jnp.where(onehot, before, 0.0), axis=1, keepdims=True)
        idx_o = jnp.where(lane_o == kk, ik, idx_o)
        gate_o = jnp.where(lane_o == kk, sel[kk] / total * ROUTED_SCALE, gate_o)
        rank_o = jnp.where(lane_o == kk, rk.astype(jnp.int32), rank_o)
    idx_ref[...] = idx_o
    gate_ref[...] = gate_o
    rank_ref[...] = rank_o
    cnt_ref[...] = cnt_ref[...] + jnp.sum(member, axis=0, keepdims=True)


def _router(x1, wr_hi, wr_lo, b_router, tt):
    m = x1.shape[0]
    return pl.pallas_call(
        _router_kernel,
        grid=(m // tt,),
        in_specs=[pl.BlockSpec((tt, D_MODEL), lambda i: (i, 0)),
                  pl.BlockSpec((D_MODEL, N_EXPERTS), lambda i: (0, 0)),
                  pl.BlockSpec((D_MODEL, N_EXPERTS), lambda i: (0, 0)),
                  pl.BlockSpec((1, N_EXPERTS), lambda i: (0, 0))],
        out_specs=[pl.BlockSpec((tt, TOP_K), lambda i: (i, 0)),
                   pl.BlockSpec((tt, TOP_K), lambda i: (i, 0)),
                   pl.BlockSpec((tt, TOP_K), lambda i: (i, 0)),
                   pl.BlockSpec((1, N_EXPERTS), lambda i: (0, 0))],
        out_shape=[jax.ShapeDtypeStruct((m, TOP_K), jnp.int32),
                   jax.ShapeDtypeStruct((m, TOP_K), F32),
                   jax.ShapeDtypeStruct((m, TOP_K), jnp.int32),
                   jax.ShapeDtypeStruct((1, N_EXPERTS), F32)],
        compiler_params=_cparams(("arbitrary",)),
        name="router",
    )(x1, wr_hi, wr_lo, b_router)


def _row_copy(src, src_row8, dst, dst_row8, sem):
    return pltpu.make_async_copy(src.at[pl.ds(pl.multiple_of(src_row8, SUBLANES), SUBLANES)],
                                 dst.at[pl.ds(pl.multiple_of(dst_row8, SUBLANES), SUBLANES)],
                                 sem)


def _dispatch_kernel(tt, start_ref, idx_ref, rank_ref, x1r_ref, xg_ref, dest_ref, zbuf, sem, zsem):
    @pl.when(pl.program_id(0) == 0)
    def _():
        zbuf[...] = jnp.zeros_like(zbuf)
        n_sorted = xg_ref.shape[0] - zbuf.shape[0]
        fill = pltpu.make_async_copy(zbuf, xg_ref.at[pl.ds(n_sorted, zbuf.shape[0])], zsem)
        fill.start()
        fill.wait()

    def issue(t, carry):
        for kk in range(TOP_K):
            j = t * TOP_K + kk
            dest8 = (start_ref[idx_ref[j]] + rank_ref[j]) * ROW_TILES
            dest_ref[j] = dest8
            _row_copy(x1r_ref, t * ROW_TILES, xg_ref, dest8, sem).start()
        return carry

    lax.fori_loop(0, tt, issue, 0)
    n_rows = tt * TOP_K * ROW_TILES
    pltpu.make_async_copy(xg_ref.at[pl.ds(0, n_rows)], xg_ref.at[pl.ds(0, n_rows)], sem).wait()


def _dispatch(seg_start, idx, rank, x1r, tt, slack):
    m = x1r.shape[0] // ROW_TILES
    smem_blk = pl.BlockSpec((tt * TOP_K,), lambda i: (i,), memory_space=pltpu.SMEM)
    return pl.pallas_call(
        functools.partial(_dispatch_kernel, tt),
        grid=(m // tt,),
        in_specs=[pl.BlockSpec(memory_space=pltpu.SMEM), smem_blk, smem_blk,
                  pl.BlockSpec((tt * ROW_TILES, LANES), lambda i: (i, 0))],
        out_specs=[pl.BlockSpec(memory_space=pl.ANY), smem_blk],
        out_shape=[jax.ShapeDtypeStruct(((m * TOP_K + slack) * ROW_TILES, LANES), F32),
                   jax.ShapeDtypeStruct((m * TOP_K,), jnp.int32)],
        scratch_shapes=[pltpu.VMEM((slack * ROW_TILES, LANES), F32),
                        pltpu.SemaphoreType.DMA(()), pltpu.SemaphoreType.DMA(())],
        compiler_params=_cparams(("arbitrary",)),
        name="dispatch",
    )(seg_start, idx, rank, x1r)


def _experts_kernel(rb, e_ref, row_ref, valid_ref, new_ref, slot_ref, nxt_ref,
                    xg_hbm, wg_hbm, wu_hbm, wd_hbm, y_hbm,
                    *scratch):
    j = pl.program_id(0)
    n_steps = pl.num_programs(0)
    ring_pos = lax.rem(j, EXPERT_RING)
    valid = valid_ref[j]
    xbufs = scratch[:EXPERT_RING]
    ybufs = scratch[EXPERT_RING:2 * EXPERT_RING]
    wg_buf, wu_buf, wd_buf, wgb, wub, wdb, wsem, xsem, ysem = scratch[2 * EXPERT_RING:]

    def tile_rows(row, n):
        return pl.ds(pl.multiple_of(row * ROW_TILES, SUBLANES), n * ROW_TILES)

    def x_copy(item, slot):
        return pltpu.make_async_copy(xg_hbm.at[tile_rows(row_ref[item], rb)], xbufs[slot],
                                     xsem.at[slot])

    def y_copies(item, slot, start):
        row0 = row_ref[item]
        n = valid_ref[item]

        def run(copy):
            if start:
                copy.start()
            else:
                copy.wait()

        @pl.when(n == rb)
        def _():
            run(pltpu.make_async_copy(ybufs[slot], y_hbm.at[tile_rows(row0, rb)], ysem.at[slot]))

        @pl.when(n < rb)
        def _():
            piece = rb // 2
            while piece >= 1:
                off = n - jnp.bitwise_and(n, 2 * piece - 1)

                @pl.when(jnp.bitwise_and(n, piece) != 0)
                def _(off=off, piece=piece):
                    run(pltpu.make_async_copy(ybufs[slot].at[tile_rows(off, piece)],
                                              y_hbm.at[tile_rows(row0 + off, piece)],
                                              ysem.at[slot]))

                piece //= 2

    def fetch(e, slot):
        return (pltpu.make_async_copy(wg_hbm.at[e], wg_buf.at[slot], wsem.at[slot, 0]),
                pltpu.make_async_copy(wu_hbm.at[e], wu_buf.at[slot], wsem.at[slot, 1]),
                pltpu.make_async_copy(wd_hbm.at[e], wd_buf.at[slot], wsem.at[slot, 2]))

    ahead = EXPERT_RING - 1

    @pl.when(j == 0)
    def _():
        for item in range(ahead):
            @pl.when(valid_ref[item] > 0)
            def _(item=item):
                x_copy(item, item).start()

    ahead_item = jnp.minimum(j + ahead, n_steps - 1)
    for slot in range(EXPERT_RING):
        here = ring_pos == slot

        @pl.when(jnp.logical_and(here, j >= EXPERT_RING))
        def _(slot=slot):
            y_copies(j - EXPERT_RING, slot, start=False)

        @pl.when(jnp.logical_and(here, valid > 0))
        def _(slot=slot):
            x_copy(j, slot).wait()

            @pl.when(jnp.logical_and(j + ahead < n_steps, valid_ref[ahead_item] > 0))
            def _():
                x_copy(ahead_item, (slot + ahead) % EXPERT_RING).start()

    @pl.when(new_ref[j] != 0)
    def _():
        slot = slot_ref[j]

        @pl.when(j == 0)
        def _():
            for copy in fetch(e_ref[j], slot):
                copy.start()

        for copy in fetch(e_ref[j], slot):
            copy.wait()
        wgb[...] = wg_buf[slot].astype(BF16)
        wub[...] = wu_buf[slot].astype(BF16)
        wdb[...] = wd_buf[slot].astype(BF16)

        @pl.when(nxt_ref[j] >= 0)
        def _():
            for copy in fetch(nxt_ref[j], 1 - slot):
                copy.start()

    def swiglu_rows(slot, n):
        xb = jnp.concatenate(
            [xbufs[slot][pl.ds(s, n, stride=ROW_TILES), :] for s in range(ROW_TILES)],
            axis=1).astype(BF16)
        g = _dot(xb, wgb[...])
        u = _dot(xb, wub[...])
        y = _dot((g * _sigmoid(g) * u).astype(BF16), wdb[...])
        for s in range(ROW_TILES):
            ybufs[slot][pl.ds(s, n, stride=ROW_TILES), :] = y[:, s * LANES:(s + 1) * LANES]

    row_variants = [rb // 4, rb // 2, rb]
    for slot in range(EXPERT_RING):
        here = ring_pos == slot

        for below, n in zip([0] + row_variants[:-1], row_variants):
            @pl.when(jnp.logical_and(here, jnp.logical_and(valid > below, valid <= n)))
            def _(slot=slot, n=n):
                swiglu_rows(slot, n)

        @pl.when(jnp.logical_and(here, valid > 0))
        def _(slot=slot):
            y_copies(j, slot, start=True)

        @pl.when(jnp.logical_and(here, j == n_steps - 1))
        def _(slot=slot):
            for back in range(EXPERT_RING - 1, -1, -1):
                y_copies(j - back, (slot - back) % EXPERT_RING, start=False)


def _experts(item_e, item_row, item_valid, item_new, item_slot, item_nxt,
             xg, w_eg, w_eu, w_ed, rb):
    n_items = item_e.shape[0]
    n_rows = xg.shape[0] // ROW_TILES - rb
    hbm = pl.BlockSpec(memory_space=pl.ANY)
    row_buf = pltpu.VMEM((rb * ROW_TILES, LANES), F32)
    grid_spec = pltpu.PrefetchScalarGridSpec(
        num_scalar_prefetch=6,
        grid=(n_items,),
        in_specs=[hbm, hbm, hbm, hbm],
        out_specs=hbm,
        scratch_shapes=[row_buf] * (2 * EXPERT_RING) + [
                        pltpu.VMEM((2, D_MODEL, D_EXPERT), F32),
                        pltpu.VMEM((2, D_MODEL, D_EXPERT), F32),
                        pltpu.VMEM((2, D_EXPERT, D_MODEL), F32),
                        pltpu.VMEM((D_MODEL, D_EXPERT), BF16),
                        pltpu.VMEM((D_MODEL, D_EXPERT), BF16),
                        pltpu.VMEM((D_EXPERT, D_MODEL), BF16),
                        pltpu.SemaphoreType.DMA((2, 3)),
                        pltpu.SemaphoreType.DMA((EXPERT_RING,)),
                        pltpu.SemaphoreType.DMA((EXPERT_RING,))],
    )
    return pl.pallas_call(
        functools.partial(_experts_kernel, rb),
        grid_spec=grid_spec,
        out_shape=jax.ShapeDtypeStruct((n_rows * ROW_TILES, LANES), F32),
        compiler_params=_cparams(("arbitrary",)),
        name="experts",
    )(item_e, item_row, item_valid, item_new, item_slot, item_nxt, xg, w_eg, w_eu, w_ed)


def _combine_kernel(alpha, n_p, n_steps, dest_ref, dnext_ref, gate_ref, x1_ref, y_ref,
                    wsg_ref, wsu_ref, wsd_ref, g_ref, b_ref, x2p_ref, x2s_ref,
                    ybuf_a, ybuf_b, sem_a, sem_b):
    step = pl.program_id(0)
    tt = x1_ref.shape[0] // 2
    n_rows = TOP_K * tt * ROW_TILES

    def issue(dref, half, buf, sem):
        def body(t, carry):
            for kk in range(TOP_K):
                _row_copy(y_ref, dref[(half * tt + t) * TOP_K + kk], buf,
                          (kk * tt + t) * ROW_TILES, sem).start()
            return carry

        lax.fori_loop(0, tt, body, 0)

    def wait(buf, sem):
        pltpu.make_async_copy(y_ref.at[pl.ds(0, n_rows)], buf, sem).wait()

    def routed(half, buf):
        gates = gate_ref[half * tt:(half + 1) * tt, :]
        chunks = []
        for s in range(ROW_TILES):
            acc = jnp.zeros((tt, LANES), F32)
            for kk in range(TOP_K):
                acc = acc + gates[:, kk:kk + 1] * buf[pl.ds(kk * tt * ROW_TILES + s, tt,
                                                            stride=ROW_TILES), :]
            chunks.append(acc)
        return jnp.concatenate(chunks, axis=1)

    @pl.when(step == 0)
    def _():
        issue(dest_ref, 0, ybuf_a, sem_a)

    issue(dest_ref, 1, ybuf_b, sem_b)

    x1 = x1_ref[...]
    xb = x1.astype(BF16)
    gs = _dot(xb, wsg_ref[...])
    us = _dot(xb, wsu_ref[...])
    base = alpha * x1 + _dot((gs * _sigmoid(gs) * us).astype(BF16), wsd_ref[...])

    wait(ybuf_a, sem_a)
    x2a = _layer_norm(base[:tt, :] + routed(0, ybuf_a), g_ref[...], b_ref[...])

    @pl.when(step + 1 < n_steps)
    def _():
        issue(dnext_ref, 0, ybuf_a, sem_a)

    wait(ybuf_b, sem_b)
    x2b = _layer_norm(base[tt:, :] + routed(1, ybuf_b), g_ref[...], b_ref[...])

    @pl.when(step < n_p)
    def _():
        x2p_ref[:tt, :] = x2a
        x2p_ref[tt:, :] = x2b

    @pl.when(step >= n_p)
    def _():
        x2s_ref[:tt, :] = x2a
        x2s_ref[tt:, :] = x2b


def _combine(dest8, gates, x1, y, wsg, wsu, wsd, g, b, alpha, tt, mp):
    m = x1.shape[0]
    st = 2 * tt
    n_steps = m // st
    n_p = mp // st
    return pl.pallas_call(
        functools.partial(_combine_kernel, alpha, n_p, n_steps),
        grid=(n_steps,),
        in_specs=[pl.BlockSpec((st * TOP_K,), lambda i: (i,), memory_space=pltpu.SMEM),
                  pl.BlockSpec((st * TOP_K,), lambda i: (jnp.minimum(i + 1, n_steps - 1),),
                               memory_space=pltpu.SMEM),
                  pl.BlockSpec((st, TOP_K), lambda i: (i, 0)),
                  pl.BlockSpec((st, D_MODEL), lambda i: (i, 0)),
                  pl.BlockSpec(memory_space=pl.ANY),
                  pl.BlockSpec((D_MODEL, D_EXPERT), lambda i: (0, 0)),
                  pl.BlockSpec((D_MODEL, D_EXPERT), lambda i: (0, 0)),
                  pl.BlockSpec((D_EXPERT, D_MODEL), lambda i: (0, 0)),
                  pl.BlockSpec((1, D_MODEL), lambda i: (0, 0)),
                  pl.BlockSpec((1, D_MODEL), lambda i: (0, 0))],
        out_specs=[pl.BlockSpec((st, D_MODEL), lambda i: (jnp.minimum(i, n_p - 1), 0)),
                   pl.BlockSpec((st, D_MODEL), lambda i: (jnp.maximum(i - n_p, 0), 0))],
        out_shape=[jax.ShapeDtypeStruct((mp, D_MODEL), F32),
                   jax.ShapeDtypeStruct((m - mp, D_MODEL), F32)],
        scratch_shapes=[pltpu.VMEM((TOP_K * tt * ROW_TILES, LANES), F32),
                        pltpu.VMEM((TOP_K * tt * ROW_TILES, LANES), F32),
                        pltpu.SemaphoreType.DMA(()), pltpu.SemaphoreType.DMA(())],
        compiler_params=_cparams(("arbitrary",)),
        name="combine",
    )(dest8, dest8, gates, x1, y, wsg, wsu, wsd, g, b)


PROJ_TM = 512
PROJ_TN = 1024
MIX_TT = 256
MIX_CHUNK = 128
SAMPLE_NSEQ = 8
ROUTER_TT = 256
DISPATCH_TT = 512
EXPERT_RB = 256
EXPERT_RING = 3
COMBINE_TT = 128


def _moe(x1, x1r, mp, w_router, b_router, w_eg, w_eu, w_ed, w_sg, w_su, w_sd, ln_g, ln_b, alpha):
    m = x1.shape[0]
    wr_hi, wr_lo = _split_hi_lo(w_router)
    idx, gates, rank, cnt = _router(x1, wr_hi, wr_lo, b_router[None, :], ROUTER_TT)

    rb = EXPERT_RB
    n_rows = m * TOP_K
    n_items = n_rows // rb + N_EXPERTS
    counts = cnt[0].astype(jnp.int32)
    seg_end = jnp.cumsum(counts)
    seg_start = seg_end - counts
    n_chunk = (counts + rb - 1) // rb
    item_end = jnp.cumsum(n_chunk)
    item_start = item_end - n_chunk
    item = jnp.arange(n_items, dtype=jnp.int32)
    item_c = jnp.minimum(item, item_end[-1] - 1)
    item_e = jnp.sum((item_end[None, :] <= item_c[:, None]).astype(jnp.int32), axis=1)
    onehot_e = item_e[:, None] == jnp.arange(N_EXPERTS, dtype=jnp.int32)[None, :]

    def pick(table):
        return jnp.sum(jnp.where(onehot_e, table[None, :], 0), axis=1)

    chunk = item_c - pick(item_start)
    valid = item < item_end[-1]
    item_row = pick(seg_start) + chunk * rb
    item_valid = jnp.where(valid, jnp.clip(pick(counts) - chunk * rb, 0, rb), 0)

    experts = jnp.arange(N_EXPERTS, dtype=jnp.int32)
    nonempty = counts > 0
    slot_e = (jnp.cumsum(nonempty.astype(jnp.int32)) - 1) % 2
    later = jnp.logical_and(experts[None, :] > experts[:, None], nonempty[None, :])
    nxt_e = jnp.min(jnp.where(later, experts[None, :], N_EXPERTS), axis=1)
    nxt_e = jnp.where(nxt_e == N_EXPERTS, -1, nxt_e)
    item_new = jnp.logical_and(valid, chunk == 0).astype(jnp.int32)

    xg, dest8 = _dispatch(seg_start, idx.reshape(-1), rank.reshape(-1), x1r, DISPATCH_TT, rb)
    y = _experts(item_e, item_row, item_valid, item_new, pick(slot_e), pick(nxt_e),
                 xg, w_eg, w_eu, w_ed, rb)
    return _combine(dest8, gates, x1, y, w_sg.astype(BF16), w_su.astype(BF16),
                    w_sd.astype(BF16), ln_g[None, :], ln_b[None, :], alpha, COMBINE_TT, mp)


def _layer(xp, xs, conv_s, c_s, n_s, m_s, w_in, b_gates, w_conv, mh_gain, w_out, ln1_g, ln1_b,
           w_router, b_router, w_eg, w_eu, w_ed, w_sg, w_su, w_sd, ln2_g, ln2_b, alpha):
    bp, tp, _ = xp.shape
    bs, ts, _ = xs.shape
    mp, ms = bp * tp, bs * ts
    xp2 = xp.reshape(mp, D_MODEL)
    xs2 = xs.reshape(ms, D_MODEL)

    w_main = jnp.concatenate([w_in[:, :GATE_COL0], w_in[:, GATE_COL0 + 2 * N_HEADS:]],
                             axis=1).astype(BF16)
    w_gate = jnp.pad(w_in[:, GATE_COL0:GATE_COL0 + 2 * N_HEADS],
                     ((0, 0), (0, LANES - 2 * N_HEADS)))
    wg_hi, wg_lo = _split_hi_lo(w_gate)

    z_s = _proj(xs2, w_main, PROJ_TM, PROJ_TN)
    zg_s = _gate_proj(xs2, wg_hi, wg_lo, PROJ_TM)
    gt_s = zg_s[:, :SUBLANES].reshape(bs, ts, SUBLANES).transpose(0, 2, 1)
    mix_s, conv_n, c_n, n_n, m_n = _mixer(
        z_s, zg_s, gt_s, b_gates, w_conv, mh_gain[None, :],
        conv_s, c_s, n_s, m_s[:, None, :], nb=bs, ts=ts, nseq=SAMPLE_NSEQ)

    x1, x1r, conv_p, c_p, n_p, m_p = _prompt_layer_half(
        xp2, xs2, mix_s, b_gates, w_main, wg_hi, wg_lo, w_conv, mh_gain[None, :],
        w_out.astype(BF16), ln1_g[None, :], ln1_b[None, :], alpha,
        nb=bp, nt=tp // MIX_TT, tt=MIX_TT, chunk=MIX_CHUNK)

    x2p, x2s = _moe(x1, x1r, mp, w_router, b_router, w_eg, w_eu, w_ed, w_sg, w_su, w_sd,
                    ln2_g, ln2_b, alpha)
    states_p = (conv_p, c_p, n_p, m_p[:, 0, :])
    states_s = (conv_n, c_n, n_n, m_n[:, 0, :])
    return x2p.reshape(bp, tp, D_MODEL), x2s.reshape(bs, ts, D_MODEL), states_p, states_s


def kernel(x_prompt, x_sample, cache_conv, state_mlstm_C, state_mlstm_n, state_mlstm_m, w_in, b_gates, w_conv, mh_gain, w_out, ln1_g, ln1_b, w_router, b_router, w_exp_gate, w_exp_up, w_exp_down, w_sh_gate, w_sh_up, w_sh_down, ln2_g, ln2_b):
    depth = w_in.shape[0]
    alpha = (2.0 * depth) ** 0.25
    hp, hs = x_prompt, x_sample
    outs_p = [[], [], [], []]
    outs_s = [[], [], [], []]
    for l in range(depth):
        hp, hs, st_p, st_s = _layer(
            hp, hs, cache_conv[l], state_mlstm_C[l], state_mlstm_n[l], state_mlstm_m[l],
            w_in[l], b_gates[l], w_conv[l], mh_gain[l], w_out[l], ln1_g[l], ln1_b[l],
            w_router[l], b_router[l], w_exp_gate[l], w_exp_up[l], w_exp_down[l],
            w_sh_gate[l], w_sh_up[l], w_sh_down[l], ln2_g[l], ln2_b[l], alpha)
        for acc, val in zip(outs_p, st_p):
            acc.append(val)
        for acc, val in zip(outs_s, st_s):
            acc.append(val)
    return (hp, hs) + tuple(jnp.stack(a) for a in outs_p) + tuple(jnp.stack(a) for a in outs_s)
```

```python
import functools

import jax
import jax.numpy as jnp
from jax import lax
from jax.experimental import pallas as pl
from jax.experimental.pallas import tpu as pltpu

F32 = jnp.float32
BF16 = jnp.bfloat16

D_MODEL = 1024
N_HEADS = 4
HEAD_V = 256
HEAD_QK = 128
N_EXPERTS = 256
TOP_K = 8
D_EXPERT = 256
ROUTED_SCALE = 2.5
LN_EPS = 1e-5
CONV_K = 3

LANES = 128
SUBLANES = 8
ROW_TILES = D_MODEL // LANES
GATE_COL0 = 6 * D_MODEL

VMEM_LIMIT = 56 * 1024 * 1024


def _cparams(sem):
    return pltpu.CompilerParams(dimension_semantics=sem, vmem_limit_bytes=VMEM_LIMIT)


def _sigmoid(x):
    return 1.0 / (1.0 + jnp.exp(-x))


def _log_sigmoid(x):
    return jnp.minimum(x, 0.0) - jnp.log(1.0 + jnp.exp(-jnp.abs(x)))


def _layer_norm(x, g, b):
    mu = jnp.mean(x, axis=-1, keepdims=True)
    xc = x - mu
    var = jnp.mean(xc * xc, axis=-1, keepdims=True)
    return xc * lax.rsqrt(var + LN_EPS) * g + b


def _split_hi_lo(x):
    hi = x.astype(BF16)
    lo = (x - hi.astype(F32)).astype(BF16)
    return hi, lo


def _dot(a, b):
    return jnp.dot(a, b, preferred_element_type=F32)


def _dot3(x, w_hi, w_lo):
    x_hi, x_lo = _split_hi_lo(x)
    return _dot(x_hi, w_hi) + _dot(x_lo, w_hi) + _dot(x_hi, w_lo)


def _proj_kernel(x_ref, w_ref, z_ref):
    z_ref[...] = _dot(x_ref[...].astype(BF16), w_ref[...])


def _proj(x, w, tm, tn):
    m, k = x.shape
    n = w.shape[1]
    return pl.pallas_call(
        _proj_kernel,
        grid=(m // tm, n // tn),
        in_specs=[pl.BlockSpec((tm, k), lambda i, j: (i, 0)),
                  pl.BlockSpec((k, tn), lambda i, j: (0, j))],
        out_specs=pl.BlockSpec((tm, tn), lambda i, j: (i, j)),
        out_shape=jax.ShapeDtypeStruct((m, n), F32),
        compiler_params=_cparams(("parallel", "arbitrary")),
        name="proj",
    )(x, w)


def _gate_proj_kernel(x_ref, wh_ref, wl_ref, z_ref):
    z_ref[...] = _dot3(x_ref[...], wh_ref[...], wl_ref[...])


def _gate_proj(x, w_hi, w_lo, tm):
    m, k = x.shape
    n = w_hi.shape[1]
    return pl.pallas_call(
        _gate_proj_kernel,
        grid=(m // tm,),
        in_specs=[pl.BlockSpec((tm, k), lambda i: (i, 0)),
                  pl.BlockSpec((k, n), lambda i: (0, 0)),
                  pl.BlockSpec((k, n), lambda i: (0, 0))],
        out_specs=pl.BlockSpec((tm, n), lambda i: (i, 0)),
        out_shape=jax.ShapeDtypeStruct((m, n), F32),
        compiler_params=_cparams(("parallel",)),
        name="gate_proj",
    )(x, w_hi, w_lo)


def _conv_branch(u, zcb, wc, conv_ref, seq=0):
    tt = u.shape[0]
    carry = conv_ref[seq]
    rows = lax.broadcasted_iota(jnp.int32, (tt, D_MODEL), 0)
    u1 = jnp.where(rows == 0, carry[1:2, :], pltpu.roll(u, 1, 0))
    u2 = jnp.where(rows == 0, carry[0:1, :],
                   jnp.where(rows == 1, carry[1:2, :], pltpu.roll(u, 2, 0)))
    conv_out = u2 * wc[0:1, :] + u1 * wc[1:2, :] + u * wc[2:3, :]
    conv_ref[seq] = u[tt - 2:tt, :]
    return zcb * conv_out


def _mlstm_chunk(q, k, v, ig_col, lf_col, ig_row, lf_row, c_state, n_row, m_prev):
    chunk = q.shape[0]
    ti = lax.broadcasted_iota(jnp.int32, (chunk, chunk), 0)
    si = lax.broadcasted_iota(jnp.int32, (chunk, chunk), 1)
    causal = si <= ti
    b_col = jnp.sum(jnp.where(causal, lf_row, 0.0), axis=1, keepdims=True)
    b_row = jnp.sum(jnp.where(ti <= si, lf_col, 0.0), axis=0, keepdims=True)
    inter = b_col + m_prev
    dmat = jnp.where(causal, b_col - b_row + ig_row, -jnp.inf)
    m_t = jnp.maximum(inter, jnp.max(dmat, axis=1, keepdims=True))
    w_intra = jnp.exp(dmat - m_t)
    w_inter = jnp.exp(inter - m_t)
    qb = q.astype(BF16)
    kb = k.astype(BF16)
    vb = v.astype(BF16)
    s = lax.dot_general(qb, kb, (((1,), (1,)), ((), ())), preferred_element_type=F32) * w_intra
    num = w_inter * _dot(qb, c_state.astype(BF16)) + _dot(s.astype(BF16), vb)
    den = (w_inter * jnp.sum(q * n_row, axis=1, keepdims=True)
           + jnp.sum(s, axis=1, keepdims=True))
    h = num / jnp.maximum(jnp.abs(den), jnp.exp(-m_t))
    m_new = m_t[chunk - 1:chunk, :]
    b_last = b_col[chunk - 1:chunk, :]
    ws_col = jnp.exp(b_last - b_col + ig_col - m_new)
    cdecay = jnp.exp(inter[chunk - 1:chunk, :] - m_new)
    kw = k * ws_col
    c_new = cdecay * c_state + lax.dot_general(
        kw.astype(BF16), vb, (((0,), (0,)), ((), ())), preferred_element_type=F32)
    n_new = cdecay * n_row + jnp.sum(kw, axis=0, keepdims=True)
    return h, c_new, n_new, m_new


def _mlstm_branch(chunk, bg_ref, zg, gt, zqk, zv, zog, gain_ref, c_ref, n_ref, m_ref, mo_ref,
                  seq=0, row0=0):
    tt = zqk.shape[0]
    for h in range(N_HEADS):
        ig_col_all = zg[:, h:h + 1] + bg_ref[h]
        lf_col_all = _log_sigmoid(zg[:, N_HEADS + h:N_HEADS + h + 1] + bg_ref[N_HEADS + h])
        ig_row_all = gt[h:h + 1, :] + bg_ref[h]
        lf_row_all = _log_sigmoid(gt[N_HEADS + h:N_HEADS + h + 1, :] + bg_ref[N_HEADS + h])
        vcols = slice(h * HEAD_V, (h + 1) * HEAD_V)
        for c in range(tt // chunk):
            r0, r1 = c * chunk, (c + 1) * chunk
            q = zqk[r0:r1, h * HEAD_QK:(h + 1) * HEAD_QK]
            k = zqk[r0:r1, (N_HEADS + h) * HEAD_QK:(N_HEADS + h + 1) * HEAD_QK] * (HEAD_QK ** -0.5)
            hh, c_new, n_new, m_new = _mlstm_chunk(
                q, k, zv[r0:r1, vcols], ig_col_all[r0:r1, :], lf_col_all[r0:r1, :],
                ig_row_all[:, r0:r1], lf_row_all[:, r0:r1],
                c_ref[seq, h], n_ref[seq, h:h + 1, :], m_ref[seq, :, h:h + 1])
            c_ref[seq, h] = c_new
            n_ref[seq, h:h + 1, :] = n_new
            m_ref[seq, :, h:h + 1] = m_new
            mu = jnp.mean(hh, axis=-1, keepdims=True)
            hc = hh - mu
            var = jnp.mean(hc * hc, axis=-1, keepdims=True)
            hn = hc * lax.rsqrt(var + LN_EPS) * gain_ref[:, vcols]
            mo_ref[row0 + r0:row0 + r1, vcols] = hn * _sigmoid(zog[r0:r1, vcols])


def _store_x1(x, mix, wout_ref, g_ref, b_ref, alpha, x1_ref, x1r_ref):
    tt = x.shape[0]
    x1 = _layer_norm(alpha * x + _dot(mix.astype(BF16), wout_ref[...]), g_ref[...], b_ref[...])
    x1_ref[...] = x1
    for s in range(ROW_TILES):
        x1r_ref[pl.ds(s, tt, stride=ROW_TILES), :] = x1[:, s * LANES:(s + 1) * LANES]


def _mixer_kernel(ts, bg_ref,
                  zcb_ref, zcc_ref, zch_ref, zqk_ref, zv_ref, zog_ref, zga_ref, zgb_ref,
                  zg_ref, gt_ref, wconv_ref, gain_ref,
                  conv0_ref, c0_ref, n0_ref, m0_ref,
                  mix_ref, conv_ref, c_ref, n_ref, m_ref, mo_ref):
    conv_ref[...] = conv0_ref[...]
    c_ref[...] = c0_ref[...]
    n_ref[...] = n0_ref[...]
    m_ref[...] = m0_ref[...]
    for seq in range(conv_ref.shape[0]):
        rows = slice(seq * ts, (seq + 1) * ts)
        a = _conv_branch(zcc_ref[rows, :] * zch_ref[rows, :], zcb_ref[rows, :], wconv_ref[...],
                         conv_ref, seq)
        _mlstm_branch(ts, bg_ref, zg_ref[rows, :], gt_ref[seq], zqk_ref[rows, :], zv_ref[rows, :],
                      zog_ref[rows, :], gain_ref, c_ref, n_ref, m_ref, mo_ref, seq, seq * ts)
        mix_ref[rows, :] = (_sigmoid(zga_ref[rows, :]) * a
                            + _sigmoid(zgb_ref[rows, :]) * mo_ref[rows, :])


def _mixer(z, zg, gt3, b_gates, w_conv, mh_gain, conv0, c0, n0, m0, *, nb, ts, nseq):
    tt = nseq * ts

    def zspec(j):
        return pl.BlockSpec((tt, D_MODEL), lambda b, j=j: (b, j))

    in_specs = [pl.BlockSpec(memory_space=pltpu.SMEM)]
    in_specs += [zspec(j) for j in range(8)]
    state_specs = [
        pl.BlockSpec((nseq, CONV_K - 1, D_MODEL), lambda b: (b, 0, 0)),
        pl.BlockSpec((nseq, N_HEADS, HEAD_QK, HEAD_V), lambda b: (b, 0, 0, 0)),
        pl.BlockSpec((nseq, N_HEADS, HEAD_QK), lambda b: (b, 0, 0)),
        pl.BlockSpec((nseq, 1, N_HEADS), lambda b: (b, 0, 0)),
    ]
    in_specs += [
        pl.BlockSpec((tt, LANES), lambda b: (b, 0)),
        pl.BlockSpec((nseq, SUBLANES, ts), lambda b: (b, 0, 0)),
        pl.BlockSpec((CONV_K, D_MODEL), lambda b: (0, 0)),
        pl.BlockSpec((1, D_MODEL), lambda b: (0, 0)),
    ] + state_specs
    out_specs = [pl.BlockSpec((tt, D_MODEL), lambda b: (b, 0))] + state_specs
    out_shape = [
        jax.ShapeDtypeStruct((nb * ts, D_MODEL), F32),
        jax.ShapeDtypeStruct((nb, CONV_K - 1, D_MODEL), F32),
        jax.ShapeDtypeStruct((nb, N_HEADS, HEAD_QK, HEAD_V), F32),
        jax.ShapeDtypeStruct((nb, N_HEADS, HEAD_QK), F32),
        jax.ShapeDtypeStruct((nb, 1, N_HEADS), F32),
    ]
    return pl.pallas_call(
        functools.partial(_mixer_kernel, ts),
        grid=(nb // nseq,),
        in_specs=in_specs,
        out_specs=out_specs,
        out_shape=out_shape,
        scratch_shapes=[pltpu.VMEM((tt, D_MODEL), F32)],
        compiler_params=_cparams(("parallel",)),
        name="mixer_sample",
    )(b_gates, z, z, z, z, z, z, z, z, zg, gt3, w_conv, mh_gain, conv0, c0, n0, m0)


def _prompt_kernel(chunk, nb, ns, alpha, bg_ref, x_ref, w_ref, wgh_ref, wgl_ref, wconv_ref,
                   gain_ref, wout_ref, g_ref, b_ref, xs_ref, mixs_ref,
                   x1_ref, x1r_ref, conv_ref, c_ref, n_ref, m_ref, mo_ref):
    b = pl.program_id(0)
    t = pl.program_id(1)

    @pl.when(b < nb)
    def _():
        @pl.when(t == 0)
        def _():
            conv_ref[...] = jnp.zeros_like(conv_ref)
            c_ref[...] = jnp.zeros_like(c_ref)
            n_ref[...] = jnp.zeros_like(n_ref)
            m_ref[...] = jnp.zeros_like(m_ref)

        x = x_ref[...]
        xb = x.astype(BF16)

        def z(j):
            return _dot(xb, w_ref[:, j * D_MODEL:(j + 1) * D_MODEL])

        a = _conv_branch(z(1) * z(2), z(0), wconv_ref[...], conv_ref)
        zg = _dot3(x, wgh_ref[...], wgl_ref[...])
        _mlstm_branch(chunk, bg_ref, zg, zg.T, z(3), z(4), z(5), gain_ref,
                      c_ref, n_ref, m_ref, mo_ref)
        mix = _sigmoid(z(6)) * a + _sigmoid(z(7)) * mo_ref[...]
        _store_x1(x, mix, wout_ref, g_ref, b_ref, alpha, x1_ref, x1r_ref)

    @pl.when(jnp.logical_and(b == nb, t < ns))
    def _():
        _store_x1(xs_ref[...], mixs_ref[...], wout_ref, g_ref, b_ref, alpha, x1_ref, x1r_ref)


def _prompt_layer_half(xp, xs, mix_s, b_gates, w_main, wg_hi, wg_lo, w_conv, mh_gain, w_out,
                       ln_g, ln_b, alpha, *, nb, nt, tt, chunk):
    mp, ms = xp.shape[0], xs.shape[0]
    ns = ms // tt
    last_p = nb * nt - 1

    def prow(b, t):
        return jnp.minimum(b * nt + t, last_p)

    def srow(b, t):
        return jnp.where(b == nb, jnp.minimum(t, ns - 1), 0)

    def orow(b, t):
        return jnp.where(b == nb, nb * nt + jnp.minimum(t, ns - 1), b * nt + t)

    def state(b):
        return jnp.minimum(b, nb - 1)

    const2 = lambda b, t: (0, 0)
    in_specs = [
        pl.BlockSpec(memory_space=pltpu.SMEM),
        pl.BlockSpec((tt, D_MODEL), lambda b, t: (prow(b, t), 0)),
        pl.BlockSpec(w_main.shape, const2, pipeline_mode=pl.Buffered(1)),
        pl.BlockSpec(wg_hi.shape, const2),
        pl.BlockSpec(wg_lo.shape, const2),
        pl.BlockSpec((CONV_K, D_MODEL), const2),
        pl.BlockSpec((1, D_MODEL), const2),
        pl.BlockSpec((D_MODEL, D_MODEL), const2),
        pl.BlockSpec((1, D_MODEL), const2),
        pl.BlockSpec((1, D_MODEL), const2),
        pl.BlockSpec((tt, D_MODEL), lambda b, t: (srow(b, t), 0)),
        pl.BlockSpec((tt, D_MODEL), lambda b, t: (srow(b, t), 0)),
    ]
    out_specs = [
        pl.BlockSpec((tt, D_MODEL), lambda b, t: (orow(b, t), 0)),
        pl.BlockSpec((tt * ROW_TILES, LANES), lambda b, t: (orow(b, t), 0)),
        pl.BlockSpec((1, CONV_K - 1, D_MODEL), lambda b, t: (state(b), 0, 0)),
        pl.BlockSpec((1, N_HEADS, HEAD_QK, HEAD_V), lambda b, t: (state(b), 0, 0, 0)),
        pl.BlockSpec((1, N_HEADS, HEAD_QK), lambda b, t: (state(b), 0, 0)),
        pl.BlockSpec((1, 1, N_HEADS), lambda b, t: (state(b), 0, 0)),
    ]
    out_shape = [
        jax.ShapeDtypeStruct((mp + ms, D_MODEL), F32),
        jax.ShapeDtypeStruct(((mp + ms) * ROW_TILES, LANES), F32),
        jax.ShapeDtypeStruct((nb, CONV_K - 1, D_MODEL), F32),
        jax.ShapeDtypeStruct((nb, N_HEADS, HEAD_QK, HEAD_V), F32),
        jax.ShapeDtypeStruct((nb, N_HEADS, HEAD_QK), F32),
        jax.ShapeDtypeStruct((nb, 1, N_HEADS), F32),
    ]
    return pl.pallas_call(
        functools.partial(_prompt_kernel, chunk, nb, ns, alpha),
        grid=(nb + 1, nt),
        in_specs=in_specs,
        out_specs=out_specs,
        out_shape=out_shape,
        scratch_shapes=[pltpu.VMEM((tt, D_MODEL), F32)],
        compiler_params=_cparams(("arbitrary", "arbitrary")),
        name="prompt_half",
    )(b_gates, xp, w_main, wg_hi, wg_lo, w_conv, mh_gain, w_out, ln_g, ln_b, xs, mix_s)


def _router_kernel(x_ref, wh_ref, wl_ref, br_ref, idx_ref, gate_ref, rank_ref, cnt_ref):
    tt = x_ref.shape[0]

    @pl.when(pl.program_id(0) == 0)
    def _():
        cnt_ref[...] = jnp.zeros_like(cnt_ref)

    scores = _sigmoid(_dot3(x_ref[...], wh_ref[...], wl_ref[...]))
    lane = lax.broadcasted_iota(jnp.int32, (tt, N_EXPERTS), 1)
    work = scores + br_ref[...]
    picked = []
    sel = []
    member = jnp.zeros((tt, N_EXPERTS), F32)
    for _ in range(TOP_K):
        mx = jnp.max(work, axis=1, keepdims=True)
        ik = jnp.min(jnp.where(work == mx, lane, N_EXPERTS), axis=1, keepdims=True)
        onehot = lane == ik
        picked.append((ik, onehot))
        sel.append(jnp.sum(jnp.where(onehot, scores, 0.0), axis=1, keepdims=True))
        work = jnp.where(onehot, -jnp.inf, work)
        member = member + onehot.astype(F32)
    total = sel[0]
    for sk in sel[1:]:
        total = total + sk

    ti = lax.broadcasted_iota(jnp.int32, (tt, tt), 0)
    si = lax.broadcasted_iota(jnp.int32, (tt, tt), 1)
    earlier = (si < ti).astype(BF16)
    before = _dot(earlier, member.astype(BF16)) + cnt_ref[...]

    lane_o = lax.broadcasted_iota(jnp.int32, (tt, TOP_K), 1)
    idx_o = jnp.zeros((tt, TOP_K), jnp.int32)
    gate_o = jnp.zeros((tt, TOP_K), F32)
    rank_o = jnp.zeros((tt, TOP_K), jnp.int32)
    for kk in range(TOP_K):
        ik, onehot = picked[kk]
        rk = jnp.sum(jnp.where(onehot, before, 0.0), axis=1, keepdims=True)
        idx_o = jnp.where(lane_o == kk, ik, idx_o)
        gate_o = jnp.where(lane_o == kk, sel[kk] / total * ROUTED_SCALE, gate_o)
        rank_o = jnp.where(lane_o == kk, rk.astype(jnp.int32), rank_o)
    idx_ref[...] = idx_o
    gate_ref[...] = gate_o
    rank_ref[...] = rank_o
    cnt_ref[...] = cnt_ref[...] + jnp.sum(member, axis=0, keepdims=True)


def _router(x1, wr_hi, wr_lo, b_router, tt):
    m = x1.shape[0]
    return pl.pallas_call(
        _router_kernel,
        grid=(m // tt,),
        in_specs=[pl.BlockSpec((tt, D_MODEL), lambda i: (i, 0)),
                  pl.BlockSpec((D_MODEL, N_EXPERTS), lambda i: (0, 0)),
                  pl.BlockSpec((D_MODEL, N_EXPERTS), lambda i: (0, 0)),
                  pl.BlockSpec((1, N_EXPERTS), lambda i: (0, 0))],
        out_specs=[pl.BlockSpec((tt, TOP_K), lambda i: (i, 0)),
                   pl.BlockSpec((tt, TOP_K), lambda i: (i, 0)),
                   pl.BlockSpec((tt, TOP_K), lambda i: (i, 0)),
                   pl.BlockSpec((1, N_EXPERTS), lambda i: (0, 0))],
        out_shape=[jax.ShapeDtypeStruct((m, TOP_K), jnp.int32),
                   jax.ShapeDtypeStruct((m, TOP_K), F32),
                   jax.ShapeDtypeStruct((m, TOP_K), jnp.int32),
                   jax.ShapeDtypeStruct((1, N_EXPERTS), F32)],
        compiler_params=_cparams(("arbitrary",)),
        name="router",
    )(x1, wr_hi, wr_lo, b_router)


def _row_copy(src, src_row8, dst, dst_row8, sem):
    return pltpu.make_async_copy(src.at[pl.ds(pl.multiple_of(src_row8, SUBLANES), SUBLANES)],
                                 dst.at[pl.ds(pl.multiple_of(dst_row8, SUBLANES), SUBLANES)],
                                 sem)


def _dispatch_kernel(tt, start_ref, idx_ref, rank_ref, x1r_ref, xg_ref, dest_ref, zbuf, sem, zsem):
    @pl.when(pl.program_id(0) == 0)
    def _():
        zbuf[...] = jnp.zeros_like(zbuf)
        n_sorted = xg_ref.shape[0] - zbuf.shape[0]
        fill = pltpu.make_async_copy(zbuf, xg_ref.at[pl.ds(n_sorted, zbuf.shape[0])], zsem)
        fill.start()
        fill.wait()

    def issue(t, carry):
        for kk in range(TOP_K):
            j = t * TOP_K + kk
            dest8 = (start_ref[idx_ref[j]] + rank_ref[j]) * ROW_TILES
            dest_ref[j] = dest8
            _row_copy(x1r_ref, t * ROW_TILES, xg_ref, dest8, sem).start(priority=kk % 2)
        return carry

    lax.fori_loop(0, tt, issue, 0)
    n_rows = tt * TOP_K * ROW_TILES
    pltpu.make_async_copy(xg_ref.at[pl.ds(0, n_rows)], xg_ref.at[pl.ds(0, n_rows)], sem).wait()


def _dispatch(seg_start, idx, rank, x1r, tt, slack):
    m = x1r.shape[0] // ROW_TILES
    smem_blk = pl.BlockSpec((tt * TOP_K,), lambda i: (i,), memory_space=pltpu.SMEM)
    return pl.pallas_call(
        functools.partial(_dispatch_kernel, tt),
        grid=(m // tt,),
        in_specs=[pl.BlockSpec(memory_space=pltpu.SMEM), smem_blk, smem_blk,
                  pl.BlockSpec((tt * ROW_TILES, LANES), lambda i: (i, 0))],
        out_specs=[pl.BlockSpec(memory_space=pl.ANY), smem_blk],
        out_shape=[jax.ShapeDtypeStruct(((m * TOP_K + slack) * ROW_TILES, LANES), F32),
                   jax.ShapeDtypeStruct((m * TOP_K,), jnp.int32)],
        scratch_shapes=[pltpu.VMEM((slack * ROW_TILES, LANES), F32),
                        pltpu.SemaphoreType.DMA(()), pltpu.SemaphoreType.DMA(())],
        compiler_params=_cparams(("arbitrary",)),
        name="dispatch",
    )(seg_start, idx, rank, x1r)


def _experts_kernel(rb, e_ref, row_ref, valid_ref, new_ref, slot_ref, nxt_ref,
                    xg_hbm, wg_hbm, wu_hbm, wd_hbm, y_hbm,
                    *scratch):
    j = pl.program_id(0)
    n_steps = pl.num_programs(0)
    ring_pos = lax.rem(j, EXPERT_RING)
    valid = valid_ref[j]
    xbufs = scratch[:EXPERT_RING]
    ybufs = scratch[EXPERT_RING:2 * EXPERT_RING]
    wg_buf, wu_buf, wd_buf, wgb, wub, wdb, wsem, xsem, ysem = scratch[2 * EXPERT_RING:]

    def tile_rows(row, n):
        return pl.ds(pl.multiple_of(row * ROW_TILES, SUBLANES), n * ROW_TILES)

    def x_copy(item, slot):
        return pltpu.make_async_copy(xg_hbm.at[tile_rows(row_ref[item], rb)], xbufs[slot],
                                     xsem.at[slot])

    def y_copies(item, slot, start):
        row0 = row_ref[item]
        n = valid_ref[item]

        def run(copy):
            if start:
                copy.start()
            else:
                copy.wait()

        @pl.when(n == rb)
        def _():
            run(pltpu.make_async_copy(ybufs[slot], y_hbm.at[tile_rows(row0, rb)], ysem.at[slot]))

        @pl.when(n < rb)
        def _():
            piece = rb // 2
            while piece >= 1:
                off = n - jnp.bitwise_and(n, 2 * piece - 1)

                @pl.when(jnp.bitwise_and(n, piece) != 0)
                def _(off=off, piece=piece):
                    run(pltpu.make_async_copy(ybufs[slot].at[tile_rows(off, piece)],
                                              y_hbm.at[tile_rows(row0 + off, piece)],
                                              ysem.at[slot]))

                piece //= 2

    def fetch(e, slot):
        return (pltpu.make_async_copy(wg_hbm.at[e], wg_buf.at[slot], wsem.at[slot, 0]),
                pltpu.make_async_copy(wu_hbm.at[e], wu_buf.at[slot], wsem.at[slot, 1]),
                pltpu.make_async_copy(wd_hbm.at[e], wd_buf.at[slot], wsem.at[slot, 2]))

    ahead = EXPERT_RING - 1

    @pl.when(j == 0)
    def _():
        for item in range(ahead):
            @pl.when(valid_ref[item] > 0)
            def _(item=item):
                x_copy(item, item).start()

    ahead_item = jnp.minimum(j + ahead, n_steps - 1)
    for slot in range(EXPERT_RING):
        here = ring_pos == slot

        @pl.when(jnp.logical_and(here, j >= EXPERT_RING))
        def _(slot=slot):
            y_copies(j - EXPERT_RING, slot, start=False)

        @pl.when(jnp.logical_and(here, valid > 0))
        def _(slot=slot):
            x_copy(j, slot).wait()

            @pl.when(jnp.logical_and(j + ahead < n_steps, valid_ref[ahead_item] > 0))
            def _():
                x_copy(ahead_item, (slot + ahead) % EXPERT_RING).start()

    @pl.when(new_ref[j] != 0)
    def _():
        slot = slot_ref[j]

        @pl.when(j == 0)
        def _():
            for copy in fetch(e_ref[j], slot):
                copy.start()

        for copy in fetch(e_ref[j], slot):
            copy.wait()
        wgb[...] = wg_buf[slot].astype(BF16)
        wub[...] = wu_buf[slot].astype(BF16)
        wdb[...] = wd_buf[slot].astype(BF16)

        @pl.when(nxt_ref[j] >= 0)
        def _():
            for copy in fetch(nxt_ref[j], 1 - slot):
                copy.start()

    def swiglu_rows(slot, n):
        xb = jnp.concatenate(
            [xbufs[slot][pl.ds(s, n, stride=ROW_TILES), :] for s in range(ROW_TILES)],
            axis=1).astype(BF16)
        g = _dot(xb, wgb[...])
        u = _dot(xb, wub[...])
        y = _dot((g * _sigmoid(g) * u).astype(BF16), wdb[...])
        for s in range(ROW_TILES):
            ybufs[slot][pl.ds(s, n, stride=ROW_TILES), :] = y[:, s * LANES:(s + 1) * LANES]

    row_variants = [rb // 4, rb // 2, rb]
    for slot in range(EXPERT_RING):
        here = ring_pos == slot

        for below, n in zip([0] + row_variants[:-1], row_variants):
            @pl.when(jnp.logical_and(here, jnp.logical_and(valid > below, valid <= n)))
            def _(slot=slot, n=n):
                swiglu_rows(slot, n)

        @pl.when(jnp.logical_and(here, valid > 0))
        def _(slot=slot):
            y_copies(j, slot, start=True)

        @pl.when(jnp.logical_and(here, j == n_steps - 1))
        def _(slot=slot):
            for back in range(EXPERT_RING - 1, -1, -1):
                y_copies(j - back, (slot - back) % EXPERT_RING, start=False)


def _experts(item_e, item_row, item_valid, item_new, item_slot, item_nxt,
             xg, w_eg, w_eu, w_ed, rb):
    n_items = item_e.shape[0]
    n_rows = xg.shape[0] // ROW_TILES - rb
    hbm = pl.BlockSpec(memory_space=pl.ANY)
    row_buf = pltpu.VMEM((rb * ROW_TILES, LANES), F32)
    grid_spec = pltpu.PrefetchScalarGridSpec(
        num_scalar_prefetch=6,
        grid=(n_items,),
        in_specs=[hbm, hbm, hbm, hbm],
        out_specs=hbm,
        scratch_shapes=[row_buf] * (2 * EXPERT_RING) + [
                        pltpu.VMEM((2, D_MODEL, D_EXPERT), F32),
                        pltpu.VMEM((2, D_MODEL, D_EXPERT), F32),
                        pltpu.VMEM((2, D_EXPERT, D_MODEL), F32),
                        pltpu.VMEM((D_MODEL, D_EXPERT), BF16),
                        pltpu.VMEM((D_MODEL, D_EXPERT), BF16),
                        pltpu.VMEM((D_EXPERT, D_MODEL), BF16),
                        pltpu.SemaphoreType.DMA((2, 3)),
                        pltpu.SemaphoreType.DMA((EXPERT_RING,)),
                        pltpu.SemaphoreType.DMA((EXPERT_RING,))],
    )
    return pl.pallas_call(
        functools.partial(_experts_kernel, rb),
        grid_spec=grid_spec,
        out_shape=jax.ShapeDtypeStruct((n_rows * ROW_TILES, LANES), F32),
        compiler_params=_cparams(("arbitrary",)),
        name="experts",
    )(item_e, item_row, item_valid, item_new, item_slot, item_nxt, xg, w_eg, w_eu, w_ed)


def _combine_kernel(alpha, n_p, n_steps, dest_ref, dnext_ref, gate_ref, x1_ref, y_ref,
                    wsg_ref, wsu_ref, wsd_ref, g_ref, b_ref, x2p_ref, x2s_ref,
                    ybuf_a, ybuf_b, sem_a, sem_b):
    step = pl.program_id(0)
    tt = x1_ref.shape[0] // 2
    n_rows = TOP_K * tt * ROW_TILES

    def issue(dref, half, buf, sem):
        def body(t, carry):
            for kk in range(TOP_K):
                _row_copy(y_ref, dref[(half * tt + t) * TOP_K + kk], buf,
                          (kk * tt + t) * ROW_TILES, sem).start(priority=kk % 2)
            return carry

        lax.fori_loop(0, tt, body, 0)

    def wait(buf, sem):
        pltpu.make_async_copy(y_ref.at[pl.ds(0, n_rows)], buf, sem).wait()

    def routed(half, buf):
        gates = gate_ref[half * tt:(half + 1) * tt, :]
        chunks = []
        for s in range(ROW_TILES):
            acc = jnp.zeros((tt, LANES), F32)
            for kk in range(TOP_K):
                acc = acc + gates[:, kk:kk + 1] * buf[pl.ds(kk * tt * ROW_TILES + s, tt,
                                                            stride=ROW_TILES), :]
            chunks.append(acc)
        return jnp.concatenate(chunks, axis=1)

    @pl.when(step == 0)
    def _():
        issue(dest_ref, 0, ybuf_a, sem_a)

    issue(dest_ref, 1, ybuf_b, sem_b)

    x1 = x1_ref[...]
    xb = x1.astype(BF16)
    gs = _dot(xb, wsg_ref[...])
    us = _dot(xb, wsu_ref[...])
    base = alpha * x1 + _dot((gs * _sigmoid(gs) * us).astype(BF16), wsd_ref[...])

    wait(ybuf_a, sem_a)
    x2a = _layer_norm(base[:tt, :] + routed(0, ybuf_a), g_ref[...], b_ref[...])

    @pl.when(step + 1 < n_steps)
    def _():
        issue(dnext_ref, 0, ybuf_a, sem_a)

    wait(ybuf_b, sem_b)
    x2b = _layer_norm(base[tt:, :] + routed(1, ybuf_b), g_ref[...], b_ref[...])

    @pl.when(step < n_p)
    def _():
        x2p_ref[:tt, :] = x2a
        x2p_ref[tt:, :] = x2b

    @pl.when(step >= n_p)
    def _():
        x2s_ref[:tt, :] = x2a
        x2s_ref[tt:, :] = x2b


def _combine(dest8, gates, x1, y, wsg, wsu, wsd, g, b, alpha, tt, mp):
    m = x1.shape[0]
    st = 2 * tt
    n_steps = m // st
    n_p = mp // st
    return pl.pallas_call(
        functools.partial(_combine_kernel, alpha, n_p, n_steps),
        grid=(n_steps,),
        in_specs=[pl.BlockSpec((st * TOP_K,), lambda i: (i,), memory_space=pltpu.SMEM),
                  pl.BlockSpec((st * TOP_K,), lambda i: (jnp.minimum(i + 1, n_steps - 1),),
                               memory_space=pltpu.SMEM),
                  pl.BlockSpec((st, TOP_K), lambda i: (i, 0)),
                  pl.BlockSpec((st, D_MODEL), lambda i: (i, 0)),
                  pl.BlockSpec(memory_space=pl.ANY),
                  pl.BlockSpec((D_MODEL, D_EXPERT), lambda i: (0, 0)),
                  pl.BlockSpec((D_MODEL, D_EXPERT), lambda i: (0, 0)),
                  pl.BlockSpec((D_EXPERT, D_MODEL), lambda i: (0, 0)),
                  pl.BlockSpec((1, D_MODEL), lambda i: (0, 0)),
                  pl.BlockSpec((1, D_MODEL), lambda i: (0, 0))],
        out_specs=[pl.BlockSpec((st, D_MODEL), lambda i: (jnp.minimum(i, n_p - 1), 0)),
                   pl.BlockSpec((st, D_MODEL), lambda i: (jnp.maximum(i - n_p, 0), 0))],
        out_shape=[jax.ShapeDtypeStruct((mp, D_MODEL), F32),
                   jax.ShapeDtypeStruct((m - mp, D_MODEL), F32)],
        scratch_shapes=[pltpu.VMEM((TOP_K * tt * ROW_TILES, LANES), F32),
                        pltpu.VMEM((TOP_K * tt * ROW_TILES, LANES), F32),
                        pltpu.SemaphoreType.DMA(()), pltpu.SemaphoreType.DMA(())],
        compiler_params=_cparams(("arbitrary",)),
        name="combine",
    )(dest8, dest8, gates, x1, y, wsg, wsu, wsd, g, b)


PROJ_TM = 512
PROJ_TN = 1024
MIX_TT = 256
MIX_CHUNK = 128
SAMPLE_NSEQ = 8
ROUTER_TT = 256
DISPATCH_TT = 512
EXPERT_RB = 256
EXPERT_RING = 3
COMBINE_TT = 128


def _moe(x1, x1r, mp, w_router, b_router, w_eg, w_eu, w_ed, w_sg, w_su, w_sd, ln_g, ln_b, alpha):
    m = x1.shape[0]
    wr_hi, wr_lo = _split_hi_lo(w_router)
    idx, gates, rank, cnt = _router(x1, wr_hi, wr_lo, b_router[None, :], ROUTER_TT)

    rb = EXPERT_RB
    n_rows = m * TOP_K
    n_items = n_rows // rb + N_EXPERTS
    counts = cnt[0].astype(jnp.int32)
    seg_end = jnp.cumsum(counts)
    seg_start = seg_end - counts
    n_chunk = (counts + rb - 1) // rb
    item_end = jnp.cumsum(n_chunk)
    item_start = item_end - n_chunk
    item = jnp.arange(n_items, dtype=jnp.int32)
    item_c = jnp.minimum(item, item_end[-1] - 1)
    item_e = jnp.sum((item_end[None, :] <= item_c[:, None]).astype(jnp.int32), axis=1)
    onehot_e = item_e[:, None] == jnp.arange(N_EXPERTS, dtype=jnp.int32)[None, :]

    def pick(table):
        return jnp.sum(jnp.where(onehot_e, table[None, :], 0), axis=1)

    chunk = item_c - pick(item_start)
    valid = item < item_end[-1]
    item_row = pick(seg_start) + chunk * rb
    item_valid = jnp.where(valid, jnp.clip(pick(counts) - chunk * rb, 0, rb), 0)

    experts = jnp.arange(N_EXPERTS, dtype=jnp.int32)
    nonempty = counts > 0
    slot_e = (jnp.cumsum(nonempty.astype(jnp.int32)) - 1) % 2
    later = jnp.logical_and(experts[None, :] > experts[:, None], nonempty[None, :])
    nxt_e = jnp.min(jnp.where(later, experts[None, :], N_EXPERTS), axis=1)
    nxt_e = jnp.where(nxt_e == N_EXPERTS, -1, nxt_e)
    item_new = jnp.logical_and(valid, chunk == 0).astype(jnp.int32)

    xg, dest8 = _dispatch(seg_start, idx.reshape(-1), rank.reshape(-1), x1r, DISPATCH_TT, rb)
    y = _experts(item_e, item_row, item_valid, item_new, pick(slot_e), pick(nxt_e),
                 xg, w_eg, w_eu, w_ed, rb)
    return _combine(dest8, gates, x1, y, w_sg.astype(BF16), w_su.astype(BF16),
                    w_sd.astype(BF16), ln_g[None, :], ln_b[None, :], alpha, COMBINE_TT, mp)


def _layer(xp, xs, conv_s, c_s, n_s, m_s, w_in, b_gates, w_conv, mh_gain, w_out, ln1_g, ln1_b,
           w_router, b_router, w_eg, w_eu, w_ed, w_sg, w_su, w_sd, ln2_g, ln2_b, alpha):
    bp, tp, _ = xp.shape
    bs, ts, _ = xs.shape
    mp, ms = bp * tp, bs * ts
    xp2 = xp.reshape(mp, D_MODEL)
    xs2 = xs.reshape(ms, D_MODEL)

    w_main = jnp.concatenate([w_in[:, :GATE_COL0], w_in[:, GATE_COL0 + 2 * N_HEADS:]],
                             axis=1).astype(BF16)
    w_gate = jnp.pad(w_in[:, GATE_COL0:GATE_COL0 + 2 * N_HEADS],
                     ((0, 0), (0, LANES - 2 * N_HEADS)))
    wg_hi, wg_lo = _split_hi_lo(w_gate)

    z_s = _proj(xs2, w_main, PROJ_TM, PROJ_TN)
    zg_s = _gate_proj(xs2, wg_hi, wg_lo, PROJ_TM)
    gt_s = zg_s[:, :SUBLANES].reshape(bs, ts, SUBLANES).transpose(0, 2, 1)
    mix_s, conv_n, c_n, n_n, m_n = _mixer(
        z_s, zg_s, gt_s, b_gates, w_conv, mh_gain[None, :],
        conv_s, c_s, n_s, m_s[:, None, :], nb=bs, ts=ts, nseq=SAMPLE_NSEQ)

    x1, x1r, conv_p, c_p, n_p, m_p = _prompt_layer_half(
        xp2, xs2, mix_s, b_gates, w_main, wg_hi, wg_lo, w_conv, mh_gain[None, :],
        w_out.astype(BF16), ln1_g[None, :], ln1_b[None, :], alpha,
        nb=bp, nt=tp // MIX_TT, tt=MIX_TT, chunk=MIX_CHUNK)

    x2p, x2s = _moe(x1, x1r, mp, w_router, b_router, w_eg, w_eu, w_ed, w_sg, w_su, w_sd,
                    ln2_g, ln2_b, alpha)
    states_p = (conv_p, c_p, n_p, m_p[:, 0, :])
    states_s = (conv_n, c_n, n_n, m_n[:, 0, :])
    return x2p.reshape(bp, tp, D_MODEL), x2s.reshape(bs, ts, D_MODEL), states_p, states_s


def kernel(x_prompt, x_sample, cache_conv, state_mlstm_C, state_mlstm_n, state_mlstm_m, w_in, b_gates, w_conv, mh_gain, w_out, ln1_g, ln1_b, w_router, b_router, w_exp_gate, w_exp_up, w_exp_down, w_sh_gate, w_sh_up, w_sh_down, ln2_g, ln2_b):
    depth = w_in.shape[0]
    alpha = (2.0 * depth) ** 0.25
    hp, hs = x_prompt, x_sample
    outs_p = [[], [], [], []]
    outs_s = [[], [], [], []]
    for l in range(depth):
        hp, hs, st_p, st_s = _layer(
            hp, hs, cache_conv[l], state_mlstm_C[l], state_mlstm_n[l], state_mlstm_m[l],
            w_in[l], b_gates[l], w_conv[l], mh_gain[l], w_out[l], ln1_g[l], ln1_b[l],
            w_router[l], b_router[l], w_exp_gate[l], w_exp_up[l], w_exp_down[l],
            w_sh_gate[l], w_sh_up[l], w_sh_down[l], ln2_g[l], ln2_b[l], alpha)
        for acc, val in zip(outs_p, st_p):
            acc.append(val)
        for acc, val in zip(outs_s, st_s):
            acc.append(val)
    return (hp, hs) + tuple(jnp.stack(a) for a in outs_p) + tuple(jnp.stack(a) for a in outs_s)
```

```python
import functools

import jax
import jax.numpy as jnp
from jax import lax
from jax.experimental import pallas as pl
from jax.experimental.pallas import tpu as pltpu

F32 = jnp.float32
BF16 = jnp.bfloat16

D_MODEL = 1024
N_HEADS = 4
HEAD_V = 256
HEAD_QK = 128
N_EXPERTS = 256
TOP_K = 8
D_EXPERT = 256
ROUTED_SCALE = 2.5
LN_EPS = 1e-5
CONV_K = 3

LANES = 128
SUBLANES = 8
ROW_TILES = D_MODEL // LANES
GATE_COL0 = 6 * D_MODEL

VMEM_LIMIT = 56 * 1024 * 1024


def _cparams(sem):
    return pltpu.CompilerParams(dimension_semantics=sem, vmem_limit_bytes=VMEM_LIMIT)


def _sigmoid(x):
    return 1.0 / (1.0 + jnp.exp(-x))


def _log_sigmoid(x):
    return jnp.minimum(x, 0.0) - jnp.log(1.0 + jnp.exp(-jnp.abs(x)))


def _layer_norm(x, g, b):
    mu = jnp.mean(x, axis=-1, keepdims=True)
    xc = x - mu
    var = jnp.mean(xc * xc, axis=-1, keepdims=True)
    return xc * lax.rsqrt(var + LN_EPS) * g + b


def _split_hi_lo(x):
    hi = x.astype(BF16)
    lo = (x - hi.astype(F32)).astype(BF16)
    return hi, lo


def _dot(a, b):
    return jnp.dot(a, b, preferred_element_type=F32)


def _store_rows(ref, v):
    n = v.shape[0]
    for s in range(ROW_TILES):
        ref[pl.ds(s, n, stride=ROW_TILES), :] = v[:, s * LANES:(s + 1) * LANES]


def _load_row_chunks(ref, n, row0=0):
    return [ref[pl.ds(row0 + s, n, stride=ROW_TILES), :] for s in range(ROW_TILES)]


def _dot3(x, w_hi, w_lo):
    x_hi, x_lo = _split_hi_lo(x)
    return _dot(x_hi, w_hi) + _dot(x_lo, w_hi) + _dot(x_hi, w_lo)


def _proj_kernel(x_ref, w_ref, z_ref):
    z_ref[...] = _dot(x_ref[...].astype(BF16), w_ref[...])


def _proj(x, w, tm, tn):
    m, k = x.shape
    n = w.shape[1]
    return pl.pallas_call(
        _proj_kernel,
        grid=(m // tm, n // tn),
        in_specs=[pl.BlockSpec((tm, k), lambda i, j: (i, 0)),
                  pl.BlockSpec((k, tn), lambda i, j: (0, j))],
        out_specs=pl.BlockSpec((tm, tn), lambda i, j: (i, j)),
        out_shape=jax.ShapeDtypeStruct((m, n), F32),
        compiler_params=_cparams(("parallel", "arbitrary")),
        name="proj",
    )(x, w)


def _gate_proj_kernel(x_ref, wh_ref, wl_ref, z_ref):
    z_ref[...] = _dot3(x_ref[...], wh_ref[...], wl_ref[...])


def _gate_proj(x, w_hi, w_lo, tm):
    m, k = x.shape
    n = w_hi.shape[1]
    return pl.pallas_call(
        _gate_proj_kernel,
        grid=(m // tm,),
        in_specs=[pl.BlockSpec((tm, k), lambda i: (i, 0)),
                  pl.BlockSpec((k, n), lambda i: (0, 0)),
                  pl.BlockSpec((k, n), lambda i: (0, 0))],
        out_specs=pl.BlockSpec((tm, n), lambda i: (i, 0)),
        out_shape=jax.ShapeDtypeStruct((m, n), F32),
        compiler_params=_cparams(("parallel",)),
        name="gate_proj",
    )(x, w_hi, w_lo)


def _conv_branch(u, zcb, wc, conv_ref, seq=0):
    tt = u.shape[0]
    carry = conv_ref[seq]
    rows = lax.broadcasted_iota(jnp.int32, (tt, D_MODEL), 0)
    u1 = jnp.where(rows == 0, carry[1:2, :], pltpu.roll(u, 1, 0))
    u2 = jnp.where(rows == 0, carry[0:1, :],
                   jnp.where(rows == 1, carry[1:2, :], pltpu.roll(u, 2, 0)))
    conv_out = u2 * wc[0:1, :] + u1 * wc[1:2, :] + u * wc[2:3, :]
    conv_ref[seq] = u[tt - 2:tt, :]
    return zcb * conv_out


def _mlstm_chunk(q, k, v, ig_col, lf_col, ig_row, lf_row, c_state, n_row, m_prev):
    chunk = q.shape[0]
    ti = lax.broadcasted_iota(jnp.int32, (chunk, chunk), 0)
    si = lax.broadcasted_iota(jnp.int32, (chunk, chunk), 1)
    causal = si <= ti
    b_col = jnp.sum(jnp.where(causal, lf_row, 0.0), axis=1, keepdims=True)
    b_row = jnp.sum(jnp.where(ti <= si, lf_col, 0.0), axis=0, keepdims=True)
    inter = b_col + m_prev
    dmat = jnp.where(causal, b_col - b_row + ig_row, -jnp.inf)
    m_t = jnp.maximum(inter, jnp.max(dmat, axis=1, keepdims=True))
    w_intra = jnp.exp(dmat - m_t)
    w_inter = jnp.exp(inter - m_t)
    qb = q.astype(BF16)
    kb = k.astype(BF16)
    vb = v.astype(BF16)
    s = lax.dot_general(qb, kb, (((1,), (1,)), ((), ())), preferred_element_type=F32) * w_intra
    num = w_inter * _dot(qb, c_state.astype(BF16)) + _dot(s.astype(BF16), vb)
    den = (w_inter * jnp.sum(q * n_row, axis=1, keepdims=True)
           + jnp.sum(s, axis=1, keepdims=True))
    h = num / jnp.maximum(jnp.abs(den), jnp.exp(-m_t))
    m_new = m_t[chunk - 1:chunk, :]
    b_last = b_col[chunk - 1:chunk, :]
    ws_col = jnp.exp(b_last - b_col + ig_col - m_new)
    cdecay = jnp.exp(inter[chunk - 1:chunk, :] - m_new)
    kw = k * ws_col
    c_new = cdecay * c_state + lax.dot_general(
        kw.astype(BF16), vb, (((0,), (0,)), ((), ())), preferred_element_type=F32)
    n_new = cdecay * n_row + jnp.sum(kw, axis=0, keepdims=True)
    return h, c_new, n_new, m_new


def _mlstm_branch(chunk, bg_ref, zg, gt, zqk, zv, zog, gain_ref, c_ref, n_ref, m_ref, mo_ref,
                  seq=0, row0=0):
    tt = zqk.shape[0]
    for h in range(N_HEADS):
        ig_col_all = zg[:, h:h + 1] + bg_ref[h]
        lf_col_all = _log_sigmoid(zg[:, N_HEADS + h:N_HEADS + h + 1] + bg_ref[N_HEADS + h])
        ig_row_all = gt[h:h + 1, :] + bg_ref[h]
        lf_row_all = _log_sigmoid(gt[N_HEADS + h:N_HEADS + h + 1, :] + bg_ref[N_HEADS + h])
        vcols = slice(h * HEAD_V, (h + 1) * HEAD_V)
        for c in range(tt // chunk):
            r0, r1 = c * chunk, (c + 1) * chunk
            q = zqk[r0:r1, h * HEAD_QK:(h + 1) * HEAD_QK]
            k = zqk[r0:r1, (N_HEADS + h) * HEAD_QK:(N_HEADS + h + 1) * HEAD_QK] * (HEAD_QK ** -0.5)
            hh, c_new, n_new, m_new = _mlstm_chunk(
                q, k, zv[r0:r1, vcols], ig_col_all[r0:r1, :], lf_col_all[r0:r1, :],
                ig_row_all[:, r0:r1], lf_row_all[:, r0:r1],
                c_ref[seq, h], n_ref[seq, h:h + 1, :], m_ref[seq, :, h:h + 1])
            c_ref[seq, h] = c_new
            n_ref[seq, h:h + 1, :] = n_new
            m_ref[seq, :, h:h + 1] = m_new
            mu = jnp.mean(hh, axis=-1, keepdims=True)
            hc = hh - mu
            var = jnp.mean(hc * hc, axis=-1, keepdims=True)
            hn = hc * lax.rsqrt(var + LN_EPS) * gain_ref[:, vcols]
            mo_ref[row0 + r0:row0 + r1, vcols] = hn * _sigmoid(zog[r0:r1, vcols])


def _store_x1(x, mix, wout_ref, g_ref, b_ref, alpha, x1_ref, x1r_ref):
    x1 = _layer_norm(alpha * x + _dot(mix.astype(BF16), wout_ref[...]), g_ref[...], b_ref[...])
    x1_ref[...] = x1
    _store_rows(x1r_ref, x1)


def _mixer_kernel(ts, bg_ref,
                  zcb_ref, zcc_ref, zch_ref, zqk_ref, zv_ref, zog_ref, zga_ref, zgb_ref,
                  zg_ref, gt_ref, wconv_ref, gain_ref,
                  conv0_ref, c0_ref, n0_ref, m0_ref,
                  mix_ref, conv_ref, c_ref, n_ref, m_ref, mo_ref):
    conv_ref[...] = conv0_ref[...]
    c_ref[...] = c0_ref[...]
    n_ref[...] = n0_ref[...]
    m_ref[...] = m0_ref[...]
    for seq in range(conv_ref.shape[0]):
        rows = slice(seq * ts, (seq + 1) * ts)
        a = _conv_branch(zcc_ref[rows, :] * zch_ref[rows, :], zcb_ref[rows, :], wconv_ref[...],
                         conv_ref, seq)
        _mlstm_branch(ts, bg_ref, zg_ref[rows, :], gt_ref[seq], zqk_ref[rows, :], zv_ref[rows, :],
                      zog_ref[rows, :], gain_ref, c_ref, n_ref, m_ref, mo_ref, seq, seq * ts)
        mix_ref[rows, :] = (_sigmoid(zga_ref[rows, :]) * a
                            + _sigmoid(zgb_ref[rows, :]) * mo_ref[rows, :])


def _mixer(z, zg, gt3, b_gates, w_conv, mh_gain, conv0, c0, n0, m0, *, nb, ts, nseq):
    tt = nseq * ts

    def zspec(j):
        return pl.BlockSpec((tt, D_MODEL), lambda b, j=j: (b, j))

    in_specs = [pl.BlockSpec(memory_space=pltpu.SMEM)]
    in_specs += [zspec(j) for j in range(8)]
    state_specs = [
        pl.BlockSpec((nseq, CONV_K - 1, D_MODEL), lambda b: (b, 0, 0)),
        pl.BlockSpec((nseq, N_HEADS, HEAD_QK, HEAD_V), lambda b: (b, 0, 0, 0)),
        pl.BlockSpec((nseq, N_HEADS, HEAD_QK), lambda b: (b, 0, 0)),
        pl.BlockSpec((nseq, 1, N_HEADS), lambda b: (b, 0, 0)),
    ]
    in_specs += [
        pl.BlockSpec((tt, LANES), lambda b: (b, 0)),
        pl.BlockSpec((nseq, SUBLANES, ts), lambda b: (b, 0, 0)),
        pl.BlockSpec((CONV_K, D_MODEL), lambda b: (0, 0)),
        pl.BlockSpec((1, D_MODEL), lambda b: (0, 0)),
    ] + state_specs
    out_specs = [pl.BlockSpec((tt, D_MODEL), lambda b: (b, 0))] + state_specs
    out_shape = [
        jax.ShapeDtypeStruct((nb * ts, D_MODEL), F32),
        jax.ShapeDtypeStruct((nb, CONV_K - 1, D_MODEL), F32),
        jax.ShapeDtypeStruct((nb, N_HEADS, HEAD_QK, HEAD_V), F32),
        jax.ShapeDtypeStruct((nb, N_HEADS, HEAD_QK), F32),
        jax.ShapeDtypeStruct((nb, 1, N_HEADS), F32),
    ]
    return pl.pallas_call(
        functools.partial(_mixer_kernel, ts),
        grid=(nb // nseq,),
        in_specs=in_specs,
        out_specs=out_specs,
        out_shape=out_shape,
        scratch_shapes=[pltpu.VMEM((tt, D_MODEL), F32)],
        compiler_params=_cparams(("parallel",)),
        name="mixer_sample",
    )(b_gates, z, z, z, z, z, z, z, z, zg, gt3, w_conv, mh_gain, conv0, c0, n0, m0)


def _prompt_kernel(chunk, nb, ns, alpha, bg_ref, x_ref, w_ref, wgh_ref, wgl_ref, wconv_ref,
                   gain_ref, wout_ref, g_ref, b_ref, xs_ref, mixs_ref,
                   x1_ref, x1r_ref, conv_ref, c_ref, n_ref, m_ref, mo_ref):
    b = pl.program_id(0)
    t = pl.program_id(1)

    @pl.when(b < nb)
    def _():
        @pl.when(t == 0)
        def _():
            conv_ref[...] = jnp.zeros_like(conv_ref)
            c_ref[...] = jnp.zeros_like(c_ref)
            n_ref[...] = jnp.zeros_like(n_ref)
            m_ref[...] = jnp.zeros_like(m_ref)

        x = x_ref[...]
        xb = x.astype(BF16)

        def z(j):
            return _dot(xb, w_ref[:, j * D_MODEL:(j + 1) * D_MODEL])

        a = _conv_branch(z(1) * z(2), z(0), wconv_ref[...], conv_ref)
        zg = _dot3(x, wgh_ref[...], wgl_ref[...])
        _mlstm_branch(chunk, bg_ref, zg, zg.T, z(3), z(4), z(5), gain_ref,
                      c_ref, n_ref, m_ref, mo_ref)
        mix = _sigmoid(z(6)) * a + _sigmoid(z(7)) * mo_ref[...]
        _store_x1(x, mix, wout_ref, g_ref, b_ref, alpha, x1_ref, x1r_ref)

    @pl.when(jnp.logical_and(b == nb, t < ns))
    def _():
        _store_x1(xs_ref[...], mixs_ref[...], wout_ref, g_ref, b_ref, alpha, x1_ref, x1r_ref)


def _prompt_layer_half(xp, xs, mix_s, b_gates, w_main, wg_hi, wg_lo, w_conv, mh_gain, w_out,
                       ln_g, ln_b, alpha, *, nb, nt, tt, chunk):
    mp, ms = xp.shape[0], xs.shape[0]
    ns = ms // tt
    last_p = nb * nt - 1

    def prow(b, t):
        return jnp.minimum(b * nt + t, last_p)

    def srow(b, t):
        return jnp.where(b == nb, jnp.minimum(t, ns - 1), 0)

    def orow(b, t):
        return jnp.where(b == nb, nb * nt + jnp.minimum(t, ns - 1), b * nt + t)

    def state(b):
        return jnp.minimum(b, nb - 1)

    const2 = lambda b, t: (0, 0)
    in_specs = [
        pl.BlockSpec(memory_space=pltpu.SMEM),
        pl.BlockSpec((tt, D_MODEL), lambda b, t: (prow(b, t), 0)),
        pl.BlockSpec(w_main.shape, const2, pipeline_mode=pl.Buffered(1)),
        pl.BlockSpec(wg_hi.shape, const2),
        pl.BlockSpec(wg_lo.shape, const2),
        pl.BlockSpec((CONV_K, D_MODEL), const2),
        pl.BlockSpec((1, D_MODEL), const2),
        pl.BlockSpec((D_MODEL, D_MODEL), const2),
        pl.BlockSpec((1, D_MODEL), const2),
        pl.BlockSpec((1, D_MODEL), const2),
        pl.BlockSpec((tt, D_MODEL), lambda b, t: (srow(b, t), 0)),
        pl.BlockSpec((tt, D_MODEL), lambda b, t: (srow(b, t), 0)),
    ]
    out_specs = [
        pl.BlockSpec((tt, D_MODEL), lambda b, t: (orow(b, t), 0)),
        pl.BlockSpec((tt * ROW_TILES, LANES), lambda b, t: (orow(b, t), 0)),
        pl.BlockSpec((1, CONV_K - 1, D_MODEL), lambda b, t: (state(b), 0, 0)),
        pl.BlockSpec((1, N_HEADS, HEAD_QK, HEAD_V), lambda b, t: (state(b), 0, 0, 0)),
        pl.BlockSpec((1, N_HEADS, HEAD_QK), lambda b, t: (state(b), 0, 0)),
        pl.BlockSpec((1, 1, N_HEADS), lambda b, t: (state(b), 0, 0)),
    ]
    out_shape = [
        jax.ShapeDtypeStruct((mp + ms, D_MODEL), F32),
        jax.ShapeDtypeStruct(((mp + ms) * ROW_TILES, LANES), F32),
        jax.ShapeDtypeStruct((nb, CONV_K - 1, D_MODEL), F32),
        jax.ShapeDtypeStruct((nb, N_HEADS, HEAD_QK, HEAD_V), F32),
        jax.ShapeDtypeStruct((nb, N_HEADS, HEAD_QK), F32),
        jax.ShapeDtypeStruct((nb, 1, N_HEADS), F32),
    ]
    return pl.pallas_call(
        functools.partial(_prompt_kernel, chunk, nb, ns, alpha),
        grid=(nb + 1, nt),
        in_specs=in_specs,
        out_specs=out_specs,
        out_shape=out_shape,
        scratch_shapes=[pltpu.VMEM((tt, D_MODEL), F32)],
        compiler_params=_cparams(("arbitrary", "arbitrary")),
        name="prompt_half",
    )(b_gates, xp, w_main, wg_hi, wg_lo, w_conv, mh_gain, w_out, ln_g, ln_b, xs, mix_s)


def _router_kernel(x_ref, wh_ref, wl_ref, br_ref, idx_ref, gate_ref, rank_ref, cnt_ref):
    tt = x_ref.shape[0]

    @pl.when(pl.program_id(0) == 0)
    def _():
        cnt_ref[...] = jnp.zeros_like(cnt_ref)

    scores = _sigmoid(_dot3(x_ref[...], wh_ref[...], wl_ref[...]))
    lane = lax.broadcasted_iota(jnp.int32, (tt, N_EXPERTS), 1)
    work = scores + br_ref[...]
    picked = []
    sel = []
    member = jnp.zeros((tt, N_EXPERTS), F32)
    for _ in range(TOP_K):
        mx = jnp.max(work, axis=1, keepdims=True)
        ik = jnp.min(jnp.where(work == mx, lane, N_EXPERTS), axis=1, keepdims=True)
        onehot = lane == ik
        picked.append((ik, onehot))
        sel.append(jnp.sum(jnp.where(onehot, scores, 0.0), axis=1, keepdims=True))
        work = jnp.where(onehot, -jnp.inf, work)
        member = member + onehot.astype(F32)
    total = sel[0]
    for sk in sel[1:]:
        total = total + sk

    ti = lax.broadcasted_iota(jnp.int32, (tt, tt), 0)
    si = lax.broadcasted_iota(jnp.int32, (tt, tt), 1)
    earlier = (si < ti).astype(BF16)
    before = _dot(earlier, member.astype(BF16)) + cnt_ref[...]

    lane_o = lax.broadcasted_iota(jnp.int32, (tt, TOP_K), 1)
    idx_o = jnp.zeros((tt, TOP_K), jnp.int32)
    gate_o = jnp.zeros((tt, TOP_K), F32)
    rank_o = jnp.zeros((tt, TOP_K), jnp.int32)
    for kk in range(TOP_K):
        ik, onehot = picked[kk]
        rk = jnp.sum(jnp.where(onehot, before, 0.0), axis=1, keepdims=True)
        idx_o = jnp.where(lane_o == kk, ik, idx_o)
        gate_o = jnp.where(lane_o == kk, sel[kk] / total * ROUTED_SCALE, gate_o)
        rank_o = jnp.where(lane_o == kk, rk.astype(jnp.int32), rank_o)
    idx_ref[...] = idx_o
    gate_ref[...] = gate_o
    rank_ref[...] = rank_o
    cnt_ref[...] = cnt_ref[...] + jnp.sum(member, axis=0, keepdims=True)


def _router(x1, wr_hi, wr_lo, b_router, tt):
    m = x1.shape[0]
    return pl.pallas_call(
        _router_kernel,
        grid=(m // tt,),
        in_specs=[pl.BlockSpec((tt, D_MODEL), lambda i: (i, 0)),
                  pl.BlockSpec((D_MODEL, N_EXPERTS), lambda i: (0, 0)),
                  pl.BlockSpec((D_MODEL, N_EXPERTS), lambda i: (0, 0)),
                  pl.BlockSpec((1, N_EXPERTS), lambda i: (0, 0))],
        out_specs=[pl.BlockSpec((tt, TOP_K), lambda i: (i, 0)),
                   pl.BlockSpec((tt, TOP_K), lambda i: (i, 0)),
                   pl.BlockSpec((tt, TOP_K), lambda i: (i, 0)),
                   pl.BlockSpec((1, N_EXPERTS), lambda i: (0, 0))],
        out_shape=[jax.ShapeDtypeStruct((m, TOP_K), jnp.int32),
                   jax.ShapeDtypeStruct((m, TOP_K), F32),
                   jax.ShapeDtypeStruct((m, TOP_K), jnp.int32),
                   jax.ShapeDtypeStruct((1, N_EXPERTS), F32)],
        compiler_params=_cparams(("arbitrary",)),
        name="router",
    )(x1, wr_hi, wr_lo, b_router)


def _row_copy(src, src_row8, dst, dst_row8, sem):
    return pltpu.make_async_copy(src.at[pl.ds(pl.multiple_of(src_row8, ROW_TILES), ROW_TILES)],
                                 dst.at[pl.ds(pl.multiple_of(dst_row8, ROW_TILES), ROW_TILES)],
                                 sem)


def _dest_kernel(idx_ref, rank_ref, start_ref, dest_ref):
    idx = idx_ref[...]
    shape = idx.shape
    low = jnp.broadcast_to(start_ref[:, :LANES], shape)
    high = jnp.broadcast_to(start_ref[:, LANES:], shape)
    lane = jnp.bitwise_and(idx, LANES - 1)
    start = jnp.where(idx < LANES, jnp.take_along_axis(low, lane, axis=1),
                      jnp.take_along_axis(high, lane, axis=1))
    dest_ref[...] = (start + rank_ref[...]) * ROW_TILES


def _dest(idx, rank, seg_start):
    assert N_EXPERTS == 2 * LANES
    full = pl.BlockSpec(idx.shape, lambda i: (0, 0))
    return pl.pallas_call(
        _dest_kernel,
        grid=(1,),
        in_specs=[full, full, pl.BlockSpec((1, N_EXPERTS), lambda i: (0, 0))],
        out_specs=full,
        out_shape=jax.ShapeDtypeStruct(idx.shape, jnp.int32),
        compiler_params=_cparams(("arbitrary",)),
        name="dest",
    )(idx, rank, seg_start[None, :])


def _dispatch_kernel(tt, dest_ref, x1r_ref, xg_ref, zbuf, sem, zsem):
    @pl.when(pl.program_id(0) == 0)
    def _():
        zbuf[...] = jnp.zeros_like(zbuf)
        n_sorted = xg_ref.shape[0] - zbuf.shape[0]
        fill = pltpu.make_async_copy(zbuf, xg_ref.at[pl.ds(n_sorted, zbuf.shape[0])], zsem)
        fill.start()
        fill.wait()

    def issue(t, carry):
        for kk in range(TOP_K):
            _row_copy(x1r_ref, t * ROW_TILES, xg_ref, dest_ref[t * TOP_K + kk],
                      sem).start(priority=kk % 2)
        return carry

    lax.fori_loop(0, tt, issue, 0)
    n_rows = tt * TOP_K * ROW_TILES
    pltpu.make_async_copy(xg_ref.at[pl.ds(0, n_rows)], xg_ref.at[pl.ds(0, n_rows)], sem).wait()


def _dispatch(dest8, x1r, tt, slack):
    m = x1r.shape[0] // ROW_TILES
    return pl.pallas_call(
        functools.partial(_dispatch_kernel, tt),
        grid=(m // tt,),
        in_specs=[pl.BlockSpec((tt * TOP_K,), lambda i: (i,), memory_space=pltpu.SMEM),
                  pl.BlockSpec((tt * ROW_TILES, LANES), lambda i: (i, 0))],
        out_specs=pl.BlockSpec(memory_space=pl.ANY),
        out_shape=jax.ShapeDtypeStruct(((m * TOP_K + slack) * ROW_TILES, LANES), F32),
        scratch_shapes=[pltpu.VMEM((slack * ROW_TILES, LANES), F32),
                        pltpu.SemaphoreType.DMA(()), pltpu.SemaphoreType.DMA(())],
        compiler_params=_cparams(("arbitrary",)),
        name="dispatch",
    )(dest8, x1r)


def _experts_kernel(rb, e_ref, row_ref, valid_ref, new_ref, slot_ref, nxt_ref,
                    xg_hbm, wg_hbm, wu_hbm, wd_hbm, y_hbm,
                    *scratch):
    j = pl.program_id(0)
    n_steps = pl.num_programs(0)
    ring_pos = lax.rem(j, EXPERT_RING)
    valid = valid_ref[j]
    xbufs = scratch[:EXPERT_RING]
    ybufs = scratch[EXPERT_RING:2 * EXPERT_RING]
    wg_buf, wu_buf, wd_buf, wgb, wub, wdb, wsem, xsem, ysem = scratch[2 * EXPERT_RING:]

    def tile_rows(row, n):
        return pl.ds(pl.multiple_of(row * ROW_TILES, ROW_TILES), n * ROW_TILES)

    def x_copy(item, slot):
        return pltpu.make_async_copy(xg_hbm.at[tile_rows(row_ref[item], rb)], xbufs[slot],
                                     xsem.at[slot])

    def y_copies(item, slot, start):
        row0 = row_ref[item]
        n = valid_ref[item]

        def run(copy):
            if start:
                copy.start()
            else:
                copy.wait()

        @pl.when(n == rb)
        def _():
            run(pltpu.make_async_copy(ybufs[slot], y_hbm.at[tile_rows(row0, rb)], ysem.at[slot]))

        @pl.when(n < rb)
        def _():
            piece = rb // 2
            while piece >= 1:
                off = n - jnp.bitwise_and(n, 2 * piece - 1)

                @pl.when(jnp.bitwise_and(n, piece) != 0)
                def _(off=off, piece=piece):
                    run(pltpu.make_async_copy(ybufs[slot].at[tile_rows(off, piece)],
                                              y_hbm.at[tile_rows(row0 + off, piece)],
                                              ysem.at[slot]))

                piece //= 2

    def fetch(e, slot):
        return (pltpu.make_async_copy(wg_hbm.at[e], wg_buf.at[slot], wsem.at[slot, 0]),
                pltpu.make_async_copy(wu_hbm.at[e], wu_buf.at[slot], wsem.at[slot, 1]),
                pltpu.make_async_copy(wd_hbm.at[e], wd_buf.at[slot], wsem.at[slot, 2]))

    ahead = EXPERT_RING - 1

    @pl.when(j == 0)
    def _():
        for item in range(ahead):
            @pl.when(valid_ref[item] > 0)
            def _(item=item):
                x_copy(item, item).start()

    ahead_item = jnp.minimum(j + ahead, n_steps - 1)
    for slot in range(EXPERT_RING):
        here = ring_pos == slot

        @pl.when(jnp.logical_and(here, j >= EXPERT_RING))
        def _(slot=slot):
            y_copies(j - EXPERT_RING, slot, start=False)

        @pl.when(jnp.logical_and(here, valid > 0))
        def _(slot=slot):
            x_copy(j, slot).wait()

            @pl.when(jnp.logical_and(j + ahead < n_steps, valid_ref[ahead_item] > 0))
            def _():
                x_copy(ahead_item, (slot + ahead) % EXPERT_RING).start()

    @pl.when(new_ref[j] != 0)
    def _():
        slot = slot_ref[j]

        @pl.when(j == 0)
        def _():
            for copy in fetch(e_ref[j], slot):
                copy.start()

        for copy in fetch(e_ref[j], slot):
            copy.wait()
        wgb[...] = wg_buf[slot].astype(BF16)
        wub[...] = wu_buf[slot].astype(BF16)
        wdb[...] = wd_buf[slot].astype(BF16)

        @pl.when(nxt_ref[j] >= 0)
        def _():
            for copy in fetch(nxt_ref[j], 1 - slot):
                copy.start()

    def swiglu_rows(slot, n):
        xb = jnp.concatenate(_load_row_chunks(xbufs[slot], n), axis=1).astype(BF16)
        g = _dot(xb, wgb[...])
        u = _dot(xb, wub[...])
        y = _dot((g * _sigmoid(g) * u).astype(BF16), wdb[...])
        _store_rows(ybufs[slot], y)

    row_variants = [rb // 4, rb // 2, rb]
    for slot in range(EXPERT_RING):
        here = ring_pos == slot

        for below, n in zip([0] + row_variants[:-1], row_variants):
            @pl.when(jnp.logical_and(here, jnp.logical_and(valid > below, valid <= n)))
            def _(slot=slot, n=n):
                swiglu_rows(slot, n)

        @pl.when(jnp.logical_and(here, valid > 0))
        def _(slot=slot):
            y_copies(j, slot, start=True)

        @pl.when(jnp.logical_and(here, j == n_steps - 1))
        def _(slot=slot):
            for back in range(EXPERT_RING - 1, -1, -1):
                y_copies(j - back, (slot - back) % EXPERT_RING, start=False)


def _experts(item_e, item_row, item_valid, item_new, item_slot, item_nxt,
             xg, w_eg, w_eu, w_ed, rb):
    n_items = item_e.shape[0]
    n_rows = xg.shape[0] // ROW_TILES - rb
    hbm = pl.BlockSpec(memory_space=pl.ANY)
    row_buf = pltpu.VMEM((rb * ROW_TILES, LANES), F32)
    grid_spec = pltpu.PrefetchScalarGridSpec(
        num_scalar_prefetch=6,
        grid=(n_items,),
        in_specs=[hbm, hbm, hbm, hbm],
        out_specs=hbm,
        scratch_shapes=[row_buf] * (2 * EXPERT_RING) + [
                        pltpu.VMEM((2, D_MODEL, D_EXPERT), F32),
                        pltpu.VMEM((2, D_MODEL, D_EXPERT), F32),
                        pltpu.VMEM((2, D_EXPERT, D_MODEL), F32),
                        pltpu.VMEM((D_MODEL, D_EXPERT), BF16),
                        pltpu.VMEM((D_MODEL, D_EXPERT), BF16),
                        pltpu.VMEM((D_EXPERT, D_MODEL), BF16),
                        pltpu.SemaphoreType.DMA((2, 3)),
                        pltpu.SemaphoreType.DMA((EXPERT_RING,)),
                        pltpu.SemaphoreType.DMA((EXPERT_RING,))],
    )
    return pl.pallas_call(
        functools.partial(_experts_kernel, rb),
        grid_spec=grid_spec,
        out_shape=jax.ShapeDtypeStruct((n_rows * ROW_TILES, LANES), F32),
        compiler_params=_cparams(("arbitrary",)),
        name="experts",
    )(item_e, item_row, item_valid, item_new, item_slot, item_nxt, xg, w_eg, w_eu, w_ed)


def _combine_kernel(alpha, n_p, n_steps, dest_ref, dnext_ref, gate_ref, x1_ref, y_ref,
                    wsg_ref, wsu_ref, wsd_ref, g_ref, b_ref, x2p_ref, x2s_ref,
                    ybuf_a, ybuf_b, sem_a, sem_b):
    step = pl.program_id(0)
    tt = x1_ref.shape[0] // 2
    n_rows = TOP_K * tt * ROW_TILES

    def issue(dref, half, buf, sem):
        def body(t, carry):
            for kk in range(TOP_K):
                _row_copy(y_ref, dref[(half * tt + t) * TOP_K + kk], buf,
                          (kk * tt + t) * ROW_TILES, sem).start(priority=kk % 2)
            return carry

        lax.fori_loop(0, tt, body, 0)

    def wait(buf, sem):
        pltpu.make_async_copy(y_ref.at[pl.ds(0, n_rows)], buf, sem).wait()

    def routed(half, buf):
        gates = gate_ref[half * tt:(half + 1) * tt, :]
        chunks = []
        for s in range(ROW_TILES):
            acc = jnp.zeros((tt, LANES), F32)
            for kk in range(TOP_K):
                acc = acc + gates[:, kk:kk + 1] * buf[pl.ds(kk * tt * ROW_TILES + s, tt,
                                                            stride=ROW_TILES), :]
            chunks.append(acc)
        return jnp.concatenate(chunks, axis=1)

    @pl.when(step == 0)
    def _():
        issue(dest_ref, 0, ybuf_a, sem_a)

    issue(dest_ref, 1, ybuf_b, sem_b)

    x1 = x1_ref[...]
    xb = x1.astype(BF16)
    gs = _dot(xb, wsg_ref[...])
    us = _dot(xb, wsu_ref[...])
    base = alpha * x1 + _dot((gs * _sigmoid(gs) * us).astype(BF16), wsd_ref[...])

    wait(ybuf_a, sem_a)
    x2a = _layer_norm(base[:tt, :] + routed(0, ybuf_a), g_ref[...], b_ref[...])

    @pl.when(step + 1 < n_steps)
    def _():
        issue(dnext_ref, 0, ybuf_a, sem_a)

    wait(ybuf_b, sem_b)
    x2b = _layer_norm(base[tt:, :] + routed(1, ybuf_b), g_ref[...], b_ref[...])

    @pl.when(step < n_p)
    def _():
        x2p_ref[:tt, :] = x2a
        x2p_ref[tt:, :] = x2b

    @pl.when(step >= n_p)
    def _():
        x2s_ref[:tt, :] = x2a
        x2s_ref[tt:, :] = x2b


def _combine(dest8, gates, x1, y, wsg, wsu, wsd, g, b, alpha, tt, mp):
    m = x1.shape[0]
    st = 2 * tt
    n_steps = m // st
    n_p = mp // st
    return pl.pallas_call(
        functools.partial(_combine_kernel, alpha, n_p, n_steps),
        grid=(n_steps,),
        in_specs=[pl.BlockSpec((st * TOP_K,), lambda i: (i,), memory_space=pltpu.SMEM),
                  pl.BlockSpec((st * TOP_K,), lambda i: (jnp.minimum(i + 1, n_steps - 1),),
                               memory_space=pltpu.SMEM),
                  pl.BlockSpec((st, TOP_K), lambda i: (i, 0)),
                  pl.BlockSpec((st, D_MODEL), lambda i: (i, 0)),
                  pl.BlockSpec(memory_space=pl.ANY),
                  pl.BlockSpec((D_MODEL, D_EXPERT), lambda i: (0, 0)),
                  pl.BlockSpec((D_MODEL, D_EXPERT), lambda i: (0, 0)),
                  pl.BlockSpec((D_EXPERT, D_MODEL), lambda i: (0, 0)),
                  pl.BlockSpec((1, D_MODEL), lambda i: (0, 0)),
                  pl.BlockSpec((1, D_MODEL), lambda i: (0, 0))],
        out_specs=[pl.BlockSpec((st, D_MODEL), lambda i: (jnp.minimum(i, n_p - 1), 0)),
                   pl.BlockSpec((st, D_MODEL), lambda i: (jnp.maximum(i - n_p, 0), 0))],
        out_shape=[jax.ShapeDtypeStruct((mp, D_MODEL), F32),
                   jax.ShapeDtypeStruct((m - mp, D_MODEL), F32)],
        scratch_shapes=[pltpu.VMEM((TOP_K * tt * ROW_TILES, LANES), F32),
                        pltpu.VMEM((TOP_K * tt * ROW_TILES, LANES), F32),
                        pltpu.SemaphoreType.DMA(()), pltpu.SemaphoreType.DMA(())],
        compiler_params=_cparams(("arbitrary",)),
        name="combine",
    )(dest8, dest8, gates, x1, y, wsg, wsu, wsd, g, b)


PROJ_TM = 512
PROJ_TN = 1024
MIX_TT = 256
MIX_CHUNK = 128
SAMPLE_NSEQ = 8
ROUTER_TT = 256
DISPATCH_TT = 512
EXPERT_RB = 256
EXPERT_RING = 3
COMBINE_TT = 128


def _moe(x1, x1r, mp, w_router, b_router, w_eg, w_eu, w_ed, w_sg, w_su, w_sd, ln_g, ln_b, alpha):
    m = x1.shape[0]
    wr_hi, wr_lo = _split_hi_lo(w_router)
    idx, gates, rank, cnt = _router(x1, wr_hi, wr_lo, b_router[None, :], ROUTER_TT)

    rb = EXPERT_RB
    n_rows = m * TOP_K
    n_items = n_rows // rb + N_EXPERTS
    counts = cnt[0].astype(jnp.int32)
    seg_end = jnp.cumsum(counts)
    seg_start = seg_end - counts
    n_chunk = (counts + rb - 1) // rb
    item_end = jnp.cumsum(n_chunk)
    item_start = item_end - n_chunk
    item = jnp.arange(n_items, dtype=jnp.int32)
    item_c = jnp.minimum(item, item_end[-1] - 1)
    item_e = jnp.sum((item_end[None, :] <= item_c[:, None]).astype(jnp.int32), axis=1)
    onehot_e = item_e[:, None] == jnp.arange(N_EXPERTS, dtype=jnp.int32)[None, :]

    def pick(table):
        return jnp.sum(jnp.where(onehot_e, table[None, :], 0), axis=1)

    chunk = item_c - pick(item_start)
    valid = item < item_end[-1]
    item_row = pick(seg_start) + chunk * rb
    item_valid = jnp.where(valid, jnp.clip(pick(counts) - chunk * rb, 0, rb), 0)

    experts = jnp.arange(N_EXPERTS, dtype=jnp.int32)
    nonempty = counts > 0
    slot_e = (jnp.cumsum(nonempty.astype(jnp.int32)) - 1) % 2
    later = jnp.logical_and(experts[None, :] > experts[:, None], nonempty[None, :])
    nxt_e = jnp.min(jnp.where(later, experts[None, :], N_EXPERTS), axis=1)
    nxt_e = jnp.where(nxt_e == N_EXPERTS, -1, nxt_e)
    item_new = jnp.logical_and(valid, chunk == 0).astype(jnp.int32)

    dest8 = _dest(idx.reshape(-1, LANES), rank.reshape(-1, LANES), seg_start).reshape(-1)
    xg = _dispatch(dest8, x1r, DISPATCH_TT, rb)
    y = _experts(item_e, item_row, item_valid, item_new, pick(slot_e), pick(nxt_e),
                 xg, w_eg, w_eu, w_ed, rb)
    return _combine(dest8, gates, x1, y, w_sg.astype(BF16), w_su.astype(BF16),
                    w_sd.astype(BF16), ln_g[None, :], ln_b[None, :], alpha, COMBINE_TT, mp)


def _layer(xp, xs, conv_s, c_s, n_s, m_s, w_in, b_gates, w_conv, mh_gain, w_out, ln1_g, ln1_b,
           w_router, b_router, w_eg, w_eu, w_ed, w_sg, w_su, w_sd, ln2_g, ln2_b, alpha):
    bp, tp, _ = xp.shape
    bs, ts, _ = xs.shape
    mp, ms = bp * tp, bs * ts
    xp2 = xp.reshape(mp, D_MODEL)
    xs2 = xs.reshape(ms, D_MODEL)

    w_main = jnp.concatenate([w_in[:, :GATE_COL0], w_in[:, GATE_COL0 + 2 * N_HEADS:]],
                             axis=1).astype(BF16)
    w_gate = jnp.pad(w_in[:, GATE_COL0:GATE_COL0 + 2 * N_HEADS],
                     ((0, 0), (0, LANES - 2 * N_HEADS)))
    wg_hi, wg_lo = _split_hi_lo(w_gate)

    z_s = _proj(xs2, w_main, PROJ_TM, PROJ_TN)
    zg_s = _gate_proj(xs2, wg_hi, wg_lo, PROJ_TM)
    gt_s = zg_s[:, :SUBLANES].reshape(bs, ts, SUBLANES).transpose(0, 2, 1)
    mix_s, conv_n, c_n, n_n, m_n = _mixer(
        z_s, zg_s, gt_s, b_gates, w_conv, mh_gain[None, :],
        conv_s, c_s, n_s, m_s[:, None, :], nb=bs, ts=ts, nseq=SAMPLE_NSEQ)

    x1, x1r, conv_p, c_p, n_p, m_p = _prompt_layer_half(
        xp2, xs2, mix_s, b_gates, w_main, wg_hi, wg_lo, w_conv, mh_gain[None, :],
        w_out.astype(BF16), ln1_g[None, :], ln1_b[None, :], alpha,
        nb=bp, nt=tp // MIX_TT, tt=MIX_TT, chunk=MIX_CHUNK)

    x2p, x2s = _moe(x1, x1r, mp, w_router, b_router, w_eg, w_eu, w_ed, w_sg, w_su, w_sd,
                    ln2_g, ln2_b, alpha)
    states_p = (conv_p, c_p, n_p, m_p[:, 0, :])
    states_s = (conv_n, c_n, n_n, m_n[:, 0, :])
    return x2p.reshape(bp, tp, D_MODEL), x2s.reshape(bs, ts, D_MODEL), states_p, states_s


def kernel(x_prompt, x_sample, cache_conv, state_mlstm_C, state_mlstm_n, state_mlstm_m, w_in, b_gates, w_conv, mh_gain, w_out, ln1_g, ln1_b, w_router, b_router, w_exp_gate, w_exp_up, w_exp_down, w_sh_gate, w_sh_up, w_sh_down, ln2_g, ln2_b):
    depth = w_in.shape[0]
    alpha = (2.0 * depth) ** 0.25
    hp, hs = x_prompt, x_sample
    outs_p = [[], [], [], []]
    outs_s = [[], [], [], []]
    for l in range(depth):
        hp, hs, st_p, st_s = _layer(
            hp, hs, cache_conv[l], state_mlstm_C[l], state_mlstm_n[l], state_mlstm_m[l],
            w_in[l], b_gates[l], w_conv[l], mh_gain[l], w_out[l], ln1_g[l], ln1_b[l],
            w_router[l], b_router[l], w_exp_gate[l], w_exp_up[l], w_exp_down[l],
            w_sh_gate[l], w_sh_up[l], w_sh_down[l], ln2_g[l], ln2_b[l], alpha)
        for acc, val in zip(outs_p, st_p):
            acc.append(val)
        for acc, val in zip(outs_s, st_s):
            acc.append(val)
    return (hp, hs) + tuple(jnp.stack(a) for a in outs_p) + tuple(jnp.stack(a) for a in outs_s)
```

```python
import functools

import jax
import jax.numpy as jnp
from jax import lax
from jax.experimental import pallas as pl
from jax.experimental.pallas import tpu as pltpu

F32 = jnp.float32
BF16 = jnp.bfloat16

D_MODEL = 1024
N_HEADS = 4
HEAD_V = 256
HEAD_QK = 128
N_EXPERTS = 256
TOP_K = 8
D_EXPERT = 256
ROUTED_SCALE = 2.5
LN_EPS = 1e-5
CONV_K = 3

LANES = 128
SUBLANES = 8
ROW_TILES = D_MODEL // LANES
GATE_COL0 = 6 * D_MODEL

VMEM_LIMIT = 56 * 1024 * 1024


def _cparams(sem):
    return pltpu.CompilerParams(dimension_semantics=sem, vmem_limit_bytes=VMEM_LIMIT)


def _sigmoid(x):
    return 1.0 / (1.0 + jnp.exp(-x))


def _log_sigmoid(x):
    return jnp.minimum(x, 0.0) - jnp.log(1.0 + jnp.exp(-jnp.abs(x)))


def _layer_norm(x, g, b):
    mu = jnp.mean(x, axis=-1, keepdims=True)
    xc = x - mu
    var = jnp.mean(xc * xc, axis=-1, keepdims=True)
    return xc * lax.rsqrt(var + LN_EPS) * g + b


def _split_hi_lo(x):
    hi = x.astype(BF16)
    lo = (x - hi.astype(F32)).astype(BF16)
    return hi, lo


def _dot(a, b):
    return jnp.dot(a, b, preferred_element_type=F32)


def _store_rows(ref, v):
    n = v.shape[0]
    for s in range(ROW_TILES):
        ref[pl.ds(s, n, stride=ROW_TILES), :] = v[:, s * LANES:(s + 1) * LANES]


def _load_row_chunks(ref, n, row0=0):
    return [ref[pl.ds(row0 + s, n, stride=ROW_TILES), :] for s in range(ROW_TILES)]


def _dot3(x, w_hi, w_lo):
    x_hi, x_lo = _split_hi_lo(x)
    return _dot(x_hi, w_hi) + _dot(x_lo, w_hi) + _dot(x_hi, w_lo)


def _proj_kernel(x_ref, w_ref, z_ref):
    z_ref[...] = _dot(x_ref[...].astype(BF16), w_ref[...])


def _proj(x, w, tm, tn):
    m, k = x.shape
    n = w.shape[1]
    return pl.pallas_call(
        _proj_kernel,
        grid=(m // tm, n // tn),
        in_specs=[pl.BlockSpec((tm, k), lambda i, j: (i, 0)),
                  pl.BlockSpec((k, tn), lambda i, j: (0, j))],
        out_specs=pl.BlockSpec((tm, tn), lambda i, j: (i, j)),
        out_shape=jax.ShapeDtypeStruct((m, n), F32),
        compiler_params=_cparams(("parallel", "arbitrary")),
        name="proj",
    )(x, w)


def _gate_proj_kernel(x_ref, wh_ref, wl_ref, z_ref):
    z_ref[...] = _dot3(x_ref[...], wh_ref[...], wl_ref[...])


def _gate_proj(x, w_hi, w_lo, tm):
    m, k = x.shape
    n = w_hi.shape[1]
    return pl.pallas_call(
        _gate_proj_kernel,
        grid=(m // tm,),
        in_specs=[pl.BlockSpec((tm, k), lambda i: (i, 0)),
                  pl.BlockSpec((k, n), lambda i: (0, 0)),
                  pl.BlockSpec((k, n), lambda i: (0, 0))],
        out_specs=pl.BlockSpec((tm, n), lambda i: (i, 0)),
        out_shape=jax.ShapeDtypeStruct((m, n), F32),
        compiler_params=_cparams(("parallel",)),
        name="gate_proj",
    )(x, w_hi, w_lo)


def _conv_branch(u, zcb, wc, conv_ref, seq=0):
    tt = u.shape[0]
    carry = conv_ref[seq]
    rows = lax.broadcasted_iota(jnp.int32, (tt, D_MODEL), 0)
    u1 = jnp.where(rows == 0, carry[1:2, :], pltpu.roll(u, 1, 0))
    u2 = jnp.where(rows == 0, carry[0:1, :],
                   jnp.where(rows == 1, carry[1:2, :], pltpu.roll(u, 2, 0)))
    conv_out = u2 * wc[0:1, :] + u1 * wc[1:2, :] + u * wc[2:3, :]
    conv_ref[seq] = u[tt - 2:tt, :]
    return zcb * conv_out


def _mlstm_chunk(q, k, v, ig_col, lf_col, ig_row, lf_row, c_state, n_row, m_prev):
    chunk = q.shape[0]
    ti = lax.broadcasted_iota(jnp.int32, (chunk, chunk), 0)
    si = lax.broadcasted_iota(jnp.int32, (chunk, chunk), 1)
    causal = si <= ti
    b_col = jnp.sum(jnp.where(causal, lf_row, 0.0), axis=1, keepdims=True)
    b_row = jnp.sum(jnp.where(ti <= si, lf_col, 0.0), axis=0, keepdims=True)
    inter = b_col + m_prev
    dmat = jnp.where(causal, b_col - b_row + ig_row, -jnp.inf)
    m_t = jnp.maximum(inter, jnp.max(dmat, axis=1, keepdims=True))
    w_intra = jnp.exp(dmat - m_t)
    w_inter = jnp.exp(inter - m_t)
    qb = q.astype(BF16)
    kb = k.astype(BF16)
    vb = v.astype(BF16)
    s = lax.dot_general(qb, kb, (((1,), (1,)), ((), ())), preferred_element_type=F32) * w_intra
    num = w_inter * _dot(qb, c_state.astype(BF16)) + _dot(s.astype(BF16), vb)
    den = (w_inter * jnp.sum(q * n_row, axis=1, keepdims=True)
           + jnp.sum(s, axis=1, keepdims=True))
    h = num / jnp.maximum(jnp.abs(den), jnp.exp(-m_t))
    m_new = m_t[chunk - 1:chunk, :]
    b_last = b_col[chunk - 1:chunk, :]
    ws_col = jnp.exp(b_last - b_col + ig_col - m_new)
    cdecay = jnp.exp(inter[chunk - 1:chunk, :] - m_new)
    kw = k * ws_col
    c_new = cdecay * c_state + lax.dot_general(
        kw.astype(BF16), vb, (((0,), (0,)), ((), ())), preferred_element_type=F32)
    n_new = cdecay * n_row + jnp.sum(kw, axis=0, keepdims=True)
    return h, c_new, n_new, m_new


def _mlstm_branch(chunk, bg_ref, zg, gt, zqk, zv, zog, gain_ref, c_ref, n_ref, m_ref, mo_ref,
                  seq=0, row0=0):
    tt = zqk.shape[0]
    for h in range(N_HEADS):
        ig_col_all = zg[:, h:h + 1] + bg_ref[h]
        lf_col_all = _log_sigmoid(zg[:, N_HEADS + h:N_HEADS + h + 1] + bg_ref[N_HEADS + h])
        ig_row_all = gt[h:h + 1, :] + bg_ref[h]
        lf_row_all = _log_sigmoid(gt[N_HEADS + h:N_HEADS + h + 1, :] + bg_ref[N_HEADS + h])
        vcols = slice(h * HEAD_V, (h + 1) * HEAD_V)
        for c in range(tt // chunk):
            r0, r1 = c * chunk, (c + 1) * chunk
            q = zqk[r0:r1, h * HEAD_QK:(h + 1) * HEAD_QK]
            k = zqk[r0:r1, (N_HEADS + h) * HEAD_QK:(N_HEADS + h + 1) * HEAD_QK] * (HEAD_QK ** -0.5)
            hh, c_new, n_new, m_new = _mlstm_chunk(
                q, k, zv[r0:r1, vcols], ig_col_all[r0:r1, :], lf_col_all[r0:r1, :],
                ig_row_all[:, r0:r1], lf_row_all[:, r0:r1],
                c_ref[seq, h], n_ref[seq, h:h + 1, :], m_ref[seq, :, h:h + 1])
            c_ref[seq, h] = c_new
            n_ref[seq, h:h + 1, :] = n_new
            m_ref[seq, :, h:h + 1] = m_new
            mu = jnp.mean(hh, axis=-1, keepdims=True)
            hc = hh - mu
            var = jnp.mean(hc * hc, axis=-1, keepdims=True)
            hn = hc * lax.rsqrt(var + LN_EPS) * gain_ref[:, vcols]
            mo_ref[row0 + r0:row0 + r1, vcols] = hn * _sigmoid(zog[r0:r1, vcols])


def _store_x1(x, mix, wout_ref, g_ref, b_ref, alpha, x1_ref, x1r_ref):
    x1 = _layer_norm(alpha * x + _dot(mix.astype(BF16), wout_ref[...]), g_ref[...], b_ref[...])
    x1_ref[...] = x1
    _store_rows(x1r_ref, x1)


def _mixer_kernel(ts, bg_ref,
                  zcb_ref, zcc_ref, zch_ref, zqk_ref, zv_ref, zog_ref, zga_ref, zgb_ref,
                  zg_ref, gt_ref, wconv_ref, gain_ref,
                  conv0_ref, c0_ref, n0_ref, m0_ref,
                  mix_ref, conv_ref, c_ref, n_ref, m_ref, mo_ref):
    conv_ref[...] = conv0_ref[...]
    c_ref[...] = c0_ref[...]
    n_ref[...] = n0_ref[...]
    m_ref[...] = m0_ref[...]
    for seq in range(conv_ref.shape[0]):
        rows = slice(seq * ts, (seq + 1) * ts)
        a = _conv_branch(zcc_ref[rows, :] * zch_ref[rows, :], zcb_ref[rows, :], wconv_ref[...],
                         conv_ref, seq)
        _mlstm_branch(ts, bg_ref, zg_ref[rows, :], gt_ref[seq], zqk_ref[rows, :], zv_ref[rows, :],
                      zog_ref[rows, :], gain_ref, c_ref, n_ref, m_ref, mo_ref, seq, seq * ts)
        mix_ref[rows, :] = (_sigmoid(zga_ref[rows, :]) * a
                            + _sigmoid(zgb_ref[rows, :]) * mo_ref[rows, :])


def _mixer(z, zg, gt3, b_gates, w_conv, mh_gain, conv0, c0, n0, m0, *, nb, ts, nseq):
    tt = nseq * ts

    def zspec(j):
        return pl.BlockSpec((tt, D_MODEL), lambda b, j=j: (b, j))

    in_specs = [pl.BlockSpec(memory_space=pltpu.SMEM)]
    in_specs += [zspec(j) for j in range(8)]
    state_specs = [
        pl.BlockSpec((nseq, CONV_K - 1, D_MODEL), lambda b: (b, 0, 0)),
        pl.BlockSpec((nseq, N_HEADS, HEAD_QK, HEAD_V), lambda b: (b, 0, 0, 0)),
        pl.BlockSpec((nseq, N_HEADS, HEAD_QK), lambda b: (b, 0, 0)),
        pl.BlockSpec((nseq, 1, N_HEADS), lambda b: (b, 0, 0)),
    ]
    in_specs += [
        pl.BlockSpec((tt, LANES), lambda b: (b, 0)),
        pl.BlockSpec((nseq, SUBLANES, ts), lambda b: (b, 0, 0)),
        pl.BlockSpec((CONV_K, D_MODEL), lambda b: (0, 0)),
        pl.BlockSpec((1, D_MODEL), lambda b: (0, 0)),
    ] + state_specs
    out_specs = [pl.BlockSpec((tt, D_MODEL), lambda b: (b, 0))] + state_specs
    out_shape = [
        jax.ShapeDtypeStruct((nb * ts, D_MODEL), F32),
        jax.ShapeDtypeStruct((nb, CONV_K - 1, D_MODEL), F32),
        jax.ShapeDtypeStruct((nb, N_HEADS, HEAD_QK, HEAD_V), F32),
        jax.ShapeDtypeStruct((nb, N_HEADS, HEAD_QK), F32),
        jax.ShapeDtypeStruct((nb, 1, N_HEADS), F32),
    ]
    return pl.pallas_call(
        functools.partial(_mixer_kernel, ts),
        grid=(nb // nseq,),
        in_specs=in_specs,
        out_specs=out_specs,
        out_shape=out_shape,
        scratch_shapes=[pltpu.VMEM((tt, D_MODEL), F32)],
        compiler_params=_cparams(("parallel",)),
        name="mixer_sample",
    )(b_gates, z, z, z, z, z, z, z, z, zg, gt3, w_conv, mh_gain, conv0, c0, n0, m0)


def _prompt_kernel(chunk, nb, ns, alpha, bg_ref, x_ref, w_ref, wgh_ref, wgl_ref, wconv_ref,
                   gain_ref, wout_ref, g_ref, b_ref, xs_ref, mixs_ref,
                   x1_ref, x1r_ref, conv_ref, c_ref, n_ref, m_ref, mo_ref):
    b = pl.program_id(0)
    t = pl.program_id(1)

    @pl.when(b < nb)
    def _():
        @pl.when(t == 0)
        def _():
            conv_ref[...] = jnp.zeros_like(conv_ref)
            c_ref[...] = jnp.zeros_like(c_ref)
            n_ref[...] = jnp.zeros_like(n_ref)
            m_ref[...] = jnp.zeros_like(m_ref)

        x = x_ref[...]
        xb = x.astype(BF16)

        def z(j):
            return _dot(xb, w_ref[:, j * D_MODEL:(j + 1) * D_MODEL])

        a = _conv_branch(z(1) * z(2), z(0), wconv_ref[...], conv_ref)
        zg = _dot3(x, wgh_ref[...], wgl_ref[...])
        _mlstm_branch(chunk, bg_ref, zg, zg.T, z(3), z(4), z(5), gain_ref,
                      c_ref, n_ref, m_ref, mo_ref)
        mix = _sigmoid(z(6)) * a + _sigmoid(z(7)) * mo_ref[...]
        _store_x1(x, mix, wout_ref, g_ref, b_ref, alpha, x1_ref, x1r_ref)

    @pl.when(jnp.logical_and(b == nb, t < ns))
    def _():
        _store_x1(xs_ref[...], mixs_ref[...], wout_ref, g_ref, b_ref, alpha, x1_ref, x1r_ref)


def _prompt_layer_half(xp, xs, mix_s, b_gates, w_main, wg_hi, wg_lo, w_conv, mh_gain, w_out,
                       ln_g, ln_b, alpha, *, nb, nt, tt, chunk):
    mp, ms = xp.shape[0], xs.shape[0]
    ns = ms // tt
    last_p = nb * nt - 1

    def prow(b, t):
        return jnp.minimum(b * nt + t, last_p)

    def srow(b, t):
        return jnp.where(b == nb, jnp.minimum(t, ns - 1), 0)

    def orow(b, t):
        return jnp.where(b == nb, nb * nt + jnp.minimum(t, ns - 1), b * nt + t)

    def state(b):
        return jnp.minimum(b, nb - 1)

    const2 = lambda b, t: (0, 0)
    in_specs = [
        pl.BlockSpec(memory_space=pltpu.SMEM),
        pl.BlockSpec((tt, D_MODEL), lambda b, t: (prow(b, t), 0)),
        pl.BlockSpec(w_main.shape, const2, pipeline_mode=pl.Buffered(1)),
        pl.BlockSpec(wg_hi.shape, const2),
        pl.BlockSpec(wg_lo.shape, const2),
        pl.BlockSpec((CONV_K, D_MODEL), const2),
        pl.BlockSpec((1, D_MODEL), const2),
        pl.BlockSpec((D_MODEL, D_MODEL), const2),
        pl.BlockSpec((1, D_MODEL), const2),
        pl.BlockSpec((1, D_MODEL), const2),
        pl.BlockSpec((tt, D_MODEL), lambda b, t: (srow(b, t), 0)),
        pl.BlockSpec((tt, D_MODEL), lambda b, t: (srow(b, t), 0)),
    ]
    out_specs = [
        pl.BlockSpec((tt, D_MODEL), lambda b, t: (orow(b, t), 0)),
        pl.BlockSpec((tt * ROW_TILES, LANES), lambda b, t: (orow(b, t), 0)),
        pl.BlockSpec((1, CONV_K - 1, D_MODEL), lambda b, t: (state(b), 0, 0)),
        pl.BlockSpec((1, N_HEADS, HEAD_QK, HEAD_V), lambda b, t: (state(b), 0, 0, 0)),
        pl.BlockSpec((1, N_HEADS, HEAD_QK), lambda b, t: (state(b), 0, 0)),
        pl.BlockSpec((1, 1, N_HEADS), lambda b, t: (state(b), 0, 0)),
    ]
    out_shape = [
        jax.ShapeDtypeStruct((mp + ms, D_MODEL), F32),
        jax.ShapeDtypeStruct(((mp + ms) * ROW_TILES, LANES), F32),
        jax.ShapeDtypeStruct((nb, CONV_K - 1, D_MODEL), F32),
        jax.ShapeDtypeStruct((nb, N_HEADS, HEAD_QK, HEAD_V), F32),
        jax.ShapeDtypeStruct((nb, N_HEADS, HEAD_QK), F32),
        jax.ShapeDtypeStruct((nb, 1, N_HEADS), F32),
    ]
    return pl.pallas_call(
        functools.partial(_prompt_kernel, chunk, nb, ns, alpha),
        grid=(nb + 1, nt),
        in_specs=in_specs,
        out_specs=out_specs,
        out_shape=out_shape,
        scratch_shapes=[pltpu.VMEM((tt, D_MODEL), F32)],
        compiler_params=_cparams(("arbitrary", "arbitrary")),
        name="prompt_half",
    )(b_gates, xp, w_main, wg_hi, wg_lo, w_conv, mh_gain, w_out, ln_g, ln_b, xs, mix_s)


def _router_kernel(x_ref, wh_ref, wl_ref, br_ref, idx_ref, gate_ref, rank_ref, cnt_ref):
    tt = x_ref.shape[0]

    @pl.when(pl.program_id(0) == 0)
    def _():
        cnt_ref[...] = jnp.zeros_like(cnt_ref)

    scores = _sigmoid(_dot3(x_ref[...], wh_ref[...], wl_ref[...]))
    lane = lax.broadcasted_iota(jnp.int32, (tt, N_EXPERTS), 1)
    work = scores + br_ref[...]
    picked = []
    sel = []
    member = jnp.zeros((tt, N_EXPERTS), F32)
    for _ in range(TOP_K):
        mx = jnp.max(work, axis=1, keepdims=True)
        ik = jnp.min(jnp.where(work == mx, lane, N_EXPERTS), axis=1, keepdims=True)
        onehot = lane == ik
        picked.append((ik, onehot))
        sel.append(jnp.sum(jnp.where(onehot, scores, 0.0), axis=1, keepdims=True))
        work = jnp.where(onehot, -jnp.inf, work)
        member = member + onehot.astype(F32)
    total = sel[0]
    for sk in sel[1:]:
        total = total + sk

    ti = lax.broadcasted_iota(jnp.int32, (tt, tt), 0)
    si = lax.broadcasted_iota(jnp.int32, (tt, tt), 1)
    earlier = (si < ti).astype(BF16)
    before = _dot(earlier, member.astype(BF16)) + cnt_ref[...]

    lane_o = lax.broadcasted_iota(jnp.int32, (tt, TOP_K), 1)
    idx_o = jnp.zeros((tt, TOP_K), jnp.int32)
    gate_o = jnp.zeros((tt, TOP_K), F32)
    rank_o = jnp.zeros((tt, TOP_K), jnp.int32)
    for kk in range(TOP_K):
        ik, onehot = picked[kk]
        rk = jnp.sum(jnp.where(onehot, before, 0.0), axis=1, keepdims=True)
        idx_o = jnp.where(lane_o == kk, ik, idx_o)
        gate_o = jnp.where(lane_o == kk, sel[kk] / total * ROUTED_SCALE, gate_o)
        rank_o = jnp.where(lane_o == kk, rk.astype(jnp.int32), rank_o)
    idx_ref[...] = idx_o
    gate_ref[...] = gate_o
    rank_ref[...] = rank_o
    cnt_ref[...] = cnt_ref[...] + jnp.sum(member, axis=0, keepdims=True)


def _router(x1, wr_hi, wr_lo, b_router, tt):
    m = x1.shape[0]
    return pl.pallas_call(
        _router_kernel,
        grid=(m // tt,),
        in_specs=[pl.BlockSpec((tt, D_MODEL), lambda i: (i, 0)),
                  pl.BlockSpec((D_MODEL, N_EXPERTS), lambda i: (0, 0)),
                  pl.BlockSpec((D_MODEL, N_EXPERTS), lambda i: (0, 0)),
                  pl.BlockSpec((1, N_EXPERTS), lambda i: (0, 0))],
        out_specs=[pl.BlockSpec((tt, TOP_K), lambda i: (i, 0)),
                   pl.BlockSpec((tt, TOP_K), lambda i: (i, 0)),
                   pl.BlockSpec((tt, TOP_K), lambda i: (i, 0)),
                   pl.BlockSpec((1, N_EXPERTS), lambda i: (0, 0))],
        out_shape=[jax.ShapeDtypeStruct((m, TOP_K), jnp.int32),
                   jax.ShapeDtypeStruct((m, TOP_K), F32),
                   jax.ShapeDtypeStruct((m, TOP_K), jnp.int32),
                   jax.ShapeDtypeStruct((1, N_EXPERTS), F32)],
        compiler_params=_cparams(("arbitrary",)),
        name="router",
    )(x1, wr_hi, wr_lo, b_router)


def _row_copy(src, src_row8, dst, dst_row8, sem):
    return pltpu.make_async_copy(src.at[pl.ds(pl.multiple_of(src_row8, ROW_TILES), ROW_TILES)],
                                 dst.at[pl.ds(pl.multiple_of(dst_row8, ROW_TILES), ROW_TILES)],
                                 sem)


def _dest_kernel(idx_ref, rank_ref, start_ref, dest_ref):
    idx = idx_ref[...]
    shape = idx.shape
    low = jnp.broadcast_to(start_ref[:, :LANES], shape)
    high = jnp.broadcast_to(start_ref[:, LANES:], shape)
    lane = jnp.bitwise_and(idx, LANES - 1)
    start = jnp.where(idx < LANES, jnp.take_along_axis(low, lane, axis=1),
                      jnp.take_along_axis(high, lane, axis=1))
    dest_ref[...] = (start + rank_ref[...]) * ROW_TILES


def _dest(idx, rank, seg_start):
    assert N_EXPERTS == 2 * LANES
    full = pl.BlockSpec(idx.shape, lambda i: (0, 0))
    return pl.pallas_call(
        _dest_kernel,
        grid=(1,),
        in_specs=[full, full, pl.BlockSpec((1, N_EXPERTS), lambda i: (0, 0))],
        out_specs=full,
        out_shape=jax.ShapeDtypeStruct(idx.shape, jnp.int32),
        compiler_params=_cparams(("arbitrary",)),
        name="dest",
    )(idx, rank, seg_start[None, :])


def _dispatch_kernel(tt, dest_ref, x1r_ref, xg_ref, zbuf, sem, zsem):
    @pl.when(pl.program_id(0) == 0)
    def _():
        zbuf[...] = jnp.zeros_like(zbuf)
        n_sorted = xg_ref.shape[0] - zbuf.shape[0]
        fill = pltpu.make_async_copy(zbuf, xg_ref.at[pl.ds(n_sorted, zbuf.shape[0])], zsem)
        fill.start()
        fill.wait()

    def issue(t, carry):
        for kk in range(TOP_K):
            _row_copy(x1r_ref, t * ROW_TILES, xg_ref, dest_ref[t * TOP_K + kk],
                      sem).start(priority=kk % 2)
        return carry

    lax.fori_loop(0, tt, issue, 0)
    n_rows = tt * TOP_K * ROW_TILES
    pltpu.make_async_copy(xg_ref.at[pl.ds(0, n_rows)], xg_ref.at[pl.ds(0, n_rows)], sem).wait()


def _dispatch(dest8, x1r, tt, slack):
    m = x1r.shape[0] // ROW_TILES
    return pl.pallas_call(
        functools.partial(_dispatch_kernel, tt),
        grid=(m // tt,),
        in_specs=[pl.BlockSpec((tt * TOP_K,), lambda i: (i,), memory_space=pltpu.SMEM),
                  pl.BlockSpec((tt * ROW_TILES, LANES), lambda i: (i, 0))],
        out_specs=pl.BlockSpec(memory_space=pl.ANY),
        out_shape=jax.ShapeDtypeStruct(((m * TOP_K + slack) * ROW_TILES, LANES), F32),
        scratch_shapes=[pltpu.VMEM((slack * ROW_TILES, LANES), F32),
                        pltpu.SemaphoreType.DMA(()), pltpu.SemaphoreType.DMA(())],
        compiler_params=_cparams(("arbitrary",)),
        name="dispatch",
    )(dest8, x1r)


def _experts_kernel(rb, e_ref, row_ref, valid_ref, new_ref, slot_ref, nxt_ref,
                    xg_hbm, wg_hbm, wu_hbm, wd_hbm, y_hbm,
                    *scratch):
    j = pl.program_id(0)
    n_steps = pl.num_programs(0)
    ring_pos = lax.rem(j, EXPERT_RING)
    valid = valid_ref[j]
    xbufs = scratch[:EXPERT_RING]
    ybufs = scratch[EXPERT_RING:2 * EXPERT_RING]
    wg_buf, wu_buf, wd_buf, wgb, wub, wdb, wsem, xsem, ysem = scratch[2 * EXPERT_RING:]

    def tile_rows(row, n):
        return pl.ds(pl.multiple_of(row * ROW_TILES, ROW_TILES), n * ROW_TILES)

    def x_copy(item, slot):
        return pltpu.make_async_copy(xg_hbm.at[tile_rows(row_ref[item], rb)], xbufs[slot],
                                     xsem.at[slot])

    def y_copies(item, slot, start):
        row0 = row_ref[item]
        n = valid_ref[item]

        def run(copy):
            if start:
                copy.start()
            else:
                copy.wait()

        @pl.when(n == rb)
        def _():
            run(pltpu.make_async_copy(ybufs[slot], y_hbm.at[tile_rows(row0, rb)], ysem.at[slot]))

        @pl.when(n < rb)
        def _():
            piece = rb // 2
            while piece >= 1:
                off = n - jnp.bitwise_and(n, 2 * piece - 1)

                @pl.when(jnp.bitwise_and(n, piece) != 0)
                def _(off=off, piece=piece):
                    run(pltpu.make_async_copy(ybufs[slot].at[tile_rows(off, piece)],
                                              y_hbm.at[tile_rows(row0 + off, piece)],
                                              ysem.at[slot]))

                piece //= 2

    def fetch(e, slot):
        return (pltpu.make_async_copy(wg_hbm.at[e], wg_buf.at[slot], wsem.at[slot, 0]),
                pltpu.make_async_copy(wu_hbm.at[e], wu_buf.at[slot], wsem.at[slot, 1]),
                pltpu.make_async_copy(wd_hbm.at[e], wd_buf.at[slot], wsem.at[slot, 2]))

    ahead = EXPERT_RING - 1

    @pl.when(j == 0)
    def _():
        for item in range(ahead):
            @pl.when(valid_ref[item] > 0)
            def _(item=item):
                x_copy(item, item).start()

    ahead_item = jnp.minimum(j + ahead, n_steps - 1)
    for slot in range(EXPERT_RING):
        here = ring_pos == slot

        @pl.when(jnp.logical_and(here, j >= EXPERT_RING))
        def _(slot=slot):
            y_copies(j - EXPERT_RING, slot, start=False)

        @pl.when(jnp.logical_and(here, valid > 0))
        def _(slot=slot):
            x_copy(j, slot).wait()

            @pl.when(jnp.logical_and(j + ahead < n_steps, valid_ref[ahead_item] > 0))
            def _():
                x_copy(ahead_item, (slot + ahead) % EXPERT_RING).start()

    @pl.when(new_ref[j] != 0)
    def _():
        slot = slot_ref[j]

        @pl.when(j == 0)
        def _():
            for copy in fetch(e_ref[j], slot):
                copy.start()

        for copy in fetch(e_ref[j], slot):
            copy.wait()
        wgb[...] = wg_buf[slot].astype(BF16)
        wub[...] = wu_buf[slot].astype(BF16)
        wdb[...] = wd_buf[slot].astype(BF16)

        @pl.when(nxt_ref[j] >= 0)
        def _():
            for copy in fetch(nxt_ref[j], 1 - slot):
                copy.start()

    def swiglu_rows(slot, n):
        xb = jnp.concatenate(_load_row_chunks(xbufs[slot], n), axis=1).astype(BF16)
        g = _dot(xb, wgb[...])
        u = _dot(xb, wub[...])
        y = _dot((g * _sigmoid(g) * u).astype(BF16), wdb[...])
        _store_rows(ybufs[slot], y)

    row_variants = [rb // 4, rb // 2, rb]
    for slot in range(EXPERT_RING):
        here = ring_pos == slot

        for below, n in zip([0] + row_variants[:-1], row_variants):
            @pl.when(jnp.logical_and(here, jnp.logical_and(valid > below, valid <= n)))
            def _(slot=slot, n=n):
                swiglu_rows(slot, n)

        @pl.when(jnp.logical_and(here, valid > 0))
        def _(slot=slot):
            y_copies(j, slot, start=True)

        @pl.when(jnp.logical_and(here, j == n_steps - 1))
        def _(slot=slot):
            for back in range(EXPERT_RING - 1, -1, -1):
                y_copies(j - back, (slot - back) % EXPERT_RING, start=False)


def _experts(item_e, item_row, item_valid, item_new, item_slot, item_nxt,
             xg, w_eg, w_eu, w_ed, rb):
    n_items = item_e.shape[0]
    n_rows = xg.shape[0] // ROW_TILES - rb
    hbm = pl.BlockSpec(memory_space=pl.ANY)
    row_buf = pltpu.VMEM((rb * ROW_TILES, LANES), F32)
    grid_spec = pltpu.PrefetchScalarGridSpec(
        num_scalar_prefetch=6,
        grid=(n_items,),
        in_specs=[hbm, hbm, hbm, hbm],
        out_specs=hbm,
        scratch_shapes=[row_buf] * (2 * EXPERT_RING) + [
                        pltpu.VMEM((2, D_MODEL, D_EXPERT), F32),
                        pltpu.VMEM((2, D_MODEL, D_EXPERT), F32),
                        pltpu.VMEM((2, D_EXPERT, D_MODEL), F32),
                        pltpu.VMEM((D_MODEL, D_EXPERT), BF16),
                        pltpu.VMEM((D_MODEL, D_EXPERT), BF16),
                        pltpu.VMEM((D_EXPERT, D_MODEL), BF16),
                        pltpu.SemaphoreType.DMA((2, 3)),
                        pltpu.SemaphoreType.DMA((EXPERT_RING,)),
                        pltpu.SemaphoreType.DMA((EXPERT_RING,))],
    )
    return pl.pallas_call(
        functools.partial(_experts_kernel, rb),
        grid_spec=grid_spec,
        out_shape=jax.ShapeDtypeStruct((n_rows * ROW_TILES, LANES), F32),
        compiler_params=_cparams(("arbitrary",)),
        name="experts",
    )(item_e, item_row, item_valid, item_new, item_slot, item_nxt, xg, w_eg, w_eu, w_ed)


def _combine_kernel(alpha, n_p, n_steps, dest0_ref, dest1_ref, dest2_ref, gate_ref, x1_ref, y_ref,
                    wsg_ref, wsu_ref, wsd_ref, g_ref, b_ref, x2p_ref, x2s_ref,
                    base_ref, x2_ref, *ring):
    step = pl.program_id(0)
    tt = x1_ref.shape[0]
    n_rows = TOP_K * tt * ROW_TILES
    group = COMBINE_GROUP
    n_groups = tt // group
    bufs, sem = ring[:-1], ring[-1]
    ahead = COMBINE_RING - 1

    def issue_group(dref, grp, slot):
        for r in range(group):
            t = grp * group + r
            for kk in range(TOP_K):
                _row_copy(y_ref, dref[t * TOP_K + kk], bufs[slot],
                          (kk * tt + t) * ROW_TILES, sem.at[slot]).start(priority=kk % 2)

    def issue_tile(dref, slot):
        def body(grp, carry):
            issue_group(dref, grp, slot)
            return carry

        lax.fori_loop(0, n_groups, body, 0)

    def reduce_group(grp, slot):
        rows = pl.ds(pl.multiple_of(grp * group, group), group)
        gates = gate_ref[rows, :]
        chunks = []
        for s in range(ROW_TILES):
            acc = jnp.zeros((group, LANES), F32)
            for kk in range(TOP_K):
                row0 = kk * tt * ROW_TILES + grp * (group * ROW_TILES) + s
                acc = acc + gates[:, kk:kk + 1] * bufs[slot][pl.ds(row0, group,
                                                                    stride=ROW_TILES), :]
            chunks.append(acc)
        routed = jnp.concatenate(chunks, axis=1)
        x2_ref[rows, :] = _layer_norm(base_ref[rows, :] + routed, g_ref[...], b_ref[...])

    @pl.when(step == 0)
    def _():
        issue_tile(dest0_ref, 0)
        if n_steps > 1:
            issue_tile(dest1_ref, 1)

    x1 = x1_ref[...]
    xb = x1.astype(BF16)
    gs = _dot(xb, wsg_ref[...])
    us = _dot(xb, wsu_ref[...])
    base_ref[...] = alpha * x1 + _dot((gs * _sigmoid(gs) * us).astype(BF16), wsd_ref[...])

    for slot in range(COMBINE_RING):
        here = lax.rem(step, COMBINE_RING) == slot
        ahead_slot = (slot + ahead) % COMBINE_RING

        @pl.when(here)
        def _(slot=slot):
            pltpu.make_async_copy(y_ref.at[pl.ds(0, n_rows)], bufs[slot], sem.at[slot]).wait()

        @pl.when(jnp.logical_and(here, step + ahead < n_steps))
        def _(slot=slot, ahead_slot=ahead_slot):
            def body(grp, carry):
                issue_group(dest2_ref, grp, ahead_slot)
                reduce_group(grp, slot)
                return carry

            lax.fori_loop(0, n_groups, body, 0)

        @pl.when(jnp.logical_and(here, step + ahead >= n_steps))
        def _(slot=slot):
            def body(grp, carry):
                reduce_group(grp, slot)
                return carry

            lax.fori_loop(0, n_groups, body, 0)

    @pl.when(step < n_p)
    def _():
        x2p_ref[...] = x2_ref[...]

    @pl.when(step >= n_p)
    def _():
        x2s_ref[...] = x2_ref[...]


def _combine(dest8, gates, x1, y, wsg, wsu, wsd, g, b, alpha, tt, mp):
    m = x1.shape[0]
    n_steps = m // tt
    n_p = mp // tt

    def dest_spec(ahead):
        return pl.BlockSpec((tt * TOP_K,), lambda i: (jnp.minimum(i + ahead, n_steps - 1),),
                            memory_space=pltpu.SMEM)

    return pl.pallas_call(
        functools.partial(_combine_kernel, alpha, n_p, n_steps),
        grid=(n_steps,),
        in_specs=[dest_spec(0), dest_spec(1), dest_spec(2),
                  pl.BlockSpec((tt, TOP_K), lambda i: (i, 0)),
                  pl.BlockSpec((tt, D_MODEL), lambda i: (i, 0)),
                  pl.BlockSpec(memory_space=pl.ANY),
                  pl.BlockSpec((D_MODEL, D_EXPERT), lambda i: (0, 0)),
                  pl.BlockSpec((D_MODEL, D_EXPERT), lambda i: (0, 0)),
                  pl.BlockSpec((D_EXPERT, D_MODEL), lambda i: (0, 0)),
                  pl.BlockSpec((1, D_MODEL), lambda i: (0, 0)),
                  pl.BlockSpec((1, D_MODEL), lambda i: (0, 0))],
        out_specs=[pl.BlockSpec((tt, D_MODEL), lambda i: (jnp.minimum(i, n_p - 1), 0)),
                   pl.BlockSpec((tt, D_MODEL), lambda i: (jnp.maximum(i - n_p, 0), 0))],
        out_shape=[jax.ShapeDtypeStruct((mp, D_MODEL), F32),
                   jax.ShapeDtypeStruct((m - mp, D_MODEL), F32)],
        scratch_shapes=[pltpu.VMEM((tt, D_MODEL), F32), pltpu.VMEM((tt, D_MODEL), F32)]
        + [pltpu.VMEM((TOP_K * tt * ROW_TILES, LANES), F32)] * COMBINE_RING
        + [pltpu.SemaphoreType.DMA((COMBINE_RING,))],
        compiler_params=_cparams(("arbitrary",)),
        name="combine",
    )(dest8, dest8, dest8, gates, x1, y, wsg, wsu, wsd, g, b)


PROJ_TM = 512
PROJ_TN = 1024
MIX_TT = 256
MIX_CHUNK = 128
SAMPLE_NSEQ = 8
ROUTER_TT = 256
DISPATCH_TT = 512
EXPERT_RB = 256
EXPERT_RING = 3
COMBINE_TT = 256
COMBINE_RING = 3
COMBINE_GROUP = 32


def _moe(x1, x1r, mp, w_router, b_router, w_eg, w_eu, w_ed, w_sg, w_su, w_sd, ln_g, ln_b, alpha):
    m = x1.shape[0]
    wr_hi, wr_lo = _split_hi_lo(w_router)
    idx, gates, rank, cnt = _router(x1, wr_hi, wr_lo, b_router[None, :], ROUTER_TT)

    rb = EXPERT_RB
    n_rows = m * TOP_K
    n_items = n_rows // rb + N_EXPERTS
    counts = cnt[0].astype(jnp.int32)
    seg_end = jnp.cumsum(counts)
    seg_start = seg_end - counts
    n_chunk = (counts + rb - 1) // rb
    item_end = jnp.cumsum(n_chunk)
    item_start = item_end - n_chunk
    item = jnp.arange(n_items, dtype=jnp.int32)
    item_c = jnp.minimum(item, item_end[-1] - 1)
    item_e = jnp.sum((item_end[None, :] <= item_c[:, None]).astype(jnp.int32), axis=1)
    onehot_e = item_e[:, None] == jnp.arange(N_EXPERTS, dtype=jnp.int32)[None, :]

    def pick(table):
        return jnp.sum(jnp.where(onehot_e, table[None, :], 0), axis=1)

    chunk = item_c - pick(item_start)
    valid = item < item_end[-1]
    item_row = pick(seg_start) + chunk * rb
    item_valid = jnp.where(valid, jnp.clip(pick(counts) - chunk * rb, 0, rb), 0)

    experts = jnp.arange(N_EXPERTS, dtype=jnp.int32)
    nonempty = counts > 0
    slot_e = (jnp.cumsum(nonempty.astype(jnp.int32)) - 1) % 2
    later = jnp.logical_and(experts[None, :] > experts[:, None], nonempty[None, :])
    nxt_e = jnp.min(jnp.where(later, experts[None, :], N_EXPERTS), axis=1)
    nxt_e = jnp.where(nxt_e == N_EXPERTS, -1, nxt_e)
    item_new = jnp.logical_and(valid, chunk == 0).astype(jnp.int32)

    dest8 = _dest(idx.reshape(-1, LANES), rank.reshape(-1, LANES), seg_start).reshape(-1)
    xg = _dispatch(dest8, x1r, DISPATCH_TT, rb)
    y = _experts(item_e, item_row, item_valid, item_new, pick(slot_e), pick(nxt_e),
                 xg, w_eg, w_eu, w_ed, rb)
    return _combine(dest8, gates, x1, y, w_sg.astype(BF16), w_su.astype(BF16),
                    w_sd.astype(BF16), ln_g[None, :], ln_b[None, :], alpha, COMBINE_TT, mp)


def _layer(xp, xs, conv_s, c_s, n_s, m_s, w_in, b_gates, w_conv, mh_gain, w_out, ln1_g, ln1_b,
           w_router, b_router, w_eg, w_eu, w_ed, w_sg, w_su, w_sd, ln2_g, ln2_b, alpha):
    bp, tp, _ = xp.shape
    bs, ts, _ = xs.shape
    mp, ms = bp * tp, bs * ts
    xp2 = xp.reshape(mp, D_MODEL)
    xs2 = xs.reshape(ms, D_MODEL)

    w_main = jnp.concatenate([w_in[:, :GATE_COL0], w_in[:, GATE_COL0 + 2 * N_HEADS:]],
                             axis=1).astype(BF16)
    w_gate = jnp.pad(w_in[:, GATE_COL0:GATE_COL0 + 2 * N_HEADS],
                     ((0, 0), (0, LANES - 2 * N_HEADS)))
    wg_hi, wg_lo = _split_hi_lo(w_gate)

    z_s = _proj(xs2, w_main, PROJ_TM, PROJ_TN)
    zg_s = _gate_proj(xs2, wg_hi, wg_lo, PROJ_TM)
    gt_s = zg_s[:, :SUBLANES].reshape(bs, ts, SUBLANES).transpose(0, 2, 1)
    mix_s, conv_n, c_n, n_n, m_n = _mixer(
        z_s, zg_s, gt_s, b_gates, w_conv, mh_gain[None, :],
        conv_s, c_s, n_s, m_s[:, None, :], nb=bs, ts=ts, nseq=SAMPLE_NSEQ)

    x1, x1r, conv_p, c_p, n_p, m_p = _prompt_layer_half(
        xp2, xs2, mix_s, b_gates, w_main, wg_hi, wg_lo, w_conv, mh_gain[None, :],
        w_out.astype(BF16), ln1_g[None, :], ln1_b[None, :], alpha,
        nb=bp, nt=tp // MIX_TT, tt=MIX_TT, chunk=MIX_CHUNK)

    x2p, x2s = _moe(x1, x1r, mp, w_router, b_router, w_eg, w_eu, w_ed, w_sg, w_su, w_sd,
                    ln2_g, ln2_b, alpha)
    states_p = (conv_p, c_p, n_p, m_p[:, 0, :])
    states_s = (conv_n, c_n, n_n, m_n[:, 0, :])
    return x2p.reshape(bp, tp, D_MODEL), x2s.reshape(bs, ts, D_MODEL), states_p, states_s


def kernel(x_prompt, x_sample, cache_conv, state_mlstm_C, state_mlstm_n, state_mlstm_m, w_in, b_gates, w_conv, mh_gain, w_out, ln1_g, ln1_b, w_router, b_router, w_exp_gate, w_exp_up, w_exp_down, w_sh_gate, w_sh_up, w_sh_down, ln2_g, ln2_b):
    depth = w_in.shape[0]
    alpha = (2.0 * depth) ** 0.25
    hp, hs = x_prompt, x_sample
    outs_p = [[], [], [], []]
    outs_s = [[], [], [], []]
    for l in range(depth):
        hp, hs, st_p, st_s = _layer(
            hp, hs, cache_conv[l], state_mlstm_C[l], state_mlstm_n[l], state_mlstm_m[l],
            w_in[l], b_gates[l], w_conv[l], mh_gain[l], w_out[l], ln1_g[l], ln1_b[l],
            w_router[l], b_router[l], w_exp_gate[l], w_exp_up[l], w_exp_down[l],
            w_sh_gate[l], w_sh_up[l], w_sh_down[l], ln2_g[l], ln2_b[l], alpha)
        for acc, val in zip(outs_p, st_p):
            acc.append(val)
        for acc, val in zip(outs_s, st_s):
            acc.append(val)
    return (hp, hs) + tuple(jnp.stack(a) for a in outs_p) + tuple(jnp.stack(a) for a in outs_s)
```

```python
import functools

import jax
import jax.numpy as jnp
from jax import lax
from jax.experimental import pallas as pl
from jax.experimental.pallas import tpu as pltpu

F32 = jnp.float32
BF16 = jnp.bfloat16

D_MODEL = 1024
N_HEADS = 4
HEAD_V = 256
HEAD_QK = 128
N_EXPERTS = 256
TOP_K = 8
D_EXPERT = 256
ROUTED_SCALE = 2.5
LN_EPS = 1e-5
CONV_K = 3

LANES = 128
SUBLANES = 8
ROW_TILES = D_MODEL // LANES
GATE_COL0 = 6 * D_MODEL

VMEM_LIMIT = 56 * 1024 * 1024


def _cparams(sem):
    return pltpu.CompilerParams(dimension_semantics=sem, vmem_limit_bytes=VMEM_LIMIT)


def _sigmoid(x):
    return 1.0 / (1.0 + jnp.exp(-x))


def _log_sigmoid(x):
    return jnp.minimum(x, 0.0) - jnp.log(1.0 + jnp.exp(-jnp.abs(x)))


def _layer_norm(x, g, b):
    mu = jnp.mean(x, axis=-1, keepdims=True)
    xc = x - mu
    var = jnp.mean(xc * xc, axis=-1, keepdims=True)
    return xc * lax.rsqrt(var + LN_EPS) * g + b


def _split_hi_lo(x):
    hi = x.astype(BF16)
    lo = (x - hi.astype(F32)).astype(BF16)
    return hi, lo


def _dot(a, b):
    return jnp.dot(a, b, preferred_element_type=F32)


def _store_rows(ref, v):
    n = v.shape[0]
    for s in range(ROW_TILES):
        ref[pl.ds(s, n, stride=ROW_TILES), :] = v[:, s * LANES:(s + 1) * LANES]


def _load_row_chunks(ref, n, row0=0):
    return [ref[pl.ds(row0 + s, n, stride=ROW_TILES), :] for s in range(ROW_TILES)]


def _dot3(x, w_hi, w_lo):
    x_hi, x_lo = _split_hi_lo(x)
    return _dot(x_hi, w_hi) + _dot(x_lo, w_hi) + _dot(x_hi, w_lo)


def _proj_kernel(x_ref, w_ref, z_ref):
    z_ref[...] = _dot(x_ref[...].astype(BF16), w_ref[...])


def _proj(x, w, tm, tn):
    m, k = x.shape
    n = w.shape[1]
    return pl.pallas_call(
        _proj_kernel,
        grid=(m // tm, n // tn),
        in_specs=[pl.BlockSpec((tm, k), lambda i, j: (i, 0)),
                  pl.BlockSpec((k, tn), lambda i, j: (0, j))],
        out_specs=pl.BlockSpec((tm, tn), lambda i, j: (i, j)),
        out_shape=jax.ShapeDtypeStruct((m, n), F32),
        compiler_params=_cparams(("parallel", "arbitrary")),
        name="proj",
    )(x, w)


def _gate_proj_kernel(x_ref, wh_ref, wl_ref, z_ref):
    z_ref[...] = _dot3(x_ref[...], wh_ref[...], wl_ref[...])


def _gate_proj(x, w_hi, w_lo, tm):
    m, k = x.shape
    n = w_hi.shape[1]
    return pl.pallas_call(
        _gate_proj_kernel,
        grid=(m // tm,),
        in_specs=[pl.BlockSpec((tm, k), lambda i: (i, 0)),
                  pl.BlockSpec((k, n), lambda i: (0, 0)),
                  pl.BlockSpec((k, n), lambda i: (0, 0))],
        out_specs=pl.BlockSpec((tm, n), lambda i: (i, 0)),
        out_shape=jax.ShapeDtypeStruct((m, n), F32),
        compiler_params=_cparams(("parallel",)),
        name="gate_proj",
    )(x, w_hi, w_lo)


def _conv_branch(u, zcb, wc, conv_ref, seq=0):
    tt = u.shape[0]
    carry = conv_ref[seq]
    rows = lax.broadcasted_iota(jnp.int32, (tt, D_MODEL), 0)
    u1 = jnp.where(rows == 0, carry[1:2, :], pltpu.roll(u, 1, 0))
    u2 = jnp.where(rows == 0, carry[0:1, :],
                   jnp.where(rows == 1, carry[1:2, :], pltpu.roll(u, 2, 0)))
    conv_out = u2 * wc[0:1, :] + u1 * wc[1:2, :] + u * wc[2:3, :]
    conv_ref[seq] = u[tt - 2:tt, :]
    return zcb * conv_out


def _mlstm_chunk(q, k, v, ig_col, lf_col, ig_row, lf_row, c_state, n_row, m_prev):
    chunk = q.shape[0]
    ti = lax.broadcasted_iota(jnp.int32, (chunk, chunk), 0)
    si = lax.broadcasted_iota(jnp.int32, (chunk, chunk), 1)
    causal = si <= ti
    b_col = jnp.sum(jnp.where(causal, lf_row, 0.0), axis=1, keepdims=True)
    b_row = jnp.sum(jnp.where(ti <= si, lf_col, 0.0), axis=0, keepdims=True)
    inter = b_col + m_prev
    dmat = jnp.where(causal, b_col - b_row + ig_row, -jnp.inf)
    m_t = jnp.maximum(inter, jnp.max(dmat, axis=1, keepdims=True))
    w_intra = jnp.exp(dmat - m_t)
    w_inter = jnp.exp(inter - m_t)
    qb = q.astype(BF16)
    kb = k.astype(BF16)
    vb = v.astype(BF16)
    s = lax.dot_general(qb, kb, (((1,), (1,)), ((), ())), preferred_element_type=F32) * w_intra
    num = w_inter * _dot(qb, c_state.astype(BF16)) + _dot(s.astype(BF16), vb)
    den = (w_inter * jnp.sum(q * n_row, axis=1, keepdims=True)
           + jnp.sum(s, axis=1, keepdims=True))
    h = num / jnp.maximum(jnp.abs(den), jnp.exp(-m_t))
    m_new = m_t[chunk - 1:chunk, :]
    b_last = b_col[chunk - 1:chunk, :]
    ws_col = jnp.exp(b_last - b_col + ig_col - m_new)
    cdecay = jnp.exp(inter[chunk - 1:chunk, :] - m_new)
    kw = k * ws_col
    c_new = cdecay * c_state + lax.dot_general(
        kw.astype(BF16), vb, (((0,), (0,)), ((), ())), preferred_element_type=F32)
    n_new = cdecay * n_row + jnp.sum(kw, axis=0, keepdims=True)
    return h, c_new, n_new, m_new


def _mlstm_branch(chunk, bg_ref, zg, gt, zqk, zv, zog, gain_ref, c_ref, n_ref, m_ref, mo_ref,
                  seq=0, row0=0):
    tt = zqk.shape[0]
    for h in range(N_HEADS):
        ig_col_all = zg[:, h:h + 1] + bg_ref[h]
        lf_col_all = _log_sigmoid(zg[:, N_HEADS + h:N_HEADS + h + 1] + bg_ref[N_HEADS + h])
        ig_row_all = gt[h:h + 1, :] + bg_ref[h]
        lf_row_all = _log_sigmoid(gt[N_HEADS + h:N_HEADS + h + 1, :] + bg_ref[N_HEADS + h])
        vcols = slice(h * HEAD_V, (h + 1) * HEAD_V)
        for c in range(tt // chunk):
            r0, r1 = c * chunk, (c + 1) * chunk
            q = zqk[r0:r1, h * HEAD_QK:(h + 1) * HEAD_QK]
            k = zqk[r0:r1, (N_HEADS + h) * HEAD_QK:(N_HEADS + h + 1) * HEAD_QK] * (HEAD_QK ** -0.5)
            hh, c_new, n_new, m_new = _mlstm_chunk(
                q, k, zv[r0:r1, vcols], ig_col_all[r0:r1, :], lf_col_all[r0:r1, :],
                ig_row_all[:, r0:r1], lf_row_all[:, r0:r1],
                c_ref[seq, h], n_ref[seq, h:h + 1, :], m_ref[seq, :, h:h + 1])
            c_ref[seq, h] = c_new
            n_ref[seq, h:h + 1, :] = n_new
            m_ref[seq, :, h:h + 1] = m_new
            mu = jnp.mean(hh, axis=-1, keepdims=True)
            hc = hh - mu
            var = jnp.mean(hc * hc, axis=-1, keepdims=True)
            hn = hc * lax.rsqrt(var + LN_EPS) * gain_ref[:, vcols]
            mo_ref[row0 + r0:row0 + r1, vcols] = hn * _sigmoid(zog[r0:r1, vcols])


def _store_x1(x, mix, wout_ref, g_ref, b_ref, alpha, x1_ref, x1r_ref):
    x1 = _layer_norm(alpha * x + _dot(mix.astype(BF16), wout_ref[...]), g_ref[...], b_ref[...])
    x1_ref[...] = x1
    _store_rows(x1r_ref, x1)


def _mixer_kernel(ts, bg_ref,
                  zcb_ref, zcc_ref, zch_ref, zqk_ref, zv_ref, zog_ref, zga_ref, zgb_ref,
                  zg_ref, gt_ref, wconv_ref, gain_ref,
                  conv0_ref, c0_ref, n0_ref, m0_ref,
                  mix_ref, conv_ref, c_ref, n_ref, m_ref, mo_ref):
    conv_ref[...] = conv0_ref[...]
    c_ref[...] = c0_ref[...]
    n_ref[...] = n0_ref[...]
    m_ref[...] = m0_ref[...]
    for seq in range(conv_ref.shape[0]):
        rows = slice(seq * ts, (seq + 1) * ts)
        a = _conv_branch(zcc_ref[rows, :] * zch_ref[rows, :], zcb_ref[rows, :], wconv_ref[...],
                         conv_ref, seq)
        _mlstm_branch(ts, bg_ref, zg_ref[rows, :], gt_ref[seq], zqk_ref[rows, :], zv_ref[rows, :],
                      zog_ref[rows, :], gain_ref, c_ref, n_ref, m_ref, mo_ref, seq, seq * ts)
        mix_ref[rows, :] = (_sigmoid(zga_ref[rows, :]) * a
                            + _sigmoid(zgb_ref[rows, :]) * mo_ref[rows, :])


def _mixer(z, zg, gt3, b_gates, w_conv, mh_gain, conv0, c0, n0, m0, *, nb, ts, nseq):
    tt = nseq * ts

    def zspec(j):
        return pl.BlockSpec((tt, D_MODEL), lambda b, j=j: (b, j))

    in_specs = [pl.BlockSpec(memory_space=pltpu.SMEM)]
    in_specs += [zspec(j) for j in range(8)]
    state_specs = [
        pl.BlockSpec((nseq, CONV_K - 1, D_MODEL), lambda b: (b, 0, 0)),
        pl.BlockSpec((nseq, N_HEADS, HEAD_QK, HEAD_V), lambda b: (b, 0, 0, 0)),
        pl.BlockSpec((nseq, N_HEADS, HEAD_QK), lambda b: (b, 0, 0)),
        pl.BlockSpec((nseq, 1, N_HEADS), lambda b: (b, 0, 0)),
    ]
    in_specs += [
        pl.BlockSpec((tt, LANES), lambda b: (b, 0)),
        pl.BlockSpec((nseq, SUBLANES, ts), lambda b: (b, 0, 0)),
        pl.BlockSpec((CONV_K, D_MODEL), lambda b: (0, 0)),
        pl.BlockSpec((1, D_MODEL), lambda b: (0, 0)),
    ] + state_specs
    out_specs = [pl.BlockSpec((tt, D_MODEL), lambda b: (b, 0))] + state_specs
    out_shape = [
        jax.ShapeDtypeStruct((nb * ts, D_MODEL), F32),
        jax.ShapeDtypeStruct((nb, CONV_K - 1, D_MODEL), F32),
        jax.ShapeDtypeStruct((nb, N_HEADS, HEAD_QK, HEAD_V), F32),
        jax.ShapeDtypeStruct((nb, N_HEADS, HEAD_QK), F32),
        jax.ShapeDtypeStruct((nb, 1, N_HEADS), F32),
    ]
    return pl.pallas_call(
        functools.partial(_mixer_kernel, ts),
        grid=(nb // nseq,),
        in_specs=in_specs,
        out_specs=out_specs,
        out_shape=out_shape,
        scratch_shapes=[pltpu.VMEM((tt, D_MODEL), F32)],
        compiler_params=_cparams(("parallel",)),
        name="mixer_sample",
    )(b_gates, z, z, z, z, z, z, z, z, zg, gt3, w_conv, mh_gain, conv0, c0, n0, m0)


def _prompt_kernel(chunk, nb, ns, alpha, bg_ref, x_ref, w_ref, wgh_ref, wgl_ref, wconv_ref,
                   gain_ref, wout_ref, g_ref, b_ref, xs_ref, mixs_ref,
                   x1_ref, x1r_ref, conv_ref, c_ref, n_ref, m_ref, mo_ref):
    b = pl.program_id(0)
    t = pl.program_id(1)

    @pl.when(b < nb)
    def _():
        @pl.when(t == 0)
        def _():
            conv_ref[...] = jnp.zeros_like(conv_ref)
            c_ref[...] = jnp.zeros_like(c_ref)
            n_ref[...] = jnp.zeros_like(n_ref)
            m_ref[...] = jnp.zeros_like(m_ref)

        x = x_ref[...]
        xb = x.astype(BF16)

        def z(j):
            return _dot(xb, w_ref[:, j * D_MODEL:(j + 1) * D_MODEL])

        a = _conv_branch(z(1) * z(2), z(0), wconv_ref[...], conv_ref)
        zg = _dot3(x, wgh_ref[...], wgl_ref[...])
        _mlstm_branch(chunk, bg_ref, zg, zg.T, z(3), z(4), z(5), gain_ref,
                      c_ref, n_ref, m_ref, mo_ref)
        mix = _sigmoid(z(6)) * a + _sigmoid(z(7)) * mo_ref[...]
        _store_x1(x, mix, wout_ref, g_ref, b_ref, alpha, x1_ref, x1r_ref)

    @pl.when(jnp.logical_and(b == nb, t < ns))
    def _():
        _store_x1(xs_ref[...], mixs_ref[...], wout_ref, g_ref, b_ref, alpha, x1_ref, x1r_ref)


def _prompt_layer_half(xp, xs, mix_s, b_gates, w_main, wg_hi, wg_lo, w_conv, mh_gain, w_out,
                       ln_g, ln_b, alpha, *, nb, nt, tt, chunk):
    mp, ms = xp.shape[0], xs.shape[0]
    ns = ms // tt
    last_p = nb * nt - 1

    def prow(b, t):
        return jnp.minimum(b * nt + t, last_p)

    def srow(b, t):
        return jnp.where(b == nb, jnp.minimum(t, ns - 1), 0)

    def orow(b, t):
        return jnp.where(b == nb, nb * nt + jnp.minimum(t, ns - 1), b * nt + t)

    def state(b):
        return jnp.minimum(b, nb - 1)

    const2 = lambda b, t: (0, 0)
    in_specs = [
        pl.BlockSpec(memory_space=pltpu.SMEM),
        pl.BlockSpec((tt, D_MODEL), lambda b, t: (prow(b, t), 0)),
        pl.BlockSpec(w_main.shape, const2, pipeline_mode=pl.Buffered(1)),
        pl.BlockSpec(wg_hi.shape, const2),
        pl.BlockSpec(wg_lo.shape, const2),
        pl.BlockSpec((CONV_K, D_MODEL), const2),
        pl.BlockSpec((1, D_MODEL), const2),
        pl.BlockSpec((D_MODEL, D_MODEL), const2),
        pl.BlockSpec((1, D_MODEL), const2),
        pl.BlockSpec((1, D_MODEL), const2),
        pl.BlockSpec((tt, D_MODEL), lambda b, t: (srow(b, t), 0)),
        pl.BlockSpec((tt, D_MODEL), lambda b, t: (srow(b, t), 0)),
    ]
    out_specs = [
        pl.BlockSpec((tt, D_MODEL), lambda b, t: (orow(b, t), 0)),
        pl.BlockSpec((tt * ROW_TILES, LANES), lambda b, t: (orow(b, t), 0)),
        pl.BlockSpec((1, CONV_K - 1, D_MODEL), lambda b, t: (state(b), 0, 0)),
        pl.BlockSpec((1, N_HEADS, HEAD_QK, HEAD_V), lambda b, t: (state(b), 0, 0, 0)),
        pl.BlockSpec((1, N_HEADS, HEAD_QK), lambda b, t: (state(b), 0, 0)),
        pl.BlockSpec((1, 1, N_HEADS), lambda b, t: (state(b), 0, 0)),
    ]
    out_shape = [
        jax.ShapeDtypeStruct((mp + ms, D_MODEL), F32),
        jax.ShapeDtypeStruct(((mp + ms) * ROW_TILES, LANES), F32),
        jax.ShapeDtypeStruct((nb, CONV_K - 1, D_MODEL), F32),
        jax.ShapeDtypeStruct((nb, N_HEADS, HEAD_QK, HEAD_V), F32),
        jax.ShapeDtypeStruct((nb, N_HEADS, HEAD_QK), F32),
        jax.ShapeDtypeStruct((nb, 1, N_HEADS), F32),
    ]
    return pl.pallas_call(
        functools.partial(_prompt_kernel, chunk, nb, ns, alpha),
        grid=(nb + 1, nt),
        in_specs=in_specs,
        out_specs=out_specs,
        out_shape=out_shape,
        scratch_shapes=[pltpu.VMEM((tt, D_MODEL), F32)],
        compiler_params=_cparams(("arbitrary", "arbitrary")),
        name="prompt_half",
    )(b_gates, xp, w_main, wg_hi, wg_lo, w_conv, mh_gain, w_out, ln_g, ln_b, xs, mix_s)


def _router_kernel(x_ref, wh_ref, wl_ref, br_ref, idx_ref, gate_ref, rank_ref, cnt_ref):
    tt = x_ref.shape[0]

    @pl.when(pl.program_id(0) == 0)
    def _():
        cnt_ref[...] = jnp.zeros_like(cnt_ref)

    scores = _sigmoid(_dot3(x_ref[...], wh_ref[...], wl_ref[...]))
    lane = lax.broadcasted_iota(jnp.int32, (tt, N_EXPERTS), 1)
    work = scores + br_ref[...]
    picked = []
    sel = []
    member = jnp.zeros((tt, N_EXPERTS), F32)
    for _ in range(TOP_K):
        mx = jnp.max(work, axis=1, keepdims=True)
        ik = jnp.min(jnp.where(work == mx, lane, N_EXPERTS), axis=1, keepdims=True)
        onehot = lane == ik
        picked.append((ik, onehot))
        sel.append(jnp.sum(jnp.where(onehot, scores, 0.0), axis=1, keepdims=True))
        work = jnp.where(onehot, -jnp.inf, work)
        member = member + onehot.astype(F32)
    total = sel[0]
    for sk in sel[1:]:
        total = total + sk

    ti = lax.broadcasted_iota(jnp.int32, (tt, tt), 0)
    si = lax.broadcasted_iota(jnp.int32, (tt, tt), 1)
    earlier = (si < ti).astype(BF16)
    before = _dot(earlier, member.astype(BF16)) + cnt_ref[...]

    lane_o = lax.broadcasted_iota(jnp.int32, (tt, TOP_K), 1)
    idx_o = jnp.zeros((tt, TOP_K), jnp.int32)
    gate_o = jnp.zeros((tt, TOP_K), F32)
    rank_o = jnp.zeros((tt, TOP_K), jnp.int32)
    for kk in range(TOP_K):
        ik, onehot = picked[kk]
        rk = jnp.sum(jnp.where(onehot, before, 0.0), axis=1, keepdims=True)
        idx_o = jnp.where(lane_o == kk, ik, idx_o)
        gate_o = jnp.where(lane_o == kk, sel[kk] / total * ROUTED_SCALE, gate_o)
        rank_o = jnp.where(lane_o == kk, rk.astype(jnp.int32), rank_o)
    idx_ref[...] = idx_o
    gate_ref[...] = gate_o
    rank_ref[...] = rank_o
    cnt_ref[...] = cnt_ref[...] + jnp.sum(member, axis=0, keepdims=True)


def _router(x1, wr_hi, wr_lo, b_router, tt):
    m = x1.shape[0]
    return pl.pallas_call(
        _router_kernel,
        grid=(m // tt,),
        in_specs=[pl.BlockSpec((tt, D_MODEL), lambda i: (i, 0)),
                  pl.BlockSpec((D_MODEL, N_EXPERTS), lambda i: (0, 0)),
                  pl.BlockSpec((D_MODEL, N_EXPERTS), lambda i: (0, 0)),
                  pl.BlockSpec((1, N_EXPERTS), lambda i: (0, 0))],
        out_specs=[pl.BlockSpec((tt, TOP_K), lambda i: (i, 0)),
                   pl.BlockSpec((tt, TOP_K), lambda i: (i, 0)),
                   pl.BlockSpec((tt, TOP_K), lambda i: (i, 0)),
                   pl.BlockSpec((1, N_EXPERTS), lambda i: (0, 0))],
        out_shape=[jax.ShapeDtypeStruct((m, TOP_K), jnp.int32),
                   jax.ShapeDtypeStruct((m, TOP_K), F32),
                   jax.ShapeDtypeStruct((m, TOP_K), jnp.int32),
                   jax.ShapeDtypeStruct((1, N_EXPERTS), F32)],
        compiler_params=_cparams(("arbitrary",)),
        name="router",
    )(x1, wr_hi, wr_lo, b_router)


def _row_copy(src, src_row8, dst, dst_row8, sem):
    return pltpu.make_async_copy(src.at[pl.ds(pl.multiple_of(src_row8, ROW_TILES), ROW_TILES)],
                                 dst.at[pl.ds(pl.multiple_of(dst_row8, ROW_TILES), ROW_TILES)],
                                 sem)


def _dest_kernel(idx_ref, rank_ref, start_ref, dest_ref):
    idx = idx_ref[...]
    shape = idx.shape
    low = jnp.broadcast_to(start_ref[:, :LANES], shape)
    high = jnp.broadcast_to(start_ref[:, LANES:], shape)
    lane = jnp.bitwise_and(idx, LANES - 1)
    start = jnp.where(idx < LANES, jnp.take_along_axis(low, lane, axis=1),
                      jnp.take_along_axis(high, lane, axis=1))
    dest_ref[...] = (start + rank_ref[...]) * ROW_TILES


def _dest(idx, rank, seg_start):
    assert N_EXPERTS == 2 * LANES
    full = pl.BlockSpec(idx.shape, lambda i: (0, 0))
    return pl.pallas_call(
        _dest_kernel,
        grid=(1,),
        in_specs=[full, full, pl.BlockSpec((1, N_EXPERTS), lambda i: (0, 0))],
        out_specs=full,
        out_shape=jax.ShapeDtypeStruct(idx.shape, jnp.int32),
        compiler_params=_cparams(("arbitrary",)),
        name="dest",
    )(idx, rank, seg_start[None, :])


def _dispatch_kernel(tt, dest_ref, x1r_ref, xg_ref, zbuf, sem, zsem):
    @pl.when(pl.program_id(0) == 0)
    def _():
        zbuf[...] = jnp.zeros_like(zbuf)
        n_sorted = xg_ref.shape[0] - zbuf.shape[0]
        fill = pltpu.make_async_copy(zbuf, xg_ref.at[pl.ds(n_sorted, zbuf.shape[0])], zsem)
        fill.start()
        fill.wait()

    def issue(t, carry):
        for kk in range(TOP_K):
            _row_copy(x1r_ref, t * ROW_TILES, xg_ref, dest_ref[t * TOP_K + kk],
                      sem).start(priority=kk % 2)
        return carry

    lax.fori_loop(0, tt, issue, 0)
    n_rows = tt * TOP_K * ROW_TILES
    pltpu.make_async_copy(xg_ref.at[pl.ds(0, n_rows)], xg_ref.at[pl.ds(0, n_rows)], sem).wait()


def _dispatch(dest8, x1r, tt, slack):
    m = x1r.shape[0] // ROW_TILES
    return pl.pallas_call(
        functools.partial(_dispatch_kernel, tt),
        grid=(m // tt,),
        in_specs=[pl.BlockSpec((tt * TOP_K,), lambda i: (i,), memory_space=pltpu.SMEM),
                  pl.BlockSpec((tt * ROW_TILES, LANES), lambda i: (i, 0))],
        out_specs=pl.BlockSpec(memory_space=pl.ANY),
        out_shape=jax.ShapeDtypeStruct(((m * TOP_K + slack) * ROW_TILES, LANES), F32),
        scratch_shapes=[pltpu.VMEM((slack * ROW_TILES, LANES), F32),
                        pltpu.SemaphoreType.DMA(()), pltpu.SemaphoreType.DMA(())],
        compiler_params=_cparams(("arbitrary",)),
        name="dispatch",
    )(dest8, x1r)


def _experts_kernel(rb, e_ref, row_ref, valid_ref, new_ref, slot_ref, nxt_ref,
                    xg_hbm, wg_hbm, wu_hbm, wd_hbm, y_hbm,
                    *scratch):
    j = pl.program_id(0)
    n_steps = pl.num_programs(0)
    ring_pos = lax.rem(j, EXPERT_RING)
    valid = valid_ref[j]
    xbufs = scratch[:EXPERT_RING]
    ybufs = scratch[EXPERT_RING:2 * EXPERT_RING]
    wg_buf, wu_buf, wd_buf, wgb, wub, wdb, wsem, xsem, ysem = scratch[2 * EXPERT_RING:]

    def tile_rows(row, n):
        return pl.ds(pl.multiple_of(row * ROW_TILES, ROW_TILES), n * ROW_TILES)

    row_variants = [rb // 4, rb // 2, rb]

    def for_row_variant(n_valid, fn):
        for below, n in zip([0] + row_variants[:-1], row_variants):
            @pl.when(jnp.logical_and(n_valid > below, n_valid <= n))
            def _(n=n):
                fn(n)

    def x_copies(item, slot, start):
        def run(n):
            copy = pltpu.make_async_copy(xg_hbm.at[tile_rows(row_ref[item], n)],
                                         xbufs[slot].at[pl.ds(0, n * ROW_TILES)], xsem.at[slot])
            if start:
                copy.start()
            else:
                copy.wait()

        for_row_variant(valid_ref[item], run)

    def y_copies(item, slot, start):
        row0 = row_ref[item]
        n = valid_ref[item]

        def run(copy):
            if start:
                copy.start()
            else:
                copy.wait()

        @pl.when(n == rb)
        def _():
            run(pltpu.make_async_copy(ybufs[slot], y_hbm.at[tile_rows(row0, rb)], ysem.at[slot]))

        @pl.when(n < rb)
        def _():
            piece = rb // 2
            while piece >= 1:
                off = n - jnp.bitwise_and(n, 2 * piece - 1)

                @pl.when(jnp.bitwise_and(n, piece) != 0)
                def _(off=off, piece=piece):
                    run(pltpu.make_async_copy(ybufs[slot].at[tile_rows(off, piece)],
                                              y_hbm.at[tile_rows(row0 + off, piece)],
                                              ysem.at[slot]))

                piece //= 2

    def fetch(e, slot):
        return (pltpu.make_async_copy(wg_hbm.at[e], wg_buf.at[slot], wsem.at[slot, 0]),
                pltpu.make_async_copy(wu_hbm.at[e], wu_buf.at[slot], wsem.at[slot, 1]),
                pltpu.make_async_copy(wd_hbm.at[e], wd_buf.at[slot], wsem.at[slot, 2]))

    ahead = EXPERT_RING - 1

    @pl.when(j == 0)
    def _():
        for item in range(ahead):
            x_copies(item, item, start=True)

    ahead_item = jnp.minimum(j + ahead, n_steps - 1)
    for slot in range(EXPERT_RING):
        here = ring_pos == slot

        @pl.when(jnp.logical_and(here, j >= EXPERT_RING))
        def _(slot=slot):
            y_copies(j - EXPERT_RING, slot, start=False)

        @pl.when(jnp.logical_and(here, valid > 0))
        def _(slot=slot):
            x_copies(j, slot, start=False)

            @pl.when(j + ahead < n_steps)
            def _():
                x_copies(ahead_item, (slot + ahead) % EXPERT_RING, start=True)

    @pl.when(new_ref[j] != 0)
    def _():
        slot = slot_ref[j]

        @pl.when(j == 0)
        def _():
            for copy in fetch(e_ref[j], slot):
                copy.start()

        for copy in fetch(e_ref[j], slot):
            copy.wait()
        wgb[...] = wg_buf[slot].astype(BF16)
        wub[...] = wu_buf[slot].astype(BF16)
        wdb[...] = wd_buf[slot].astype(BF16)

        @pl.when(nxt_ref[j] >= 0)
        def _():
            for copy in fetch(nxt_ref[j], 1 - slot):
                copy.start()

    def swiglu_rows(slot, n):
        xb = jnp.concatenate(_load_row_chunks(xbufs[slot], n), axis=1).astype(BF16)
        g = _dot(xb, wgb[...])
        u = _dot(xb, wub[...])
        y = _dot((g * _sigmoid(g) * u).astype(BF16), wdb[...])
        _store_rows(ybufs[slot], y)

    for slot in range(EXPERT_RING):
        here = ring_pos == slot

        @pl.when(jnp.logical_and(here, valid > 0))
        def _(slot=slot):
            for_row_variant(valid, functools.partial(swiglu_rows, slot))
            y_copies(j, slot, start=True)

        @pl.when(jnp.logical_and(here, j == n_steps - 1))
        def _(slot=slot):
            for back in range(EXPERT_RING - 1, -1, -1):
                y_copies(j - back, (slot - back) % EXPERT_RING, start=False)


def _experts(item_e, item_row, item_valid, item_new, item_slot, item_nxt,
             xg, w_eg, w_eu, w_ed, rb):
    n_items = item_e.shape[0]
    n_rows = xg.shape[0] // ROW_TILES - rb
    hbm = pl.BlockSpec(memory_space=pl.ANY)
    row_buf = pltpu.VMEM((rb * ROW_TILES, LANES), F32)
    grid_spec = pltpu.PrefetchScalarGridSpec(
        num_scalar_prefetch=6,
        grid=(n_items,),
        in_specs=[hbm, hbm, hbm, hbm],
        out_specs=hbm,
        scratch_shapes=[row_buf] * (2 * EXPERT_RING) + [
                        pltpu.VMEM((2, D_MODEL, D_EXPERT), F32),
                        pltpu.VMEM((2, D_MODEL, D_EXPERT), F32),
                        pltpu.VMEM((2, D_EXPERT, D_MODEL), F32),
                        pltpu.VMEM((D_MODEL, D_EXPERT), BF16),
                        pltpu.VMEM((D_MODEL, D_EXPERT), BF16),
                        pltpu.VMEM((D_EXPERT, D_MODEL), BF16),
                        pltpu.SemaphoreType.DMA((2, 3)),
                        pltpu.SemaphoreType.DMA((EXPERT_RING,)),
                        pltpu.SemaphoreType.DMA((EXPERT_RING,))],
    )
    return pl.pallas_call(
        functools.partial(_experts_kernel, rb),
        grid_spec=grid_spec,
        out_shape=jax.ShapeDtypeStruct((n_rows * ROW_TILES, LANES), F32),
        compiler_params=_cparams(("arbitrary",)),
        name="experts",
    )(item_e, item_row, item_valid, item_new, item_slot, item_nxt, xg, w_eg, w_eu, w_ed)


def _combine_kernel(alpha, n_p, n_steps, dest0_ref, dest1_ref, dest2_ref, gate_ref, x1_ref, y_ref,
                    wsg_ref, wsu_ref, wsd_ref, g_ref, b_ref, x2p_ref, x2s_ref,
                    base_ref, x2_ref, *ring):
    step = pl.program_id(0)
    tt = x1_ref.shape[0]
    n_rows = TOP_K * tt * ROW_TILES
    group = COMBINE_GROUP
    n_groups = tt // group
    bufs, sem = ring[:-1], ring[-1]
    ahead = COMBINE_RING - 1

    def issue_group(dref, grp, slot):
        for r in range(group):
            t = grp * group + r
            for kk in range(TOP_K):
                _row_copy(y_ref, dref[t * TOP_K + kk], bufs[slot],
                          (kk * tt + t) * ROW_TILES, sem.at[slot]).start(priority=kk % 2)

    def issue_tile(dref, slot):
        def body(grp, carry):
            issue_group(dref, grp, slot)
            return carry

        lax.fori_loop(0, n_groups, body, 0)

    def reduce_group(grp, slot):
        rows = pl.ds(pl.multiple_of(grp * group, group), group)
        gates = gate_ref[rows, :]
        chunks = []
        for s in range(ROW_TILES):
            acc = jnp.zeros((group, LANES), F32)
            for kk in range(TOP_K):
                row0 = kk * tt * ROW_TILES + grp * (group * ROW_TILES) + s
                acc = acc + gates[:, kk:kk + 1] * bufs[slot][pl.ds(row0, group,
                                                                    stride=ROW_TILES), :]
            chunks.append(acc)
        routed = jnp.concatenate(chunks, axis=1)
        x2_ref[rows, :] = _layer_norm(base_ref[rows, :] + routed, g_ref[...], b_ref[...])

    @pl.when(step == 0)
    def _():
        issue_tile(dest0_ref, 0)
        if n_steps > 1:
            issue_tile(dest1_ref, 1)

    x1 = x1_ref[...]
    xb = x1.astype(BF16)
    gs = _dot(xb, wsg_ref[...])
    us = _dot(xb, wsu_ref[...])
    base_ref[...] = alpha * x1 + _dot((gs * _sigmoid(gs) * us).astype(BF16), wsd_ref[...])

    for slot in range(COMBINE_RING):
        here = lax.rem(step, COMBINE_RING) == slot
        ahead_slot = (slot + ahead) % COMBINE_RING

        @pl.when(here)
        def _(slot=slot):
            pltpu.make_async_copy(y_ref.at[pl.ds(0, n_rows)], bufs[slot], sem.at[slot]).wait()

        @pl.when(jnp.logical_and(here, step + ahead < n_steps))
        def _(slot=slot, ahead_slot=ahead_slot):
            def body(grp, carry):
                issue_group(dest2_ref, grp, ahead_slot)
                reduce_group(grp, slot)
                return carry

            lax.fori_loop(0, n_groups, body, 0)

        @pl.when(jnp.logical_and(here, step + ahead >= n_steps))
        def _(slot=slot):
            def body(grp, carry):
                reduce_group(grp, slot)
                return carry

            lax.fori_loop(0, n_groups, body, 0)

    @pl.when(step < n_p)
    def _():
        x2p_ref[...] = x2_ref[...]

    @pl.when(step >= n_p)
    def _():
        x2s_ref[...] = x2_ref[...]


def _combine(dest8, gates, x1, y, wsg, wsu, wsd, g, b, alpha, tt, mp):
    m = x1.shape[0]
    n_steps = m // tt
    n_p = mp // tt

    def dest_spec(ahead):
        return pl.BlockSpec((tt * TOP_K,), lambda i: (jnp.minimum(i + ahead, n_steps - 1),),
                            memory_space=pltpu.SMEM)

    return pl.pallas_call(
        functools.partial(_combine_kernel, alpha, n_p, n_steps),
        grid=(n_steps,),
        in_specs=[dest_spec(0), dest_spec(1), dest_spec(2),
                  pl.BlockSpec((tt, TOP_K), lambda i: (i, 0)),
                  pl.BlockSpec((tt, D_MODEL), lambda i: (i, 0)),
                  pl.BlockSpec(memory_space=pl.ANY),
                  pl.BlockSpec((D_MODEL, D_EXPERT), lambda i: (0, 0)),
                  pl.BlockSpec((D_MODEL, D_EXPERT), lambda i: (0, 0)),
                  pl.BlockSpec((D_EXPERT, D_MODEL), lambda i: (0, 0)),
                  pl.BlockSpec((1, D_MODEL), lambda i: (0, 0)),
                  pl.BlockSpec((1, D_MODEL), lambda i: (0, 0))],
        out_specs=[pl.BlockSpec((tt, D_MODEL), lambda i: (jnp.minimum(i, n_p - 1), 0)),
                   pl.BlockSpec((tt, D_MODEL), lambda i: (jnp.maximum(i - n_p, 0), 0))],
        out_shape=[jax.ShapeDtypeStruct((mp, D_MODEL), F32),
                   jax.ShapeDtypeStruct((m - mp, D_MODEL), F32)],
        scratch_shapes=[pltpu.VMEM((tt, D_MODEL), F32), pltpu.VMEM((tt, D_MODEL), F32)]
        + [pltpu.VMEM((TOP_K * tt * ROW_TILES, LANES), F32)] * COMBINE_RING
        + [pltpu.SemaphoreType.DMA((COMBINE_RING,))],
        compiler_params=_cparams(("arbitrary",)),
        name="combine",
    )(dest8, dest8, dest8, gates, x1, y, wsg, wsu, wsd, g, b)


PROJ_TM = 512
PROJ_TN = 1024
MIX_TT = 256
MIX_CHUNK = 128
SAMPLE_NSEQ = 8
ROUTER_TT = 512
DISPATCH_TT = 512
EXPERT_RB = 256
EXPERT_RING = 3
COMBINE_TT = 256
COMBINE_RING = 3
COMBINE_GROUP = 32


def _moe(x1, x1r, mp, w_router, b_router, w_eg, w_eu, w_ed, w_sg, w_su, w_sd, ln_g, ln_b, alpha):
    m = x1.shape[0]
    wr_hi, wr_lo = _split_hi_lo(w_router)
    idx, gates, rank, cnt = _router(x1, wr_hi, wr_lo, b_router[None, :], ROUTER_TT)

    rb = EXPERT_RB
    n_rows = m * TOP_K
    n_items = n_rows // rb + N_EXPERTS
    counts = cnt[0].astype(jnp.int32)
    seg_end = jnp.cumsum(counts)
    seg_start = seg_end - counts
    n_chunk = (counts + rb - 1) // rb
    item_end = jnp.cumsum(n_chunk)
    item_start = item_end - n_chunk
    item = jnp.arange(n_items, dtype=jnp.int32)
    item_c = jnp.minimum(item, item_end[-1] - 1)
    item_e = jnp.sum((item_end[None, :] <= item_c[:, None]).astype(jnp.int32), axis=1)
    onehot_e = item_e[:, None] == jnp.arange(N_EXPERTS, dtype=jnp.int32)[None, :]

    def pick(table):
        return jnp.sum(jnp.where(onehot_e, table[None, :], 0), axis=1)

    chunk = item_c - pick(item_start)
    valid = item < item_end[-1]
    item_row = pick(seg_start) + chunk * rb
    item_valid = jnp.where(valid, jnp.clip(pick(counts) - chunk * rb, 0, rb), 0)

    experts = jnp.arange(N_EXPERTS, dtype=jnp.int32)
    nonempty = counts > 0
    slot_e = (jnp.cumsum(nonempty.astype(jnp.int32)) - 1) % 2
    later = jnp.logical_and(experts[None, :] > experts[:, None], nonempty[None, :])
    nxt_e = jnp.min(jnp.where(later, experts[None, :], N_EXPERTS), axis=1)
    nxt_e = jnp.where(nxt_e == N_EXPERTS, -1, nxt_e)
    item_new = jnp.logical_and(valid, chunk == 0).astype(jnp.int32)

    dest8 = _dest(idx.reshape(-1, LANES), rank.reshape(-1, LANES), seg_start).reshape(-1)
    xg = _dispatch(dest8, x1r, DISPATCH_TT, rb)
    y = _experts(item_e, item_row, item_valid, item_new, pick(slot_e), pick(nxt_e),
                 xg, w_eg, w_eu, w_ed, rb)
    return _combine(dest8, gates, x1, y, w_sg.astype(BF16), w_su.astype(BF16),
                    w_sd.astype(BF16), ln_g[None, :], ln_b[None, :], alpha, COMBINE_TT, mp)


def _layer(xp, xs, conv_s, c_s, n_s, m_s, w_in, b_gates, w_conv, mh_gain, w_out, ln1_g, ln1_b,
           w_router, b_router, w_eg, w_eu, w_ed, w_sg, w_su, w_sd, ln2_g, ln2_b, alpha):
    bp, tp, _ = xp.shape
    bs, ts, _ = xs.shape
    mp, ms = bp * tp, bs * ts
    xp2 = xp.reshape(mp, D_MODEL)
    xs2 = xs.reshape(ms, D_MODEL)

    w_in_bf = w_in.astype(BF16)
    w_main = jnp.concatenate([w_in_bf[:, :GATE_COL0], w_in_bf[:, GATE_COL0 + 2 * N_HEADS:]],
                             axis=1)
    w_gate = jnp.pad(w_in[:, GATE_COL0:GATE_COL0 + 2 * N_HEADS],
                     ((0, 0), (0, LANES - 2 * N_HEADS)))
    wg_hi, wg_lo = _split_hi_lo(w_gate)

    z_s = _proj(xs2, w_main, PROJ_TM, PROJ_TN)
    zg_s = _gate_proj(xs2, wg_hi, wg_lo, PROJ_TM)
    gt_s = zg_s[:, :SUBLANES].reshape(bs, ts, SUBLANES).transpose(0, 2, 1)
    mix_s, conv_n, c_n, n_n, m_n = _mixer(
        z_s, zg_s, gt_s, b_gates, w_conv, mh_gain[None, :],
        conv_s, c_s, n_s, m_s[:, None, :], nb=bs, ts=ts, nseq=SAMPLE_NSEQ)

    x1, x1r, conv_p, c_p, n_p, m_p = _prompt_layer_half(
        xp2, xs2, mix_s, b_gates, w_main, wg_hi, wg_lo, w_conv, mh_gain[None, :],
        w_out.astype(BF16), ln1_g[None, :], ln1_b[None, :], alpha,
        nb=bp, nt=tp // MIX_TT, tt=MIX_TT, chunk=MIX_CHUNK)

    x2p, x2s = _moe(x1, x1r, mp, w_router, b_router, w_eg, w_eu, w_ed, w_sg, w_su, w_sd,
                    ln2_g, ln2_b, alpha)
    states_p = (conv_p, c_p, n_p, m_p[:, 0, :])
    states_s = (conv_n, c_n, n_n, m_n[:, 0, :])
    return x2p.reshape(bp, tp, D_MODEL), x2s.reshape(bs, ts, D_MODEL), states_p, states_s


def kernel(x_prompt, x_sample, cache_conv, state_mlstm_C, state_mlstm_n, state_mlstm_m, w_in, b_gates, w_conv, mh_gain, w_out, ln1_g, ln1_b, w_router, b_router, w_exp_gate, w_exp_up, w_exp_down, w_sh_gate, w_sh_up, w_sh_down, ln2_g, ln2_b):
    depth = w_in.shape[0]
    alpha = (2.0 * depth) ** 0.25
    hp, hs = x_prompt, x_sample
    outs_p = [[], [], [], []]
    outs_s = [[], [], [], []]
    for l in range(depth):
        hp, hs, st_p, st_s = _layer(
            hp, hs, cache_conv[l], state_mlstm_C[l], state_mlstm_n[l], state_mlstm_m[l],
            w_in[l], b_gates[l], w_conv[l], mh_gain[l], w_out[l], ln1_g[l], ln1_b[l],
            w_router[l], b_router[l], w_exp_gate[l], w_exp_up[l], w_exp_down[l],
            w_sh_gate[l], w_sh_up[l], w_sh_down[l], ln2_g[l], ln2_b[l], alpha)
        for acc, val in zip(outs_p, st_p):
            acc.append(val)
        for acc, val in zip(outs_s, st_s):
            acc.append(val)
    return (hp, hs) + tuple(jnp.stack(a) for a in outs_p) + tuple(jnp.stack(a) for a in outs_s)
```

```python
import functools

import jax
import jax.numpy as jnp
from jax import lax
from jax.experimental import pallas as pl
from jax.experimental.pallas import tpu as pltpu

F32 = jnp.float32
BF16 = jnp.bfloat16

D_MODEL = 1024
N_HEADS = 4
HEAD_V = 256
HEAD_QK = 128
N_EXPERTS = 256
TOP_K = 8
D_EXPERT = 256
ROUTED_SCALE = 2.5
LN_EPS = 1e-5
CONV_K = 3

LANES = 128
SUBLANES = 8
ROW_TILES = D_MODEL // LANES
N_GROUPS_A = 6
N_GROUPS_B = 2
GATE_COL0 = N_GROUPS_A * D_MODEL

VMEM_LIMIT = 56 * 1024 * 1024


def _cparams(sem):
    return pltpu.CompilerParams(dimension_semantics=sem, vmem_limit_bytes=VMEM_LIMIT)


def _sigmoid(x):
    return 1.0 / (1.0 + jnp.exp(-x))


def _log_sigmoid(x):
    return jnp.minimum(x, 0.0) - jnp.log(1.0 + jnp.exp(-jnp.abs(x)))


def _layer_norm(x, g, b):
    mu = jnp.mean(x, axis=-1, keepdims=True)
    xc = x - mu
    var = jnp.mean(xc * xc, axis=-1, keepdims=True)
    return xc * lax.rsqrt(var + LN_EPS) * g + b


def _split_hi_lo(x):
    hi = x.astype(BF16)
    lo = (x - hi.astype(F32)).astype(BF16)
    return hi, lo


def _dot(a, b):
    return jnp.dot(a, b, preferred_element_type=F32)


def _store_rows(ref, v):
    n = v.shape[0]
    for s in range(ROW_TILES):
        ref[pl.ds(s, n, stride=ROW_TILES), :] = v[:, s * LANES:(s + 1) * LANES]


def _load_row_chunks(ref, n, row0=0):
    return [ref[pl.ds(row0 + s, n, stride=ROW_TILES), :] for s in range(ROW_TILES)]


def _dot3(x, w_hi, w_lo):
    x_hi, x_lo = _split_hi_lo(x)
    return _dot(x_hi, w_hi) + _dot(x_lo, w_hi) + _dot(x_hi, w_lo)


def _proj_kernel(x_ref, w_ref, z_ref):
    z_ref[...] = _dot(x_ref[...].astype(BF16), w_ref[...])


def _proj(x, w, tm, tn):
    m, k = x.shape
    n = w.shape[1]
    return pl.pallas_call(
        _proj_kernel,
        grid=(m // tm, n // tn),
        in_specs=[pl.BlockSpec((tm, k), lambda i, j: (i, 0)),
                  pl.BlockSpec((k, tn), lambda i, j: (0, j))],
        out_specs=pl.BlockSpec((tm, tn), lambda i, j: (i, j)),
        out_shape=jax.ShapeDtypeStruct((m, n), F32),
        compiler_params=_cparams(("parallel", "arbitrary")),
        name="proj",
    )(x, w)


def _gate_proj_kernel(x_ref, wh_ref, wl_ref, z_ref):
    z_ref[...] = _dot3(x_ref[...], wh_ref[...], wl_ref[...])


def _gate_proj(x, w_hi, w_lo, tm):
    m, k = x.shape
    n = w_hi.shape[1]
    return pl.pallas_call(
        _gate_proj_kernel,
        grid=(m // tm,),
        in_specs=[pl.BlockSpec((tm, k), lambda i: (i, 0)),
                  pl.BlockSpec((k, n), lambda i: (0, 0)),
                  pl.BlockSpec((k, n), lambda i: (0, 0))],
        out_specs=pl.BlockSpec((tm, n), lambda i: (i, 0)),
        out_shape=jax.ShapeDtypeStruct((m, n), F32),
        compiler_params=_cparams(("parallel",)),
        name="gate_proj",
    )(x, w_hi, w_lo)


def _conv_branch(u, zcb, wc, conv_ref, seq=0):
    tt = u.shape[0]
    carry = conv_ref[seq]
    rows = lax.broadcasted_iota(jnp.int32, (tt, D_MODEL), 0)
    u1 = jnp.where(rows == 0, carry[1:2, :], pltpu.roll(u, 1, 0))
    u2 = jnp.where(rows == 0, carry[0:1, :],
                   jnp.where(rows == 1, carry[1:2, :], pltpu.roll(u, 2, 0)))
    conv_out = u2 * wc[0:1, :] + u1 * wc[1:2, :] + u * wc[2:3, :]
    conv_ref[seq] = u[tt - 2:tt, :]
    return zcb * conv_out


def _mlstm_chunk(q, k, v, ig_col, lf_col, ig_row, lf_row, c_state, n_row, m_prev):
    chunk = q.shape[0]
    ti = lax.broadcasted_iota(jnp.int32, (chunk, chunk), 0)
    si = lax.broadcasted_iota(jnp.int32, (chunk, chunk), 1)
    causal = si <= ti
    b_col = jnp.sum(jnp.where(causal, lf_row, 0.0), axis=1, keepdims=True)
    b_row = jnp.sum(jnp.where(ti <= si, lf_col, 0.0), axis=0, keepdims=True)
    inter = b_col + m_prev
    dmat = jnp.where(causal, b_col - b_row + ig_row, -jnp.inf)
    m_t = jnp.maximum(inter, jnp.max(dmat, axis=1, keepdims=True))
    w_intra = jnp.exp(dmat - m_t)
    w_inter = jnp.exp(inter - m_t)
    qb = q.astype(BF16)
    kb = k.astype(BF16)
    vb = v.astype(BF16)
    s = lax.dot_general(qb, kb, (((1,), (1,)), ((), ())), preferred_element_type=F32) * w_intra
    num = w_inter * _dot(qb, c_state.astype(BF16)) + _dot(s.astype(BF16), vb)
    den = (w_inter * jnp.sum(q * n_row, axis=1, keepdims=True)
           + jnp.sum(s, axis=1, keepdims=True))
    h = num / jnp.maximum(jnp.abs(den), jnp.exp(-m_t))
    m_new = m_t[chunk - 1:chunk, :]
    b_last = b_col[chunk - 1:chunk, :]
    ws_col = jnp.exp(b_last - b_col + ig_col - m_new)
    cdecay = jnp.exp(inter[chunk - 1:chunk, :] - m_new)
    kw = k * ws_col
    c_new = cdecay * c_state + lax.dot_general(
        kw.astype(BF16), vb, (((0,), (0,)), ((), ())), preferred_element_type=F32)
    n_new = cdecay * n_row + jnp.sum(kw, axis=0, keepdims=True)
    return h, c_new, n_new, m_new


def _mlstm_branch(chunk, bg_ref, zg, gt, zqk, zv, zog, gain_ref, c_ref, n_ref, m_ref, mo_ref,
                  seq=0, row0=0):
    tt = zqk.shape[0]
    for h in range(N_HEADS):
        ig_col_all = zg[:, h:h + 1] + bg_ref[h]
        lf_col_all = _log_sigmoid(zg[:, N_HEADS + h:N_HEADS + h + 1] + bg_ref[N_HEADS + h])
        ig_row_all = gt[h:h + 1, :] + bg_ref[h]
        lf_row_all = _log_sigmoid(gt[N_HEADS + h:N_HEADS + h + 1, :] + bg_ref[N_HEADS + h])
        vcols = slice(h * HEAD_V, (h + 1) * HEAD_V)
        for c in range(tt // chunk):
            r0, r1 = c * chunk, (c + 1) * chunk
            q = zqk[r0:r1, h * HEAD_QK:(h + 1) * HEAD_QK]
            k = zqk[r0:r1, (N_HEADS + h) * HEAD_QK:(N_HEADS + h + 1) * HEAD_QK] * (HEAD_QK ** -0.5)
            hh, c_new, n_new, m_new = _mlstm_chunk(
                q, k, zv[r0:r1, vcols], ig_col_all[r0:r1, :], lf_col_all[r0:r1, :],
                ig_row_all[:, r0:r1], lf_row_all[:, r0:r1],
                c_ref[seq, h], n_ref[seq, h:h + 1, :], m_ref[seq, :, h:h + 1])
            c_ref[seq, h] = c_new
            n_ref[seq, h:h + 1, :] = n_new
            m_ref[seq, :, h:h + 1] = m_new
            mu = jnp.mean(hh, axis=-1, keepdims=True)
            hc = hh - mu
            var = jnp.mean(hc * hc, axis=-1, keepdims=True)
            hn = hc * lax.rsqrt(var + LN_EPS) * gain_ref[:, vcols]
            mo_ref[row0 + r0:row0 + r1, vcols] = hn * _sigmoid(zog[r0:r1, vcols])


def _store_x1(x, mix, wout_ref, g_ref, b_ref, alpha, x1_ref, x1r_ref):
    x1 = _layer_norm(alpha * x + _dot(mix.astype(BF16), wout_ref[...]), g_ref[...], b_ref[...])
    x1_ref[...] = x1
    _store_rows(x1r_ref, x1)


def _mixer_kernel(ts, bg_ref,
                  zcb_ref, zcc_ref, zch_ref, zqk_ref, zv_ref, zog_ref, zga_ref, zgb_ref,
                  zg_ref, gt_ref, wconv_ref, gain_ref,
                  conv0_ref, c0_ref, n0_ref, m0_ref,
                  mix_ref, conv_ref, c_ref, n_ref, m_ref, mo_ref):
    conv_ref[...] = conv0_ref[...]
    c_ref[...] = c0_ref[...]
    n_ref[...] = n0_ref[...]
    m_ref[...] = m0_ref[...]
    for seq in range(conv_ref.shape[0]):
        rows = slice(seq * ts, (seq + 1) * ts)
        a = _conv_branch(zcc_ref[rows, :] * zch_ref[rows, :], zcb_ref[rows, :], wconv_ref[...],
                         conv_ref, seq)
        _mlstm_branch(ts, bg_ref, zg_ref[rows, :], gt_ref[seq], zqk_ref[rows, :], zv_ref[rows, :],
                      zog_ref[rows, :], gain_ref, c_ref, n_ref, m_ref, mo_ref, seq, seq * ts)
        mix_ref[rows, :] = (_sigmoid(zga_ref[rows, :]) * a
                            + _sigmoid(zgb_ref[rows, :]) * mo_ref[rows, :])


def _mixer(z_a, z_b, zg, gt3, b_gates, w_conv, mh_gain, conv0, c0, n0, m0, *, nb, ts, nseq):
    tt = nseq * ts
    groups = [(z_a, j) for j in range(N_GROUPS_A)] + [(z_b, j) for j in range(N_GROUPS_B)]

    in_specs = [pl.BlockSpec(memory_space=pltpu.SMEM)]
    in_specs += [pl.BlockSpec((tt, D_MODEL), lambda b, j=j: (b, j)) for _, j in groups]
    state_specs = [
        pl.BlockSpec((nseq, CONV_K - 1, D_MODEL), lambda b: (b, 0, 0)),
        pl.BlockSpec((nseq, N_HEADS, HEAD_QK, HEAD_V), lambda b: (b, 0, 0, 0)),
        pl.BlockSpec((nseq, N_HEADS, HEAD_QK), lambda b: (b, 0, 0)),
        pl.BlockSpec((nseq, 1, N_HEADS), lambda b: (b, 0, 0)),
    ]
    in_specs += [
        pl.BlockSpec((tt, LANES), lambda b: (b, 0)),
        pl.BlockSpec((nseq, SUBLANES, ts), lambda b: (b, 0, 0)),
        pl.BlockSpec((CONV_K, D_MODEL), lambda b: (0, 0)),
        pl.BlockSpec((1, D_MODEL), lambda b: (0, 0)),
    ] + state_specs
    out_specs = [pl.BlockSpec((tt, D_MODEL), lambda b: (b, 0))] + state_specs
    out_shape = [
        jax.ShapeDtypeStruct((nb * ts, D_MODEL), F32),
        jax.ShapeDtypeStruct((nb, CONV_K - 1, D_MODEL), F32),
        jax.ShapeDtypeStruct((nb, N_HEADS, HEAD_QK, HEAD_V), F32),
        jax.ShapeDtypeStruct((nb, N_HEADS, HEAD_QK), F32),
        jax.ShapeDtypeStruct((nb, 1, N_HEADS), F32),
    ]
    return pl.pallas_call(
        functools.partial(_mixer_kernel, ts),
        grid=(nb // nseq,),
        in_specs=in_specs,
        out_specs=out_specs,
        out_shape=out_shape,
        scratch_shapes=[pltpu.VMEM((tt, D_MODEL), F32)],
        compiler_params=_cparams(("parallel",)),
        name="mixer_sample",
    )(b_gates, *[z for z, _ in groups], zg, gt3, w_conv, mh_gain, conv0, c0, n0, m0)


def _prompt_kernel(chunk, nb, ns, alpha, bg_ref, x_ref, wa_ref, wb_ref, wgh_ref, wgl_ref, wconv_ref,
                   gain_ref, wout_ref, g_ref, b_ref, xs_ref, mixs_ref,
                   x1_ref, x1r_ref, conv_ref, c_ref, n_ref, m_ref, mo_ref):
    b = pl.program_id(0)
    t = pl.program_id(1)

    @pl.when(b < nb)
    def _():
        @pl.when(t == 0)
        def _():
            conv_ref[...] = jnp.zeros_like(conv_ref)
            c_ref[...] = jnp.zeros_like(c_ref)
            n_ref[...] = jnp.zeros_like(n_ref)
            m_ref[...] = jnp.zeros_like(m_ref)

        x = x_ref[...]
        xb = x.astype(BF16)

        def z(j):
            w_ref, jj = (wa_ref, j) if j < N_GROUPS_A else (wb_ref, j - N_GROUPS_A)
            return _dot(xb, w_ref[:, jj * D_MODEL:(jj + 1) * D_MODEL])

        a = _conv_branch(z(1) * z(2), z(0), wconv_ref[...], conv_ref)
        zg = _dot3(x, wgh_ref[...], wgl_ref[...])
        _mlstm_branch(chunk, bg_ref, zg, zg.T, z(3), z(4), z(5), gain_ref,
                      c_ref, n_ref, m_ref, mo_ref)
        mix = _sigmoid(z(6)) * a + _sigmoid(z(7)) * mo_ref[...]
        _store_x1(x, mix, wout_ref, g_ref, b_ref, alpha, x1_ref, x1r_ref)

    @pl.when(jnp.logical_and(b == nb, t < ns))
    def _():
        _store_x1(xs_ref[...], mixs_ref[...], wout_ref, g_ref, b_ref, alpha, x1_ref, x1r_ref)


def _prompt_layer_half(xp, xs, mix_s, b_gates, w_a, w_b, wg_hi, wg_lo, w_conv, mh_gain, w_out,
                       ln_g, ln_b, alpha, *, nb, nt, tt, chunk):
    mp, ms = xp.shape[0], xs.shape[0]
    ns = ms // tt
    last_p = nb * nt - 1

    def prow(b, t):
        return jnp.minimum(b * nt + t, last_p)

    def srow(b, t):
        return jnp.where(b == nb, jnp.minimum(t, ns - 1), 0)

    def orow(b, t):
        return jnp.where(b == nb, nb * nt + jnp.minimum(t, ns - 1), b * nt + t)

    def state(b):
        return jnp.minimum(b, nb - 1)

    const2 = lambda b, t: (0, 0)
    in_specs = [
        pl.BlockSpec(memory_space=pltpu.SMEM),
        pl.BlockSpec((tt, D_MODEL), lambda b, t: (prow(b, t), 0)),
        pl.BlockSpec(w_a.shape, const2, pipeline_mode=pl.Buffered(1)),
        pl.BlockSpec(w_b.shape, const2, pipeline_mode=pl.Buffered(1)),
        pl.BlockSpec(wg_hi.shape, const2),
        pl.BlockSpec(wg_lo.shape, const2),
        pl.BlockSpec((CONV_K, D_MODEL), const2),
        pl.BlockSpec((1, D_MODEL), const2),
        pl.BlockSpec((D_MODEL, D_MODEL), const2),
        pl.BlockSpec((1, D_MODEL), const2),
        pl.BlockSpec((1, D_MODEL), const2),
        pl.BlockSpec((tt, D_MODEL), lambda b, t: (srow(b, t), 0)),
        pl.BlockSpec((tt, D_MODEL), lambda b, t: (srow(b, t), 0)),
    ]
    out_specs = [
        pl.BlockSpec((tt, D_MODEL), lambda b, t: (orow(b, t), 0)),
        pl.BlockSpec((tt * ROW_TILES, LANES), lambda b, t: (orow(b, t), 0)),
        pl.BlockSpec((1, CONV_K - 1, D_MODEL), lambda b, t: (state(b), 0, 0)),
        pl.BlockSpec((1, N_HEADS, HEAD_QK, HEAD_V), lambda b, t: (state(b), 0, 0, 0)),
        pl.BlockSpec((1, N_HEADS, HEAD_QK), lambda b, t: (state(b), 0, 0)),
        pl.BlockSpec((1, 1, N_HEADS), lambda b, t: (state(b), 0, 0)),
    ]
    out_shape = [
        jax.ShapeDtypeStruct((mp + ms, D_MODEL), F32),
        jax.ShapeDtypeStruct(((mp + ms) * ROW_TILES, LANES), F32),
        jax.ShapeDtypeStruct((nb, CONV_K - 1, D_MODEL), F32),
        jax.ShapeDtypeStruct((nb, N_HEADS, HEAD_QK, HEAD_V), F32),
        jax.ShapeDtypeStruct((nb, N_HEADS, HEAD_QK), F32),
        jax.ShapeDtypeStruct((nb, 1, N_HEADS), F32),
    ]
    return pl.pallas_call(
        functools.partial(_prompt_kernel, chunk, nb, ns, alpha),
        grid=(nb + 1, nt),
        in_specs=in_specs,
        out_specs=out_specs,
        out_shape=out_shape,
        scratch_shapes=[pltpu.VMEM((tt, D_MODEL), F32)],
        compiler_params=_cparams(("arbitrary", "arbitrary")),
        name="prompt_half",
    )(b_gates, xp, w_a, w_b, wg_hi, wg_lo, w_conv, mh_gain, w_out, ln_g, ln_b, xs, mix_s)


def _router_kernel(x_ref, wh_ref, wl_ref, br_ref, idx_ref, gate_ref, rank_ref, cnt_ref):
    tt = x_ref.shape[0]

    @pl.when(pl.program_id(0) == 0)
    def _():
        cnt_ref[...] = jnp.zeros_like(cnt_ref)

    scores = _sigmoid(_dot3(x_ref[...], wh_ref[...], wl_ref[...]))
    lane = lax.broadcasted_iota(jnp.int32, (tt, N_EXPERTS), 1)
    work = scores + br_ref[...]
    picked = []
    sel = []
    member = jnp.zeros((tt, N_EXPERTS), F32)
    for _ in range(TOP_K):
        mx = jnp.max(work, axis=1, keepdims=True)
        ik = jnp.min(jnp.where(work == mx, lane, N_EXPERTS), axis=1, keepdims=True)
        onehot = lane == ik
        picked.append((ik, onehot))
        sel.append(jnp.sum(jnp.where(onehot, scores, 0.0), axis=1, keepdims=True))
        work = jnp.where(onehot, -jnp.inf, work)
        member = member + onehot.astype(F32)
    total = sel[0]
    for sk in sel[1:]:
        total = total + sk

    ti = lax.broadcasted_iota(jnp.int32, (tt, tt), 0)
    si = lax.broadcasted_iota(jnp.int32, (tt, tt), 1)
    earlier = (si < ti).astype(BF16)
    before = _dot(earlier, member.astype(BF16)) + cnt_ref[...]

    lane_o = lax.broadcasted_iota(jnp.int32, (tt, TOP_K), 1)
    idx_o = jnp.zeros((tt, TOP_K), jnp.int32)
    gate_o = jnp.zeros((tt, TOP_K), F32)
    rank_o = jnp.zeros((tt, TOP_K), jnp.int32)
    for kk in range(TOP_K):
        ik, onehot = picked[kk]
        rk = jnp.sum(jnp.where(onehot, before, 0.0), axis=1, keepdims=True)
        idx_o = jnp.where(lane_o == kk, ik, idx_o)
        gate_o = jnp.where(lane_o == kk, sel[kk] / total * ROUTED_SCALE, gate_o)
        rank_o = jnp.where(lane_o == kk, rk.astype(jnp.int32), rank_o)
    idx_ref[...] = idx_o
    gate_ref[...] = gate_o
    rank_ref[...] = rank_o
    cnt_ref[...] = cnt_ref[...] + jnp.sum(member, axis=0, keepdims=True)


def _router(x1, wr_hi, wr_lo, b_router, tt):
    m = x1.shape[0]
    return pl.pallas_call(
        _router_kernel,
        grid=(m // tt,),
        in_specs=[pl.BlockSpec((tt, D_MODEL), lambda i: (i, 0)),
                  pl.BlockSpec((D_MODEL, N_EXPERTS), lambda i: (0, 0)),
                  pl.BlockSpec((D_MODEL, N_EXPERTS), lambda i: (0, 0)),
                  pl.BlockSpec((1, N_EXPERTS), lambda i: (0, 0))],
        out_specs=[pl.BlockSpec((tt, TOP_K), lambda i: (i, 0)),
                   pl.BlockSpec((tt, TOP_K), lambda i: (i, 0)),
                   pl.BlockSpec((tt, TOP_K), lambda i: (i, 0)),
                   pl.BlockSpec((1, N_EXPERTS), lambda i: (0, 0))],
        out_shape=[jax.ShapeDtypeStruct((m, TOP_K), jnp.int32),
                   jax.ShapeDtypeStruct((m, TOP_K), F32),
                   jax.ShapeDtypeStruct((m, TOP_K), jnp.int32),
                   jax.ShapeDtypeStruct((1, N_EXPERTS), F32)],
        compiler_params=_cparams(("arbitrary",)),
        name="router",
    )(x1, wr_hi, wr_lo, b_router)


def _row_copy(src, src_row8, dst, dst_row8, sem):
    return pltpu.make_async_copy(src.at[pl.ds(pl.multiple_of(src_row8, ROW_TILES), ROW_TILES)],
                                 dst.at[pl.ds(pl.multiple_of(dst_row8, ROW_TILES), ROW_TILES)],
                                 sem)


def _dest_kernel(idx_ref, rank_ref, start_ref, dest_ref):
    idx = idx_ref[...]
    shape = idx.shape
    low = jnp.broadcast_to(start_ref[:, :LANES], shape)
    high = jnp.broadcast_to(start_ref[:, LANES:], shape)
    lane = jnp.bitwise_and(idx, LANES - 1)
    start = jnp.where(idx < LANES, jnp.take_along_axis(low, lane, axis=1),
                      jnp.take_along_axis(high, lane, axis=1))
    dest_ref[...] = (start + rank_ref[...]) * ROW_TILES


def _dest(idx, rank, seg_start):
    assert N_EXPERTS == 2 * LANES
    full = pl.BlockSpec(idx.shape, lambda i: (0, 0))
    return pl.pallas_call(
        _dest_kernel,
        grid=(1,),
        in_specs=[full, full, pl.BlockSpec((1, N_EXPERTS), lambda i: (0, 0))],
        out_specs=full,
        out_shape=jax.ShapeDtypeStruct(idx.shape, jnp.int32),
        compiler_params=_cparams(("arbitrary",)),
        name="dest",
    )(idx, rank, seg_start[None, :])


def _dispatch_kernel(tt, dest_ref, x1r_ref, xg_ref, zbuf, sem, zsem):
    @pl.when(pl.program_id(0) == 0)
    def _():
        zbuf[...] = jnp.zeros_like(zbuf)
        n_sorted = xg_ref.shape[0] - zbuf.shape[0]
        fill = pltpu.make_async_copy(zbuf, xg_ref.at[pl.ds(n_sorted, zbuf.shape[0])], zsem)
        fill.start()
        fill.wait()

    def issue(t, carry):
        for kk in range(TOP_K):
            _row_copy(x1r_ref, t * ROW_TILES, xg_ref, dest_ref[t * TOP_K + kk],
                      sem).start(priority=kk % 2)
        return carry

    lax.fori_loop(0, tt, issue, 0)
    n_rows = tt * TOP_K * ROW_TILES
    pltpu.make_async_copy(xg_ref.at[pl.ds(0, n_rows)], xg_ref.at[pl.ds(0, n_rows)], sem).wait()


def _dispatch(dest8, x1r, tt, slack):
    m = x1r.shape[0] // ROW_TILES
    return pl.pallas_call(
        functools.partial(_dispatch_kernel, tt),
        grid=(m // tt,),
        in_specs=[pl.BlockSpec((tt * TOP_K,), lambda i: (i,), memory_space=pltpu.SMEM),
                  pl.BlockSpec((tt * ROW_TILES, LANES), lambda i: (i, 0))],
        out_specs=pl.BlockSpec(memory_space=pl.ANY),
        out_shape=jax.ShapeDtypeStruct(((m * TOP_K + slack) * ROW_TILES, LANES), F32),
        scratch_shapes=[pltpu.VMEM((slack * ROW_TILES, LANES), F32),
                        pltpu.SemaphoreType.DMA(()), pltpu.SemaphoreType.DMA(())],
        compiler_params=_cparams(("arbitrary",)),
        name="dispatch",
    )(dest8, x1r)


def _experts_kernel(rb, e_ref, row_ref, valid_ref, new_ref, slot_ref, nxt_ref,
                    xg_hbm, wg_hbm, wu_hbm, wd_hbm, y_hbm,
                    *scratch):
    j = pl.program_id(0)
    n_steps = pl.num_programs(0)
    ring_pos = lax.rem(j, EXPERT_RING)
    valid = valid_ref[j]
    xbufs = scratch[:EXPERT_RING]
    ybufs = scratch[EXPERT_RING:2 * EXPERT_RING]
    wg_buf, wu_buf, wd_buf, wgb, wub, wdb, wsem, xsem, ysem = scratch[2 * EXPERT_RING:]

    def tile_rows(row, n):
        return pl.ds(pl.multiple_of(row * ROW_TILES, ROW_TILES), n * ROW_TILES)

    row_variants = [rb]

    def for_row_variant(n_valid, fn):
        for below, n in zip([0] + row_variants[:-1], row_variants):
            @pl.when(jnp.logical_and(n_valid > below, n_valid <= n))
            def _(n=n):
                fn(n)

    def x_copies(item, slot, start):
        def run(n):
            copy = pltpu.make_async_copy(xg_hbm.at[tile_rows(row_ref[item], n)],
                                         xbufs[slot].at[pl.ds(0, n * ROW_TILES)], xsem.at[slot])
            if start:
                copy.start()
            else:
                copy.wait()

        for_row_variant(valid_ref[item], run)

    def y_copies(item, slot, start):
        row0 = row_ref[item]
        n = valid_ref[item]

        def run(copy):
            if start:
                copy.start()
            else:
                copy.wait()

        @pl.when(n == rb)
        def _():
            run(pltpu.make_async_copy(ybufs[slot], y_hbm.at[tile_rows(row0, rb)], ysem.at[slot]))

        @pl.when(n < rb)
        def _():
            piece = rb // 2
            while piece >= 1:
                off = n - jnp.bitwise_and(n, 2 * piece - 1)

                @pl.when(jnp.bitwise_and(n, piece) != 0)
                def _(off=off, piece=piece):
                    run(pltpu.make_async_copy(ybufs[slot].at[tile_rows(off, piece)],
                                              y_hbm.at[tile_rows(row0 + off, piece)],
                                              ysem.at[slot]))

                piece //= 2

    def fetch(e, slot):
        return (pltpu.make_async_copy(wg_hbm.at[e], wg_buf.at[slot], wsem.at[slot, 0]),
                pltpu.make_async_copy(wu_hbm.at[e], wu_buf.at[slot], wsem.at[slot, 1]),
                pltpu.make_async_copy(wd_hbm.at[e], wd_buf.at[slot], wsem.at[slot, 2]))

    ahead = EXPERT_RING - 1

    @pl.when(j == 0)
    def _():
        for item in range(ahead):
            x_copies(item, item, start=True)

    ahead_item = jnp.minimum(j + ahead, n_steps - 1)
    for slot in range(EXPERT_RING):
        here = ring_pos == slot

        @pl.when(jnp.logical_and(here, j >= EXPERT_RING))
        def _(slot=slot):
            y_copies(j - EXPERT_RING, slot, start=False)

        @pl.when(jnp.logical_and(here, valid > 0))
        def _(slot=slot):
            x_copies(j, slot, start=False)

            @pl.when(j + ahead < n_steps)
            def _():
                x_copies(ahead_item, (slot + ahead) % EXPERT_RING, start=True)

    @pl.when(new_ref[j] != 0)
    def _():
        slot = slot_ref[j]

        @pl.when(j == 0)
        def _():
            for copy in fetch(e_ref[j], slot):
                copy.start()

        for copy in fetch(e_ref[j], slot):
            copy.wait()
        wgb[...] = wg_buf[slot].astype(BF16)
        wub[...] = wu_buf[slot].astype(BF16)
        wdb[...] = wd_buf[slot].astype(BF16)

        @pl.when(nxt_ref[j] >= 0)
        def _():
            for copy in fetch(nxt_ref[j], 1 - slot):
                copy.start()

    def swiglu_rows(slot, n):
        xb = jnp.concatenate(_load_row_chunks(xbufs[slot], n), axis=1).astype(BF16)
        g = _dot(xb, wgb[...])
        u = _dot(xb, wub[...])
        y = _dot((g * _sigmoid(g) * u).astype(BF16), wdb[...])
        _store_rows(ybufs[slot], y)

    for slot in range(EXPERT_RING):
        here = ring_pos == slot

        @pl.when(jnp.logical_and(here, valid > 0))
        def _(slot=slot):
            for_row_variant(valid, functools.partial(swiglu_rows, slot))
            y_copies(j, slot, start=True)

        @pl.when(jnp.logical_and(here, j == n_steps - 1))
        def _(slot=slot):
            for back in range(EXPERT_RING - 1, -1, -1):
                y_copies(j - back, (slot - back) % EXPERT_RING, start=False)


def _experts(item_e, item_row, item_valid, item_new, item_slot, item_nxt,
             xg, w_eg, w_eu, w_ed, rb):
    n_items = item_e.shape[0]
    n_rows = xg.shape[0] // ROW_TILES - rb
    hbm = pl.BlockSpec(memory_space=pl.ANY)
    row_buf = pltpu.VMEM((rb * ROW_TILES, LANES), F32)
    grid_spec = pltpu.PrefetchScalarGridSpec(
        num_scalar_prefetch=6,
        grid=(n_items,),
        in_specs=[hbm, hbm, hbm, hbm],
        out_specs=hbm,
        scratch_shapes=[row_buf] * (2 * EXPERT_RING) + [
                        pltpu.VMEM((2, D_MODEL, D_EXPERT), F32),
                        pltpu.VMEM((2, D_MODEL, D_EXPERT), F32),
                        pltpu.VMEM((2, D_EXPERT, D_MODEL), F32),
                        pltpu.VMEM((D_MODEL, D_EXPERT), BF16),
                        pltpu.VMEM((D_MODEL, D_EXPERT), BF16),
                        pltpu.VMEM((D_EXPERT, D_MODEL), BF16),
                        pltpu.SemaphoreType.DMA((2, 3)),
                        pltpu.SemaphoreType.DMA((EXPERT_RING,)),
                        pltpu.SemaphoreType.DMA((EXPERT_RING,))],
    )
    return pl.pallas_call(
        functools.partial(_experts_kernel, rb),
        grid_spec=grid_spec,
        out_shape=jax.ShapeDtypeStruct((n_rows * ROW_TILES, LANES), F32),
        compiler_params=_cparams(("arbitrary",)),
        name="experts",
    )(item_e, item_row, item_valid, item_new, item_slot, item_nxt, xg, w_eg, w_eu, w_ed)


def _combine_kernel(alpha, n_p, n_steps, dest0_ref, dest1_ref, dest2_ref, gate_ref, x1_ref, y_ref,
                    wsg_ref, wsu_ref, wsd_ref, g_ref, b_ref, x2p_ref, x2s_ref,
                    base_ref, x2_ref, *ring):
    step = pl.program_id(0)
    tt = x1_ref.shape[0]
    n_rows = TOP_K * tt * ROW_TILES
    group = COMBINE_GROUP
    n_groups = tt // group
    bufs, sem = ring[:-1], ring[-1]
    ahead = COMBINE_RING - 1

    def issue_group(dref, grp, slot):
        for r in range(group):
            t = grp * group + r
            for kk in range(TOP_K):
                _row_copy(y_ref, dref[t * TOP_K + kk], bufs[slot],
                          (kk * tt + t) * ROW_TILES, sem.at[slot]).start(priority=kk % 2)

    def issue_tile(dref, slot):
        def body(grp, carry):
            issue_group(dref, grp, slot)
            return carry

        lax.fori_loop(0, n_groups, body, 0)

    def reduce_group(grp, slot):
        rows = pl.ds(pl.multiple_of(grp * group, group), group)
        gates = gate_ref[rows, :]
        chunks = []
        for s in range(ROW_TILES):
            acc = jnp.zeros((group, LANES), F32)
            for kk in range(TOP_K):
                row0 = kk * tt * ROW_TILES + grp * (group * ROW_TILES) + s
                acc = acc + gates[:, kk:kk + 1] * bufs[slot][pl.ds(row0, group,
                                                                    stride=ROW_TILES), :]
            chunks.append(acc)
        routed = jnp.concatenate(chunks, axis=1)
        x2_ref[rows, :] = _layer_norm(base_ref[rows, :] + routed, g_ref[...], b_ref[...])

    @pl.when(step == 0)
    def _():
        issue_tile(dest0_ref, 0)
        if n_steps > 1:
            issue_tile(dest1_ref, 1)

    x1 = x1_ref[...]
    xb = x1.astype(BF16)
    gs = _dot(xb, wsg_ref[...])
    us = _dot(xb, wsu_ref[...])
    base_ref[...] = alpha * x1 + _dot((gs * _sigmoid(gs) * us).astype(BF16), wsd_ref[...])

    for slot in range(COMBINE_RING):
        here = lax.rem(step, COMBINE_RING) == slot
        ahead_slot = (slot + ahead) % COMBINE_RING

        @pl.when(here)
        def _(slot=slot):
            pltpu.make_async_copy(y_ref.at[pl.ds(0, n_rows)], bufs[slot], sem.at[slot]).wait()

        @pl.when(jnp.logical_and(here, step + ahead < n_steps))
        def _(slot=slot, ahead_slot=ahead_slot):
            def body(grp, carry):
                issue_group(dest2_ref, grp, ahead_slot)
                reduce_group(grp, slot)
                return carry

            lax.fori_loop(0, n_groups, body, 0)

        @pl.when(jnp.logical_and(here, step + ahead >= n_steps))
        def _(slot=slot):
            def body(grp, carry):
                reduce_group(grp, slot)
                return carry

            lax.fori_loop(0, n_groups, body, 0)

    @pl.when(step < n_p)
    def _():
        x2p_ref[...] = x2_ref[...]

    @pl.when(step >= n_p)
    def _():
        x2s_ref[...] = x2_ref[...]


def _combine(dest8, gates, x1, y, wsg, wsu, wsd, g, b, alpha, tt, mp):
    m = x1.shape[0]
    n_steps = m // tt
    n_p = mp // tt

    def dest_spec(ahead):
        return pl.BlockSpec((tt * TOP_K,), lambda i: (jnp.minimum(i + ahead, n_steps - 1),),
                            memory_space=pltpu.SMEM)

    return pl.pallas_call(
        functools.partial(_combine_kernel, alpha, n_p, n_steps),
        grid=(n_steps,),
        in_specs=[dest_spec(0), dest_spec(1), dest_spec(2),
                  pl.BlockSpec((tt, TOP_K), lambda i: (i, 0)),
                  pl.BlockSpec((tt, D_MODEL), lambda i: (i, 0)),
                  pl.BlockSpec(memory_space=pl.ANY),
                  pl.BlockSpec((D_MODEL, D_EXPERT), lambda i: (0, 0)),
                  pl.BlockSpec((D_MODEL, D_EXPERT), lambda i: (0, 0)),
                  pl.BlockSpec((D_EXPERT, D_MODEL), lambda i: (0, 0)),
                  pl.BlockSpec((1, D_MODEL), lambda i: (0, 0)),
                  pl.BlockSpec((1, D_MODEL), lambda i: (0, 0))],
        out_specs=[pl.BlockSpec((tt, D_MODEL), lambda i: (jnp.minimum(i, n_p - 1), 0)),
                   pl.BlockSpec((tt, D_MODEL), lambda i: (jnp.maximum(i - n_p, 0), 0))],
        out_shape=[jax.ShapeDtypeStruct((mp, D_MODEL), F32),
                   jax.ShapeDtypeStruct((m - mp, D_MODEL), F32)],
        scratch_shapes=[pltpu.VMEM((tt, D_MODEL), F32), pltpu.VMEM((tt, D_MODEL), F32)]
        + [pltpu.VMEM((TOP_K * tt * ROW_TILES, LANES), F32)] * COMBINE_RING
        + [pltpu.SemaphoreType.DMA((COMBINE_RING,))],
        compiler_params=_cparams(("arbitrary",)),
        name="combine",
    )(dest8, dest8, dest8, gates, x1, y, wsg, wsu, wsd, g, b)


PROJ_TM = 512
PROJ_TN = 1024
MIX_TT = 256
MIX_CHUNK = 128
SAMPLE_NSEQ = 8
ROUTER_TT = 512
DISPATCH_TT = 512
EXPERT_RB = 256
EXPERT_RING = 3
COMBINE_TT = 256
COMBINE_RING = 3
COMBINE_GROUP = 64


def _moe(x1, x1r, mp, w_router, b_router, w_eg, w_eu, w_ed, w_sg, w_su, w_sd, ln_g, ln_b, alpha):
    m = x1.shape[0]
    wr_hi, wr_lo = _split_hi_lo(w_router)
    idx, gates, rank, cnt = _router(x1, wr_hi, wr_lo, b_router[None, :], ROUTER_TT)

    rb = EXPERT_RB
    n_rows = m * TOP_K
    n_items = n_rows // rb + N_EXPERTS
    counts = cnt[0].astype(jnp.int32)
    seg_end = jnp.cumsum(counts)
    seg_start = seg_end - counts
    n_chunk = (counts + rb - 1) // rb
    item_end = jnp.cumsum(n_chunk)
    item_start = item_end - n_chunk
    item = jnp.arange(n_items, dtype=jnp.int32)
    item_c = jnp.minimum(item, item_end[-1] - 1)
    item_e = jnp.sum((item_end[None, :] <= item_c[:, None]).astype(jnp.int32), axis=1)
    onehot_e = item_e[:, None] == jnp.arange(N_EXPERTS, dtype=jnp.int32)[None, :]

    def pick(table):
        return jnp.sum(jnp.where(onehot_e, table[None, :], 0), axis=1)

    chunk = item_c - pick(item_start)
    valid = item < item_end[-1]
    item_row = pick(seg_start) + chunk * rb
    item_valid = jnp.where(valid, jnp.clip(pick(counts) - chunk * rb, 0, rb), 0)

    experts = jnp.arange(N_EXPERTS, dtype=jnp.int32)
    nonempty = counts > 0
    slot_e = (jnp.cumsum(nonempty.astype(jnp.int32)) - 1) % 2
    later = jnp.logical_and(experts[None, :] > experts[:, None], nonempty[None, :])
    nxt_e = jnp.min(jnp.where(later, experts[None, :], N_EXPERTS), axis=1)
    nxt_e = jnp.where(nxt_e == N_EXPERTS, -1, nxt_e)
    item_new = jnp.logical_and(valid, chunk == 0).astype(jnp.int32)

    dest8 = _dest(idx.reshape(-1, LANES), rank.reshape(-1, LANES), seg_start).reshape(-1)
    xg = _dispatch(dest8, x1r, DISPATCH_TT, rb)
    y = _experts(item_e, item_row, item_valid, item_new, pick(slot_e), pick(nxt_e),
                 xg, w_eg, w_eu, w_ed, rb)
    return _combine(dest8, gates, x1, y, w_sg.astype(BF16), w_su.astype(BF16),
                    w_sd.astype(BF16), ln_g[None, :], ln_b[None, :], alpha, COMBINE_TT, mp)


def _layer(xp, xs, conv_s, c_s, n_s, m_s, w_in, b_gates, w_conv, mh_gain, w_out, ln1_g, ln1_b,
           w_router, b_router, w_eg, w_eu, w_ed, w_sg, w_su, w_sd, ln2_g, ln2_b, alpha):
    bp, tp, _ = xp.shape
    bs, ts, _ = xs.shape
    mp, ms = bp * tp, bs * ts
    xp2 = xp.reshape(mp, D_MODEL)
    xs2 = xs.reshape(ms, D_MODEL)

    w_a = w_in[:, :GATE_COL0].astype(BF16)
    w_b = w_in[:, GATE_COL0 + 2 * N_HEADS:].astype(BF16)
    w_gate = jnp.pad(w_in[:, GATE_COL0:GATE_COL0 + 2 * N_HEADS],
                     ((0, 0), (0, LANES - 2 * N_HEADS)))
    wg_hi, wg_lo = _split_hi_lo(w_gate)

    zg_s = _gate_proj(xs2, wg_hi, wg_lo, PROJ_TM)
    gt_s = zg_s[:, :SUBLANES].reshape(bs, ts, SUBLANES).transpose(0, 2, 1)
    mix_s, conv_n, c_n, n_n, m_n = _mixer(
        _proj(xs2, w_a, PROJ_TM, PROJ_TN), _proj(xs2, w_b, PROJ_TM, PROJ_TN),
        zg_s, gt_s, b_gates, w_conv, mh_gain[None, :],
        conv_s, c_s, n_s, m_s[:, None, :], nb=bs, ts=ts, nseq=SAMPLE_NSEQ)

    x1, x1r, conv_p, c_p, n_p, m_p = _prompt_layer_half(
        xp2, xs2, mix_s, b_gates, w_a, w_b, wg_hi, wg_lo, w_conv, mh_gain[None, :],
        w_out.astype(BF16), ln1_g[None, :], ln1_b[None, :], alpha,
        nb=bp, nt=tp // MIX_TT, tt=MIX_TT, chunk=MIX_CHUNK)

    x2p, x2s = _moe(x1, x1r, mp, w_router, b_router, w_eg, w_eu, w_ed, w_sg, w_su, w_sd,
                    ln2_g, ln2_b, alpha)
    states_p = (conv_p, c_p, n_p, m_p[:, 0, :])
    states_s = (conv_n, c_n, n_n, m_n[:, 0, :])
    return x2p.reshape(bp, tp, D_MODEL), x2s.reshape(bs, ts, D_MODEL), states_p, states_s


def kernel(x_prompt, x_sample, cache_conv, state_mlstm_C, state_mlstm_n, state_mlstm_m, w_in, b_gates, w_conv, mh_gain, w_out, ln1_g, ln1_b, w_router, b_router, w_exp_gate, w_exp_up, w_exp_down, w_sh_gate, w_sh_up, w_sh_down, ln2_g, ln2_b):
    depth = w_in.shape[0]
    alpha = (2.0 * depth) ** 0.25
    hp, hs = x_prompt, x_sample
    outs_p = [[], [], [], []]
    outs_s = [[], [], [], []]
    for l in range(depth):
        hp, hs, st_p, st_s = _layer(
            hp, hs, cache_conv[l], state_mlstm_C[l], state_mlstm_n[l], state_mlstm_m[l],
            w_in[l], b_gates[l], w_conv[l], mh_gain[l], w_out[l], ln1_g[l], ln1_b[l],
            w_router[l], b_router[l], w_exp_gate[l], w_exp_up[l], w_exp_down[l],
            w_sh_gate[l], w_sh_up[l], w_sh_down[l], ln2_g[l], ln2_b[l], alpha)
        for acc, val in zip(outs_p, st_p):
            acc.append(val)
        for acc, val in zip(outs_s, st_s):
            acc.append(val)
    return (hp, hs) + tuple(jnp.stack(a) for a in outs_p) + tuple(jnp.stack(a) for a in outs_s)
```

```python
import functools

import jax
import jax.numpy as jnp
from jax import lax
from jax.experimental import pallas as pl
from jax.experimental.pallas import tpu as pltpu

F32 = jnp.float32
BF16 = jnp.bfloat16

D_MODEL = 1024
N_HEADS = 4
HEAD_V = 256
HEAD_QK = 128
N_EXPERTS = 256
TOP_K = 8
D_EXPERT = 256
ROUTED_SCALE = 2.5
LN_EPS = 1e-5
CONV_K = 3

LANES = 128
SUBLANES = 8
ROW_TILES = D_MODEL // LANES
N_GROUPS_A = 6
N_GROUPS_B = 2
GATE_COL0 = N_GROUPS_A * D_MODEL

VMEM_LIMIT = 56 * 1024 * 1024


def _cparams(sem):
    return pltpu.CompilerParams(dimension_semantics=sem, vmem_limit_bytes=VMEM_LIMIT)


def _sigmoid(x):
    return 1.0 / (1.0 + jnp.exp(-x))


def _log_sigmoid(x):
    return jnp.minimum(x, 0.0) - jnp.log(1.0 + jnp.exp(-jnp.abs(x)))


def _layer_norm(x, g, b):
    mu = jnp.mean(x, axis=-1, keepdims=True)
    xc = x - mu
    var = jnp.mean(xc * xc, axis=-1, keepdims=True)
    return xc * lax.rsqrt(var + LN_EPS) * g + b


def _split_hi_lo(x):
    hi = x.astype(BF16)
    lo = (x - hi.astype(F32)).astype(BF16)
    return hi, lo


def _dot(a, b):
    return jnp.dot(a, b, preferred_element_type=F32)


def _store_rows(ref, v):
    n = v.shape[0]
    for s in range(ROW_TILES):
        ref[pl.ds(s, n, stride=ROW_TILES), :] = v[:, s * LANES:(s + 1) * LANES]


def _load_row_chunks(ref, n, row0=0):
    return [ref[pl.ds(row0 + s, n, stride=ROW_TILES), :] for s in range(ROW_TILES)]


def _dot3(x, w_hi, w_lo):
    x_hi, x_lo = _split_hi_lo(x)
    return _dot(x_hi, w_hi) + _dot(x_lo, w_hi) + _dot(x_hi, w_lo)


def _proj_kernel(x_ref, w_ref, z_ref):
    z_ref[...] = _dot(x_ref[...].astype(BF16), w_ref[...])


def _proj(x, w, tm, tn):
    m, k = x.shape
    n = w.shape[1]
    return pl.pallas_call(
        _proj_kernel,
        grid=(m // tm, n // tn),
        in_specs=[pl.BlockSpec((tm, k), lambda i, j: (i, 0)),
                  pl.BlockSpec((k, tn), lambda i, j: (0, j))],
        out_specs=pl.BlockSpec((tm, tn), lambda i, j: (i, j)),
        out_shape=jax.ShapeDtypeStruct((m, n), F32),
        compiler_params=_cparams(("parallel", "arbitrary")),
        name="proj",
    )(x, w)


def _gate_proj_kernel(x_ref, wh_ref, wl_ref, z_ref):
    z_ref[...] = _dot3(x_ref[...], wh_ref[...], wl_ref[...])


def _gate_proj(x, w_hi, w_lo, tm):
    m, k = x.shape
    n = w_hi.shape[1]
    return pl.pallas_call(
        _gate_proj_kernel,
        grid=(m // tm,),
        in_specs=[pl.BlockSpec((tm, k), lambda i: (i, 0)),
                  pl.BlockSpec((k, n), lambda i: (0, 0)),
                  pl.BlockSpec((k, n), lambda i: (0, 0))],
        out_specs=pl.BlockSpec((tm, n), lambda i: (i, 0)),
        out_shape=jax.ShapeDtypeStruct((m, n), F32),
        compiler_params=_cparams(("parallel",)),
        name="gate_proj",
    )(x, w_hi, w_lo)


def _conv_branch(u, zcb, wc, conv_ref, seq=0):
    tt = u.shape[0]
    carry = conv_ref[seq]
    rows = lax.broadcasted_iota(jnp.int32, (tt, D_MODEL), 0)
    u1 = jnp.where(rows == 0, carry[1:2, :], pltpu.roll(u, 1, 0))
    u2 = jnp.where(rows == 0, carry[0:1, :],
                   jnp.where(rows == 1, carry[1:2, :], pltpu.roll(u, 2, 0)))
    conv_out = u2 * wc[0:1, :] + u1 * wc[1:2, :] + u * wc[2:3, :]
    conv_ref[seq] = u[tt - 2:tt, :]
    return zcb * conv_out


def _mlstm_chunk(q, k, v, ig_col, lf_col, ig_row, lf_row, c_state, n_row, m_prev):
    chunk = q.shape[0]
    ti = lax.broadcasted_iota(jnp.int32, (chunk, chunk), 0)
    si = lax.broadcasted_iota(jnp.int32, (chunk, chunk), 1)
    causal = si <= ti
    b_col = jnp.sum(jnp.where(causal, lf_row, 0.0), axis=1, keepdims=True)
    b_row = jnp.sum(jnp.where(ti <= si, lf_col, 0.0), axis=0, keepdims=True)
    inter = b_col + m_prev
    dmat = jnp.where(causal, b_col - b_row + ig_row, -jnp.inf)
    m_t = jnp.maximum(inter, jnp.max(dmat, axis=1, keepdims=True))
    w_intra = jnp.exp(dmat - m_t)
    w_inter = jnp.exp(inter - m_t)
    qb = q.astype(BF16)
    kb = k.astype(BF16)
    vb = v.astype(BF16)
    s = lax.dot_general(qb, kb, (((1,), (1,)), ((), ())), preferred_element_type=F32) * w_intra
    num = w_inter * _dot(qb, c_state.astype(BF16)) + _dot(s.astype(BF16), vb)
    den = (w_inter * jnp.sum(q * n_row, axis=1, keepdims=True)
           + jnp.sum(s, axis=1, keepdims=True))
    h = num / jnp.maximum(jnp.abs(den), jnp.exp(-m_t))
    m_new = m_t[chunk - 1:chunk, :]
    b_last = b_col[chunk - 1:chunk, :]
    ws_col = jnp.exp(b_last - b_col + ig_col - m_new)
    cdecay = jnp.exp(inter[chunk - 1:chunk, :] - m_new)
    kw = k * ws_col
    c_new = cdecay * c_state + lax.dot_general(
        kw.astype(BF16), vb, (((0,), (0,)), ((), ())), preferred_element_type=F32)
    n_new = cdecay * n_row + jnp.sum(kw, axis=0, keepdims=True)
    return h, c_new, n_new, m_new


def _mlstm_branch(chunk, bg_ref, zg, gt, zqk, zv, zog, gain_ref, c_ref, n_ref, m_ref, mo_ref,
                  seq=0, row0=0):
    tt = zqk.shape[0]
    for h in range(N_HEADS):
        ig_col_all = zg[:, h:h + 1] + bg_ref[h]
        lf_col_all = _log_sigmoid(zg[:, N_HEADS + h:N_HEADS + h + 1] + bg_ref[N_HEADS + h])
        ig_row_all = gt[h:h + 1, :] + bg_ref[h]
        lf_row_all = _log_sigmoid(gt[N_HEADS + h:N_HEADS + h + 1, :] + bg_ref[N_HEADS + h])
        vcols = slice(h * HEAD_V, (h + 1) * HEAD_V)
        for c in range(tt // chunk):
            r0, r1 = c * chunk, (c + 1) * chunk
            q = zqk[r0:r1, h * HEAD_QK:(h + 1) * HEAD_QK]
            k = zqk[r0:r1, (N_HEADS + h) * HEAD_QK:(N_HEADS + h + 1) * HEAD_QK] * (HEAD_QK ** -0.5)
            hh, c_new, n_new, m_new = _mlstm_chunk(
                q, k, zv[r0:r1, vcols], ig_col_all[r0:r1, :], lf_col_all[r0:r1, :],
                ig_row_all[:, r0:r1], lf_row_all[:, r0:r1],
                c_ref[seq, h], n_ref[seq, h:h + 1, :], m_ref[seq, :, h:h + 1])
            c_ref[seq, h] = c_new
            n_ref[seq, h:h + 1, :] = n_new
            m_ref[seq, :, h:h + 1] = m_new
            mu = jnp.mean(hh, axis=-1, keepdims=True)
            hc = hh - mu
            var = jnp.mean(hc * hc, axis=-1, keepdims=True)
            hn = hc * lax.rsqrt(var + LN_EPS) * gain_ref[:, vcols]
            mo_ref[row0 + r0:row0 + r1, vcols] = hn * _sigmoid(zog[r0:r1, vcols])


def _store_x1(x, mix, wout_ref, g_ref, b_ref, alpha, x1_ref, x1r_ref):
    x1 = _layer_norm(alpha * x + _dot(mix.astype(BF16), wout_ref[...]), g_ref[...], b_ref[...])
    x1_ref[...] = x1
    _store_rows(x1r_ref, x1)


def _mixer_kernel(ts, bg_ref,
                  zcb_ref, zcc_ref, zch_ref, zqk_ref, zv_ref, zog_ref, zga_ref, zgb_ref,
                  zg_ref, gt_ref, wconv_ref, gain_ref,
                  conv0_ref, c0_ref, n0_ref, m0_ref,
                  mix_ref, conv_ref, c_ref, n_ref, m_ref, mo_ref):
    conv_ref[...] = conv0_ref[...]
    c_ref[...] = c0_ref[...]
    n_ref[...] = n0_ref[...]
    m_ref[...] = m0_ref[...]
    for seq in range(conv_ref.shape[0]):
        rows = slice(seq * ts, (seq + 1) * ts)
        a = _conv_branch(zcc_ref[rows, :] * zch_ref[rows, :], zcb_ref[rows, :], wconv_ref[...],
                         conv_ref, seq)
        _mlstm_branch(ts, bg_ref, zg_ref[rows, :], gt_ref[seq], zqk_ref[rows, :], zv_ref[rows, :],
                      zog_ref[rows, :], gain_ref, c_ref, n_ref, m_ref, mo_ref, seq, seq * ts)
        mix_ref[rows, :] = (_sigmoid(zga_ref[rows, :]) * a
                            + _sigmoid(zgb_ref[rows, :]) * mo_ref[rows, :])


def _mixer(z_a, z_b, zg, gt3, b_gates, w_conv, mh_gain, conv0, c0, n0, m0, *, nb, ts, nseq):
    tt = nseq * ts
    groups = [(z_a, j) for j in range(N_GROUPS_A)] + [(z_b, j) for j in range(N_GROUPS_B)]

    in_specs = [pl.BlockSpec(memory_space=pltpu.SMEM)]
    in_specs += [pl.BlockSpec((tt, D_MODEL), lambda b, j=j: (b, j)) for _, j in groups]
    state_specs = [
        pl.BlockSpec((nseq, CONV_K - 1, D_MODEL), lambda b: (b, 0, 0)),
        pl.BlockSpec((nseq, N_HEADS, HEAD_QK, HEAD_V), lambda b: (b, 0, 0, 0)),
        pl.BlockSpec((nseq, N_HEADS, HEAD_QK), lambda b: (b, 0, 0)),
        pl.BlockSpec((nseq, 1, N_HEADS), lambda b: (b, 0, 0)),
    ]
    in_specs += [
        pl.BlockSpec((tt, LANES), lambda b: (b, 0)),
        pl.BlockSpec((nseq, SUBLANES, ts), lambda b: (b, 0, 0)),
        pl.BlockSpec((CONV_K, D_MODEL), lambda b: (0, 0)),
        pl.BlockSpec((1, D_MODEL), lambda b: (0, 0)),
    ] + state_specs
    out_specs = [pl.BlockSpec((tt, D_MODEL), lambda b: (b, 0))] + state_specs
    out_shape = [
        jax.ShapeDtypeStruct((nb * ts, D_MODEL), F32),
        jax.ShapeDtypeStruct((nb, CONV_K - 1, D_MODEL), F32),
        jax.ShapeDtypeStruct((nb, N_HEADS, HEAD_QK, HEAD_V), F32),
        jax.ShapeDtypeStruct((nb, N_HEADS, HEAD_QK), F32),
        jax.ShapeDtypeStruct((nb, 1, N_HEADS), F32),
    ]
    return pl.pallas_call(
        functools.partial(_mixer_kernel, ts),
        grid=(nb // nseq,),
        in_specs=in_specs,
        out_specs=out_specs,
        out_shape=out_shape,
        scratch_shapes=[pltpu.VMEM((tt, D_MODEL), F32)],
        compiler_params=_cparams(("parallel",)),
        name="mixer_sample",
    )(b_gates, *[z for z, _ in groups], zg, gt3, w_conv, mh_gain, conv0, c0, n0, m0)


def _prompt_kernel(chunk, nb, ns, alpha, bg_ref, x_ref, wa_ref, wb_ref, wgh_ref, wgl_ref, wconv_ref,
                   gain_ref, wout_ref, g_ref, b_ref, xs_ref, mixs_ref,
                   x1_ref, x1r_ref, conv_ref, c_ref, n_ref, m_ref, mo_ref):
    b = pl.program_id(0)
    t = pl.program_id(1)

    @pl.when(b < nb)
    def _():
        @pl.when(t == 0)
        def _():
            conv_ref[...] = jnp.zeros_like(conv_ref)
            c_ref[...] = jnp.zeros_like(c_ref)
            n_ref[...] = jnp.zeros_like(n_ref)
            m_ref[...] = jnp.zeros_like(m_ref)

        x = x_ref[...]
        xb = x.astype(BF16)

        def z(j):
            w_ref, jj = (wa_ref, j) if j < N_GROUPS_A else (wb_ref, j - N_GROUPS_A)
            return _dot(xb, w_ref[:, jj * D_MODEL:(jj + 1) * D_MODEL])

        a = _conv_branch(z(1) * z(2), z(0), wconv_ref[...], conv_ref)
        zg = _dot3(x, wgh_ref[...], wgl_ref[...])
        _mlstm_branch(chunk, bg_ref, zg, zg.T, z(3), z(4), z(5), gain_ref,
                      c_ref, n_ref, m_ref, mo_ref)
        mix = _sigmoid(z(6)) * a + _sigmoid(z(7)) * mo_ref[...]
        _store_x1(x, mix, wout_ref, g_ref, b_ref, alpha, x1_ref, x1r_ref)

    @pl.when(jnp.logical_and(b == nb, t < ns))
    def _():
        _store_x1(xs_ref[...], mixs_ref[...], wout_ref, g_ref, b_ref, alpha, x1_ref, x1r_ref)


def _prompt_layer_half(xp, xs, mix_s, b_gates, w_a, w_b, wg_hi, wg_lo, w_conv, mh_gain, w_out,
                       ln_g, ln_b, alpha, *, nb, nt, tt, chunk):
    mp, ms = xp.shape[0], xs.shape[0]
    ns = ms // tt
    last_p = nb * nt - 1

    def prow(b, t):
        return jnp.minimum(b * nt + t, last_p)

    def srow(b, t):
        return jnp.where(b == nb, jnp.minimum(t, ns - 1), 0)

    def orow(b, t):
        return jnp.where(b == nb, nb * nt + jnp.minimum(t, ns - 1), b * nt + t)

    def state(b):
        return jnp.minimum(b, nb - 1)

    const2 = lambda b, t: (0, 0)
    in_specs = [
        pl.BlockSpec(memory_space=pltpu.SMEM),
        pl.BlockSpec((tt, D_MODEL), lambda b, t: (prow(b, t), 0)),
        pl.BlockSpec(w_a.shape, const2, pipeline_mode=pl.Buffered(1)),
        pl.BlockSpec(w_b.shape, const2, pipeline_mode=pl.Buffered(1)),
        pl.BlockSpec(wg_hi.shape, const2),
        pl.BlockSpec(wg_lo.shape, const2),
        pl.BlockSpec((CONV_K, D_MODEL), const2),
        pl.BlockSpec((1, D_MODEL), const2),
        pl.BlockSpec((D_MODEL, D_MODEL), const2),
        pl.BlockSpec((1, D_MODEL), const2),
        pl.BlockSpec((1, D_MODEL), const2),
        pl.BlockSpec((tt, D_MODEL), lambda b, t: (srow(b, t), 0)),
        pl.BlockSpec((tt, D_MODEL), lambda b, t: (srow(b, t), 0)),
    ]
    out_specs = [
        pl.BlockSpec((tt, D_MODEL), lambda b, t: (orow(b, t), 0)),
        pl.BlockSpec((tt * ROW_TILES, LANES), lambda b, t: (orow(b, t), 0)),
        pl.BlockSpec((1, CONV_K - 1, D_MODEL), lambda b, t: (state(b), 0, 0)),
        pl.BlockSpec((1, N_HEADS, HEAD_QK, HEAD_V), lambda b, t: (state(b), 0, 0, 0)),
        pl.BlockSpec((1, N_HEADS, HEAD_QK), lambda b, t: (state(b), 0, 0)),
        pl.BlockSpec((1, 1, N_HEADS), lambda b, t: (state(b), 0, 0)),
    ]
    out_shape = [
        jax.ShapeDtypeStruct((mp + ms, D_MODEL), F32),
        jax.ShapeDtypeStruct(((mp + ms) * ROW_TILES, LANES), F32),
        jax.ShapeDtypeStruct((nb, CONV_K - 1, D_MODEL), F32),
        jax.ShapeDtypeStruct((nb, N_HEADS, HEAD_QK, HEAD_V), F32),
        jax.ShapeDtypeStruct((nb, N_HEADS, HEAD_QK), F32),
        jax.ShapeDtypeStruct((nb, 1, N_HEADS), F32),
    ]
    return pl.pallas_call(
        functools.partial(_prompt_kernel, chunk, nb, ns, alpha),
        grid=(nb + 1, nt),
        in_specs=in_specs,
        out_specs=out_specs,
        out_shape=out_shape,
        scratch_shapes=[pltpu.VMEM((tt, D_MODEL), F32)],
        compiler_params=_cparams(("arbitrary", "arbitrary")),
        name="prompt_half",
    )(b_gates, xp, w_a, w_b, wg_hi, wg_lo, w_conv, mh_gain, w_out, ln_g, ln_b, xs, mix_s)


def _router_kernel(x_ref, wh_ref, wl_ref, br_ref, idx_ref, gate_ref, rank_ref, cnt_ref):
    tt = x_ref.shape[0]

    @pl.when(pl.program_id(0) == 0)
    def _():
        cnt_ref[...] = jnp.zeros_like(cnt_ref)

    scores = _sigmoid(_dot3(x_ref[...], wh_ref[...], wl_ref[...]))
    lane = lax.broadcasted_iota(jnp.int32, (tt, N_EXPERTS), 1)
    work = scores + br_ref[...]
    picked = []
    sel = []
    member = jnp.zeros((tt, N_EXPERTS), F32)
    for _ in range(TOP_K):
        mx = jnp.max(work, axis=1, keepdims=True)
        ik = jnp.min(jnp.where(work == mx, lane, N_EXPERTS), axis=1, keepdims=True)
        onehot = lane == ik
        picked.append((ik, onehot))
        sel.append(jnp.sum(jnp.where(onehot, scores, 0.0), axis=1, keepdims=True))
        work = jnp.where(onehot, -jnp.inf, work)
        member = member + onehot.astype(F32)
    total = sel[0]
    for sk in sel[1:]:
        total = total + sk

    ti = lax.broadcasted_iota(jnp.int32, (tt, tt), 0)
    si = lax.broadcasted_iota(jnp.int32, (tt, tt), 1)
    earlier = (si < ti).astype(BF16)
    before = _dot(earlier, member.astype(BF16)) + cnt_ref[...]

    lane_o = lax.broadcasted_iota(jnp.int32, (tt, TOP_K), 1)
    idx_o = jnp.zeros((tt, TOP_K), jnp.int32)
    gate_o = jnp.zeros((tt, TOP_K), F32)
    rank_o = jnp.zeros((tt, TOP_K), jnp.int32)
    for kk in range(TOP_K):
        ik, onehot = picked[kk]
        rk = jnp.sum(jnp.where(onehot, before, 0.0), axis=1, keepdims=True)
        idx_o = jnp.where(lane_o == kk, ik, idx_o)
        gate_o = jnp.where(lane_o == kk, sel[kk] / total * ROUTED_SCALE, gate_o)
        rank_o = jnp.where(lane_o == kk, rk.astype(jnp.int32), rank_o)
    idx_ref[...] = idx_o
    gate_ref[...] = gate_o
    rank_ref[...] = rank_o
    cnt_ref[...] = cnt_ref[...] + jnp.sum(member, axis=0, keepdims=True)


def _router(x1, wr_hi, wr_lo, b_router, tt):
    m = x1.shape[0]
    return pl.pallas_call(
        _router_kernel,
        grid=(m // tt,),
        in_specs=[pl.BlockSpec((tt, D_MODEL), lambda i: (i, 0)),
                  pl.BlockSpec((D_MODEL, N_EXPERTS), lambda i: (0, 0)),
                  pl.BlockSpec((D_MODEL, N_EXPERTS), lambda i: (0, 0)),
                  pl.BlockSpec((1, N_EXPERTS), lambda i: (0, 0))],
        out_specs=[pl.BlockSpec((tt, TOP_K), lambda i: (i, 0)),
                   pl.BlockSpec((tt, TOP_K), lambda i: (i, 0)),
                   pl.BlockSpec((tt, TOP_K), lambda i: (i, 0)),
                   pl.BlockSpec((1, N_EXPERTS), lambda i: (0, 0))],
        out_shape=[jax.ShapeDtypeStruct((m, TOP_K), jnp.int32),
                   jax.ShapeDtypeStruct((m, TOP_K), F32),
                   jax.ShapeDtypeStruct((m, TOP_K), jnp.int32),
                   jax.ShapeDtypeStruct((1, N_EXPERTS), F32)],
        compiler_params=_cparams(("arbitrary",)),
        name="router",
    )(x1, wr_hi, wr_lo, b_router)


def _row_copy(src, src_row8, dst, dst_row8, sem):
    return pltpu.make_async_copy(src.at[pl.ds(pl.multiple_of(src_row8, ROW_TILES), ROW_TILES)],
                                 dst.at[pl.ds(pl.multiple_of(dst_row8, ROW_TILES), ROW_TILES)],
                                 sem)


def _dest_kernel(idx_ref, rank_ref, start_ref, dest_ref):
    idx = idx_ref[...]
    shape = idx.shape
    low = jnp.broadcast_to(start_ref[:, :LANES], shape)
    high = jnp.broadcast_to(start_ref[:, LANES:], shape)
    lane = jnp.bitwise_and(idx, LANES - 1)
    start = jnp.where(idx < LANES, jnp.take_along_axis(low, lane, axis=1),
                      jnp.take_along_axis(high, lane, axis=1))
    dest_ref[...] = (start + rank_ref[...]) * ROW_TILES


def _dest(idx, rank, seg_start):
    assert N_EXPERTS == 2 * LANES
    full = pl.BlockSpec(idx.shape, lambda i: (0, 0))
    return pl.pallas_call(
        _dest_kernel,
        grid=(1,),
        in_specs=[full, full, pl.BlockSpec((1, N_EXPERTS), lambda i: (0, 0))],
        out_specs=full,
        out_shape=jax.ShapeDtypeStruct(idx.shape, jnp.int32),
        compiler_params=_cparams(("arbitrary",)),
        name="dest",
    )(idx, rank, seg_start[None, :])


def _dispatch_kernel(tt, dest_ref, x1r_ref, xg_ref, zbuf, sem, zsem):
    @pl.when(pl.program_id(0) == 0)
    def _():
        zbuf[...] = jnp.zeros_like(zbuf)
        n_sorted = xg_ref.shape[0] - zbuf.shape[0]
        fill = pltpu.make_async_copy(zbuf, xg_ref.at[pl.ds(n_sorted, zbuf.shape[0])], zsem)
        fill.start()
        fill.wait()

    def issue(t, carry):
        for kk in range(TOP_K):
            _row_copy(x1r_ref, t * ROW_TILES, xg_ref, dest_ref[t * TOP_K + kk],
                      sem).start(priority=kk % 2)
        return carry

    lax.fori_loop(0, tt, issue, 0)
    n_rows = tt * TOP_K * ROW_TILES
    pltpu.make_async_copy(xg_ref.at[pl.ds(0, n_rows)], xg_ref.at[pl.ds(0, n_rows)], sem).wait()


def _dispatch(dest8, x1r, tt, slack):
    m = x1r.shape[0] // ROW_TILES
    return pl.pallas_call(
        functools.partial(_dispatch_kernel, tt),
        grid=(m // tt,),
        in_specs=[pl.BlockSpec((tt * TOP_K,), lambda i: (i,), memory_space=pltpu.SMEM),
                  pl.BlockSpec((tt * ROW_TILES, LANES), lambda i: (i, 0))],
        out_specs=pl.BlockSpec(memory_space=pl.ANY),
        out_shape=jax.ShapeDtypeStruct(((m * TOP_K + slack) * ROW_TILES, LANES), F32),
        scratch_shapes=[pltpu.VMEM((slack * ROW_TILES, LANES), F32),
                        pltpu.SemaphoreType.DMA(()), pltpu.SemaphoreType.DMA(())],
        compiler_params=_cparams(("arbitrary",)),
        name="dispatch",
    )(dest8, x1r)


def _experts_kernel(rb, e_ref, row_ref, valid_ref, new_ref, slot_ref, nxt_ref,
                    xg_hbm, wg_hbm, wu_hbm, wd_hbm, y_hbm,
                    *scratch):
    j = pl.program_id(0)
    n_steps = pl.num_programs(0)
    ring_pos = lax.rem(j, EXPERT_RING)
    valid = valid_ref[j]
    xbufs = scratch[:EXPERT_RING]
    ybufs = scratch[EXPERT_RING:2 * EXPERT_RING]
    wg_buf, wu_buf, wd_buf, wgb, wub, wdb, wsem, xsem, ysem = scratch[2 * EXPERT_RING:]

    def tile_rows(row, n):
        return pl.ds(pl.multiple_of(row * ROW_TILES, ROW_TILES), n * ROW_TILES)

    row_variants = [rb]

    def for_row_variant(n_valid, fn):
        for below, n in zip([0] + row_variants[:-1], row_variants):
            @pl.when(jnp.logical_and(n_valid > below, n_valid <= n))
            def _(n=n):
                fn(n)

    def x_copies(item, slot, start):
        def run(n):
            copy = pltpu.make_async_copy(xg_hbm.at[tile_rows(row_ref[item], n)],
                                         xbufs[slot].at[pl.ds(0, n * ROW_TILES)], xsem.at[slot])
            if start:
                copy.start()
            else:
                copy.wait()

        for_row_variant(valid_ref[item], run)

    def y_copies(item, slot, start):
        row0 = row_ref[item]
        n = valid_ref[item]

        def run(copy):
            if start:
                copy.start()
            else:
                copy.wait()

        @pl.when(n == rb)
        def _():
            run(pltpu.make_async_copy(ybufs[slot], y_hbm.at[tile_rows(row0, rb)], ysem.at[slot]))

        @pl.when(n < rb)
        def _():
            piece = rb // 2
            while piece >= 1:
                off = n - jnp.bitwise_and(n, 2 * piece - 1)

                @pl.when(jnp.bitwise_and(n, piece) != 0)
                def _(off=off, piece=piece):
                    run(pltpu.make_async_copy(ybufs[slot].at[tile_rows(off, piece)],
                                              y_hbm.at[tile_rows(row0 + off, piece)],
                                              ysem.at[slot]))

                piece //= 2

    def fetch(e, slot):
        return (pltpu.make_async_copy(wg_hbm.at[e], wg_buf.at[slot], wsem.at[slot, 0]),
                pltpu.make_async_copy(wu_hbm.at[e], wu_buf.at[slot], wsem.at[slot, 1]),
                pltpu.make_async_copy(wd_hbm.at[e], wd_buf.at[slot], wsem.at[slot, 2]))

    ahead = EXPERT_RING - 1

    @pl.when(j == 0)
    def _():
        for item in range(ahead):
            x_copies(item, item, start=True)

    ahead_item = jnp.minimum(j + ahead, n_steps - 1)
    for slot in range(EXPERT_RING):
        here = ring_pos == slot

        @pl.when(jnp.logical_and(here, j >= EXPERT_RING))
        def _(slot=slot):
            y_copies(j - EXPERT_RING, slot, start=False)

        @pl.when(jnp.logical_and(here, valid > 0))
        def _(slot=slot):
            x_copies(j, slot, start=False)

            @pl.when(j + ahead < n_steps)
            def _():
                x_copies(ahead_item, (slot + ahead) % EXPERT_RING, start=True)

    @pl.when(new_ref[j] != 0)
    def _():
        slot = slot_ref[j]

        @pl.when(j == 0)
        def _():
            for copy in fetch(e_ref[j], slot):
                copy.start()

        for copy in fetch(e_ref[j], slot):
            copy.wait()
        wgb[...] = wg_buf[slot].astype(BF16)
        wub[...] = wu_buf[slot].astype(BF16)
        wdb[...] = wd_buf[slot].astype(BF16)

        @pl.when(nxt_ref[j] >= 0)
        def _():
            for copy in fetch(nxt_ref[j], 1 - slot):
                copy.start()

    def swiglu_rows(slot, n):
        xb = jnp.concatenate(_load_row_chunks(xbufs[slot], n), axis=1).astype(BF16)
        g = _dot(xb, wgb[...])
        u = _dot(xb, wub[...])
        y = _dot((g * _sigmoid(g) * u).astype(BF16), wdb[...])
        _store_rows(ybufs[slot], y)

    for slot in range(EXPERT_RING):
        here = ring_pos == slot

        @pl.when(jnp.logical_and(here, valid > 0))
        def _(slot=slot):
            for_row_variant(valid, functools.partial(swiglu_rows, slot))
            y_copies(j, slot, start=True)

        @pl.when(jnp.logical_and(here, j == n_steps - 1))
        def _(slot=slot):
            for back in range(EXPERT_RING - 1, -1, -1):
                y_copies(j - back, (slot - back) % EXPERT_RING, start=False)


def _experts(item_e, item_row, item_valid, item_new, item_slot, item_nxt,
             xg, w_eg, w_eu, w_ed, rb):
    n_items = item_e.shape[0]
    n_rows = xg.shape[0] // ROW_TILES - rb
    hbm = pl.BlockSpec(memory_space=pl.ANY)
    row_buf = pltpu.VMEM((rb * ROW_TILES, LANES), F32)
    grid_spec = pltpu.PrefetchScalarGridSpec(
        num_scalar_prefetch=6,
        grid=(n_items,),
        in_specs=[hbm, hbm, hbm, hbm],
        out_specs=hbm,
        scratch_shapes=[row_buf] * (2 * EXPERT_RING) + [
                        pltpu.VMEM((2, D_MODEL, D_EXPERT), F32),
                        pltpu.VMEM((2, D_MODEL, D_EXPERT), F32),
                        pltpu.VMEM((2, D_EXPERT, D_MODEL), F32),
                        pltpu.VMEM((D_MODEL, D_EXPERT), BF16),
                        pltpu.VMEM((D_MODEL, D_EXPERT), BF16),
                        pltpu.VMEM((D_EXPERT, D_MODEL), BF16),
                        pltpu.SemaphoreType.DMA((2, 3)),
                        pltpu.SemaphoreType.DMA((EXPERT_RING,)),
                        pltpu.SemaphoreType.DMA((EXPERT_RING,))],
    )
    return pl.pallas_call(
        functools.partial(_experts_kernel, rb),
        grid_spec=grid_spec,
        out_shape=jax.ShapeDtypeStruct((n_rows * ROW_TILES, LANES), F32),
        compiler_params=_cparams(("arbitrary",)),
        name="experts",
    )(item_e, item_row, item_valid, item_new, item_slot, item_nxt, xg, w_eg, w_eu, w_ed)


def _combine_kernel(alpha, n_p, n_steps, dest0_ref, dest1_ref, dest2_ref, gate_ref, x1_ref, y_ref,
                    wsg_ref, wsu_ref, wsd_ref, g_ref, b_ref, x2p_ref, x2s_ref,
                    base_ref, x2_ref, *ring):
    step = pl.program_id(0)
    tt = x1_ref.shape[0]
    n_rows = TOP_K * tt * ROW_TILES
    group = COMBINE_GROUP
    n_groups = tt // group
    bufs, sem = ring[:-1], ring[-1]
    ahead = COMBINE_RING - 1

    def issue_group(dref, grp, slot):
        for r in range(group):
            t = grp * group + r
            for kk in range(TOP_K):
                _row_copy(y_ref, dref[t * TOP_K + kk], bufs[slot],
                          (kk * tt + t) * ROW_TILES, sem.at[slot]).start(priority=kk % 2)

    def issue_tile(dref, slot):
        def body(grp, carry):
            issue_group(dref, grp, slot)
            return carry

        lax.fori_loop(0, n_groups, body, 0)

    def reduce_group(grp, slot):
        rows = pl.ds(pl.multiple_of(grp * group, group), group)
        gates = gate_ref[rows, :]
        chunks = []
        for s in range(ROW_TILES):
            acc = jnp.zeros((group, LANES), F32)
            for kk in range(TOP_K):
                row0 = kk * tt * ROW_TILES + grp * (group * ROW_TILES) + s
                acc = acc + gates[:, kk:kk + 1] * bufs[slot][pl.ds(row0, group,
                                                                    stride=ROW_TILES), :]
            chunks.append(acc)
        routed = jnp.concatenate(chunks, axis=1)
        x2_ref[rows, :] = _layer_norm(base_ref[rows, :] + routed, g_ref[...], b_ref[...])

    @pl.when(step == 0)
    def _():
        issue_tile(dest0_ref, 0)
        if n_steps > 1:
            issue_tile(dest1_ref, 1)

    x1 = x1_ref[...]
    xb = x1.astype(BF16)
    gs = _dot(xb, wsg_ref[...])
    us = _dot(xb, wsu_ref[...])
    base_ref[...] = alpha * x1 + _dot((gs * _sigmoid(gs) * us).astype(BF16), wsd_ref[...])

    for slot in range(COMBINE_RING):
        here = lax.rem(step, COMBINE_RING) == slot
        ahead_slot = (slot + ahead) % COMBINE_RING

        @pl.when(here)
        def _(slot=slot):
            pltpu.make_async_copy(y_ref.at[pl.ds(0, n_rows)], bufs[slot], sem.at[slot]).wait()

        @pl.when(jnp.logical_and(here, step + ahead < n_steps))
        def _(slot=slot, ahead_slot=ahead_slot):
            def body(grp, carry):
                issue_group(dest2_ref, grp, ahead_slot)
                reduce_group(grp, slot)
                return carry

            lax.fori_loop(0, n_groups, body, 0)

        @pl.when(jnp.logical_and(here, step + ahead >= n_steps))
        def _(slot=slot):
            def body(grp, carry):
                reduce_group(grp, slot)
                return carry

            lax.fori_loop(0, n_groups, body, 0)

    @pl.when(step < n_p)
    def _():
        x2p_ref[...] = x2_ref[...]

    @pl.when(step >= n_p)
    def _():
        x2s_ref[...] = x2_ref[...]


def _combine(dest8, gates, x1, y, wsg, wsu, wsd, g, b, alpha, tt, mp):
    m = x1.shape[0]
    n_steps = m // tt
    n_p = mp // tt

    def dest_spec(ahead):
        return pl.BlockSpec((tt * TOP_K,), lambda i: (jnp.minimum(i + ahead, n_steps - 1),),
                            memory_space=pltpu.SMEM)

    return pl.pallas_call(
        functools.partial(_combine_kernel, alpha, n_p, n_steps),
        grid=(n_steps,),
        in_specs=[dest_spec(0), dest_spec(1), dest_spec(2),
                  pl.BlockSpec((tt, TOP_K), lambda i: (i, 0)),
                  pl.BlockSpec((tt, D_MODEL), lambda i: (i, 0)),
                  pl.BlockSpec(memory_space=pl.ANY),
                  pl.BlockSpec((D_MODEL, D_EXPERT), lambda i: (0, 0)),
                  pl.BlockSpec((D_MODEL, D_EXPERT), lambda i: (0, 0)),
                  pl.BlockSpec((D_EXPERT, D_MODEL), lambda i: (0, 0)),
                  pl.BlockSpec((1, D_MODEL), lambda i: (0, 0)),
                  pl.BlockSpec((1, D_MODEL), lambda i: (0, 0))],
        out_specs=[pl.BlockSpec((tt, D_MODEL), lambda i: (jnp.minimum(i, n_p - 1), 0)),
                   pl.BlockSpec((tt, D_MODEL), lambda i: (jnp.maximum(i - n_p, 0), 0))],
        out_shape=[jax.ShapeDtypeStruct((mp, D_MODEL), F32),
                   jax.ShapeDtypeStruct((m - mp, D_MODEL), F32)],
        scratch_shapes=[pltpu.VMEM((tt, D_MODEL), F32), pltpu.VMEM((tt, D_MODEL), F32)]
        + [pltpu.VMEM((TOP_K * tt * ROW_TILES, LANES), F32)] * COMBINE_RING
        + [pltpu.SemaphoreType.DMA((COMBINE_RING,))],
        compiler_params=_cparams(("arbitrary",)),
        name="combine",
    )(dest8, dest8, dest8, gates, x1, y, wsg, wsu, wsd, g, b)


PROJ_TM = 512
PROJ_TN = 1024
MIX_TT = 256
MIX_CHUNK = 256
SAMPLE_NSEQ = 8
ROUTER_TT = 512
DISPATCH_TT = 512
EXPERT_RB = 256
EXPERT_RING = 3
COMBINE_TT = 256
COMBINE_RING = 3
COMBINE_GROUP = 64


def _moe(x1, x1r, mp, w_router, b_router, w_eg, w_eu, w_ed, w_sg, w_su, w_sd, ln_g, ln_b, alpha):
    m = x1.shape[0]
    wr_hi, wr_lo = _split_hi_lo(w_router)
    idx, gates, rank, cnt = _router(x1, wr_hi, wr_lo, b_router[None, :], ROUTER_TT)

    rb = EXPERT_RB
    n_rows = m * TOP_K
    n_items = n_rows // rb + N_EXPERTS
    counts = cnt[0].astype(jnp.int32)
    seg_end = jnp.cumsum(counts)
    seg_start = seg_end - counts
    n_chunk = (counts + rb - 1) // rb
    item_end = jnp.cumsum(n_chunk)
    item_start = item_end - n_chunk
    item = jnp.arange(n_items, dtype=jnp.int32)
    item_c = jnp.minimum(item, item_end[-1] - 1)
    item_e = jnp.sum((item_end[None, :] <= item_c[:, None]).astype(jnp.int32), axis=1)
    onehot_e = item_e[:, None] == jnp.arange(N_EXPERTS, dtype=jnp.int32)[None, :]

    def pick(table):
        return jnp.sum(jnp.where(onehot_e, table[None, :], 0), axis=1)

    chunk = item_c - pick(item_start)
    valid = item < item_end[-1]
    item_row = pick(seg_start) + chunk * rb
    item_valid = jnp.where(valid, jnp.clip(pick(counts) - chunk * rb, 0, rb), 0)

    experts = jnp.arange(N_EXPERTS, dtype=jnp.int32)
    nonempty = counts > 0
    slot_e = (jnp.cumsum(nonempty.astype(jnp.int32)) - 1) % 2
    later = jnp.logical_and(experts[None, :] > experts[:, None], nonempty[None, :])
    nxt_e = jnp.min(jnp.where(later, experts[None, :], N_EXPERTS), axis=1)
    nxt_e = jnp.where(nxt_e == N_EXPERTS, -1, nxt_e)
    item_new = jnp.logical_and(valid, chunk == 0).astype(jnp.int32)

    dest8 = _dest(idx.reshape(-1, LANES), rank.reshape(-1, LANES), seg_start).reshape(-1)
    xg = _dispatch(dest8, x1r, DISPATCH_TT, rb)
    y = _experts(item_e, item_row, item_valid, item_new, pick(slot_e), pick(nxt_e),
                 xg, w_eg, w_eu, w_ed, rb)
    return _combine(dest8, gates, x1, y, w_sg.astype(BF16), w_su.astype(BF16),
                    w_sd.astype(BF16), ln_g[None, :], ln_b[None, :], alpha, COMBINE_TT, mp)


def _layer(xp, xs, conv_s, c_s, n_s, m_s, w_in, b_gates, w_conv, mh_gain, w_out, ln1_g, ln1_b,
           w_router, b_router, w_eg, w_eu, w_ed, w_sg, w_su, w_sd, ln2_g, ln2_b, alpha):
    bp, tp, _ = xp.shape
    bs, ts, _ = xs.shape
    mp, ms = bp * tp, bs * ts
    xp2 = xp.reshape(mp, D_MODEL)
    xs2 = xs.reshape(ms, D_MODEL)

    w_a = w_in[:, :GATE_COL0].astype(BF16)
    w_b = w_in[:, GATE_COL0 + 2 * N_HEADS:].astype(BF16)
    w_gate = jnp.pad(w_in[:, GATE_COL0:GATE_COL0 + 2 * N_HEADS],
                     ((0, 0), (0, LANES - 2 * N_HEADS)))
    wg_hi, wg_lo = _split_hi_lo(w_gate)

    zg_s = _gate_proj(xs2, wg_hi, wg_lo, PROJ_TM)
    gt_s = zg_s[:, :SUBLANES].reshape(bs, ts, SUBLANES).transpose(0, 2, 1)
    mix_s, conv_n, c_n, n_n, m_n = _mixer(
        _proj(xs2, w_a, PROJ_TM, PROJ_TN), _proj(xs2, w_b, PROJ_TM, PROJ_TN),
        zg_s, gt_s, b_gates, w_conv, mh_gain[None, :],
        conv_s, c_s, n_s, m_s[:, None, :], nb=bs, ts=ts, nseq=SAMPLE_NSEQ)

    x1, x1r, conv_p, c_p, n_p, m_p = _prompt_layer_half(
        xp2, xs2, mix_s, b_gates, w_a, w_b, wg_hi, wg_lo, w_conv, mh_gain[None, :],
        w_out.astype(BF16), ln1_g[None, :], ln1_b[None, :], alpha,
        nb=bp, nt=tp // MIX_TT, tt=MIX_TT, chunk=MIX_CHUNK)

    x2p, x2s = _moe(x1, x1r, mp, w_router, b_router, w_eg, w_eu, w_ed, w_sg, w_su, w_sd,
                    ln2_g, ln2_b, alpha)
    states_p = (conv_p, c_p, n_p, m_p[:, 0, :])
    states_s = (conv_n, c_n, n_n, m_n[:, 0, :])
    return x2p.reshape(bp, tp, D_MODEL), x2s.reshape(bs, ts, D_MODEL), states_p, states_s


def kernel(x_prompt, x_sample, cache_conv, state_mlstm_C, state_mlstm_n, state_mlstm_m, w_in, b_gates, w_conv, mh_gain, w_out, ln1_g, ln1_b, w_router, b_router, w_exp_gate, w_exp_up, w_exp_down, w_sh_gate, w_sh_up, w_sh_down, ln2_g, ln2_b):
    depth = w_in.shape[0]
    alpha = (2.0 * depth) ** 0.25
    hp, hs = x_prompt, x_sample
    outs_p = [[], [], [], []]
    outs_s = [[], [], [], []]
    for l in range(depth):
        hp, hs, st_p, st_s = _layer(
            hp, hs, cache_conv[l], state_mlstm_C[l], state_mlstm_n[l], state_mlstm_m[l],
            w_in[l], b_gates[l], w_conv[l], mh_gain[l], w_out[l], ln1_g[l], ln1_b[l],
            w_router[l], b_router[l], w_exp_gate[l], w_exp_up[l], w_exp_down[l],
            w_sh_gate[l], w_sh_up[l], w_sh_down[l], ln2_g[l], ln2_b[l], alpha)
        for acc, val in zip(outs_p, st_p):
            acc.append(val)
        for acc, val in zip(outs_s, st_s):
            acc.append(val)
    return (hp, hs) + tuple(jnp.stack(a) for a in outs_p) + tuple(jnp.stack(a) for a in outs_s)
```

```python
import functools

import jax
import jax.numpy as jnp
from jax import lax
from jax.experimental import pallas as pl
from jax.experimental.pallas import tpu as pltpu

F32 = jnp.float32
BF16 = jnp.bfloat16

D_MODEL = 1024
N_HEADS = 4
HEAD_V = 256
HEAD_QK = 128
N_EXPERTS = 256
TOP_K = 8
D_EXPERT = 256
ROUTED_SCALE = 2.5
LN_EPS = 1e-5
CONV_K = 3

LANES = 128
SUBLANES = 8
ROW_TILES = D_MODEL // LANES
N_GROUPS_A = 6
N_GROUPS_B = 2
GATE_COL0 = N_GROUPS_A * D_MODEL

VMEM_LIMIT = 56 * 1024 * 1024


def _cparams(sem):
    return pltpu.CompilerParams(dimension_semantics=sem, vmem_limit_bytes=VMEM_LIMIT)


def _sigmoid(x):
    return 1.0 / (1.0 + jnp.exp(-x))


def _log_sigmoid(x):
    return jnp.minimum(x, 0.0) - jnp.log(1.0 + jnp.exp(-jnp.abs(x)))


def _layer_norm(x, g, b):
    mu = jnp.mean(x, axis=-1, keepdims=True)
    xc = x - mu
    var = jnp.mean(xc * xc, axis=-1, keepdims=True)
    return xc * lax.rsqrt(var + LN_EPS) * g + b


def _split_hi_lo(x):
    hi = x.astype(BF16)
    lo = (x - hi.astype(F32)).astype(BF16)
    return hi, lo


def _dot(a, b):
    return jnp.dot(a, b, preferred_element_type=F32)


def _store_rows(ref, v):
    n = v.shape[0]
    for s in range(ROW_TILES):
        ref[pl.ds(s, n, stride=ROW_TILES), :] = v[:, s * LANES:(s + 1) * LANES]


def _load_row_chunks(ref, n, row0=0):
    return [ref[pl.ds(row0 + s, n, stride=ROW_TILES), :] for s in range(ROW_TILES)]


def _dot3(x, w_hi, w_lo):
    x_hi, x_lo = _split_hi_lo(x)
    return _dot(x_hi, w_hi) + _dot(x_lo, w_hi) + _dot(x_hi, w_lo)


def _proj_kernel(x_ref, w_ref, z_ref):
    z_ref[...] = _dot(x_ref[...].astype(BF16), w_ref[...])


def _proj(x, w, tm, tn):
    m, k = x.shape
    n = w.shape[1]
    return pl.pallas_call(
        _proj_kernel,
        grid=(m // tm, n // tn),
        in_specs=[pl.BlockSpec((tm, k), lambda i, j: (i, 0)),
                  pl.BlockSpec((k, tn), lambda i, j: (0, j))],
        out_specs=pl.BlockSpec((tm, tn), lambda i, j: (i, j)),
        out_shape=jax.ShapeDtypeStruct((m, n), F32),
        compiler_params=_cparams(("parallel", "arbitrary")),
        name="proj",
    )(x, w)


def _gate_proj_kernel(x_ref, wh_ref, wl_ref, z_ref):
    z_ref[...] = _dot3(x_ref[...], wh_ref[...], wl_ref[...])


def _gate_proj(x, w_hi, w_lo, tm):
    m, k = x.shape
    n = w_hi.shape[1]
    return pl.pallas_call(
        _gate_proj_kernel,
        grid=(m // tm,),
        in_specs=[pl.BlockSpec((tm, k), lambda i: (i, 0)),
                  pl.BlockSpec((k, n), lambda i: (0, 0)),
                  pl.BlockSpec((k, n), lambda i: (0, 0))],
        out_specs=pl.BlockSpec((tm, n), lambda i: (i, 0)),
        out_shape=jax.ShapeDtypeStruct((m, n), F32),
        compiler_params=_cparams(("parallel",)),
        name="gate_proj",
    )(x, w_hi, w_lo)


def _conv_branch(u, zcb, wc, conv_ref, seq=0):
    tt = u.shape[0]
    carry = conv_ref[seq]
    rows = lax.broadcasted_iota(jnp.int32, (tt, D_MODEL), 0)
    u1 = jnp.where(rows == 0, carry[1:2, :], pltpu.roll(u, 1, 0))
    u2 = jnp.where(rows == 0, carry[0:1, :],
                   jnp.where(rows == 1, carry[1:2, :], pltpu.roll(u, 2, 0)))
    conv_out = u2 * wc[0:1, :] + u1 * wc[1:2, :] + u * wc[2:3, :]
    conv_ref[seq] = u[tt - 2:tt, :]
    return zcb * conv_out


def _mlstm_chunk(q, k, v, ig_col, lf_col, ig_row, lf_row, c_state, n_row, m_prev):
    chunk = q.shape[0]
    ti = lax.broadcasted_iota(jnp.int32, (chunk, chunk), 0)
    si = lax.broadcasted_iota(jnp.int32, (chunk, chunk), 1)
    causal = si <= ti
    b_col = jnp.sum(jnp.where(causal, lf_row, 0.0), axis=1, keepdims=True)
    b_row = jnp.sum(jnp.where(ti <= si, lf_col, 0.0), axis=0, keepdims=True)
    inter = b_col + m_prev
    dmat = jnp.where(causal, b_col - b_row + ig_row, -jnp.inf)
    m_t = jnp.maximum(inter, jnp.max(dmat, axis=1, keepdims=True))
    w_intra = jnp.exp(dmat - m_t)
    w_inter = jnp.exp(inter - m_t)
    qb = q.astype(BF16)
    kb = k.astype(BF16)
    vb = v.astype(BF16)
    s = lax.dot_general(qb, kb, (((1,), (1,)), ((), ())), preferred_element_type=F32) * w_intra
    num = w_inter * _dot(qb, c_state.astype(BF16)) + _dot(s.astype(BF16), vb)
    den = (w_inter * jnp.sum(q * n_row, axis=1, keepdims=True)
           + jnp.sum(s, axis=1, keepdims=True))
    h = num / jnp.maximum(jnp.abs(den), jnp.exp(-m_t))
    m_new = m_t[chunk - 1:chunk, :]
    b_last = b_col[chunk - 1:chunk, :]
    ws_col = jnp.exp(b_last - b_col + ig_col - m_new)
    cdecay = jnp.exp(inter[chunk - 1:chunk, :] - m_new)
    kw = k * ws_col
    c_new = cdecay * c_state + lax.dot_general(
        kw.astype(BF16), vb, (((0,), (0,)), ((), ())), preferred_element_type=F32)
    n_new = cdecay * n_row + jnp.sum(kw, axis=0, keepdims=True)
    return h, c_new, n_new, m_new


def _mlstm_branch(chunk, bg_ref, zg, gt, zqk, zv, zog, gain_ref, c_ref, n_ref, m_ref, mo_ref,
                  seq=0, row0=0):
    tt = zqk.shape[0]
    for h in range(N_HEADS):
        ig_col_all = zg[:, h:h + 1] + bg_ref[h]
        lf_col_all = _log_sigmoid(zg[:, N_HEADS + h:N_HEADS + h + 1] + bg_ref[N_HEADS + h])
        ig_row_all = gt[h:h + 1, :] + bg_ref[h]
        lf_row_all = _log_sigmoid(gt[N_HEADS + h:N_HEADS + h + 1, :] + bg_ref[N_HEADS + h])
        vcols = slice(h * HEAD_V, (h + 1) * HEAD_V)
        for c in range(tt // chunk):
            r0, r1 = c * chunk, (c + 1) * chunk
            q = zqk[r0:r1, h * HEAD_QK:(h + 1) * HEAD_QK]
            k = zqk[r0:r1, (N_HEADS + h) * HEAD_QK:(N_HEADS + h + 1) * HEAD_QK] * (HEAD_QK ** -0.5)
            hh, c_new, n_new, m_new = _mlstm_chunk(
                q, k, zv[r0:r1, vcols], ig_col_all[r0:r1, :], lf_col_all[r0:r1, :],
                ig_row_all[:, r0:r1], lf_row_all[:, r0:r1],
                c_ref[seq, h], n_ref[seq, h:h + 1, :], m_ref[seq, :, h:h + 1])
            c_ref[seq, h] = c_new
            n_ref[seq, h:h + 1, :] = n_new
            m_ref[seq, :, h:h + 1] = m_new
            mu = jnp.mean(hh, axis=-1, keepdims=True)
            hc = hh - mu
            var = jnp.mean(hc * hc, axis=-1, keepdims=True)
            hn = hc * lax.rsqrt(var + LN_EPS) * gain_ref[:, vcols]
            mo_ref[row0 + r0:row0 + r1, vcols] = hn * _sigmoid(zog[r0:r1, vcols])


def _store_x1(x, mix, wout_ref, g_ref, b_ref, alpha, x1_ref, x1r_ref):
    x1 = _layer_norm(alpha * x + _dot(mix.astype(BF16), wout_ref[...]), g_ref[...], b_ref[...])
    x1_ref[...] = x1
    _store_rows(x1r_ref, x1)


def _mixer_kernel(ts, bg_ref,
                  zcb_ref, zcc_ref, zch_ref, zqk_ref, zv_ref, zog_ref, zga_ref, zgb_ref,
                  zg_ref, gt_ref, wconv_ref, gain_ref,
                  conv0_ref, c0_ref, n0_ref, m0_ref,
                  mix_ref, conv_ref, c_ref, n_ref, m_ref, mo_ref):
    conv_ref[...] = conv0_ref[...]
    c_ref[...] = c0_ref[...]
    n_ref[...] = n0_ref[...]
    m_ref[...] = m0_ref[...]
    for seq in range(conv_ref.shape[0]):
        rows = slice(seq * ts, (seq + 1) * ts)
        a = _conv_branch(zcc_ref[rows, :] * zch_ref[rows, :], zcb_ref[rows, :], wconv_ref[...],
                         conv_ref, seq)
        _mlstm_branch(ts, bg_ref, zg_ref[rows, :], gt_ref[seq], zqk_ref[rows, :], zv_ref[rows, :],
                      zog_ref[rows, :], gain_ref, c_ref, n_ref, m_ref, mo_ref, seq, seq * ts)
        mix_ref[rows, :] = (_sigmoid(zga_ref[rows, :]) * a
                            + _sigmoid(zgb_ref[rows, :]) * mo_ref[rows, :])


def _mixer(z_a, z_b, zg, gt3, b_gates, w_conv, mh_gain, conv0, c0, n0, m0, *, nb, ts, nseq):
    tt = nseq * ts
    groups = [(z_a, j) for j in range(N_GROUPS_A)] + [(z_b, j) for j in range(N_GROUPS_B)]

    in_specs = [pl.BlockSpec(memory_space=pltpu.SMEM)]
    in_specs += [pl.BlockSpec((tt, D_MODEL), lambda b, j=j: (b, j)) for _, j in groups]
    state_specs = [
        pl.BlockSpec((nseq, CONV_K - 1, D_MODEL), lambda b: (b, 0, 0)),
        pl.BlockSpec((nseq, N_HEADS, HEAD_QK, HEAD_V), lambda b: (b, 0, 0, 0)),
        pl.BlockSpec((nseq, N_HEADS, HEAD_QK), lambda b: (b, 0, 0)),
        pl.BlockSpec((nseq, 1, N_HEADS), lambda b: (b, 0, 0)),
    ]
    in_specs += [
        pl.BlockSpec((tt, LANES), lambda b: (b, 0)),
        pl.BlockSpec((nseq, SUBLANES, ts), lambda b: (b, 0, 0)),
        pl.BlockSpec((CONV_K, D_MODEL), lambda b: (0, 0)),
        pl.BlockSpec((1, D_MODEL), lambda b: (0, 0)),
    ] + state_specs
    out_specs = [pl.BlockSpec((tt, D_MODEL), lambda b: (b, 0))] + state_specs
    out_shape = [
        jax.ShapeDtypeStruct((nb * ts, D_MODEL), F32),
        jax.ShapeDtypeStruct((nb, CONV_K - 1, D_MODEL), F32),
        jax.ShapeDtypeStruct((nb, N_HEADS, HEAD_QK, HEAD_V), F32),
        jax.ShapeDtypeStruct((nb, N_HEADS, HEAD_QK), F32),
        jax.ShapeDtypeStruct((nb, 1, N_HEADS), F32),
    ]
    return pl.pallas_call(
        functools.partial(_mixer_kernel, ts),
        grid=(nb // nseq,),
        in_specs=in_specs,
        out_specs=out_specs,
        out_shape=out_shape,
        scratch_shapes=[pltpu.VMEM((tt, D_MODEL), F32)],
        compiler_params=_cparams(("parallel",)),
        name="mixer_sample",
    )(b_gates, *[z for z, _ in groups], zg, gt3, w_conv, mh_gain, conv0, c0, n0, m0)


def _prompt_kernel(chunk, nb, ns, alpha, bg_ref, x_ref, wa_ref, wb_ref, wgh_ref, wgl_ref, wconv_ref,
                   gain_ref, wout_ref, g_ref, b_ref, xs_ref, mixs_ref,
                   x1_ref, x1r_ref, conv_ref, c_ref, n_ref, m_ref, mo_ref):
    b = pl.program_id(0)
    t = pl.program_id(1)

    @pl.when(b < nb)
    def _():
        @pl.when(t == 0)
        def _():
            conv_ref[...] = jnp.zeros_like(conv_ref)
            c_ref[...] = jnp.zeros_like(c_ref)
            n_ref[...] = jnp.zeros_like(n_ref)
            m_ref[...] = jnp.zeros_like(m_ref)

        x = x_ref[...]
        xb = x.astype(BF16)

        def z(j):
            w_ref, jj = (wa_ref, j) if j < N_GROUPS_A else (wb_ref, j - N_GROUPS_A)
            return _dot(xb, w_ref[:, jj * D_MODEL:(jj + 1) * D_MODEL])

        a = _conv_branch(z(1) * z(2), z(0), wconv_ref[...], conv_ref)
        zg = _dot3(x, wgh_ref[...], wgl_ref[...])
        _mlstm_branch(chunk, bg_ref, zg, zg.T, z(3), z(4), z(5), gain_ref,
                      c_ref, n_ref, m_ref, mo_ref)
        mix = _sigmoid(z(6)) * a + _sigmoid(z(7)) * mo_ref[...]
        _store_x1(x, mix, wout_ref, g_ref, b_ref, alpha, x1_ref, x1r_ref)

    @pl.when(jnp.logical_and(b == nb, t < ns))
    def _():
        _store_x1(xs_ref[...], mixs_ref[...], wout_ref, g_ref, b_ref, alpha, x1_ref, x1r_ref)


def _prompt_layer_half(xp, xs, mix_s, b_gates, w_a, w_b, wg_hi, wg_lo, w_conv, mh_gain, w_out,
                       ln_g, ln_b, alpha, *, nb, nt, tt, chunk):
    mp, ms = xp.shape[0], xs.shape[0]
    ns = ms // tt
    last_p = nb * nt - 1

    def prow(b, t):
        return jnp.minimum(b * nt + t, last_p)

    def srow(b, t):
        return jnp.where(b == nb, jnp.minimum(t, ns - 1), 0)

    def orow(b, t):
        return jnp.where(b == nb, nb * nt + jnp.minimum(t, ns - 1), b * nt + t)

    def state(b):
        return jnp.minimum(b, nb - 1)

    const2 = lambda b, t: (0, 0)
    in_specs = [
        pl.BlockSpec(memory_space=pltpu.SMEM),
        pl.BlockSpec((tt, D_MODEL), lambda b, t: (prow(b, t), 0)),
        pl.BlockSpec(w_a.shape, const2, pipeline_mode=pl.Buffered(1)),
        pl.BlockSpec(w_b.shape, const2, pipeline_mode=pl.Buffered(1)),
        pl.BlockSpec(wg_hi.shape, const2),
        pl.BlockSpec(wg_lo.shape, const2),
        pl.BlockSpec((CONV_K, D_MODEL), const2),
        pl.BlockSpec((1, D_MODEL), const2),
        pl.BlockSpec((D_MODEL, D_MODEL), const2),
        pl.BlockSpec((1, D_MODEL), const2),
        pl.BlockSpec((1, D_MODEL), const2),
        pl.BlockSpec((tt, D_MODEL), lambda b, t: (srow(b, t), 0)),
        pl.BlockSpec((tt, D_MODEL), lambda b, t: (srow(b, t), 0)),
    ]
    out_specs = [
        pl.BlockSpec((tt, D_MODEL), lambda b, t: (orow(b, t), 0)),
        pl.BlockSpec((tt * ROW_TILES, LANES), lambda b, t: (orow(b, t), 0)),
        pl.BlockSpec((1, CONV_K - 1, D_MODEL), lambda b, t: (state(b), 0, 0)),
        pl.BlockSpec((1, N_HEADS, HEAD_QK, HEAD_V), lambda b, t: (state(b), 0, 0, 0)),
        pl.BlockSpec((1, N_HEADS, HEAD_QK), lambda b, t: (state(b), 0, 0)),
        pl.BlockSpec((1, 1, N_HEADS), lambda b, t: (state(b), 0, 0)),
    ]
    out_shape = [
        jax.ShapeDtypeStruct((mp + ms, D_MODEL), F32),
        jax.ShapeDtypeStruct(((mp + ms) * ROW_TILES, LANES), F32),
        jax.ShapeDtypeStruct((nb, CONV_K - 1, D_MODEL), F32),
        jax.ShapeDtypeStruct((nb, N_HEADS, HEAD_QK, HEAD_V), F32),
        jax.ShapeDtypeStruct((nb, N_HEADS, HEAD_QK), F32),
        jax.ShapeDtypeStruct((nb, 1, N_HEADS), F32),
    ]
    return pl.pallas_call(
        functools.partial(_prompt_kernel, chunk, nb, ns, alpha),
        grid=(nb + 1, nt),
        in_specs=in_specs,
        out_specs=out_specs,
        out_shape=out_shape,
        scratch_shapes=[pltpu.VMEM((tt, D_MODEL), F32)],
        compiler_params=_cparams(("arbitrary", "arbitrary")),
        name="prompt_half",
    )(b_gates, xp, w_a, w_b, wg_hi, wg_lo, w_conv, mh_gain, w_out, ln_g, ln_b, xs, mix_s)


def _router_kernel(x_ref, wh_ref, wl_ref, br_ref, idx_ref, gate_ref, rank_ref, cnt_ref):
    tt = x_ref.shape[0]

    @pl.when(pl.program_id(0) == 0)
    def _():
        cnt_ref[...] = jnp.zeros_like(cnt_ref)

    scores = _sigmoid(_dot3(x_ref[...], wh_ref[...], wl_ref[...]))
    lane = lax.broadcasted_iota(jnp.int32, (tt, N_EXPERTS), 1)
    work = scores + br_ref[...]
    picked = []
    sel = []
    member = jnp.zeros((tt, N_EXPERTS), F32)
    for _ in range(TOP_K):
        mx = jnp.max(work, axis=1, keepdims=True)
        ik = jnp.min(jnp.where(work == mx, lane, N_EXPERTS), axis=1, keepdims=True)
        onehot = lane == ik
        picked.append((ik, onehot))
        sel.append(jnp.sum(jnp.where(onehot, scores, 0.0), axis=1, keepdims=True))
        work = jnp.where(onehot, -jnp.inf, work)
        member = member + onehot.astype(F32)
    total = sel[0]
    for sk in sel[1:]:
        total = total + sk

    ti = lax.broadcasted_iota(jnp.int32, (tt, tt), 0)
    si = lax.broadcasted_iota(jnp.int32, (tt, tt), 1)
    earlier = (si < ti).astype(BF16)
    before = _dot(earlier, member.astype(BF16)) + cnt_ref[...]

    lane_o = lax.broadcasted_iota(jnp.int32, (tt, TOP_K), 1)
    idx_o = jnp.zeros((tt, TOP_K), jnp.int32)
    gate_o = jnp.zeros((tt, TOP_K), F32)
    rank_o = jnp.zeros((tt, TOP_K), jnp.int32)
    for kk in range(TOP_K):
        ik, onehot = picked[kk]
        rk = jnp.sum(jnp.where(onehot, before, 0.0), axis=1, keepdims=True)
        idx_o = jnp.where(lane_o == kk, ik, idx_o)
        gate_o = jnp.where(lane_o == kk, sel[kk] / total * ROUTED_SCALE, gate_o)
        rank_o = jnp.where(lane_o == kk, rk.astype(jnp.int32), rank_o)
    idx_ref[...] = idx_o
    gate_ref[...] = gate_o
    rank_ref[...] = rank_o
    cnt_ref[...] = cnt_ref[...] + jnp.sum(member, axis=0, keepdims=True)


def _router(x1, wr_hi, wr_lo, b_router, tt):
    m = x1.shape[0]
    return pl.pallas_call(
        _router_kernel,
        grid=(m // tt,),
        in_specs=[pl.BlockSpec((tt, D_MODEL), lambda i: (i, 0)),
                  pl.BlockSpec((D_MODEL, N_EXPERTS), lambda i: (0, 0)),
                  pl.BlockSpec((D_MODEL, N_EXPERTS), lambda i: (0, 0)),
                  pl.BlockSpec((1, N_EXPERTS), lambda i: (0, 0))],
        out_specs=[pl.BlockSpec((tt, TOP_K), lambda i: (i, 0)),
                   pl.BlockSpec((tt, TOP_K), lambda i: (i, 0)),
                   pl.BlockSpec((tt, TOP_K), lambda i: (i, 0)),
                   pl.BlockSpec((1, N_EXPERTS), lambda i: (0, 0))],
        out_shape=[jax.ShapeDtypeStruct((m, TOP_K), jnp.int32),
                   jax.ShapeDtypeStruct((m, TOP_K), F32),
                   jax.ShapeDtypeStruct((m, TOP_K), jnp.int32),
                   jax.ShapeDtypeStruct((1, N_EXPERTS), F32)],
        compiler_params=_cparams(("arbitrary",)),
        name="router",
    )(x1, wr_hi, wr_lo, b_router)


def _row_copy(src, src_row8, dst, dst_row8, sem):
    return pltpu.make_async_copy(src.at[pl.ds(pl.multiple_of(src_row8, ROW_TILES), ROW_TILES)],
                                 dst.at[pl.ds(pl.multiple_of(dst_row8, ROW_TILES), ROW_TILES)],
                                 sem)


def _dest_kernel(idx_ref, rank_ref, start_ref, dest_ref):
    idx = idx_ref[...]
    shape = idx.shape
    low = jnp.broadcast_to(start_ref[:, :LANES], shape)
    high = jnp.broadcast_to(start_ref[:, LANES:], shape)
    lane = jnp.bitwise_and(idx, LANES - 1)
    start = jnp.where(idx < LANES, jnp.take_along_axis(low, lane, axis=1),
                      jnp.take_along_axis(high, lane, axis=1))
    dest_ref[...] = (start + rank_ref[...]) * ROW_TILES


def _dest(idx, rank, seg_start):
    assert N_EXPERTS == 2 * LANES
    full = pl.BlockSpec(idx.shape, lambda i: (0, 0))
    return pl.pallas_call(
        _dest_kernel,
        grid=(1,),
        in_specs=[full, full, pl.BlockSpec((1, N_EXPERTS), lambda i: (0, 0))],
        out_specs=full,
        out_shape=jax.ShapeDtypeStruct(idx.shape, jnp.int32),
        compiler_params=_cparams(("arbitrary",)),
        name="dest",
    )(idx, rank, seg_start[None, :])


def _dispatch_kernel(tt, dest_ref, x1r_ref, xg_ref, zbuf, sem, zsem):
    @pl.when(pl.program_id(0) == 0)
    def _():
        zbuf[...] = jnp.zeros_like(zbuf)
        n_sorted = xg_ref.shape[0] - zbuf.shape[0]
        fill = pltpu.make_async_copy(zbuf, xg_ref.at[pl.ds(n_sorted, zbuf.shape[0])], zsem)
        fill.start()
        fill.wait()

    def issue(t, carry):
        for kk in range(TOP_K):
            _row_copy(x1r_ref, t * ROW_TILES, xg_ref, dest_ref[t * TOP_K + kk],
                      sem).start(priority=kk % 2)
        return carry

    lax.fori_loop(0, tt, issue, 0)
    n_rows = tt * TOP_K * ROW_TILES
    pltpu.make_async_copy(xg_ref.at[pl.ds(0, n_rows)], xg_ref.at[pl.ds(0, n_rows)], sem).wait()


def _dispatch(dest8, x1r, tt, slack):
    m = x1r.shape[0] // ROW_TILES
    return pl.pallas_call(
        functools.partial(_dispatch_kernel, tt),
        grid=(m // tt,),
        in_specs=[pl.BlockSpec((tt * TOP_K,), lambda i: (i,), memory_space=pltpu.SMEM),
                  pl.BlockSpec((tt * ROW_TILES, LANES), lambda i: (i, 0))],
        out_specs=pl.BlockSpec(memory_space=pl.ANY),
        out_shape=jax.ShapeDtypeStruct(((m * TOP_K + slack) * ROW_TILES, LANES), F32),
        scratch_shapes=[pltpu.VMEM((slack * ROW_TILES, LANES), F32),
                        pltpu.SemaphoreType.DMA(()), pltpu.SemaphoreType.DMA(())],
        compiler_params=_cparams(("arbitrary",)),
        name="dispatch",
    )(dest8, x1r)


def _experts_kernel(rb, e_ref, row_ref, valid_ref, new_ref, slot_ref, nxt_ref,
                    xg_hbm, wg_hbm, wu_hbm, wd_hbm, y_hbm,
                    *scratch):
    j = pl.program_id(0)
    n_steps = pl.num_programs(0)
    ring_pos = lax.rem(j, EXPERT_RING)
    valid = valid_ref[j]
    xbufs = scratch[:EXPERT_RING]
    ybufs = scratch[EXPERT_RING:2 * EXPERT_RING]
    wg_buf, wu_buf, wd_buf, wgb, wub, wdb, wsem, xsem, ysem = scratch[2 * EXPERT_RING:]

    def tile_rows(row, n):
        return pl.ds(pl.multiple_of(row * ROW_TILES, ROW_TILES), n * ROW_TILES)

    row_variants = [rb]

    def for_row_variant(n_valid, fn):
        for below, n in zip([0] + row_variants[:-1], row_variants):
            @pl.when(jnp.logical_and(n_valid > below, n_valid <= n))
            def _(n=n):
                fn(n)

    def x_copies(item, slot, start):
        def run(n):
            copy = pltpu.make_async_copy(xg_hbm.at[tile_rows(row_ref[item], n)],
                                         xbufs[slot].at[pl.ds(0, n * ROW_TILES)], xsem.at[slot])
            if start:
                copy.start()
            else:
                copy.wait()

        for_row_variant(valid_ref[item], run)

    def y_copies(item, slot, start):
        row0 = row_ref[item]
        n = valid_ref[item]

        def run(copy):
            if start:
                copy.start()
            else:
                copy.wait()

        @pl.when(n == rb)
        def _():
            run(pltpu.make_async_copy(ybufs[slot], y_hbm.at[tile_rows(row0, rb)], ysem.at[slot]))

        @pl.when(n < rb)
        def _():
            piece = rb // 2
            while piece >= 1:
                off = n - jnp.bitwise_and(n, 2 * piece - 1)

                @pl.when(jnp.bitwise_and(n, piece) != 0)
                def _(off=off, piece=piece):
                    run(pltpu.make_async_copy(ybufs[slot].at[tile_rows(off, piece)],
                                              y_hbm.at[tile_rows(row0 + off, piece)],
                                              ysem.at[slot]))

                piece //= 2

    def fetch(e, slot):
        return (pltpu.make_async_copy(wg_hbm.at[e], wg_buf.at[slot], wsem.at[slot, 0]),
                pltpu.make_async_copy(wu_hbm.at[e], wu_buf.at[slot], wsem.at[slot, 1]),
                pltpu.make_async_copy(wd_hbm.at[e], wd_buf.at[slot], wsem.at[slot, 2]))

    ahead = EXPERT_RING - 1

    @pl.when(j == 0)
    def _():
        for item in range(ahead):
            x_copies(item, item, start=True)

    ahead_item = jnp.minimum(j + ahead, n_steps - 1)
    for slot in range(EXPERT_RING):
        here = ring_pos == slot

        @pl.when(jnp.logical_and(here, j >= EXPERT_RING))
        def _(slot=slot):
            y_copies(j - EXPERT_RING, slot, start=False)

        @pl.when(jnp.logical_and(here, valid > 0))
        def _(slot=slot):
            x_copies(j, slot, start=False)

            @pl.when(j + ahead < n_steps)
            def _():
                x_copies(ahead_item, (slot + ahead) % EXPERT_RING, start=True)

    @pl.when(new_ref[j] != 0)
    def _():
        slot = slot_ref[j]

        @pl.when(j == 0)
        def _():
            for copy in fetch(e_ref[j], slot):
                copy.start()

        for copy in fetch(e_ref[j], slot):
            copy.wait()
        wgb[...] = wg_buf[slot].astype(BF16)
        wub[...] = wu_buf[slot].astype(BF16)
        wdb[...] = wd_buf[slot].astype(BF16)

        @pl.when(nxt_ref[j] >= 0)
        def _():
            for copy in fetch(nxt_ref[j], 1 - slot):
                copy.start()

    def swiglu_rows(slot, n):
        xb = jnp.concatenate(_load_row_chunks(xbufs[slot], n), axis=1).astype(BF16)
        g = _dot(xb, wgb[...])
        u = _dot(xb, wub[...])
        y = _dot((g * _sigmoid(g) * u).astype(BF16), wdb[...])
        _store_rows(ybufs[slot], y)

    for slot in range(EXPERT_RING):
        here = ring_pos == slot

        @pl.when(jnp.logical_and(here, valid > 0))
        def _(slot=slot):
            for_row_variant(valid, functools.partial(swiglu_rows, slot))
            y_copies(j, slot, start=True)

        @pl.when(jnp.logical_and(here, j == n_steps - 1))
        def _(slot=slot):
            for back in range(EXPERT_RING - 1, -1, -1):
                y_copies(j - back, (slot - back) % EXPERT_RING, start=False)


def _experts(item_e, item_row, item_valid, item_new, item_slot, item_nxt,
             xg, w_eg, w_eu, w_ed, rb):
    n_items = item_e.shape[0]
    n_rows = xg.shape[0] // ROW_TILES - rb
    hbm = pl.BlockSpec(memory_space=pl.ANY)
    row_buf = pltpu.VMEM((rb * ROW_TILES, LANES), F32)
    grid_spec = pltpu.PrefetchScalarGridSpec(
        num_scalar_prefetch=6,
        grid=(n_items,),
        in_specs=[hbm, hbm, hbm, hbm],
        out_specs=hbm,
        scratch_shapes=[row_buf] * (2 * EXPERT_RING) + [
                        pltpu.VMEM((2, D_MODEL, D_EXPERT), F32),
                        pltpu.VMEM((2, D_MODEL, D_EXPERT), F32),
                        pltpu.VMEM((2, D_EXPERT, D_MODEL), F32),
                        pltpu.VMEM((D_MODEL, D_EXPERT), BF16),
                        pltpu.VMEM((D_MODEL, D_EXPERT), BF16),
                        pltpu.VMEM((D_EXPERT, D_MODEL), BF16),
                        pltpu.SemaphoreType.DMA((2, 3)),
                        pltpu.SemaphoreType.DMA((EXPERT_RING,)),
                        pltpu.SemaphoreType.DMA((EXPERT_RING,))],
    )
    return pl.pallas_call(
        functools.partial(_experts_kernel, rb),
        grid_spec=grid_spec,
        out_shape=jax.ShapeDtypeStruct((n_rows * ROW_TILES, LANES), F32),
        compiler_params=_cparams(("arbitrary",)),
        name="experts",
    )(item_e, item_row, item_valid, item_new, item_slot, item_nxt, xg, w_eg, w_eu, w_ed)


def _combine_kernel(alpha, n_p, n_steps, dest0_ref, dest1_ref, dest2_ref, gate_ref, x1_ref, y_ref,
                    wsg_ref, wsu_ref, wsd_ref, g_ref, b_ref, x2p_ref, x2s_ref,
                    base_ref, x2_ref, *ring):
    step = pl.program_id(0)
    tt = x1_ref.shape[0]
    n_rows = TOP_K * tt * ROW_TILES
    group = COMBINE_GROUP
    n_groups = tt // group
    bufs, sem = ring[:-1], ring[-1]
    ahead = COMBINE_RING - 1

    def issue_group(dref, grp, slot):
        for r in range(group):
            t = grp * group + r
            for kk in range(TOP_K):
                _row_copy(y_ref, dref[t * TOP_K + kk], bufs[slot],
                          (kk * tt + t) * ROW_TILES, sem.at[slot]).start(priority=kk % 2)

    def issue_tile(dref, slot):
        def body(grp, carry):
            issue_group(dref, grp, slot)
            return carry

        lax.fori_loop(0, n_groups, body, 0)

    def reduce_group(grp, slot):
        rows = pl.ds(pl.multiple_of(grp * group, group), group)
        gates = gate_ref[rows, :]
        chunks = []
        for s in range(ROW_TILES):
            acc = jnp.zeros((group, LANES), F32)
            for kk in range(TOP_K):
                row0 = kk * tt * ROW_TILES + grp * (group * ROW_TILES) + s
                acc = acc + gates[:, kk:kk + 1] * bufs[slot][pl.ds(row0, group,
                                                                    stride=ROW_TILES), :]
            chunks.append(acc)
        routed = jnp.concatenate(chunks, axis=1)
        x2_ref[rows, :] = _layer_norm(base_ref[rows, :] + routed, g_ref[...], b_ref[...])

    @pl.when(step == 0)
    def _():
        issue_tile(dest0_ref, 0)
        if n_steps > 1:
            issue_tile(dest1_ref, 1)

    x1 = x1_ref[...]
    xb = x1.astype(BF16)
    gs = _dot(xb, wsg_ref[...])
    us = _dot(xb, wsu_ref[...])
    base_ref[...] = alpha * x1 + _dot((gs * _sigmoid(gs) * us).astype(BF16), wsd_ref[...])

    for slot in range(COMBINE_RING):
        here = lax.rem(step, COMBINE_RING) == slot
        ahead_slot = (slot + ahead) % COMBINE_RING

        @pl.when(here)
        def _(slot=slot):
            pltpu.make_async_copy(y_ref.at[pl.ds(0, n_rows)], bufs[slot], sem.at[slot]).wait()

        @pl.when(jnp.logical_and(here, step + ahead < n_steps))
        def _(slot=slot, ahead_slot=ahead_slot):
            def body(grp, carry):
                issue_group(dest2_ref, grp, ahead_slot)
                reduce_group(grp, slot)
                return carry

            lax.fori_loop(0, n_groups, body, 0)

        @pl.when(jnp.logical_and(here, step + ahead >= n_steps))
        def _(slot=slot):
            def body(grp, carry):
                reduce_group(grp, slot)
                return carry

            lax.fori_loop(0, n_groups, body, 0)

    @pl.when(step < n_p)
    def _():
        x2p_ref[...] = x2_ref[...]

    @pl.when(step >= n_p)
    def _():
        x2s_ref[...] = x2_ref[...]


def _combine(dest8, gates, x1, y, wsg, wsu, wsd, g, b, alpha, tt, mp):
    m = x1.shape[0]
    n_steps = m // tt
    n_p = mp // tt

    def dest_spec(ahead):
        return pl.BlockSpec((tt * TOP_K,), lambda i: (jnp.minimum(i + ahead, n_steps - 1),),
                            memory_space=pltpu.SMEM)

    return pl.pallas_call(
        functools.partial(_combine_kernel, alpha, n_p, n_steps),
        grid=(n_steps,),
        in_specs=[dest_spec(0), dest_spec(1), dest_spec(2),
                  pl.BlockSpec((tt, TOP_K), lambda i: (i, 0)),
                  pl.BlockSpec((tt, D_MODEL), lambda i: (i, 0)),
                  pl.BlockSpec(memory_space=pl.ANY),
                  pl.BlockSpec((D_MODEL, D_EXPERT), lambda i: (0, 0)),
                  pl.BlockSpec((D_MODEL, D_EXPERT), lambda i: (0, 0)),
                  pl.BlockSpec((D_EXPERT, D_MODEL), lambda i: (0, 0)),
                  pl.BlockSpec((1, D_MODEL), lambda i: (0, 0)),
                  pl.BlockSpec((1, D_MODEL), lambda i: (0, 0))],
        out_specs=[pl.BlockSpec((tt, D_MODEL), lambda i: (jnp.minimum(i, n_p - 1), 0)),
                   pl.BlockSpec((tt, D_MODEL), lambda i: (jnp.maximum(i - n_p, 0), 0))],
        out_shape=[jax.ShapeDtypeStruct((mp, D_MODEL), F32),
                   jax.ShapeDtypeStruct((m - mp, D_MODEL), F32)],
        scratch_shapes=[pltpu.VMEM((tt, D_MODEL), F32), pltpu.VMEM((tt, D_MODEL), F32)]
        + [pltpu.VMEM((TOP_K * tt * ROW_TILES, LANES), F32)] * COMBINE_RING
        + [pltpu.SemaphoreType.DMA((COMBINE_RING,))],
        compiler_params=_cparams(("arbitrary",)),
        name="combine",
    )(dest8, dest8, dest8, gates, x1, y, wsg, wsu, wsd, g, b)


PROJ_TM = 512
PROJ_TN = 1024
MIX_TT = 256
MIX_CHUNK = 256
SAMPLE_NSEQ = 8
ROUTER_TT = 512
DISPATCH_TT = 1024
EXPERT_RB = 256
EXPERT_RING = 3
COMBINE_TT = 256
COMBINE_RING = 3
COMBINE_GROUP = 64


def _moe(x1, x1r, mp, w_router, b_router, w_eg, w_eu, w_ed, w_sg, w_su, w_sd, ln_g, ln_b, alpha):
    m = x1.shape[0]
    wr_hi, wr_lo = _split_hi_lo(w_router)
    idx, gates, rank, cnt = _router(x1, wr_hi, wr_lo, b_router[None, :], ROUTER_TT)

    rb = EXPERT_RB
    n_rows = m * TOP_K
    n_items = n_rows // rb + N_EXPERTS
    counts = cnt[0].astype(jnp.int32)
    seg_end = jnp.cumsum(counts)
    seg_start = seg_end - counts
    n_chunk = (counts + rb - 1) // rb
    item_end = jnp.cumsum(n_chunk)
    item_start = item_end - n_chunk
    item = jnp.arange(n_items, dtype=jnp.int32)
    item_c = jnp.minimum(item, item_end[-1] - 1)
    item_e = jnp.sum((item_end[None, :] <= item_c[:, None]).astype(jnp.int32), axis=1)
    onehot_e = item_e[:, None] == jnp.arange(N_EXPERTS, dtype=jnp.int32)[None, :]

    def pick(table):
        return jnp.sum(jnp.where(onehot_e, table[None, :], 0), axis=1)

    chunk = item_c - pick(item_start)
    valid = item < item_end[-1]
    item_row = pick(seg_start) + chunk * rb
    item_valid = jnp.where(valid, jnp.clip(pick(counts) - chunk * rb, 0, rb), 0)

    experts = jnp.arange(N_EXPERTS, dtype=jnp.int32)
    nonempty = counts > 0
    slot_e = (jnp.cumsum(nonempty.astype(jnp.int32)) - 1) % 2
    later = jnp.logical_and(experts[None, :] > experts[:, None], nonempty[None, :])
    nxt_e = jnp.min(jnp.where(later, experts[None, :], N_EXPERTS), axis=1)
    nxt_e = jnp.where(nxt_e == N_EXPERTS, -1, nxt_e)
    item_new = jnp.logical_and(valid, chunk == 0).astype(jnp.int32)

    dest8 = _dest(idx.reshape(-1, LANES), rank.reshape(-1, LANES), seg_start).reshape(-1)
    xg = _dispatch(dest8, x1r, DISPATCH_TT, rb)
    y = _experts(item_e, item_row, item_valid, item_new, pick(slot_e), pick(nxt_e),
                 xg, w_eg, w_eu, w_ed, rb)
    return _combine(dest8, gates, x1, y, w_sg.astype(BF16), w_su.astype(BF16),
                    w_sd.astype(BF16), ln_g[None, :], ln_b[None, :], alpha, COMBINE_TT, mp)


def _layer(xp, xs, conv_s, c_s, n_s, m_s, w_in, b_gates, w_conv, mh_gain, w_out, ln1_g, ln1_b,
           w_router, b_router, w_eg, w_eu, w_ed, w_sg, w_su, w_sd, ln2_g, ln2_b, alpha):
    bp, tp, _ = xp.shape
    bs, ts, _ = xs.shape
    mp, ms = bp * tp, bs * ts
    xp2 = xp.reshape(mp, D_MODEL)
    xs2 = xs.reshape(ms, D_MODEL)

    w_a = w_in[:, :GATE_COL0].astype(BF16)
    w_b = w_in[:, GATE_COL0 + 2 * N_HEADS:].astype(BF16)
    w_gate = jnp.pad(w_in[:, GATE_COL0:GATE_COL0 + 2 * N_HEADS],
                     ((0, 0), (0, LANES - 2 * N_HEADS)))
    wg_hi, wg_lo = _split_hi_lo(w_gate)

    zg_s = _gate_proj(xs2, wg_hi, wg_lo, PROJ_TM)
    gt_s = zg_s[:, :SUBLANES].reshape(bs, ts, SUBLANES).transpose(0, 2, 1)
    mix_s, conv_n, c_n, n_n, m_n = _mixer(
        _proj(xs2, w_a, PROJ_TM, PROJ_TN), _proj(xs2, w_b, PROJ_TM, PROJ_TN),
        zg_s, gt_s, b_gates, w_conv, mh_gain[None, :],
        conv_s, c_s, n_s, m_s[:, None, :], nb=bs, ts=ts, nseq=SAMPLE_NSEQ)

    x1, x1r, conv_p, c_p, n_p, m_p = _prompt_layer_half(
        xp2, xs2, mix_s, b_gates, w_a, w_b, wg_hi, wg_lo, w_conv, mh_gain[None, :],
        w_out.astype(BF16), ln1_g[None, :], ln1_b[None, :], alpha,
        nb=bp, nt=tp // MIX_TT, tt=MIX_TT, chunk=MIX_CHUNK)

    x2p, x2s = _moe(x1, x1r, mp, w_router, b_router, w_eg, w_eu, w_ed, w_sg, w_su, w_sd,
                    ln2_g, ln2_b, alpha)
    states_p = (conv_p, c_p, n_p, m_p[:, 0, :])
    states_s = (conv_n, c_n, n_n, m_n[:, 0, :])
    return x2p.reshape(bp, tp, D_MODEL), x2s.reshape(bs, ts, D_MODEL), states_p, states_s


def kernel(x_prompt, x_sample, cache_conv, state_mlstm_C, state_mlstm_n, state_mlstm_m, w_in, b_gates, w_conv, mh_gain, w_out, ln1_g, ln1_b, w_router, b_router, w_exp_gate, w_exp_up, w_exp_down, w_sh_gate, w_sh_up, w_sh_down, ln2_g, ln2_b):
    depth = w_in.shape[0]
    alpha = (2.0 * depth) ** 0.25
    hp, hs = x_prompt, x_sample
    outs_p = [[], [], [], []]
    outs_s = [[], [], [], []]
    for l in range(depth):
        hp, hs, st_p, st_s = _layer(
            hp, hs, cache_conv[l], state_mlstm_C[l], state_mlstm_n[l], state_mlstm_m[l],
            w_in[l], b_gates[l], w_conv[l], mh_gain[l], w_out[l], ln1_g[l], ln1_b[l],
            w_router[l], b_router[l], w_exp_gate[l], w_exp_up[l], w_exp_down[l],
            w_sh_gate[l], w_sh_up[l], w_sh_down[l], ln2_g[l], ln2_b[l], alpha)
        for acc, val in zip(outs_p, st_p):
            acc.append(val)
        for acc, val in zip(outs_s, st_s):
            acc.append(val)
    return (hp, hs) + tuple(jnp.stack(a) for a in outs_p) + tuple(jnp.stack(a) for a in outs_s)
```

```python
import functools

import jax
import jax.numpy as jnp
from jax import lax
from jax.experimental import pallas as pl
from jax.experimental.pallas import tpu as pltpu

F32 = jnp.float32
BF16 = jnp.bfloat16

D_MODEL = 1024
N_HEADS = 4
HEAD_V = 256
HEAD_QK = 128
N_EXPERTS = 256
TOP_K = 8
D_EXPERT = 256
ROUTED_SCALE = 2.5
LN_EPS = 1e-5
CONV_K = 3

LANES = 128
SUBLANES = 8
ROW_TILES = D_MODEL // LANES
N_GROUPS_A = 6
N_GROUPS_B = 2
GATE_COL0 = N_GROUPS_A * D_MODEL

VMEM_LIMIT = 56 * 1024 * 1024


def _cparams(sem):
    return pltpu.CompilerParams(dimension_semantics=sem, vmem_limit_bytes=VMEM_LIMIT)


def _sigmoid(x):
    return 1.0 / (1.0 + jnp.exp(-x))


def _log_sigmoid(x):
    return jnp.minimum(x, 0.0) - jnp.log(1.0 + jnp.exp(-jnp.abs(x)))


def _layer_norm(x, g, b):
    mu = jnp.mean(x, axis=-1, keepdims=True)
    xc = x - mu
    var = jnp.mean(xc * xc, axis=-1, keepdims=True)
    return xc * lax.rsqrt(var + LN_EPS) * g + b


def _split_hi_lo(x):
    hi = x.astype(BF16)
    lo = (x - hi.astype(F32)).astype(BF16)
    return hi, lo


def _dot(a, b):
    return jnp.dot(a, b, preferred_element_type=F32)


def _store_rows(ref, v):
    n = v.shape[0]
    for s in range(ROW_TILES):
        ref[pl.ds(s, n, stride=ROW_TILES), :] = v[:, s * LANES:(s + 1) * LANES]


def _load_row_chunks(ref, n, row0=0):
    return [ref[pl.ds(row0 + s, n, stride=ROW_TILES), :] for s in range(ROW_TILES)]


def _dot3(x, w_hi, w_lo):
    x_hi, x_lo = _split_hi_lo(x)
    return _dot(x_hi, w_hi) + _dot(x_lo, w_hi) + _dot(x_hi, w_lo)


def _proj_kernel(x_ref, w_ref, z_ref):
    z_ref[...] = _dot(x_ref[...].astype(BF16), w_ref[...])


def _proj(x, w, tm, tn):
    m, k = x.shape
    n = w.shape[1]
    return pl.pallas_call(
        _proj_kernel,
        grid=(m // tm, n // tn),
        in_specs=[pl.BlockSpec((tm, k), lambda i, j: (i, 0)),
                  pl.BlockSpec((k, tn), lambda i, j: (0, j))],
        out_specs=pl.BlockSpec((tm, tn), lambda i, j: (i, j)),
        out_shape=jax.ShapeDtypeStruct((m, n), F32),
        compiler_params=_cparams(("parallel", "arbitrary")),
        name="proj",
    )(x, w)


def _gate_proj_kernel(x_ref, wh_ref, wl_ref, z_ref):
    z_ref[...] = _dot3(x_ref[...], wh_ref[...], wl_ref[...])


def _gate_proj(x, w_hi, w_lo, tm):
    m, k = x.shape
    n = w_hi.shape[1]
    return pl.pallas_call(
        _gate_proj_kernel,
        grid=(m // tm,),
        in_specs=[pl.BlockSpec((tm, k), lambda i: (i, 0)),
                  pl.BlockSpec((k, n), lambda i: (0, 0)),
                  pl.BlockSpec((k, n), lambda i: (0, 0))],
        out_specs=pl.BlockSpec((tm, n), lambda i: (i, 0)),
        out_shape=jax.ShapeDtypeStruct((m, n), F32),
        compiler_params=_cparams(("parallel",)),
        name="gate_proj",
    )(x, w_hi, w_lo)


def _conv_branch(u, zcb, wc, conv_ref, seq=0):
    tt = u.shape[0]
    carry = conv_ref[seq]
    rows = lax.broadcasted_iota(jnp.int32, (tt, D_MODEL), 0)
    u1 = jnp.where(rows == 0, carry[1:2, :], pltpu.roll(u, 1, 0))
    u2 = jnp.where(rows == 0, carry[0:1, :],
                   jnp.where(rows == 1, carry[1:2, :], pltpu.roll(u, 2, 0)))
    conv_out = u2 * wc[0:1, :] + u1 * wc[1:2, :] + u * wc[2:3, :]
    conv_ref[seq] = u[tt - 2:tt, :]
    return zcb * conv_out


def _mlstm_chunk(q, k, v, ig_col, lf_col, ig_row, lf_row, c_state, n_row, m_prev):
    chunk = q.shape[0]
    ti = lax.broadcasted_iota(jnp.int32, (chunk, chunk), 0)
    si = lax.broadcasted_iota(jnp.int32, (chunk, chunk), 1)
    causal = si <= ti
    b_col = jnp.sum(jnp.where(causal, lf_row, 0.0), axis=1, keepdims=True)
    b_row = jnp.sum(jnp.where(ti <= si, lf_col, 0.0), axis=0, keepdims=True)
    inter = b_col + m_prev
    dmat = jnp.where(causal, b_col - b_row + ig_row, -jnp.inf)
    m_t = jnp.maximum(inter, jnp.max(dmat, axis=1, keepdims=True))
    w_intra = jnp.exp(dmat - m_t)
    w_inter = jnp.exp(inter - m_t)
    qb = q.astype(BF16)
    kb = k.astype(BF16)
    vb = v.astype(BF16)
    s = lax.dot_general(qb, kb, (((1,), (1,)), ((), ())), preferred_element_type=F32) * w_intra
    num = w_inter * _dot(qb, c_state.astype(BF16)) + _dot(s.astype(BF16), vb)
    den = (w_inter * jnp.sum(q * n_row, axis=1, keepdims=True)
           + jnp.sum(s, axis=1, keepdims=True))
    h = num / jnp.maximum(jnp.abs(den), jnp.exp(-m_t))
    m_new = m_t[chunk - 1:chunk, :]
    b_last = b_col[chunk - 1:chunk, :]
    ws_col = jnp.exp(b_last - b_col + ig_col - m_new)
    cdecay = jnp.exp(inter[chunk - 1:chunk, :] - m_new)
    kw = k * ws_col
    c_new = cdecay * c_state + lax.dot_general(
        kw.astype(BF16), vb, (((0,), (0,)), ((), ())), preferred_element_type=F32)
    n_new = cdecay * n_row + jnp.sum(kw, axis=0, keepdims=True)
    return h, c_new, n_new, m_new


def _mlstm_branch(chunk, bg_ref, zg, gt, zqk, zv, zog, gain_ref, c_ref, n_ref, m_ref, mo_ref,
                  seq=0, row0=0):
    tt = zqk.shape[0]
    for h in range(N_HEADS):
        ig_col_all = zg[:, h:h + 1] + bg_ref[h]
        lf_col_all = _log_sigmoid(zg[:, N_HEADS + h:N_HEADS + h + 1] + bg_ref[N_HEADS + h])
        ig_row_all = gt[h:h + 1, :] + bg_ref[h]
        lf_row_all = _log_sigmoid(gt[N_HEADS + h:N_HEADS + h + 1, :] + bg_ref[N_HEADS + h])
        vcols = slice(h * HEAD_V, (h + 1) * HEAD_V)
        for c in range(tt // chunk):
            r0, r1 = c * chunk, (c + 1) * chunk
            q = zqk[r0:r1, h * HEAD_QK:(h + 1) * HEAD_QK]
            k = zqk[r0:r1, (N_HEADS + h) * HEAD_QK:(N_HEADS + h + 1) * HEAD_QK] * (HEAD_QK ** -0.5)
            hh, c_new, n_new, m_new = _mlstm_chunk(
                q, k, zv[r0:r1, vcols], ig_col_all[r0:r1, :], lf_col_all[r0:r1, :],
                ig_row_all[:, r0:r1], lf_row_all[:, r0:r1],
                c_ref[seq, h], n_ref[seq, h:h + 1, :], m_ref[seq, :, h:h + 1])
            c_ref[seq, h] = c_new
            n_ref[seq, h:h + 1, :] = n_new
            m_ref[seq, :, h:h + 1] = m_new
            mu = jnp.mean(hh, axis=-1, keepdims=True)
            hc = hh - mu
            var = jnp.mean(hc * hc, axis=-1, keepdims=True)
            hn = hc * lax.rsqrt(var + LN_EPS) * gain_ref[:, vcols]
            mo_ref[row0 + r0:row0 + r1, vcols] = hn * _sigmoid(zog[r0:r1, vcols])


def _store_x1(x, mix, wout_ref, g_ref, b_ref, alpha, x1_ref, x1r_ref):
    x1 = _layer_norm(alpha * x + _dot(mix.astype(BF16), wout_ref[...]), g_ref[...], b_ref[...])
    x1_ref[...] = x1
    _store_rows(x1r_ref, x1)


def _mixer_kernel(ts, bg_ref,
                  zcb_ref, zcc_ref, zch_ref, zqk_ref, zv_ref, zog_ref, zga_ref, zgb_ref,
                  zg_ref, gt_ref, wconv_ref, gain_ref,
                  conv0_ref, c0_ref, n0_ref, m0_ref,
                  mix_ref, conv_ref, c_ref, n_ref, m_ref, mo_ref):
    conv_ref[...] = conv0_ref[...]
    c_ref[...] = c0_ref[...]
    n_ref[...] = n0_ref[...]
    m_ref[...] = m0_ref[...]
    for seq in range(conv_ref.shape[0]):
        rows = slice(seq * ts, (seq + 1) * ts)
        a = _conv_branch(zcc_ref[rows, :] * zch_ref[rows, :], zcb_ref[rows, :], wconv_ref[...],
                         conv_ref, seq)
        _mlstm_branch(ts, bg_ref, zg_ref[rows, :], gt_ref[seq], zqk_ref[rows, :], zv_ref[rows, :],
                      zog_ref[rows, :], gain_ref, c_ref, n_ref, m_ref, mo_ref, seq, seq * ts)
        mix_ref[rows, :] = (_sigmoid(zga_ref[rows, :]) * a
                            + _sigmoid(zgb_ref[rows, :]) * mo_ref[rows, :])


def _mixer(z_a, z_b, zg, gt3, b_gates, w_conv, mh_gain, conv0, c0, n0, m0, *, nb, ts, nseq):
    tt = nseq * ts
    groups = [(z_a, j) for j in range(N_GROUPS_A)] + [(z_b, j) for j in range(N_GROUPS_B)]

    in_specs = [pl.BlockSpec(memory_space=pltpu.SMEM)]
    in_specs += [pl.BlockSpec((tt, D_MODEL), lambda b, j=j: (b, j)) for _, j in groups]
    state_specs = [
        pl.BlockSpec((nseq, CONV_K - 1, D_MODEL), lambda b: (b, 0, 0)),
        pl.BlockSpec((nseq, N_HEADS, HEAD_QK, HEAD_V), lambda b: (b, 0, 0, 0)),
        pl.BlockSpec((nseq, N_HEADS, HEAD_QK), lambda b: (b, 0, 0)),
        pl.BlockSpec((nseq, 1, N_HEADS), lambda b: (b, 0, 0)),
    ]
    in_specs += [
        pl.BlockSpec((tt, LANES), lambda b: (b, 0)),
        pl.BlockSpec((nseq, SUBLANES, ts), lambda b: (b, 0, 0)),
        pl.BlockSpec((CONV_K, D_MODEL), lambda b: (0, 0)),
        pl.BlockSpec((1, D_MODEL), lambda b: (0, 0)),
    ] + state_specs
    out_specs = [pl.BlockSpec((tt, D_MODEL), lambda b: (b, 0))] + state_specs
    out_shape = [
        jax.ShapeDtypeStruct((nb * ts, D_MODEL), F32),
        jax.ShapeDtypeStruct((nb, CONV_K - 1, D_MODEL), F32),
        jax.ShapeDtypeStruct((nb, N_HEADS, HEAD_QK, HEAD_V), F32),
        jax.ShapeDtypeStruct((nb, N_HEADS, HEAD_QK), F32),
        jax.ShapeDtypeStruct((nb, 1, N_HEADS), F32),
    ]
    return pl.pallas_call(
        functools.partial(_mixer_kernel, ts),
        grid=(nb // nseq,),
        in_specs=in_specs,
        out_specs=out_specs,
        out_shape=out_shape,
        scratch_shapes=[pltpu.VMEM((tt, D_MODEL), F32)],
        compiler_params=_cparams(("parallel",)),
        name="mixer_sample",
    )(b_gates, *[z for z, _ in groups], zg, gt3, w_conv, mh_gain, conv0, c0, n0, m0)


def _prompt_kernel(chunk, nb, ns, alpha, bg_ref, x_ref, wa_ref, wb_ref, wgh_ref, wgl_ref, wconv_ref,
                   gain_ref, wout_ref, g_ref, b_ref, xs_ref, mixs_ref,
                   x1_ref, x1r_ref, conv_ref, c_ref, n_ref, m_ref, mo_ref):
    b = pl.program_id(0)
    t = pl.program_id(1)

    @pl.when(b < nb)
    def _():
        @pl.when(t == 0)
        def _():
            conv_ref[...] = jnp.zeros_like(conv_ref)
            c_ref[...] = jnp.zeros_like(c_ref)
            n_ref[...] = jnp.zeros_like(n_ref)
            m_ref[...] = jnp.zeros_like(m_ref)

        x = x_ref[...]
        xb = x.astype(BF16)

        def z(j):
            w_ref, jj = (wa_ref, j) if j < N_GROUPS_A else (wb_ref, j - N_GROUPS_A)
            return _dot(xb, w_ref[:, jj * D_MODEL:(jj + 1) * D_MODEL])

        a = _conv_branch(z(1) * z(2), z(0), wconv_ref[...], conv_ref)
        zg = _dot3(x, wgh_ref[...], wgl_ref[...])
        _mlstm_branch(chunk, bg_ref, zg, zg.T, z(3), z(4), z(5), gain_ref,
                      c_ref, n_ref, m_ref, mo_ref)
        mix = _sigmoid(z(6)) * a + _sigmoid(z(7)) * mo_ref[...]
        _store_x1(x, mix, wout_ref, g_ref, b_ref, alpha, x1_ref, x1r_ref)

    @pl.when(jnp.logical_and(b == nb, t < ns))
    def _():
        _store_x1(xs_ref[...], mixs_ref[...], wout_ref, g_ref, b_ref, alpha, x1_ref, x1r_ref)


def _prompt_layer_half(xp, xs, mix_s, b_gates, w_a, w_b, wg_hi, wg_lo, w_conv, mh_gain, w_out,
                       ln_g, ln_b, alpha, *, nb, nt, tt, chunk):
    mp, ms = xp.shape[0], xs.shape[0]
    ns = ms // tt
    last_p = nb * nt - 1

    def prow(b, t):
        return jnp.minimum(b * nt + t, last_p)

    def srow(b, t):
        return jnp.where(b == nb, jnp.minimum(t, ns - 1), 0)

    def orow(b, t):
        return jnp.where(b == nb, nb * nt + jnp.minimum(t, ns - 1), b * nt + t)

    def state(b):
        return jnp.minimum(b, nb - 1)

    const2 = lambda b, t: (0, 0)
    in_specs = [
        pl.BlockSpec(memory_space=pltpu.SMEM),
        pl.BlockSpec((tt, D_MODEL), lambda b, t: (prow(b, t), 0)),
        pl.BlockSpec(w_a.shape, const2, pipeline_mode=pl.Buffered(1)),
        pl.BlockSpec(w_b.shape, const2, pipeline_mode=pl.Buffered(1)),
        pl.BlockSpec(wg_hi.shape, const2),
        pl.BlockSpec(wg_lo.shape, const2),
        pl.BlockSpec((CONV_K, D_MODEL), const2),
        pl.BlockSpec((1, D_MODEL), const2),
        pl.BlockSpec((D_MODEL, D_MODEL), const2),
        pl.BlockSpec((1, D_MODEL), const2),
        pl.BlockSpec((1, D_MODEL), const2),
        pl.BlockSpec((tt, D_MODEL), lambda b, t: (srow(b, t), 0)),
        pl.BlockSpec((tt, D_MODEL), lambda b, t: (srow(b, t), 0)),
    ]
    out_specs = [
        pl.BlockSpec((tt, D_MODEL), lambda b, t: (orow(b, t), 0)),
        pl.BlockSpec((tt * ROW_TILES, LANES), lambda b, t: (orow(b, t), 0)),
        pl.BlockSpec((1, CONV_K - 1, D_MODEL), lambda b, t: (state(b), 0, 0)),
        pl.BlockSpec((1, N_HEADS, HEAD_QK, HEAD_V), lambda b, t: (state(b), 0, 0, 0)),
        pl.BlockSpec((1, N_HEADS, HEAD_QK), lambda b, t: (state(b), 0, 0)),
        pl.BlockSpec((1, 1, N_HEADS), lambda b, t: (state(b), 0, 0)),
    ]
    out_shape = [
        jax.ShapeDtypeStruct((mp + ms, D_MODEL), F32),
        jax.ShapeDtypeStruct(((mp + ms) * ROW_TILES, LANES), F32),
        jax.ShapeDtypeStruct((nb, CONV_K - 1, D_MODEL), F32),
        jax.ShapeDtypeStruct((nb, N_HEADS, HEAD_QK, HEAD_V), F32),
        jax.ShapeDtypeStruct((nb, N_HEADS, HEAD_QK), F32),
        jax.ShapeDtypeStruct((nb, 1, N_HEADS), F32),
    ]
    return pl.pallas_call(
        functools.partial(_prompt_kernel, chunk, nb, ns, alpha),
        grid=(nb + 1, nt),
        in_specs=in_specs,
        out_specs=out_specs,
        out_shape=out_shape,
        scratch_shapes=[pltpu.VMEM((tt, D_MODEL), F32)],
        compiler_params=_cparams(("arbitrary", "arbitrary")),
        name="prompt_half",
    )(b_gates, xp, w_a, w_b, wg_hi, wg_lo, w_conv, mh_gain, w_out, ln_g, ln_b, xs, mix_s)


def _router_kernel(x_ref, wh_ref, wl_ref, br_ref, idx_ref, gate_ref, rank_ref, cnt_ref):
    tt = x_ref.shape[0]

    @pl.when(pl.program_id(0) == 0)
    def _():
        cnt_ref[...] = jnp.zeros_like(cnt_ref)

    scores = _sigmoid(_dot3(x_ref[...], wh_ref[...], wl_ref[...]))
    lane = lax.broadcasted_iota(jnp.int32, (tt, N_EXPERTS), 1)
    work = scores + br_ref[...]
    picked = []
    sel = []
    member = jnp.zeros((tt, N_EXPERTS), F32)
    for _ in range(TOP_K):
        mx = jnp.max(work, axis=1, keepdims=True)
        ik = jnp.min(jnp.where(work == mx, lane, N_EXPERTS), axis=1, keepdims=True)
        onehot = lane == ik
        picked.append((ik, onehot))
        sel.append(jnp.sum(jnp.where(onehot, scores, 0.0), axis=1, keepdims=True))
        work = jnp.where(onehot, -jnp.inf, work)
        member = member + onehot.astype(F32)
    total = sel[0]
    for sk in sel[1:]:
        total = total + sk

    ti = lax.broadcasted_iota(jnp.int32, (tt, tt), 0)
    si = lax.broadcasted_iota(jnp.int32, (tt, tt), 1)
    earlier = (si < ti).astype(BF16)
    before = _dot(earlier, member.astype(BF16)) + cnt_ref[...]

    lane_o = lax.broadcasted_iota(jnp.int32, (tt, TOP_K), 1)
    idx_o = jnp.zeros((tt, TOP_K), jnp.int32)
    gate_o = jnp.zeros((tt, TOP_K), F32)
    rank_o = jnp.zeros((tt, TOP_K), jnp.int32)
    for kk in range(TOP_K):
        ik, onehot = picked[kk]
        rk = jnp.sum(jnp.where(onehot, before, 0.0), axis=1, keepdims=True)
        idx_o = jnp.where(lane_o == kk, ik, idx_o)
        gate_o = jnp.where(lane_o == kk, sel[kk] / total * ROUTED_SCALE, gate_o)
        rank_o = jnp.where(lane_o == kk, rk.astype(jnp.int32), rank_o)
    idx_ref[...] = idx_o
    gate_ref[...] = gate_o
    rank_ref[...] = rank_o
    cnt_ref[...] = cnt_ref[...] + jnp.sum(member, axis=0, keepdims=True)


def _router(x1, wr_hi, wr_lo, b_router, tt):
    m = x1.shape[0]
    return pl.pallas_call(
        _router_kernel,
        grid=(m // tt,),
        in_specs=[pl.BlockSpec((tt, D_MODEL), lambda i: (i, 0)),
                  pl.BlockSpec((D_MODEL, N_EXPERTS), lambda i: (0, 0)),
                  pl.BlockSpec((D_MODEL, N_EXPERTS), lambda i: (0, 0)),
                  pl.BlockSpec((1, N_EXPERTS), lambda i: (0, 0))],
        out_specs=[pl.BlockSpec((tt, TOP_K), lambda i: (i, 0)),
                   pl.BlockSpec((tt, TOP_K), lambda i: (i, 0)),
                   pl.BlockSpec((tt, TOP_K), lambda i: (i, 0)),
                   pl.BlockSpec((1, N_EXPERTS), lambda i: (0, 0))],
        out_shape=[jax.ShapeDtypeStruct((m, TOP_K), jnp.int32),
                   jax.ShapeDtypeStruct((m, TOP_K), F32),
                   jax.ShapeDtypeStruct((m, TOP_K), jnp.int32),
                   jax.ShapeDtypeStruct((1, N_EXPERTS), F32)],
        compiler_params=_cparams(("arbitrary",)),
        name="router",
    )(x1, wr_hi, wr_lo, b_router)


def _row_copy(src, src_row8, dst, dst_row8, sem):
    return pltpu.make_async_copy(src.at[pl.ds(pl.multiple_of(src_row8, ROW_TILES), ROW_TILES)],
                                 dst.at[pl.ds(pl.multiple_of(dst_row8, ROW_TILES), ROW_TILES)],
                                 sem)


def _dest_kernel(idx_ref, rank_ref, start_ref, dest_ref):
    idx = idx_ref[...]
    shape = idx.shape
    low = jnp.broadcast_to(start_ref[:, :LANES], shape)
    high = jnp.broadcast_to(start_ref[:, LANES:], shape)
    lane = jnp.bitwise_and(idx, LANES - 1)
    start = jnp.where(idx < LANES, jnp.take_along_axis(low, lane, axis=1),
                      jnp.take_along_axis(high, lane, axis=1))
    dest_ref[...] = (start + rank_ref[...]) * ROW_TILES


def _dest(idx, rank, seg_start):
    assert N_EXPERTS == 2 * LANES
    full = pl.BlockSpec(idx.shape, lambda i: (0, 0))
    return pl.pallas_call(
        _dest_kernel,
        grid=(1,),
        in_specs=[full, full, pl.BlockSpec((1, N_EXPERTS), lambda i: (0, 0))],
        out_specs=full,
        out_shape=jax.ShapeDtypeStruct(idx.shape, jnp.int32),
        compiler_params=_cparams(("arbitrary",)),
        name="dest",
    )(idx, rank, seg_start[None, :])


def _dispatch_kernel(tt, dest_ref, x1r_ref, xg_ref, zbuf, sem, zsem):
    @pl.when(pl.program_id(0) == 0)
    def _():
        zbuf[...] = jnp.zeros_like(zbuf)
        n_sorted = xg_ref.shape[0] - zbuf.shape[0]
        fill = pltpu.make_async_copy(zbuf, xg_ref.at[pl.ds(n_sorted, zbuf.shape[0])], zsem)
        fill.start()
        fill.wait()

    def issue(t, carry):
        for kk in range(TOP_K):
            _row_copy(x1r_ref, t * ROW_TILES, xg_ref, dest_ref[t * TOP_K + kk],
                      sem).start(priority=kk % 2)
        return carry

    lax.fori_loop(0, tt, issue, 0)
    n_rows = tt * TOP_K * ROW_TILES
    pltpu.make_async_copy(xg_ref.at[pl.ds(0, n_rows)], xg_ref.at[pl.ds(0, n_rows)], sem).wait()


def _dispatch(dest8, x1r, tt, slack):
    m = x1r.shape[0] // ROW_TILES
    return pl.pallas_call(
        functools.partial(_dispatch_kernel, tt),
        grid=(m // tt,),
        in_specs=[pl.BlockSpec((tt * TOP_K,), lambda i: (i,), memory_space=pltpu.SMEM),
                  pl.BlockSpec((tt * ROW_TILES, LANES), lambda i: (i, 0))],
        out_specs=pl.BlockSpec(memory_space=pl.ANY),
        out_shape=jax.ShapeDtypeStruct(((m * TOP_K + slack) * ROW_TILES, LANES), F32),
        scratch_shapes=[pltpu.VMEM((slack * ROW_TILES, LANES), F32),
                        pltpu.SemaphoreType.DMA(()), pltpu.SemaphoreType.DMA(())],
        compiler_params=_cparams(("arbitrary",)),
        name="dispatch",
    )(dest8, x1r)


def _experts_kernel(rb, e_ref, row_ref, valid_ref, new_ref, slot_ref, nxt_ref,
                    xg_hbm, wg_hbm, wu_hbm, wd_hbm, y_hbm,
                    *scratch):
    j = pl.program_id(0)
    n_steps = pl.num_programs(0)
    ring_pos = lax.rem(j, EXPERT_RING)
    valid = valid_ref[j]
    xbufs = scratch[:EXPERT_RING]
    ybufs = scratch[EXPERT_RING:2 * EXPERT_RING]
    wg_buf, wu_buf, wd_buf, wgb, wub, wdb, wsem, xsem, ysem = scratch[2 * EXPERT_RING:]

    def tile_rows(row, n):
        return pl.ds(pl.multiple_of(row * ROW_TILES, ROW_TILES), n * ROW_TILES)

    row_variants = [rb]

    def for_row_variant(n_valid, fn):
        for below, n in zip([0] + row_variants[:-1], row_variants):
            @pl.when(jnp.logical_and(n_valid > below, n_valid <= n))
            def _(n=n):
                fn(n)

    def x_copies(item, slot, start):
        def run(n):
            copy = pltpu.make_async_copy(xg_hbm.at[tile_rows(row_ref[item], n)],
                                         xbufs[slot].at[pl.ds(0, n * ROW_TILES)], xsem.at[slot])
            if start:
                copy.start()
            else:
                copy.wait()

        for_row_variant(valid_ref[item], run)

    def y_copies(item, slot, start):
        row0 = row_ref[item]
        n = valid_ref[item]

        def run(copy):
            if start:
                copy.start()
            else:
                copy.wait()

        @pl.when(n == rb)
        def _():
            run(pltpu.make_async_copy(ybufs[slot], y_hbm.at[tile_rows(row0, rb)], ysem.at[slot]))

        @pl.when(n < rb)
        def _():
            piece = rb // 2
            while piece >= 1:
                off = n - jnp.bitwise_and(n, 2 * piece - 1)

                @pl.when(jnp.bitwise_and(n, piece) != 0)
                def _(off=off, piece=piece):
                    run(pltpu.make_async_copy(ybufs[slot].at[tile_rows(off, piece)],
                                              y_hbm.at[tile_rows(row0 + off, piece)],
                                              ysem.at[slot]))

                piece //= 2

    def fetch(e, slot):
        return (pltpu.make_async_copy(wg_hbm.at[e], wg_buf.at[slot], wsem.at[slot, 0]),
                pltpu.make_async_copy(wu_hbm.at[e], wu_buf.at[slot], wsem.at[slot, 1]),
                pltpu.make_async_copy(wd_hbm.at[e], wd_buf.at[slot], wsem.at[slot, 2]))

    ahead = EXPERT_RING - 1

    @pl.when(j == 0)
    def _():
        for item in range(ahead):
            x_copies(item, item, start=True)

    ahead_item = jnp.minimum(j + ahead, n_steps - 1)
    for slot in range(EXPERT_RING):
        here = ring_pos == slot

        @pl.when(jnp.logical_and(here, j >= EXPERT_RING))
        def _(slot=slot):
            y_copies(j - EXPERT_RING, slot, start=False)

        @pl.when(jnp.logical_and(here, valid > 0))
        def _(slot=slot):
            x_copies(j, slot, start=False)

            @pl.when(j + ahead < n_steps)
            def _():
                x_copies(ahead_item, (slot + ahead) % EXPERT_RING, start=True)

    @pl.when(new_ref[j] != 0)
    def _():
        slot = slot_ref[j]

        @pl.when(j == 0)
        def _():
            for copy in fetch(e_ref[j], slot):
                copy.start()

        for copy in fetch(e_ref[j], slot):
            copy.wait()
        wgb[...] = wg_buf[slot].astype(BF16)
        wub[...] = wu_buf[slot].astype(BF16)
        wdb[...] = wd_buf[slot].astype(BF16)

        @pl.when(nxt_ref[j] >= 0)
        def _():
            for copy in fetch(nxt_ref[j], 1 - slot):
                copy.start()

    def swiglu_rows(slot, n):
        xb = jnp.concatenate(_load_row_chunks(xbufs[slot], n), axis=1).astype(BF16)
        g = _dot(xb, wgb[...])
        u = _dot(xb, wub[...])
        y = _dot((g * _sigmoid(g) * u).astype(BF16), wdb[...])
        _store_rows(ybufs[slot], y)

    for slot in range(EXPERT_RING):
        here = ring_pos == slot

        @pl.when(jnp.logical_and(here, valid > 0))
        def _(slot=slot):
            for_row_variant(valid, functools.partial(swiglu_rows, slot))
            y_copies(j, slot, start=True)

        @pl.when(jnp.logical_and(here, j == n_steps - 1))
        def _(slot=slot):
            for back in range(EXPERT_RING - 1, -1, -1):
                y_copies(j - back, (slot - back) % EXPERT_RING, start=False)


def _experts(item_e, item_row, item_valid, item_new, item_slot, item_nxt,
             xg, w_eg, w_eu, w_ed, rb):
    n_items = item_e.shape[0]
    n_rows = xg.shape[0] // ROW_TILES - rb
    hbm = pl.BlockSpec(memory_space=pl.ANY)
    row_buf = pltpu.VMEM((rb * ROW_TILES, LANES), F32)
    grid_spec = pltpu.PrefetchScalarGridSpec(
        num_scalar_prefetch=6,
        grid=(n_items,),
        in_specs=[hbm, hbm, hbm, hbm],
        out_specs=hbm,
        scratch_shapes=[row_buf] * (2 * EXPERT_RING) + [
                        pltpu.VMEM((2, D_MODEL, D_EXPERT), F32),
                        pltpu.VMEM((2, D_MODEL, D_EXPERT), F32),
                        pltpu.VMEM((2, D_EXPERT, D_MODEL), F32),
                        pltpu.VMEM((D_MODEL, D_EXPERT), BF16),
                        pltpu.VMEM((D_MODEL, D_EXPERT), BF16),
                        pltpu.VMEM((D_EXPERT, D_MODEL), BF16),
                        pltpu.SemaphoreType.DMA((2, 3)),
                        pltpu.SemaphoreType.DMA((EXPERT_RING,)),
                        pltpu.SemaphoreType.DMA((EXPERT_RING,))],
    )
    return pl.pallas_call(
        functools.partial(_experts_kernel, rb),
        grid_spec=grid_spec,
        out_shape=jax.ShapeDtypeStruct((n_rows * ROW_TILES, LANES), F32),
        compiler_params=_cparams(("arbitrary",)),
        name="experts",
    )(item_e, item_row, item_valid, item_new, item_slot, item_nxt, xg, w_eg, w_eu, w_ed)


def _combine_kernel(alpha, n_p, n_steps, dest0_ref, dest1_ref, dest2_ref, gate_ref, x1_ref, y_ref,
                    wsg_ref, wsu_ref, wsd_ref, g_ref, b_ref, x2p_ref, x2s_ref,
                    base_ref, x2_ref, *ring):
    step = pl.program_id(0)
    tt = x1_ref.shape[0]
    n_rows = TOP_K * tt * ROW_TILES
    group = COMBINE_GROUP
    n_groups = tt // group
    bufs, sem = ring[:-1], ring[-1]
    ahead = COMBINE_RING - 1

    def issue_group(dref, grp, slot):
        for r in range(group):
            t = grp * group + r
            for kk in range(TOP_K):
                _row_copy(y_ref, dref[t * TOP_K + kk], bufs[slot],
                          (kk * tt + t) * ROW_TILES, sem.at[slot]).start(priority=kk % 2)

    def issue_tile(dref, slot):
        def body(grp, carry):
            issue_group(dref, grp, slot)
            return carry

        lax.fori_loop(0, n_groups, body, 0)

    def reduce_group(grp, slot):
        rows = pl.ds(pl.multiple_of(grp * group, group), group)
        gates = gate_ref[rows, :]
        chunks = []
        for s in range(ROW_TILES):
            acc = jnp.zeros((group, LANES), F32)
            for kk in range(TOP_K):
                row0 = kk * tt * ROW_TILES + grp * (group * ROW_TILES) + s
                acc = acc + gates[:, kk:kk + 1] * bufs[slot][pl.ds(row0, group,
                                                                    stride=ROW_TILES), :]
            chunks.append(acc)
        routed = jnp.concatenate(chunks, axis=1)
        x2_ref[rows, :] = _layer_norm(base_ref[rows, :] + routed, g_ref[...], b_ref[...])

    @pl.when(step == 0)
    def _():
        issue_tile(dest0_ref, 0)
        if n_steps > 1:
            issue_tile(dest1_ref, 1)

    x1 = x1_ref[...]
    xb = x1.astype(BF16)
    gs = _dot(xb, wsg_ref[...])
    us = _dot(xb, wsu_ref[...])
    base_ref[...] = alpha * x1 + _dot((gs * _sigmoid(gs) * us).astype(BF16), wsd_ref[...])

    for slot in range(COMBINE_RING):
        here = lax.rem(step, COMBINE_RING) == slot
        ahead_slot = (slot + ahead) % COMBINE_RING

        @pl.when(here)
        def _(slot=slot):
            pltpu.make_async_copy(y_ref.at[pl.ds(0, n_rows)], bufs[slot], sem.at[slot]).wait()

        @pl.when(jnp.logical_and(here, step + ahead < n_steps))
        def _(slot=slot, ahead_slot=ahead_slot):
            def body(grp, carry):
                issue_group(dest2_ref, grp, ahead_slot)
                reduce_group(grp, slot)
                return carry

            lax.fori_loop(0, n_groups, body, 0)

        @pl.when(jnp.logical_and(here, step + ahead >= n_steps))
        def _(slot=slot):
            def body(grp, carry):
                reduce_group(grp, slot)
                return carry

            lax.fori_loop(0, n_groups, body, 0)

    @pl.when(step < n_p)
    def _():
        x2p_ref[...] = x2_ref[...]

    @pl.when(step >= n_p)
    def _():
        x2s_ref[...] = x2_ref[...]


def _combine(dest8, gates, x1, y, wsg, wsu, wsd, g, b, alpha, tt, mp):
    m = x1.shape[0]
    n_steps = m // tt
    n_p = mp // tt

    def dest_spec(ahead):
        return pl.BlockSpec((tt * TOP_K,), lambda i: (jnp.minimum(i + ahead, n_steps - 1),),
                            memory_space=pltpu.SMEM)

    return pl.pallas_call(
        functools.partial(_combine_kernel, alpha, n_p, n_steps),
        grid=(n_steps,),
        in_specs=[dest_spec(0), dest_spec(1), dest_spec(2),
                  pl.BlockSpec((tt, TOP_K), lambda i: (i, 0)),
                  pl.BlockSpec((tt, D_MODEL), lambda i: (i, 0)),
                  pl.BlockSpec(memory_space=pl.ANY),
                  pl.BlockSpec((D_MODEL, D_EXPERT), lambda i: (0, 0)),
                  pl.BlockSpec((D_MODEL, D_EXPERT), lambda i: (0, 0)),
                  pl.BlockSpec((D_EXPERT, D_MODEL), lambda i: (0, 0)),
                  pl.BlockSpec((1, D_MODEL), lambda i: (0, 0)),
                  pl.BlockSpec((1, D_MODEL), lambda i: (0, 0))],
        out_specs=[pl.BlockSpec((tt, D_MODEL), lambda i: (jnp.minimum(i, n_p - 1), 0)),
                   pl.BlockSpec((tt, D_MODEL), lambda i: (jnp.maximum(i - n_p, 0), 0))],
        out_shape=[jax.ShapeDtypeStruct((mp, D_MODEL), F32),
                   jax.ShapeDtypeStruct((m - mp, D_MODEL), F32)],
        scratch_shapes=[pltpu.VMEM((tt, D_MODEL), F32), pltpu.VMEM((tt, D_MODEL), F32)]
        + [pltpu.VMEM((TOP_K * tt * ROW_TILES, LANES), F32)] * COMBINE_RING
        + [pltpu.SemaphoreType.DMA((COMBINE_RING,))],
        compiler_params=_cparams(("arbitrary",)),
        name="combine",
    )(dest8, dest8, dest8, gates, x1, y, wsg, wsu, wsd, g, b)


PROJ_TM = 512
PROJ_TN = 1024
MIX_TT = 512
MIX_CHUNK = 256
SAMPLE_NSEQ = 8
ROUTER_TT = 512
DISPATCH_TT = 1024
EXPERT_RB = 256
EXPERT_RING = 3
COMBINE_TT = 256
COMBINE_RING = 3
COMBINE_GROUP = 64


def _moe(x1, x1r, mp, w_router, b_router, w_eg, w_eu, w_ed, w_sg, w_su, w_sd, ln_g, ln_b, alpha):
    m = x1.shape[0]
    wr_hi, wr_lo = _split_hi_lo(w_router)
    idx, gates, rank, cnt = _router(x1, wr_hi, wr_lo, b_router[None, :], ROUTER_TT)

    rb = EXPERT_RB
    n_rows = m * TOP_K
    n_items = n_rows // rb + N_EXPERTS
    counts = cnt[0].astype(jnp.int32)
    seg_end = jnp.cumsum(counts)
    seg_start = seg_end - counts
    n_chunk = (counts + rb - 1) // rb
    item_end = jnp.cumsum(n_chunk)
    item_start = item_end - n_chunk
    item = jnp.arange(n_items, dtype=jnp.int32)
    item_c = jnp.minimum(item, item_end[-1] - 1)
    item_e = jnp.sum((item_end[None, :] <= item_c[:, None]).astype(jnp.int32), axis=1)
    onehot_e = item_e[:, None] == jnp.arange(N_EXPERTS, dtype=jnp.int32)[None, :]

    def pick(table):
        return jnp.sum(jnp.where(onehot_e, table[None, :], 0), axis=1)

    chunk = item_c - pick(item_start)
    valid = item < item_end[-1]
    item_row = pick(seg_start) + chunk * rb
    item_valid = jnp.where(valid, jnp.clip(pick(counts) - chunk * rb, 0, rb), 0)

    experts = jnp.arange(N_EXPERTS, dtype=jnp.int32)
    nonempty = counts > 0
    slot_e = (jnp.cumsum(nonempty.astype(jnp.int32)) - 1) % 2
    later = jnp.logical_and(experts[None, :] > experts[:, None], nonempty[None, :])
    nxt_e = jnp.min(jnp.where(later, experts[None, :], N_EXPERTS), axis=1)
    nxt_e = jnp.where(nxt_e == N_EXPERTS, -1, nxt_e)
    item_new = jnp.logical_and(valid, chunk == 0).astype(jnp.int32)

    dest8 = _dest(idx.reshape(-1, LANES), rank.reshape(-1, LANES), seg_start).reshape(-1)
    xg = _dispatch(dest8, x1r, DISPATCH_TT, rb)
    y = _experts(item_e, item_row, item_valid, item_new, pick(slot_e), pick(nxt_e),
                 xg, w_eg, w_eu, w_ed, rb)
    return _combine(dest8, gates, x1, y, w_sg.astype(BF16), w_su.astype(BF16),
                    w_sd.astype(BF16), ln_g[None, :], ln_b[None, :], alpha, COMBINE_TT, mp)


def _layer(xp, xs, conv_s, c_s, n_s, m_s, w_in, b_gates, w_conv, mh_gain, w_out, ln1_g, ln1_b,
           w_router, b_router, w_eg, w_eu, w_ed, w_sg, w_su, w_sd, ln2_g, ln2_b, alpha):
    bp, tp, _ = xp.shape
    bs, ts, _ = xs.shape
    mp, ms = bp * tp, bs * ts
    xp2 = xp.reshape(mp, D_MODEL)
    xs2 = xs.reshape(ms, D_MODEL)

    w_a = w_in[:, :GATE_COL0].astype(BF16)
    w_b = w_in[:, GATE_COL0 + 2 * N_HEADS:].astype(BF16)
    w_gate = jnp.pad(w_in[:, GATE_COL0:GATE_COL0 + 2 * N_HEADS],
                     ((0, 0), (0, LANES - 2 * N_HEADS)))
    wg_hi, wg_lo = _split_hi_lo(w_gate)

    zg_s = _gate_proj(xs2, wg_hi, wg_lo, PROJ_TM)
    gt_s = zg_s[:, :SUBLANES].reshape(bs, ts, SUBLANES).transpose(0, 2, 1)
    mix_s, conv_n, c_n, n_n, m_n = _mixer(
        _proj(xs2, w_a, PROJ_TM, PROJ_TN), _proj(xs2, w_b, PROJ_TM, PROJ_TN),
        zg_s, gt_s, b_gates, w_conv, mh_gain[None, :],
        conv_s, c_s, n_s, m_s[:, None, :], nb=bs, ts=ts, nseq=SAMPLE_NSEQ)

    x1, x1r, conv_p, c_p, n_p, m_p = _prompt_layer_half(
        xp2, xs2, mix_s, b_gates, w_a, w_b, wg_hi, wg_lo, w_conv, mh_gain[None, :],
        w_out.astype(BF16), ln1_g[None, :], ln1_b[None, :], alpha,
        nb=bp, nt=tp // MIX_TT, tt=MIX_TT, chunk=MIX_CHUNK)

    x2p, x2s = _moe(x1, x1r, mp, w_router, b_router, w_eg, w_eu, w_ed, w_sg, w_su, w_sd,
                    ln2_g, ln2_b, alpha)
    states_p = (conv_p, c_p, n_p, m_p[:, 0, :])
    states_s = (conv_n, c_n, n_n, m_n[:, 0, :])
    return x2p.reshape(bp, tp, D_MODEL), x2s.reshape(bs, ts, D_MODEL), states_p, states_s


def kernel(x_prompt, x_sample, cache_conv, state_mlstm_C, state_mlstm_n, state_mlstm_m, w_in, b_gates, w_conv, mh_gain, w_out, ln1_g, ln1_b, w_router, b_router, w_exp_gate, w_exp_up, w_exp_down, w_sh_gate, w_sh_up, w_sh_down, ln2_g, ln2_b):
    depth = w_in.shape[0]
    alpha = (2.0 * depth) ** 0.25
    hp, hs = x_prompt, x_sample
    outs_p = [[], [], [], []]
    outs_s = [[], [], [], []]
    for l in range(depth):
        hp, hs, st_p, st_s = _layer(
            hp, hs, cache_conv[l], state_mlstm_C[l], state_mlstm_n[l], state_mlstm_m[l],
            w_in[l], b_gates[l], w_conv[l], mh_gain[l], w_out[l], ln1_g[l], ln1_b[l],
            w_router[l], b_router[l], w_exp_gate[l], w_exp_up[l], w_exp_down[l],
            w_sh_gate[l], w_sh_up[l], w_sh_down[l], ln2_g[l], ln2_b[l], alpha)
        for acc, val in zip(outs_p, st_p):
            acc.append(val)
        for acc, val in zip(outs_s, st_s):
            acc.append(val)
    return (hp, hs) + tuple(jnp.stack(a) for a in outs_p) + tuple(jnp.stack(a) for a in outs_s)
```
